```python
import math
import jax, jax.numpy as jnp
from jax import lax
import numpy as np

D_MODEL = 1024
BATCH = 2
SEQ = 8192
DEPTH = 2

EPS = 1e-6
D_MIX = D_MODEL

POOL_WINDOWS = (2, 4, 8, 16)
POOL_GROUPS = 4
D_POOL = D_MIX // 4
POOL_GROUP_DIM = D_POOL // POOL_GROUPS

D_SSD = D_MIX // 2
SSD_HEAD_DIM = 64
SSD_HEADS = D_SSD // SSD_HEAD_DIM
SSD_GROUPS = 2
SSD_STATE = 128
SSD_CONV = 4
SSD_CHUNK = 128
D_SSD_XBC = D_SSD + 2 * SSD_GROUPS * SSD_STATE

D_MLSTM = D_MIX // 4
MLSTM_HEAD_DIM = 64
MLSTM_HEADS = D_MLSTM // MLSTM_HEAD_DIM
MLSTM_CONV = 4
MLSTM_CHUNK = 128

D_FF = 2816
FFN_CONV = 3

IN_SIZES = (D_POOL, D_SSD, D_SSD_XBC, SSD_HEADS, 2 * D_MLSTM, D_MLSTM, D_MLSTM, MLSTM_HEADS, MLSTM_HEADS)
IN_COLS = D_POOL + D_SSD + D_SSD_XBC + SSD_HEADS + 4 * D_MLSTM + 2 * MLSTM_HEADS

kernel_name = "hybrid_pool_ssd_mlstm_convffn"


def rmsnorm(x, w):
    xf = x.astype(jnp.float32)
    y = xf * lax.rsqrt(jnp.mean(xf * xf, axis=-1, keepdims=True) + EPS)
    return (y * w.astype(jnp.float32)).astype(x.dtype)


def causal_dwconv(x, w, b):
    K = w.shape[0]
    T = x.shape[1]
    xp = jnp.pad(x, ((0, 0), (K - 1, 0), (0, 0)))
    w = w.astype(x.dtype)
    y = xp[:, 0:T] * w[0]
    for k in range(1, K):
        y = y + xp[:, k:k + T] * w[k]
    return y + b.astype(x.dtype)


def split_cols(p, sizes):
    idx, acc = [], 0
    for s in sizes[:-1]:
        acc += s
        idx.append(acc)
    return jnp.split(p, idx, axis=-1)


def causal_mask(L):
    return jnp.tril(jnp.ones((L, L), dtype=bool))


def pool_mixer(u, w_pool, b_pool, pool_scale):
    Bsz, T, _ = u.shape
    uf = u.astype(jnp.float32).reshape(Bsz, T, POOL_GROUPS, POOL_GROUP_DIM)
    cs = jnp.pad(jnp.cumsum(uf, axis=1), ((0, 0), (1, 0), (0, 0), (0, 0)))
    pos = jnp.arange(1, T + 1, dtype=jnp.float32)
    means = []
    for g, win in enumerate(POOL_WINDOWS):
        c = cs[:, :, g]
        upper = c[:, 1:]
        lower = jnp.pad(c, ((0, 0), (win - 1, 0), (0, 0)))[:, :T]
        count = jnp.minimum(pos, float(win))
        means.append((upper - lower) / count[None, :, None])
    pooled = jnp.stack(means, axis=2) - uf
    y = jnp.einsum('btgc,gcd->btgd', pooled, w_pool.astype(jnp.float32))
    y = y + b_pool.astype(jnp.float32).reshape(POOL_GROUPS, POOL_GROUP_DIM)
    y = y.reshape(Bsz, T, D_POOL) * pool_scale.astype(jnp.float32)
    return y.astype(u.dtype)


def segsum(a):
    cs = jnp.cumsum(a, axis=-1)
    seg = cs[..., :, None] - cs[..., None, :]
    return jnp.where(causal_mask(a.shape[-1]), seg, -jnp.inf)


def ssd_mixer(z, xbc, dt_raw, conv_w, conv_b, dt_bias, a_log, d_skip, norm_w):
    Bsz, T, _ = z.shape
    H, P, G, N, L = SSD_HEADS, SSD_HEAD_DIM, SSD_GROUPS, SSD_STATE, SSD_CHUNK
    nc = T // L
    xbc = jax.nn.silu(causal_dwconv(xbc, conv_w, conv_b))
    xs, Bm, Cm = jnp.split(xbc, [D_SSD, D_SSD + G * N], axis=-1)
    dt = jax.nn.softplus(dt_raw.astype(jnp.float32) + dt_bias.astype(jnp.float32))
    A = -jnp.exp(a_log.astype(jnp.float32))
    x = xs.astype(jnp.float32).reshape(Bsz, T, H, P)
    Bh = jnp.repeat(Bm.astype(jnp.float32).reshape(Bsz, T, G, N), H // G, axis=2)
    Ch = jnp.repeat(Cm.astype(jnp.float32).reshape(Bsz, T, G, N), H // G, axis=2)
    X = (x * dt[..., None]).reshape(Bsz, nc, L, H, P)
    Bc = Bh.reshape(Bsz, nc, L, H, N)
    Cc = Ch.reshape(Bsz, nc, L, H, N)
    Adt = (dt * A).reshape(Bsz, nc, L, H).transpose(0, 3, 1, 2)
    A_cs = jnp.cumsum(Adt, axis=-1)
    Lmat = jnp.exp(segsum(Adt))
    y_diag = jnp.einsum('bclhn,bcshn,bhcls,bcshp->bclhp', Cc, Bc, Lmat, X)
    decay = jnp.exp(A_cs[..., -1:] - A_cs)
    states = jnp.einsum('bclhn,bhcl,bclhp->bchpn', Bc, decay, X)
    chunk_decay = jnp.exp(A_cs[..., -1])

    def step(carry, inp):
        dec, st = inp
        return carry * dec[..., None, None] + st, carry

    init = jnp.zeros((Bsz, H, P, N), jnp.float32)
    _, prev = lax.scan(step, init, (chunk_decay.transpose(2, 0, 1), states.transpose(1, 0, 2, 3, 4)))
    prev = prev.transpose(1, 0, 2, 3, 4)
    y_off = jnp.einsum('bclhn,bchpn,bhcl->bclhp', Cc, prev, jnp.exp(A_cs))
    y = (y_diag + y_off).reshape(Bsz, T, H, P) + x * d_skip.astype(jnp.float32)[:, None]
    y = y.reshape(Bsz, T, D_SSD) * jax.nn.silu(z.astype(jnp.float32))
    return rmsnorm(y, norm_w).astype(z.dtype)


def mlstm_mixer(qk, v, o_raw, i_raw, f_raw, conv_w, conv_b, i_bias, f_bias, norm_w):
    Bsz, T, _ = v.shape
    H, Dh, L = MLSTM_HEADS, MLSTM_HEAD_DIM, MLSTM_CHUNK
    nc = T // L
    qk = jax.nn.silu(causal_dwconv(qk, conv_w, conv_b))
    q, k = jnp.split(qk, 2, axis=-1)
    q = q.astype(jnp.float32).reshape(Bsz, nc, L, H, Dh) * (Dh ** -0.5)
    k = k.astype(jnp.float32).reshape(Bsz, nc, L, H, Dh)
    vv = v.astype(jnp.float32).reshape(Bsz, nc, L, H, Dh)
    log_i = (i_raw.astype(jnp.float32) + i_bias.astype(jnp.float32))
    log_f = jax.nn.log_sigmoid(f_raw.astype(jnp.float32) + f_bias.astype(jnp.float32))
    log_i = log_i.reshape(Bsz, nc, L, H).transpose(0, 3, 1, 2)
    log_f = log_f.reshape(Bsz, nc, L, H).transpose(0, 3, 1, 2)
    b = jnp.cumsum(log_f, axis=-1)
    Dlog = b[..., :, None] - b[..., None, :] + log_i[..., None, :]
    Dlog = jnp.where(causal_mask(L), Dlog, -jnp.inf)
    a = b[..., -1:] - b + log_i
    m_loc = jnp.max(a, axis=-1)
    w_loc = jnp.exp(a - m_loc[..., None])
    C_loc = jnp.einsum('bhcl,bclhd,bclhe->bchde', w_loc, k, vv)
    n_loc = jnp.einsum('bhcl,bclhd->bchd', w_loc, k)
    b_last = b[..., -1]

    def step(carry, inp):
        C, n, m = carry
        bl, ml, Cl, nl = inp
        m_new = jnp.maximum(bl + m, ml)
        sp = jnp.exp(bl + m - m_new)
        sl = jnp.exp(ml - m_new)
        C_new = sp[..., None, None] * C + sl[..., None, None] * Cl
        n_new = sp[..., None] * n + sl[..., None] * nl
        return (C_new, n_new, m_new), (C, n, m)

    init = (jnp.zeros((Bsz, H, Dh, Dh), jnp.float32), jnp.zeros((Bsz, H, Dh), jnp.float32),
            jnp.zeros((Bsz, H), jnp.float32))
    xs_in = (b_last.transpose(2, 0, 1), m_loc.transpose(2, 0, 1),
             C_loc.transpose(1, 0, 2, 3, 4), n_loc.transpose(1, 0, 2, 3))
    _, (prev_C, prev_n, prev_m) = lax.scan(step, init, xs_in)
    prev_C = prev_C.transpose(1, 0, 2, 3, 4)
    prev_n = prev_n.transpose(1, 0, 2, 3)
    prev_m = prev_m.transpose(1, 2, 0)
    inter_log = b + prev_m[..., None]
    m_t = jnp.maximum(inter_log, jnp.max(Dlog, axis=-1))
    w_intra = jnp.exp(Dlog - m_t[..., None])
    w_inter = jnp.exp(inter_log - m_t)
    s = jnp.einsum('bclhd,bcshd->bhcls', q, k) * w_intra
    num = (jnp.einsum('bhcls,bcshe->bclhe', s, vv)
           + jnp.einsum('bclhd,bchde,bhcl->bclhe', q, prev_C, w_inter))
    den = jnp.sum(s, axis=-1) + jnp.einsum('bclhd,bchd,bhcl->bhcl', q, prev_n, w_inter)
    den = jnp.maximum(jnp.abs(den), jnp.exp(-m_t)).transpose(0, 2, 3, 1)
    h = (num / den[..., None]).reshape(Bsz, T, H, Dh)
    h = jax.nn.sigmoid(o_raw.astype(jnp.float32)).reshape(Bsz, T, H, Dh) * h
    h = h * lax.rsqrt(jnp.mean(h * h, axis=-1, keepdims=True) + EPS)
    h = h.reshape(Bsz, T, D_MLSTM) * norm_w.astype(jnp.float32)
    return h.astype(v.dtype)


def conv_ffn(h, w_up, conv_w, conv_b, w_down):
    u = causal_dwconv(h @ w_up, conv_w, conv_b)
    g, val = jnp.split(u, 2, axis=-1)
    return (jax.nn.gelu(g, approximate=True) * val) @ w_down


def setup_inputs(seed: int = 0) -> dict:
    key = jax.random.key(seed)
    ks = jax.random.split(key, 32)
    f32 = jnp.float32

    def nrm(k, shape, scale):
        return jax.random.normal(k, shape, f32) * scale

    def gain(k, shape):
        return 1.0 + 0.02 * jax.random.normal(k, shape, f32)

    dt0 = jnp.exp(jax.random.uniform(ks[9], (DEPTH, SSD_HEADS), f32, math.log(1e-3), math.log(1e-1)))
    dt_bias = dt0 + jnp.log(-jnp.expm1(-dt0))
    f_bias = jnp.linspace(3.0, 6.0, MLSTM_HEADS, dtype=f32)[None, :] + 0.1 * jax.random.normal(ks[16], (DEPTH, MLSTM_HEADS), f32)
    return {
        "x": jax.random.normal(ks[0], (BATCH, SEQ, D_MODEL), f32),
        "pre_mix_norm": gain(ks[1], (DEPTH, D_MODEL)),
        "w_in": nrm(ks[2], (DEPTH, D_MODEL, IN_COLS), D_MODEL ** -0.5),
        "pool_w": nrm(ks[3], (DEPTH, POOL_GROUPS, POOL_GROUP_DIM, POOL_GROUP_DIM), POOL_GROUP_DIM ** -0.5),
        "pool_b": nrm(ks[4], (DEPTH, D_POOL), 0.02),
        "pool_scale": gain(ks[5], (DEPTH, D_POOL)),
        "ssd_conv_w": nrm(ks[6], (DEPTH, SSD_CONV, D_SSD_XBC), SSD_CONV ** -0.5),
        "ssd_conv_b": nrm(ks[7], (DEPTH, D_SSD_XBC), 0.02),
        "ssd_dt_bias": dt_bias,
        "ssd_a_log": jnp.log(jax.random.uniform(ks[10], (DEPTH, SSD_HEADS), f32, 1.0, 16.0)),
        "ssd_d": gain(ks[11], (DEPTH, SSD_HEADS)),
        "ssd_norm": gain(ks[12], (DEPTH, D_SSD)),
        "mlstm_conv_w": nrm(ks[13], (DEPTH, MLSTM_CONV, 2 * D_MLSTM), MLSTM_CONV ** -0.5),
        "mlstm_conv_b": nrm(ks[14], (DEPTH, 2 * D_MLSTM), 0.02),
        "mlstm_i_bias": nrm(ks[15], (DEPTH, MLSTM_HEADS), 0.1),
        "mlstm_f_bias": f_bias,
        "mlstm_norm": gain(ks[17], (DEPTH, D_MLSTM)),
        "w_out": nrm(ks[18], (DEPTH, D_MIX, D_MODEL), D_MIX ** -0.5),
        "post_mix_norm": gain(ks[19], (DEPTH, D_MODEL)),
        "pre_ffn_norm": gain(ks[20], (DEPTH, D_MODEL)),
        "ffn_w_up": nrm(ks[21], (DEPTH, D_MODEL, 2 * D_FF), D_MODEL ** -0.5),
        "ffn_conv_w": nrm(ks[22], (DEPTH, FFN_CONV, 2 * D_FF), FFN_CONV ** -0.5),
        "ffn_conv_b": nrm(ks[23], (DEPTH, 2 * D_FF), 0.02),
        "ffn_w_down": nrm(ks[24], (DEPTH, D_FF, D_MODEL), D_FF ** -0.5),
        "post_ffn_norm": gain(ks[25], (DEPTH, D_MODEL)),
    }


def reference(x, pre_mix_norm, w_in, pool_w, pool_b, pool_scale, ssd_conv_w, ssd_conv_b,
              ssd_dt_bias, ssd_a_log, ssd_d, ssd_norm, mlstm_conv_w, mlstm_conv_b,
              mlstm_i_bias, mlstm_f_bias, mlstm_norm, w_out, post_mix_norm, pre_ffn_norm,
              ffn_w_up, ffn_conv_w, ffn_conv_b, ffn_w_down, post_ffn_norm):
    for l in range(DEPTH):
        h = rmsnorm(x, pre_mix_norm[l])
        p = h @ w_in[l]
        u_pool, z, xbc, dt_raw, qk, v, o_raw, i_raw, f_raw = split_cols(p, IN_SIZES)
        y_pool = pool_mixer(u_pool, pool_w[l], pool_b[l], pool_scale[l])
        y_ssd = ssd_mixer(z, xbc, dt_raw, ssd_conv_w[l], ssd_conv_b[l], ssd_dt_bias[l],
                          ssd_a_log[l], ssd_d[l], ssd_norm[l])
        y_mlstm = mlstm_mixer(qk, v, o_raw, i_raw, f_raw, mlstm_conv_w[l], mlstm_conv_b[l],
                              mlstm_i_bias[l], mlstm_f_bias[l], mlstm_norm[l])
        mix = jnp.concatenate([y_pool, y_ssd, y_mlstm], axis=-1) @ w_out[l]
        x = x + rmsnorm(mix, post_mix_norm[l])
        h = rmsnorm(x, pre_ffn_norm[l])
        f = conv_ffn(h, ffn_w_up[l], ffn_conv_w[l], ffn_conv_b[l], ffn_w_down[l])
        x = x + rmsnorm(f, post_ffn_norm[l])
    return x
```

```python
import functools
import math

import jax
import jax.numpy as jnp
from jax import lax
from jax.experimental import pallas as pl
from jax.experimental.pallas import tpu as pltpu

F32 = jnp.float32
BF16 = jnp.bfloat16

D_MODEL = 1024
EPS = 1e-6

D_POOL = 256
POOL_GROUP_DIM = 64
POOL_WINDOWS = (2, 4, 8, 16)

D_SSD = 512
SSD_HEADS = 8
SSD_HEAD_DIM = 64
SSD_GROUPS = 2
SSD_STATE = 128
SSD_CONV = 4
D_SSD_XBC = D_SSD + 2 * SSD_GROUPS * SSD_STATE

D_MLSTM = 256
MLSTM_HEADS = 4
MLSTM_HEAD_DIM = 64
MLSTM_CONV = 4

D_FF = 2816
FFN_CONV = 3

CHUNK = 128
LANES = 128
SUBLANES = 8

C_POOL = 0
C_XBC = C_POOL + D_POOL
C_QK = C_XBC + D_SSD_XBC
N_HALO_COLS = C_QK + 2 * D_MLSTM
R_Z = 0
R_V = R_Z + D_SSD
R_O = R_V + D_MLSTM
R_G = R_O + D_MLSTM
N_REST_COLS = R_G + LANES
N_IN_COLS = N_HALO_COLS + N_REST_COLS
G_DT = 0
G_I = SSD_HEADS
G_F = G_I + MLSTM_HEADS

HALO = 32
MIX_TT = 256
FFN_TT = 512
FFN_FT = 256
VMEM_LIMIT = 56 * 1024 * 1024


def _dot(a, b):
    return jnp.dot(a, b, preferred_element_type=F32)


def _dot_nt(a, b):
    return lax.dot_general(a, b, (((1,), (1,)), ((), ())), preferred_element_type=F32)


def _silu(x):
    return x * jax.nn.sigmoid(x)


def _split3(a):
    hi = a.astype(BF16)
    r = a - hi.astype(F32)
    mid = r.astype(BF16)
    lo = (r - mid.astype(F32)).astype(BF16)
    return hi, mid, lo


def _bcast_lane(a, j, shape):
    return jnp.broadcast_to(a[:, j:j + 1], shape)


def _pair_expand(a, h_even, shape, lo_half):
    return jnp.where(lo_half, _bcast_lane(a, h_even, shape), _bcast_lane(a, h_even + 1, shape))


def mix_kernel(x_ref, nw_ref, win_ref, gbias_ref, alog_ref, xcw_ref, xcb_ref, qcw_ref, qcb_ref,
               poolw_ref, poolb_ref, pools_ref, dskip_ref, snorm_ref, mnorm_ref, wout_ref, postn_ref,
               out_ref, hp_ref, rest_ref, act_ref, mix_ref, ps_ref, sstate_ref, mstate_ref, mm_ref):
    TT = x_ref.shape[1]
    L = CHUNK
    i = pl.program_id(1)

    @pl.when(i == 0)
    def _():
        hp_ref[0:HALO, :] = jnp.zeros((HALO, N_HALO_COLS), F32)
        sstate_ref[...] = jnp.zeros(sstate_ref.shape, F32)
        mstate_ref[...] = jnp.zeros(mstate_ref.shape, F32)
        mm_ref[...] = jnp.zeros(mm_ref.shape, F32)

    @pl.when(i > 0)
    def _():
        hp_ref[0:HALO, :] = hp_ref[TT:TT + HALO, :]

    x = x_ref[0]
    ms = jnp.mean(x * x, axis=-1, keepdims=True)
    h = (x * lax.rsqrt(ms + EPS) * nw_ref[...]).astype(BF16)
    hp_ref[HALO:HALO + TT, :] = _dot(h, win_ref[:, 0:N_HALO_COLS])
    rest_ref[...] = _dot(h, win_ref[:, N_HALO_COLS:N_IN_COLS])

    lane = lax.broadcasted_iota(jnp.int32, (L, LANES), 1)
    row = lax.broadcasted_iota(jnp.int32, (L, LANES), 0)
    lo_half = lane < 64
    causal = lane <= row
    tril = jnp.where(causal, 1.0, 0.0).astype(BF16)
    lane_row = lax.broadcasted_iota(jnp.int32, (1, LANES), 1)
    a_row = -jnp.exp(alog_ref[...])
    neg_inf = -jnp.inf

    for c in range(TT // L):
        r0 = c * L

        def conv_block(col, cw_ref, cb_ref, wcol, k_taps):
            acc = cb_ref[:, wcol:wcol + LANES]
            for k in range(k_taps):
                acc = acc + (hp_ref[pl.ds(HALO + r0 - (k_taps - 1) + k, L), col:col + LANES]
                             * cw_ref[k:k + 1, wcol:wcol + LANES])
            return _silu(acc)

        for blk in range(D_SSD_XBC // LANES):
            act_ref[r0:r0 + L, blk * LANES:(blk + 1) * LANES] = conv_block(
                C_XBC + blk * LANES, xcw_ref, xcb_ref, blk * LANES, SSD_CONV)
        for blk in range(2 * D_MLSTM // LANES):
            act_ref[r0:r0 + L, D_SSD_XBC + blk * LANES:D_SSD_XBC + (blk + 1) * LANES] = conv_block(
                C_QK + blk * LANES, qcw_ref, qcb_ref, blk * LANES, MLSTM_CONV)

        gb = rest_ref[r0:r0 + L, R_G:R_G + LANES] + gbias_ref[...]
        sp_term = jnp.log1p(jnp.exp(-jnp.abs(gb)))
        dt = jnp.maximum(gb, 0.0) + sp_term
        log_f = jnp.minimum(gb, 0.0) - sp_term
        is_dt = lane < G_I
        is_f = (lane >= G_F) & (lane < G_F + MLSTM_HEADS)
        v_cum = jnp.where(is_dt, dt * a_row, jnp.where(is_f, log_f, 0.0))
        hi, mid, lo = _split3(v_cum)
        cs = _dot(tril, hi) + _dot(tril, mid) + _dot(tril, lo)
        u_gate = jnp.where(is_dt, dt, gb)
        cs_t = cs.T
        ug_t = u_gate.T
        cs_last = cs[L - 1:L, :]
        e_col = jnp.exp(cs)
        w_col = jnp.exp(cs_last - cs) * dt
        e_last = jnp.exp(cs_last)

        for g in range(SSD_GROUPS):
            b_g = act_ref[r0:r0 + L, D_SSD + g * SSD_STATE:D_SSD + (g + 1) * SSD_STATE].astype(BF16)
            c_g = act_ref[r0:r0 + L, D_SSD + (SSD_GROUPS + g) * SSD_STATE:
                          D_SSD + (SSD_GROUPS + g + 1) * SSD_STATE].astype(BF16)
            s_g = _dot_nt(c_g, b_g)
            state_g = sstate_ref[g]
            y_off = _dot(c_g, state_g.astype(BF16))
            xd_blocks = []
            cd_blocks = []
            for pr in range(2):
                h_even = 4 * g + 2 * pr
                col = h_even * SSD_HEAD_DIM
                xs = act_ref[r0:r0 + L, col:col + LANES]
                xs_b = xs.astype(BF16)
                yd = []
                for hh in range(2):
                    hd = h_even + hh
                    seg = jnp.where(causal, cs[:, hd:hd + 1] - cs_t[hd:hd + 1, :], neg_inf)
                    m_h = (s_g * (jnp.exp(seg) * ug_t[hd:hd + 1, :])).astype(BF16)
                    yd.append(_dot(m_h, xs_b))
                y_diag = jnp.where(lo_half, yd[0], yd[1])
                e_exp = _pair_expand(e_col, h_even, (L, LANES), lo_half)
                w_exp = _pair_expand(w_col, h_even, (L, LANES), lo_half)
                y = (y_diag + y_off[:, pr * LANES:(pr + 1) * LANES] * e_exp
                     + xs * dskip_ref[:, col:col + LANES])
                z = rest_ref[r0:r0 + L, R_Z + col:R_Z + col + LANES]
                mix_ref[r0:r0 + L, D_POOL + col:D_POOL + col + LANES] = y * _silu(z)
                xd_blocks.append((xs * w_exp).astype(BF16))
                cd_blocks.append(_pair_expand(e_last, h_even, (1, LANES), lane_row < 64))
            xd_g = jnp.concatenate(xd_blocks, axis=1)
            cd_g = jnp.concatenate(cd_blocks, axis=1)
            new_states = lax.dot_general(b_g, xd_g, (((0,), (0,)), ((), ())),
                                         preferred_element_type=F32)
            sstate_ref[g] = state_g * cd_g + new_states
        y_all = mix_ref[r0:r0 + L, D_POOL:D_POOL + D_SSD]
        ms_y = jnp.mean(y_all * y_all, axis=-1, keepdims=True)
        mix_ref[r0:r0 + L, D_POOL:D_POOL + D_SSD] = y_all * lax.rsqrt(ms_y + EPS) * snorm_ref[...]

        for pr in range(MLSTM_HEADS // 2):
            qcol = D_SSD_XBC + pr * LANES
            kcol = D_SSD_XBC + D_MLSTM + pr * LANES
            q_b = act_ref[r0:r0 + L, qcol:qcol + LANES] * (MLSTM_HEAD_DIM ** -0.5)
            k_t = act_ref[r0:r0 + L, kcol:kcol + LANES].T
            k_tb = k_t.astype(BF16)
            v_b = rest_ref[r0:r0 + L, R_V + pr * LANES:R_V + (pr + 1) * LANES]
            o_b = rest_ref[r0:r0 + L, R_O + pr * LANES:R_O + (pr + 1) * LANES]
            hv = []
            for hh in range(2):
                hd = 2 * pr + hh
                in_half = lo_half if hh == 0 else jnp.logical_not(lo_half)
                row_in_half = (row < 64) if hh == 0 else (row >= 64)
                ones_lane = (lane == 64) if hh == 0 else (lane == 0)
                ol = 64 if hh == 0 else 0
                qm = jnp.where(in_half, q_b, 0.0).astype(BF16)
                s = _dot(qm, k_tb)
                b_row = cs_t[G_F + hd:G_F + hd + 1, :]
                r_row = ug_t[G_I + hd:G_I + hd + 1, :] - b_row
                b_last = jnp.sum(jnp.where(lane_row == L - 1, b_row, 0.0), axis=-1, keepdims=True)
                al_row = b_last + r_row
                m_loc = jnp.max(al_row, axis=-1, keepdims=True)
                prev_m = mm_ref[hd:hd + 1, 0:1]
                rmask = jnp.where(causal, r_row, neg_inf)
                g_col = jnp.maximum(jnp.max(rmask, axis=-1, keepdims=True), prev_m)
                p = (s * jnp.exp(rmask - g_col)).astype(BF16)
                v_ext = jnp.where(in_half, v_b, jnp.where(ones_lane, 1.0, 0.0)).astype(BF16)
                cn = mstate_ref[hd]
                res = _dot(p, v_ext) + _dot(qm, cn.astype(BF16)) * jnp.exp(prev_m - g_col)
                den = jnp.maximum(jnp.abs(res[:, ol:ol + 1]),
                                  jnp.exp(-(cs[:, G_F + hd:G_F + hd + 1] + g_col)))
                hv.append(res / den)
                w_row = jnp.exp(al_row - m_loc)
                ktw = jnp.where(row_in_half, k_t * w_row, 0.0).astype(BF16)
                c_loc = _dot(ktw, v_ext)
                m_new = jnp.maximum(b_last + prev_m, m_loc)
                mstate_ref[hd] = (jnp.exp(b_last + prev_m - m_new) * cn
                                  + jnp.exp(m_loc - m_new) * c_loc)
                mm_ref[hd:hd + 1, :] = jnp.broadcast_to(m_new, (1, LANES))
            hcat = jax.nn.sigmoid(o_b) * jnp.where(lo_half, hv[0], hv[1])
            sq = hcat * hcat
            ss_lo = jnp.sum(jnp.where(lo_half, sq, 0.0), axis=-1, keepdims=True)
            ss_hi = jnp.sum(jnp.where(lo_half, 0.0, sq), axis=-1, keepdims=True)
            inv = jnp.where(lo_half, lax.rsqrt(ss_lo * (1.0 / MLSTM_HEAD_DIM) + EPS),
                            lax.rsqrt(ss_hi * (1.0 / MLSTM_HEAD_DIM) + EPS))
            mcol = D_POOL + D_SSD + pr * LANES
            mix_ref[r0:r0 + L, mcol:mcol + LANES] = hcat * inv * mnorm_ref[:, pr * LANES:(pr + 1) * LANES]

    n2 = TT + HALO - 8
    ps_ref[0, 8:8 + n2, :] = hp_ref[pl.ds(8, n2), 0:D_POOL] + hp_ref[pl.ds(7, n2), 0:D_POOL]
    n4 = TT + HALO - 16
    ps_ref[1, 16:16 + n4, :] = ps_ref[0, pl.ds(16, n4), :] + ps_ref[0, pl.ds(14, n4), :]
    n8 = TT + HALO - 24
    ps_ref[2, 24:24 + n8, :] = ps_ref[1, pl.ds(24, n8), :] + ps_ref[1, pl.ds(20, n8), :]
    s2 = ps_ref[0, HALO:HALO + TT, :]
    s4 = ps_ref[1, HALO:HALO + TT, :]
    s8 = ps_ref[2, HALO:HALO + TT, :]
    s16 = s8 + ps_ref[2, pl.ds(HALO - 8, TT), :]
    lane_p = lax.broadcasted_iota(jnp.int32, (TT, D_POOL), 1)
    pos = (lax.broadcasted_iota(jnp.int32, (TT, D_POOL), 0) + (i * TT + 1)).astype(F32)
    g0 = lane_p < POOL_GROUP_DIM
    g1 = lane_p < 2 * POOL_GROUP_DIM
    g2 = lane_p < 3 * POOL_GROUP_DIM
    wsum = jnp.where(g0, s2, jnp.where(g1, s4, jnp.where(g2, s8, s16)))
    win = jnp.where(g0, float(POOL_WINDOWS[0]), jnp.where(g1, float(POOL_WINDOWS[1]),
                    jnp.where(g2, float(POOL_WINDOWS[2]), float(POOL_WINDOWS[3]))))
    pooled = wsum / jnp.minimum(pos, win) - hp_ref[HALO:HALO + TT, 0:D_POOL]
    y_pool = (_dot(pooled.astype(BF16), poolw_ref[...]) + poolb_ref[...]) * pools_ref[...]
    mix_ref[:, 0:D_POOL] = y_pool

    o = _dot(mix_ref[...].astype(BF16), wout_ref[...])
    ms_o = jnp.mean(o * o, axis=-1, keepdims=True)
    out_ref[0] = x_ref[0] + o * lax.rsqrt(ms_o + EPS) * postn_ref[...]


def ffn_kernel(x_ref, nw_ref, wup_ref, cw_ref, cb_ref, wdn_ref, postn_ref, out_ref,
               u_ref, halo_ref, a_ref):
    TT = x_ref.shape[1]
    FT = FFN_FT
    i = pl.program_id(1)

    @pl.when(i == 0)
    def _():
        halo_ref[...] = jnp.zeros(halo_ref.shape, F32)

    x = x_ref[0]
    ms = jnp.mean(x * x, axis=-1, keepdims=True)
    h = (x * lax.rsqrt(ms + EPS) * nw_ref[...]).astype(BF16)

    def conv(slot, col):
        acc = cb_ref[:, col:col + FT]
        for k in range(FFN_CONV):
            acc = acc + (u_ref[slot, pl.ds(SUBLANES - (FFN_CONV - 1) + k, TT), :]
                         * cw_ref[k:k + 1, col:col + FT])
        return acc

    def load_u(slot, col):
        u_ref[slot, 0:SUBLANES, :] = halo_ref[:, col:col + FT]
        u_ref[slot, SUBLANES:SUBLANES + TT, :] = _dot(h, wup_ref[:, col:col + FT])
        halo_ref[:, col:col + FT] = u_ref[slot, TT:TT + SUBLANES, :]

    for j in range(D_FF // FT):
        sg = 2 * (j % 2)
        sv = sg + 1
        cg = j * FT
        cv = D_FF + j * FT
        load_u(sg, cg)
        load_u(sv, cv)
        gt = conv(sg, cg)
        val = conv(sv, cv)
        gelu = 0.5 * gt * (1.0 + jnp.tanh(math.sqrt(2.0 / math.pi) * (gt + 0.044715 * (gt * gt * gt))))
        a_ref[:, cg:cg + FT] = (gelu * val).astype(BF16)

    f = _dot(a_ref[...], wdn_ref[...])
    ms_f = jnp.mean(f * f, axis=-1, keepdims=True)
    out_ref[0] = x_ref[0] + f * lax.rsqrt(ms_f + EPS) * postn_ref[...]


def _const_spec(shape):
    nd = len(shape)
    return pl.BlockSpec(shape, lambda b, i: (0,) * nd, pipeline_mode=pl.Buffered(1))


def _mix_layer(x, nw, win, gbias, alog, xcw, xcb, qcw, qcb, poolw, poolb, pools, dskip, snorm, mnorm,
               wout, postn):
    B, T, D = x.shape
    TT = MIX_TT
    consts = (nw, win, gbias, alog, xcw, xcb, qcw, qcb, poolw, poolb, pools, dskip, snorm, mnorm, wout, postn)
    x_spec = pl.BlockSpec((1, TT, D), lambda b, i: (b, i, 0))
    return pl.pallas_call(
        mix_kernel,
        grid=(B, T // TT),
        in_specs=[x_spec] + [_const_spec(c.shape) for c in consts],
        out_specs=x_spec,
        out_shape=jax.ShapeDtypeStruct(x.shape, x.dtype),
        scratch_shapes=[
            pltpu.VMEM((HALO + TT, N_HALO_COLS), F32),
            pltpu.VMEM((TT, N_REST_COLS), F32),
            pltpu.VMEM((TT, D_SSD_XBC + 2 * D_MLSTM), F32),
            pltpu.VMEM((TT, D_MODEL), F32),
            pltpu.VMEM((3, HALO + TT, D_POOL), F32),
            pltpu.VMEM((SSD_GROUPS, SSD_STATE, 4 * SSD_HEAD_DIM), F32),
            pltpu.VMEM((MLSTM_HEADS, LANES, LANES), F32),
            pltpu.VMEM((SUBLANES, LANES), F32),
        ],
        compiler_params=pltpu.CompilerParams(
            dimension_semantics=("arbitrary", "arbitrary"), vmem_limit_bytes=VMEM_LIMIT),
        name="mix_layer",
    )(x, *consts)


def _ffn_layer(x, nw, wup, cw, cb, wdn, postn):
    B, T, D = x.shape
    TT = FFN_TT
    consts = (nw, wup, cw, cb, wdn, postn)
    x_spec = pl.BlockSpec((1, TT, D), lambda b, i: (b, i, 0))
    return pl.pallas_call(
        ffn_kernel,
        grid=(B, T // TT),
        in_specs=[x_spec] + [_const_spec(c.shape) for c in consts],
        out_specs=x_spec,
        out_shape=jax.ShapeDtypeStruct(x.shape, x.dtype),
        scratch_shapes=[
            pltpu.VMEM((4, SUBLANES + TT, FFN_FT), F32),
            pltpu.VMEM((SUBLANES, 2 * D_FF), F32),
            pltpu.VMEM((TT, D_FF), BF16),
        ],
        compiler_params=pltpu.CompilerParams(
            dimension_semantics=("arbitrary", "arbitrary"), vmem_limit_bytes=VMEM_LIMIT),
        name="ffn_layer",
    )(x, *consts)


def _row(v):
    return v.reshape(1, -1).astype(F32)


def _pad_lanes(v):
    return jnp.pad(v.astype(F32), (0, LANES - v.shape[0])).reshape(1, LANES)


def _prep_w_in(w):
    sizes = (D_POOL, D_SSD, D_SSD_XBC, SSD_HEADS, 2 * D_MLSTM, D_MLSTM, D_MLSTM, MLSTM_HEADS, MLSTM_HEADS)
    offs = [0]
    for s in sizes:
        offs.append(offs[-1] + s)
    u_pool, z, xbc, dt, qk, v, o, ig, fg = [w[:, offs[k]:offs[k + 1]] for k in range(len(sizes))]
    pad = jnp.zeros((w.shape[0], LANES - SSD_HEADS - 2 * MLSTM_HEADS), w.dtype)
    return jnp.concatenate([u_pool, xbc, qk, z, v, o, dt, ig, fg, pad], axis=1).astype(BF16)


def _prep_pool_w(w):
    out = jnp.zeros((D_POOL, D_POOL), F32)
    for g in range(len(POOL_WINDOWS)):
        s = g * POOL_GROUP_DIM
        out = lax.dynamic_update_slice(out, w[g].astype(F32), (s, s))
    return out.astype(BF16)


def kernel(x, pre_mix_norm, w_in, pool_w, pool_b, pool_scale, ssd_conv_w, ssd_conv_b, ssd_dt_bias, ssd_a_log, ssd_d, ssd_norm, mlstm_conv_w, mlstm_conv_b, mlstm_i_bias, mlstm_f_bias, mlstm_norm, w_out, post_mix_norm, pre_ffn_norm, ffn_w_up, ffn_conv_w, ffn_conv_b, ffn_w_down, post_ffn_norm):
    depth = w_in.shape[0]
    for l in range(depth):
        gbias = _pad_lanes(jnp.concatenate([ssd_dt_bias[l], mlstm_i_bias[l], mlstm_f_bias[l]]))
        x = _mix_layer(
            x, _row(pre_mix_norm[l]), _prep_w_in(w_in[l]), gbias, _pad_lanes(ssd_a_log[l]),
            ssd_conv_w[l].astype(F32), _row(ssd_conv_b[l]), mlstm_conv_w[l].astype(F32), _row(mlstm_conv_b[l]),
            _prep_pool_w(pool_w[l]), _row(pool_b[l]), _row(pool_scale[l]),
            _row(jnp.repeat(ssd_d[l], SSD_HEAD_DIM)), _row(ssd_norm[l]), _row(mlstm_norm[l]),
            w_out[l].astype(BF16), _row(post_mix_norm[l]))
        x = _ffn_layer(
            x, _row(pre_ffn_norm[l]), ffn_w_up[l].astype(BF16), ffn_conv_w[l].astype(F32),
            _row(ffn_conv_b[l]), ffn_w_down[l].astype(BF16), _row(post_ffn_norm[l]))
    return x
```

```python
import math

import jax
import jax.numpy as jnp
from jax import lax
from jax.experimental import pallas as pl
from jax.experimental.pallas import tpu as pltpu

F32 = jnp.float32
BF16 = jnp.bfloat16

D_MODEL = 1024
EPS = 1e-6

D_POOL = 256
POOL_GROUP_DIM = 64
POOL_WINDOWS = (2, 4, 8, 16)

D_SSD = 512
SSD_HEADS = 8
SSD_HEAD_DIM = 64
SSD_GROUPS = 2
SSD_STATE = 128
SSD_CONV = 4
D_SSD_XBC = D_SSD + 2 * SSD_GROUPS * SSD_STATE

D_MLSTM = 256
MLSTM_HEADS = 4
MLSTM_HEAD_DIM = 64
MLSTM_CONV = 4

D_FF = 2816
FFN_CONV = 3

CHUNK = 128
LANES = 128
SUBLANES = 8
VROWS = CHUNK // SUBLANES

C_POOL = 0
C_XBC = C_POOL + D_POOL
C_QK = C_XBC + D_SSD_XBC
N_HALO_COLS = C_QK + 2 * D_MLSTM
R_Z = 0
R_V = R_Z + D_SSD
R_O = R_V + D_MLSTM
R_G = R_O + D_MLSTM
N_REST_COLS = R_G + LANES
N_IN_COLS = N_HALO_COLS + N_REST_COLS
G_DT = 0
G_I = SSD_HEADS
G_F = G_I + MLSTM_HEADS

MIX_TT = 256
FFN_TT = 512
FFN_FT = 256
VMEM_LIMIT = 56 * 1024 * 1024


def _dot(a, b):
    return jnp.dot(a, b, preferred_element_type=F32)


def _dot_nt(a, b):
    return lax.dot_general(a, b, (((1,), (1,)), ((), ())), preferred_element_type=F32)


def _silu(x):
    return x * jax.nn.sigmoid(x)


def _split3(a):
    hi = a.astype(BF16)
    r = a - hi.astype(F32)
    mid = r.astype(BF16)
    lo = (r - mid.astype(F32)).astype(BF16)
    return hi, mid, lo


def _bcast_lane(a, j, shape):
    return jnp.broadcast_to(a[:, j:j + 1], shape)


def _pair_expand(a, h_even, shape, lo_half):
    return jnp.where(lo_half, _bcast_lane(a, h_even, shape), _bcast_lane(a, h_even + 1, shape))


def _tau(p):
    return (p % SUBLANES) * VROWS + p // SUBLANES


def _ext_rows(prev_tail, cur_tail):
    n = cur_tail.shape[0] // SUBLANES
    sub0 = lax.broadcasted_iota(jnp.int32, (SUBLANES, cur_tail.shape[1]), 0) == 0
    out = []
    for j in range(n):
        sl = slice(j * SUBLANES, (j + 1) * SUBLANES)
        out.append(jnp.where(sub0, pltpu.roll(prev_tail[sl], 1, 0), pltpu.roll(cur_tail[sl], 1, 0)))
    return out


def _shifted(ext, cur, k):
    if k == 0:
        return cur
    return jnp.concatenate(ext[len(ext) - k:] + [cur[0:CHUNK - SUBLANES * k]], axis=0)


def _causal_conv(prev_tail, cur, w_ref, b_row, wcol, ncols, k_taps):
    n = k_taps - 1
    ext = _ext_rows(prev_tail, cur[CHUNK - n * SUBLANES:CHUNK])
    acc = b_row
    for k in range(k_taps):
        acc = acc + _shifted(ext, cur, n - k) * w_ref[k:k + 1, wcol:wcol + ncols]
    return acc


def mix_kernel(x_ref, nw_ref, win_ref, gbias_ref, alog_ref, xcw_ref, xcb_ref, qcw_ref, qcb_ref,
               poolw_ref, poolb_ref, pools_ref, dskip_ref, snorm_ref, mnorm_ref, wout_ref, postn_ref,
               out_ref, hp_ref, rest_ref, act_ref, mix_ref, ps_ref, sstate_ref, mstate_ref, mm_ref):
    TT = x_ref.shape[1]
    L = CHUNK
    i = pl.program_id(1)

    ps_carried = ((0, 0), (0, 1), (1, 1), (2, 1))

    @pl.when(i == 0)
    def _():
        hp_ref[0:L, :] = jnp.zeros((L, N_HALO_COLS), F32)
        for lv, b in ps_carried:
            ps_ref[lv, 0:L, b * LANES:(b + 1) * LANES] = jnp.zeros((L, LANES), F32)
        sstate_ref[...] = jnp.zeros(sstate_ref.shape, F32)
        mstate_ref[...] = jnp.zeros(mstate_ref.shape, F32)
        mm_ref[...] = jnp.zeros(mm_ref.shape, F32)

    @pl.when(i > 0)
    def _():
        hp_ref[0:L, :] = hp_ref[TT:TT + L, :]
        for lv, b in ps_carried:
            ps_ref[lv, 0:L, b * LANES:(b + 1) * LANES] = ps_ref[lv, TT:TT + L, b * LANES:(b + 1) * LANES]

    x = x_ref[0]
    ms = jnp.mean(x * x, axis=-1, keepdims=True)
    h = (x * lax.rsqrt(ms + EPS) * nw_ref[...]).astype(BF16)
    hp_ref[L:L + TT, :] = _dot(h, win_ref[:, 0:N_HALO_COLS])
    rest_ref[...] = _dot(h, win_ref[:, N_HALO_COLS:N_IN_COLS])

    lane = lax.broadcasted_iota(jnp.int32, (L, LANES), 1)
    row = lax.broadcasted_iota(jnp.int32, (L, LANES), 0)
    lo_half = lane < 64
    tau_row = _tau(row)
    causal = _tau(lane) <= tau_row
    tril = jnp.where(causal, 1.0, 0.0).astype(BF16)
    lane_row = lax.broadcasted_iota(jnp.int32, (1, LANES), 1)
    a_row = -jnp.exp(alog_ref[...])
    neg_inf = -jnp.inf
    win_blk = [jnp.where(lo_half, float(POOL_WINDOWS[2 * b]), float(POOL_WINDOWS[2 * b + 1])) for b in range(2)]
    tau_f = tau_row.astype(F32)

    for c in range(TT // L):
        r0 = c * L
        base = L + r0

        def conv_block(col, cw_ref, cb_ref, wcol, k_taps):
            n = k_taps - 1
            cur = hp_ref[base:base + L, col:col + LANES]
            prev_tail = hp_ref[base - n * SUBLANES:base, col:col + LANES]
            return _silu(_causal_conv(prev_tail, cur, cw_ref, cb_ref[:, wcol:wcol + LANES], wcol, LANES, k_taps))

        for blk in range(D_SSD_XBC // LANES):
            act_ref[r0:r0 + L, blk * LANES:(blk + 1) * LANES] = conv_block(
                C_XBC + blk * LANES, xcw_ref, xcb_ref, blk * LANES, SSD_CONV)
        for blk in range(2 * D_MLSTM // LANES):
            act_ref[r0:r0 + L, D_SSD_XBC + blk * LANES:D_SSD_XBC + (blk + 1) * LANES] = conv_block(
                C_QK + blk * LANES, qcw_ref, qcb_ref, blk * LANES, MLSTM_CONV)

        pos = tau_f + (i * TT + r0 + 1).astype(F32)
        for b in range(2):
            cs_ = slice(b * LANES, (b + 1) * LANES)
            u_cur = hp_ref[base:base + L, cs_]
            lvl = u_cur
            sums = []
            for li, sh in enumerate((1, 2, 4, 8)):
                if li == 0:
                    prev_tail = hp_ref[base - sh * SUBLANES:base, cs_]
                else:
                    prev_tail = ps_ref[li - 1, base - sh * SUBLANES:base, cs_]
                ext = _ext_rows(prev_tail, lvl[L - sh * SUBLANES:L])
                lvl = lvl + _shifted(ext, lvl, sh)
                sums.append(lvl)
                if (li, b) in ps_carried:
                    ps_ref[li, base:base + L, cs_] = lvl
                if b == 0 and li == 1:
                    break
            wsum = jnp.where(lo_half, sums[0], sums[1]) if b == 0 else jnp.where(lo_half, sums[2], sums[3])
            pooled = wsum / jnp.minimum(pos, win_blk[b]) - u_cur
            mix_ref[r0:r0 + L, cs_] = pooled

        gb = rest_ref[r0:r0 + L, R_G:R_G + LANES] + gbias_ref[...]
        sp_term = jnp.log1p(jnp.exp(-jnp.abs(gb)))
        dt = jnp.maximum(gb, 0.0) + sp_term
        log_f = jnp.minimum(gb, 0.0) - sp_term
        is_dt = lane < G_I
        is_f = (lane >= G_F) & (lane < G_F + MLSTM_HEADS)
        v_cum = jnp.where(is_dt, dt * a_row, jnp.where(is_f, log_f, 0.0))
        hi, mid, lo = _split3(v_cum)
        cs = _dot(tril, hi) + _dot(tril, mid) + _dot(tril, lo)
        u_gate = jnp.where(is_dt, dt, gb)
        cs_t = cs.T
        ug_t = u_gate.T
        cs_last = cs[L - 1:L, :]
        e_col = jnp.exp(cs)
        w_col = jnp.exp(cs_last - cs) * dt
        e_last = jnp.exp(cs_last)

        for g in range(SSD_GROUPS):
            b_g = act_ref[r0:r0 + L, D_SSD + g * SSD_STATE:D_SSD + (g + 1) * SSD_STATE].astype(BF16)
            c_g = act_ref[r0:r0 + L, D_SSD + (SSD_GROUPS + g) * SSD_STATE:
                          D_SSD + (SSD_GROUPS + g + 1) * SSD_STATE].astype(BF16)
            s_g = _dot_nt(c_g, b_g)
            state_g = sstate_ref[g]
            y_off = _dot(c_g, state_g.astype(BF16))
            xd_blocks = []
            cd_blocks = []
            for pr in range(2):
                h_even = 4 * g + 2 * pr
                col = h_even * SSD_HEAD_DIM
                xs = act_ref[r0:r0 + L, col:col + LANES]
                xs_b = xs.astype(BF16)
                yd = []
                for hh in range(2):
                    hd = h_even + hh
                    seg = jnp.where(causal, cs[:, hd:hd + 1] - cs_t[hd:hd + 1, :], neg_inf)
                    m_h = (s_g * (jnp.exp(seg) * ug_t[hd:hd + 1, :])).astype(BF16)
                    yd.append(_dot(m_h, xs_b))
                y_diag = jnp.where(lo_half, yd[0], yd[1])
                e_exp = _pair_expand(e_col, h_even, (L, LANES), lo_half)
                w_exp = _pair_expand(w_col, h_even, (L, LANES), lo_half)
                y = (y_diag + y_off[:, pr * LANES:(pr + 1) * LANES] * e_exp
                     + xs * dskip_ref[:, col:col + LANES])
                z = rest_ref[r0:r0 + L, R_Z + col:R_Z + col + LANES]
                mix_ref[r0:r0 + L, D_POOL + col:D_POOL + col + LANES] = y * _silu(z)
                xd_blocks.append((xs * w_exp).astype(BF16))
                cd_blocks.append(_pair_expand(e_last, h_even, (1, LANES), lane_row < 64))
            xd_g = jnp.concatenate(xd_blocks, axis=1)
            cd_g = jnp.concatenate(cd_blocks, axis=1)
            new_states = lax.dot_general(b_g, xd_g, (((0,), (0,)), ((), ())),
                                         preferred_element_type=F32)
            sstate_ref[g] = state_g * cd_g + new_states
        y_all = mix_ref[r0:r0 + L, D_POOL:D_POOL + D_SSD]
        ms_y = jnp.mean(y_all * y_all, axis=-1, keepdims=True)
        mix_ref[r0:r0 + L, D_POOL:D_POOL + D_SSD] = y_all * lax.rsqrt(ms_y + EPS) * snorm_ref[...]

        for pr in range(MLSTM_HEADS // 2):
            qcol = D_SSD_XBC + pr * LANES
            kcol = D_SSD_XBC + D_MLSTM + pr * LANES
            q_b = act_ref[r0:r0 + L, qcol:qcol + LANES] * (MLSTM_HEAD_DIM ** -0.5)
            k_t = act_ref[r0:r0 + L, kcol:kcol + LANES].T
            k_tb = k_t.astype(BF16)
            v_b = rest_ref[r0:r0 + L, R_V + pr * LANES:R_V + (pr + 1) * LANES]
            o_b = rest_ref[r0:r0 + L, R_O + pr * LANES:R_O + (pr + 1) * LANES]
            hv = []
            for hh in range(2):
                hd = 2 * pr + hh
                in_half = lo_half if hh == 0 else jnp.logical_not(lo_half)
                row_in_half = (row < 64) if hh == 0 else (row >= 64)
                ones_lane = (lane == 64) if hh == 0 else (lane == 0)
                ol = 64 if hh == 0 else 0
                qm = jnp.where(in_half, q_b, 0.0).astype(BF16)
                s = _dot(qm, k_tb)
                b_row = cs_t[G_F + hd:G_F + hd + 1, :]
                r_row = ug_t[G_I + hd:G_I + hd + 1, :] - b_row
                b_last = jnp.sum(jnp.where(lane_row == L - 1, b_row, 0.0), axis=-1, keepdims=True)
                al_row = b_last + r_row
                m_loc = jnp.max(al_row, axis=-1, keepdims=True)
                prev_m = mm_ref[hd:hd + 1, 0:1]
                rmask = jnp.where(causal, r_row, neg_inf)
                g_col = jnp.maximum(jnp.max(rmask, axis=-1, keepdims=True), prev_m)
                p = (s * jnp.exp(rmask - g_col)).astype(BF16)
                v_ext = jnp.where(in_half, v_b, jnp.where(ones_lane, 1.0, 0.0)).astype(BF16)
                cn = mstate_ref[hd]
                res = _dot(p, v_ext) + _dot(qm, cn.astype(BF16)) * jnp.exp(prev_m - g_col)
                den = jnp.maximum(jnp.abs(res[:, ol:ol + 1]),
                                  jnp.exp(-(cs[:, G_F + hd:G_F + hd + 1] + g_col)))
                hv.append(res / den)
                w_row = jnp.exp(al_row - m_loc)
                ktw = jnp.where(row_in_half, k_t * w_row, 0.0).astype(BF16)
                c_loc = _dot(ktw, v_ext)
                m_new = jnp.maximum(b_last + prev_m, m_loc)
                mstate_ref[hd] = (jnp.exp(b_last + prev_m - m_new) * cn
                                  + jnp.exp(m_loc - m_new) * c_loc)
                mm_ref[hd:hd + 1, :] = jnp.broadcast_to(m_new, (1, LANES))
            hcat = jax.nn.sigmoid(o_b) * jnp.where(lo_half, hv[0], hv[1])
            sq = hcat * hcat
            ss_lo = jnp.sum(jnp.where(lo_half, sq, 0.0), axis=-1, keepdims=True)
            ss_hi = jnp.sum(jnp.where(lo_half, 0.0, sq), axis=-1, keepdims=True)
            inv = jnp.where(lo_half, lax.rsqrt(ss_lo * (1.0 / MLSTM_HEAD_DIM) + EPS),
                            lax.rsqrt(ss_hi * (1.0 / MLSTM_HEAD_DIM) + EPS))
            mcol = D_POOL + D_SSD + pr * LANES
            mix_ref[r0:r0 + L, mcol:mcol + LANES] = hcat * inv * mnorm_ref[:, pr * LANES:(pr + 1) * LANES]

    y_pool = (_dot(mix_ref[:, 0:D_POOL].astype(BF16), poolw_ref[...]) + poolb_ref[...]) * pools_ref[...]
    mix_ref[:, 0:D_POOL] = y_pool

    o = _dot(mix_ref[...].astype(BF16), wout_ref[...])
    ms_o = jnp.mean(o * o, axis=-1, keepdims=True)
    out_ref[0] = x_ref[0] + o * lax.rsqrt(ms_o + EPS) * postn_ref[...]


def ffn_kernel(x_ref, nw_ref, wup_ref, cw_ref, cb_ref, wdn_ref, postn_ref, out_ref, halo_ref, a_ref):
    TT = x_ref.shape[1]
    FT = FFN_FT
    L = CHUNK
    n_tail = (FFN_CONV - 1) * SUBLANES
    i = pl.program_id(1)

    @pl.when(i == 0)
    def _():
        halo_ref[...] = jnp.zeros(halo_ref.shape, F32)

    x = x_ref[0]
    ms = jnp.mean(x * x, axis=-1, keepdims=True)
    h = (x * lax.rsqrt(ms + EPS) * nw_ref[...]).astype(BF16)

    def conv_cols(col):
        u = _dot(h, wup_ref[:, col:col + FT])
        outs = []
        for c in range(TT // L):
            cur = u[c * L:(c + 1) * L]
            prev_tail = halo_ref[:, col:col + FT] if c == 0 else u[c * L - n_tail:c * L]
            outs.append(_causal_conv(prev_tail, cur, cw_ref, cb_ref[:, col:col + FT], col, FT, FFN_CONV))
        halo_ref[:, col:col + FT] = u[TT - n_tail:TT]
        return outs

    for j in range(D_FF // FT):
        gts = conv_cols(j * FT)
        vals = conv_cols(D_FF + j * FT)
        for c in range(TT // L):
            gt = gts[c]
            gelu = 0.5 * gt * (1.0 + jnp.tanh(math.sqrt(2.0 / math.pi) * (gt + 0.044715 * (gt * gt * gt))))
            a_ref[c * L:(c + 1) * L, j * FT:(j + 1) * FT] = (gelu * vals[c]).astype(BF16)

    f = _dot(a_ref[...], wdn_ref[...])
    ms_f = jnp.mean(f * f, axis=-1, keepdims=True)
    out_ref[0] = x_ref[0] + f * lax.rsqrt(ms_f + EPS) * postn_ref[...]


def _const_spec(shape):
    nd = len(shape)
    return pl.BlockSpec(shape, lambda b, i: (0,) * nd, pipeline_mode=pl.Buffered(1))


def _mix_layer(x, nw, win, gbias, alog, xcw, xcb, qcw, qcb, poolw, poolb, pools, dskip, snorm, mnorm,
               wout, postn):
    B, T, D = x.shape
    TT = MIX_TT
    consts = (nw, win, gbias, alog, xcw, xcb, qcw, qcb, poolw, poolb, pools, dskip, snorm, mnorm, wout, postn)
    x_spec = pl.BlockSpec((1, TT, D), lambda b, i: (b, i, 0))
    return pl.pallas_call(
        mix_kernel,
        grid=(B, T // TT),
        in_specs=[x_spec] + [_const_spec(c.shape) for c in consts],
        out_specs=x_spec,
        out_shape=jax.ShapeDtypeStruct(x.shape, x.dtype),
        scratch_shapes=[
            pltpu.VMEM((CHUNK + TT, N_HALO_COLS), F32),
            pltpu.VMEM((TT, N_REST_COLS), F32),
            pltpu.VMEM((TT, D_SSD_XBC + 2 * D_MLSTM), F32),
            pltpu.VMEM((TT, D_MODEL), F32),
            pltpu.VMEM((3, CHUNK + TT, D_POOL), F32),
            pltpu.VMEM((SSD_GROUPS, SSD_STATE, 4 * SSD_HEAD_DIM), F32),
            pltpu.VMEM((MLSTM_HEADS, LANES, LANES), F32),
            pltpu.VMEM((SUBLANES, LANES), F32),
        ],
        compiler_params=pltpu.CompilerParams(
            dimension_semantics=("arbitrary", "arbitrary"), vmem_limit_bytes=VMEM_LIMIT),
        name="mix_layer",
    )(x, *consts)


def _ffn_layer(x, nw, wup, cw, cb, wdn, postn):
    B, T, D = x.shape
    TT = FFN_TT
    consts = (nw, wup, cw, cb, wdn, postn)
    x_spec = pl.BlockSpec((1, TT, D), lambda b, i: (b, i, 0))
    return pl.pallas_call(
        ffn_kernel,
        grid=(B, T // TT),
        in_specs=[x_spec] + [_const_spec(c.shape) for c in consts],
        out_specs=x_spec,
        out_shape=jax.ShapeDtypeStruct(x.shape, x.dtype),
        scratch_shapes=[
            pltpu.VMEM(((FFN_CONV - 1) * SUBLANES, 2 * D_FF), F32),
            pltpu.VMEM((TT, D_FF), BF16),
        ],
        compiler_params=pltpu.CompilerParams(
            dimension_semantics=("arbitrary", "arbitrary"), vmem_limit_bytes=VMEM_LIMIT),
        name="ffn_layer",
    )(x, *consts)


def _row(v):
    return v.reshape(1, -1).astype(F32)


def _pad_lanes(v):
    return jnp.pad(v.astype(F32), (0, LANES - v.shape[0])).reshape(1, LANES)


def _prep_w_in(w):
    sizes = (D_POOL, D_SSD, D_SSD_XBC, SSD_HEADS, 2 * D_MLSTM, D_MLSTM, D_MLSTM, MLSTM_HEADS, MLSTM_HEADS)
    offs = [0]
    for s in sizes:
        offs.append(offs[-1] + s)
    u_pool, z, xbc, dt, qk, v, o, ig, fg = [w[:, offs[k]:offs[k + 1]] for k in range(len(sizes))]
    pad = jnp.zeros((w.shape[0], LANES - SSD_HEADS - 2 * MLSTM_HEADS), w.dtype)
    return jnp.concatenate([u_pool, xbc, qk, z, v, o, dt, ig, fg, pad], axis=1).astype(BF16)


def _prep_pool_w(w):
    out = jnp.zeros((D_POOL, D_POOL), F32)
    for g in range(len(POOL_WINDOWS)):
        s = g * POOL_GROUP_DIM
        out = lax.dynamic_update_slice(out, w[g].astype(F32), (s, s))
    return out.astype(BF16)


def _permute_chunks(x, to_kernel_order):
    B, T, D = x.shape
    a, b = (SUBLANES, VROWS) if to_kernel_order else (VROWS, SUBLANES)
    return x.reshape(B, T // CHUNK, a, b, D).transpose(0, 1, 3, 2, 4).reshape(B, T, D)


def kernel(x, pre_mix_norm, w_in, pool_w, pool_b, pool_scale, ssd_conv_w, ssd_conv_b, ssd_dt_bias, ssd_a_log, ssd_d, ssd_norm, mlstm_conv_w, mlstm_conv_b, mlstm_i_bias, mlstm_f_bias, mlstm_norm, w_out, post_mix_norm, pre_ffn_norm, ffn_w_up, ffn_conv_w, ffn_conv_b, ffn_w_down, post_ffn_norm):
    depth = w_in.shape[0]
    x = _permute_chunks(x, True)
    for l in range(depth):
        gbias = _pad_lanes(jnp.concatenate([ssd_dt_bias[l], mlstm_i_bias[l], mlstm_f_bias[l]]))
        x = _mix_layer(
            x, _row(pre_mix_norm[l]), _prep_w_in(w_in[l]), gbias, _pad_lanes(ssd_a_log[l]),
            ssd_conv_w[l].astype(F32), _row(ssd_conv_b[l]), mlstm_conv_w[l].astype(F32), _row(mlstm_conv_b[l]),
            _prep_pool_w(pool_w[l]), _row(pool_b[l]), _row(pool_scale[l]),
            _row(jnp.repeat(ssd_d[l], SSD_HEAD_DIM)), _row(ssd_norm[l]), _row(mlstm_norm[l]),
            w_out[l].astype(BF16), _row(post_mix_norm[l]))
        x = _ffn_layer(
            x, _row(pre_ffn_norm[l]), ffn_w_up[l].astype(BF16), ffn_conv_w[l].astype(F32),
            _row(ffn_conv_b[l]), ffn_w_down[l].astype(BF16), _row(post_ffn_norm[l]))
    return _permute_chunks(x, False)
```

```python
import math

import jax
import jax.numpy as jnp
from jax import lax
from jax.experimental import pallas as pl
from jax.experimental.pallas import tpu as pltpu

F32 = jnp.float32
BF16 = jnp.bfloat16

D_MODEL = 1024
EPS = 1e-6

D_POOL = 256
POOL_GROUP_DIM = 64
POOL_WINDOWS = (2, 4, 8, 16)

D_SSD = 512
SSD_HEADS = 8
SSD_HEAD_DIM = 64
SSD_GROUPS = 2
SSD_STATE = 128
SSD_CONV = 4
D_SSD_XBC = D_SSD + 2 * SSD_GROUPS * SSD_STATE

D_MLSTM = 256
MLSTM_HEADS = 4
MLSTM_HEAD_DIM = 64
MLSTM_CONV = 4

D_FF = 2816
FFN_CONV = 3

CHUNK = 128
LANES = 128
SUBLANES = 8
VROWS = CHUNK // SUBLANES

C_POOL = 0
C_XBC = C_POOL + D_POOL
C_QK = C_XBC + D_SSD_XBC
N_HALO_COLS = C_QK + 2 * D_MLSTM
R_Z = 0
R_V = R_Z + D_SSD
R_O = R_V + D_MLSTM
R_G = R_O + D_MLSTM
N_REST_COLS = R_G + LANES
N_IN_COLS = N_HALO_COLS + N_REST_COLS
G_DT = 0
G_I = SSD_HEADS
G_F = G_I + MLSTM_HEADS

MIX_TT = 256
PROJ_COLS = 256
FFN_TT = 512
FFN_FT = 256
VMEM_LIMIT = 56 * 1024 * 1024


def _dot(a, b):
    return jnp.dot(a, b, preferred_element_type=F32)


def _dot_nt(a, b):
    return lax.dot_general(a, b, (((1,), (1,)), ((), ())), preferred_element_type=F32)


def _silu(x):
    return x * jax.nn.sigmoid(x)


def _split3(a):
    hi = a.astype(BF16)
    r = a - hi.astype(F32)
    mid = r.astype(BF16)
    lo = (r - mid.astype(F32)).astype(BF16)
    return hi, mid, lo


def _bcast_lane(a, j, shape):
    return jnp.broadcast_to(a[:, j:j + 1], shape)


def _pair_expand(a, h_even, shape, lo_half):
    return jnp.where(lo_half, _bcast_lane(a, h_even, shape), _bcast_lane(a, h_even + 1, shape))


def _tau(p):
    return (p % SUBLANES) * VROWS + p // SUBLANES


def _ext_rows(prev_tail, cur_tail):
    n = cur_tail.shape[0] // SUBLANES
    sub0 = lax.broadcasted_iota(jnp.int32, (SUBLANES, cur_tail.shape[1]), 0) == 0
    out = []
    for j in range(n):
        sl = slice(j * SUBLANES, (j + 1) * SUBLANES)
        out.append(jnp.where(sub0, pltpu.roll(prev_tail[sl], 1, 0), pltpu.roll(cur_tail[sl], 1, 0)))
    return out


def _shifted(ext, cur, k):
    if k == 0:
        return cur
    return jnp.concatenate(ext[len(ext) - k:] + [cur[0:CHUNK - SUBLANES * k]], axis=0)


def _causal_conv(prev_tail, cur, w_ref, b_row, wcol, ncols, k_taps):
    n = k_taps - 1
    ext = _ext_rows(prev_tail, cur[CHUNK - n * SUBLANES:CHUNK])
    acc = b_row
    for k in range(k_taps):
        acc = acc + _shifted(ext, cur, n - k) * w_ref[k:k + 1, wcol:wcol + ncols]
    return acc


def mix_kernel(xc_ref, xp_ref, nw_ref, win_ref, gbias_ref, alog_ref, xcw_ref, xcb_ref, qcw_ref, qcb_ref,
               poolw_ref, poolb_ref, pools_ref, dskip_ref, snorm_ref, mnorm_ref, wout_ref, postn_ref,
               out_ref, h_ref, hp_ref, rest_ref, hph_ref, psh_ref, act_ref, mix_ref, ps_ref,
               sstate_ref, mstate_ref, mm_ref):
    TT = xc_ref.shape[1]
    i = pl.program_id(1)

    @pl.when(i == 0)
    def _():
        hp_ref[1] = jnp.zeros(hp_ref.shape[1:], F32)
        rest_ref[1] = jnp.zeros(rest_ref.shape[1:], F32)

    @pl.when(i <= 1)
    def _():
        hph_ref[...] = jnp.zeros(hph_ref.shape, F32)
        psh_ref[...] = jnp.zeros(psh_ref.shape, F32)
        sstate_ref[...] = jnp.zeros(sstate_ref.shape, F32)
        mstate_ref[...] = jnp.zeros(mstate_ref.shape, F32)
        mm_ref[...] = jnp.zeros(mm_ref.shape, F32)

    def step(slot_proj, slot_mix):
        x = xc_ref[0]
        ms = jnp.mean(x * x, axis=-1, keepdims=True)
        h_ref[...] = (x * lax.rsqrt(ms + EPS) * nw_ref[...]).astype(BF16)

        def proj_piece(dst_ref, c0, c1, w0):
            def piece():
                dst_ref[slot_proj, :, c0:c1] = _dot(h_ref[...], win_ref[:, w0 + c0:w0 + c1])
            return piece

        pending = [proj_piece(hp_ref, c0, min(c0 + PROJ_COLS, N_HALO_COLS), 0)
                   for c0 in range(0, N_HALO_COLS, PROJ_COLS)]
        pending += [proj_piece(rest_ref, c0, min(c0 + PROJ_COLS, N_REST_COLS), N_HALO_COLS)
                    for c0 in range(0, N_REST_COLS, PROJ_COLS)]
        _mixers(jnp.maximum(i - 1, 0), xp_ref, hp_ref.at[slot_mix], rest_ref.at[slot_mix], gbias_ref, alog_ref,
                xcw_ref, xcb_ref, qcw_ref, qcb_ref, poolw_ref, poolb_ref, pools_ref, dskip_ref, snorm_ref,
                mnorm_ref, wout_ref, postn_ref, out_ref, hph_ref, psh_ref, act_ref, mix_ref, ps_ref,
                sstate_ref, mstate_ref, mm_ref, pending)

    @pl.when(i % 2 == 0)
    def _():
        step(0, 1)

    @pl.when(i % 2 == 1)
    def _():
        step(1, 0)


def _mixers(tile, x_ref, hp_ref, rest_ref, gbias_ref, alog_ref, xcw_ref, xcb_ref, qcw_ref, qcb_ref,
            poolw_ref, poolb_ref, pools_ref, dskip_ref, snorm_ref, mnorm_ref, wout_ref, postn_ref,
            out_ref, hph_ref, psh_ref, act_ref, mix_ref, ps_ref, sstate_ref, mstate_ref, mm_ref, pending):
    def between():
        if pending:
            pending.pop(0)()

    TT = x_ref.shape[1]
    L = CHUNK
    n_hph = hph_ref.shape[0]
    n_psh = psh_ref.shape[1]
    ps_carried = ((0, 0), (0, 1), (1, 1), (2, 1))

    def tail(cur_ref, halo, n_halo, r0, n_rows, cols):
        if r0 == 0:
            return halo[n_halo - n_rows:n_halo, cols]
        return cur_ref[r0 - n_rows:r0, cols]

    lane = lax.broadcasted_iota(jnp.int32, (L, LANES), 1)
    row = lax.broadcasted_iota(jnp.int32, (L, LANES), 0)
    lo_half = lane < 64
    tau_row = _tau(row)
    causal = _tau(lane) <= tau_row
    tril = jnp.where(causal, 1.0, 0.0).astype(BF16)
    lane_row = lax.broadcasted_iota(jnp.int32, (1, LANES), 1)
    a_row = -jnp.exp(alog_ref[...])
    neg_inf = -jnp.inf
    win_blk = [jnp.where(lo_half, float(POOL_WINDOWS[2 * b]), float(POOL_WINDOWS[2 * b + 1])) for b in range(2)]
    tau_f = tau_row.astype(F32)

    def chunk_stages(c):
        r0 = c * L

        def conv_block(col, cw_ref, cb_ref, wcol, k_taps):
            cols = slice(col, col + LANES)
            cur = hp_ref[r0:r0 + L, cols]
            prev_tail = tail(hp_ref, hph_ref, n_hph, r0, (k_taps - 1) * SUBLANES, cols)
            return _silu(_causal_conv(prev_tail, cur, cw_ref, cb_ref[:, wcol:wcol + LANES], wcol, LANES, k_taps))

        for blk in range(D_SSD_XBC // LANES):
            act_ref[r0:r0 + L, blk * LANES:(blk + 1) * LANES] = conv_block(
                C_XBC + blk * LANES, xcw_ref, xcb_ref, blk * LANES, SSD_CONV)
        for blk in range(2 * D_MLSTM // LANES):
            act_ref[r0:r0 + L, D_SSD_XBC + blk * LANES:D_SSD_XBC + (blk + 1) * LANES] = conv_block(
                C_QK + blk * LANES, qcw_ref, qcb_ref, blk * LANES, MLSTM_CONV)
        yield

        pos = tau_f + (tile * TT + r0 + 1).astype(F32)
        pooled_blocks = []
        for b in range(2):
            cs_ = slice(b * LANES, (b + 1) * LANES)
            u_cur = hp_ref[r0:r0 + L, cs_]
            lvl = u_cur
            sums = []
            for li, sh in enumerate((1, 2, 4, 8)):
                if li == 0:
                    prev_tail = tail(hp_ref, hph_ref, n_hph, r0, sh * SUBLANES, cs_)
                else:
                    prev_tail = tail(ps_ref.at[li - 1], psh_ref.at[li - 1], n_psh, r0, sh * SUBLANES, cs_)
                ext = _ext_rows(prev_tail, lvl[L - sh * SUBLANES:L])
                lvl = lvl + _shifted(ext, lvl, sh)
                sums.append(lvl)
                if (li, b) in ps_carried:
                    ps_ref[li, r0:r0 + L, cs_] = lvl
                if b == 0 and li == 1:
                    break
            wsum = jnp.where(lo_half, sums[0], sums[1]) if b == 0 else jnp.where(lo_half, sums[2], sums[3])
            pooled_blocks.append((wsum / jnp.minimum(pos, win_blk[b]) - u_cur).astype(BF16))
        mix_ref[r0:r0 + L, 0:D_POOL] = (
            (_dot(jnp.concatenate(pooled_blocks, axis=1), poolw_ref[...]) + poolb_ref[...]) * pools_ref[...])
        yield

        gb = rest_ref[r0:r0 + L, R_G:R_G + LANES] + gbias_ref[...]
        sp_term = jnp.log1p(jnp.exp(-jnp.abs(gb)))
        dt = jnp.maximum(gb, 0.0) + sp_term
        log_f = jnp.minimum(gb, 0.0) - sp_term
        is_dt = lane < G_I
        is_f = (lane >= G_F) & (lane < G_F + MLSTM_HEADS)
        v_cum = jnp.where(is_dt, dt * a_row, jnp.where(is_f, log_f, 0.0))
        hi, mid, lo = _split3(v_cum)
        cs = _dot(tril, hi) + _dot(tril, mid) + _dot(tril, lo)
        u_gate = jnp.where(is_dt, dt, gb)
        cs_t = cs.T
        ug_t = u_gate.T
        cs_last = cs[L - 1:L, :]
        e_col = jnp.exp(cs)
        w_col = jnp.exp(cs_last - cs) * dt
        e_last = jnp.exp(cs_last)
        yield

        for g in range(SSD_GROUPS):
            b_g = act_ref[r0:r0 + L, D_SSD + g * SSD_STATE:D_SSD + (g + 1) * SSD_STATE].astype(BF16)
            c_g = act_ref[r0:r0 + L, D_SSD + (SSD_GROUPS + g) * SSD_STATE:
                          D_SSD + (SSD_GROUPS + g + 1) * SSD_STATE].astype(BF16)
            s_g = _dot_nt(c_g, b_g)
            state_g = sstate_ref[g]
            y_off = _dot(c_g, state_g.astype(BF16))
            xd_blocks = []
            cd_blocks = []
            for pr in range(2):
                h_even = 4 * g + 2 * pr
                col = h_even * SSD_HEAD_DIM
                xs = act_ref[r0:r0 + L, col:col + LANES]
                xs_b = xs.astype(BF16)
                yd = []
                for hh in range(2):
                    hd = h_even + hh
                    seg = jnp.where(causal, cs[:, hd:hd + 1] - cs_t[hd:hd + 1, :], neg_inf)
                    m_h = (s_g * (jnp.exp(seg) * ug_t[hd:hd + 1, :])).astype(BF16)
                    yd.append(_dot(m_h, xs_b))
                y_diag = jnp.where(lo_half, yd[0], yd[1])
                e_exp = _pair_expand(e_col, h_even, (L, LANES), lo_half)
                w_exp = _pair_expand(w_col, h_even, (L, LANES), lo_half)
                y = (y_diag + y_off[:, pr * LANES:(pr + 1) * LANES] * e_exp
                     + xs * dskip_ref[:, col:col + LANES])
                z = rest_ref[r0:r0 + L, R_Z + col:R_Z + col + LANES]
                mix_ref[r0:r0 + L, D_POOL + col:D_POOL + col + LANES] = y * _silu(z)
                xd_blocks.append((xs * w_exp).astype(BF16))
                cd_blocks.append(_pair_expand(e_last, h_even, (1, LANES), lane_row < 64))
            xd_g = jnp.concatenate(xd_blocks, axis=1)
            cd_g = jnp.concatenate(cd_blocks, axis=1)
            new_states = lax.dot_general(b_g, xd_g, (((0,), (0,)), ((), ())),
                                         preferred_element_type=F32)
            sstate_ref[g] = state_g * cd_g + new_states
            yield
        y_all = mix_ref[r0:r0 + L, D_POOL:D_POOL + D_SSD]
        ms_y = jnp.mean(y_all * y_all, axis=-1, keepdims=True)
        mix_ref[r0:r0 + L, D_POOL:D_POOL + D_SSD] = y_all * lax.rsqrt(ms_y + EPS) * snorm_ref[...]

        for pr in range(MLSTM_HEADS // 2):
            qcol = D_SSD_XBC + pr * LANES
            kcol = D_SSD_XBC + D_MLSTM + pr * LANES
            q_b = act_ref[r0:r0 + L, qcol:qcol + LANES] * (MLSTM_HEAD_DIM ** -0.5)
            k_t = act_ref[r0:r0 + L, kcol:kcol + LANES].T
            k_tb = k_t.astype(BF16)
            v_b = rest_ref[r0:r0 + L, R_V + pr * LANES:R_V + (pr + 1) * LANES]
            o_b = rest_ref[r0:r0 + L, R_O + pr * LANES:R_O + (pr + 1) * LANES]
            hv = []
            for hh in range(2):
                hd = 2 * pr + hh
                in_half = lo_half if hh == 0 else jnp.logical_not(lo_half)
                row_in_half = (row < 64) if hh == 0 else (row >= 64)
                ones_lane = (lane == 64) if hh == 0 else (lane == 0)
                ol = 64 if hh == 0 else 0
                qm = jnp.where(in_half, q_b, 0.0).astype(BF16)
                s = _dot(qm, k_tb)
                b_row = cs_t[G_F + hd:G_F + hd + 1, :]
                r_row = ug_t[G_I + hd:G_I + hd + 1, :] - b_row
                b_last = jnp.sum(jnp.where(lane_row == L - 1, b_row, 0.0), axis=-1, keepdims=True)
                al_row = b_last + r_row
                m_loc = jnp.max(al_row, axis=-1, keepdims=True)
                prev_m = mm_ref[hd:hd + 1, 0:1]
                rmask = jnp.where(causal, r_row, neg_inf)
                g_col = jnp.maximum(jnp.max(rmask, axis=-1, keepdims=True), prev_m)
                p = (s * jnp.exp(rmask - g_col)).astype(BF16)
                v_ext = jnp.where(in_half, v_b, jnp.where(ones_lane, 1.0, 0.0)).astype(BF16)
                cn = mstate_ref[hd]
                res = _dot(p, v_ext) + _dot(qm, cn.astype(BF16)) * jnp.exp(prev_m - g_col)
                den = jnp.maximum(jnp.abs(res[:, ol:ol + 1]),
                                  jnp.exp(-(cs[:, G_F + hd:G_F + hd + 1] + g_col)))
                hv.append(res / den)
                w_row = jnp.exp(al_row - m_loc)
                ktw = jnp.where(row_in_half, k_t * w_row, 0.0).astype(BF16)
                c_loc = _dot(ktw, v_ext)
                m_new = jnp.maximum(b_last + prev_m, m_loc)
                mstate_ref[hd] = (jnp.exp(b_last + prev_m - m_new) * cn
                                  + jnp.exp(m_loc - m_new) * c_loc)
                mm_ref[hd:hd + 1, :] = jnp.broadcast_to(m_new, (1, LANES))
            hcat = jax.nn.sigmoid(o_b) * jnp.where(lo_half, hv[0], hv[1])
            sq = hcat * hcat
            ss_lo = jnp.sum(jnp.where(lo_half, sq, 0.0), axis=-1, keepdims=True)
            ss_hi = jnp.sum(jnp.where(lo_half, 0.0, sq), axis=-1, keepdims=True)
            inv = jnp.where(lo_half, lax.rsqrt(ss_lo * (1.0 / MLSTM_HEAD_DIM) + EPS),
                            lax.rsqrt(ss_hi * (1.0 / MLSTM_HEAD_DIM) + EPS))
            mcol = D_POOL + D_SSD + pr * LANES
            mix_ref[r0:r0 + L, mcol:mcol + LANES] = hcat * inv * mnorm_ref[:, pr * LANES:(pr + 1) * LANES]
            if pr + 1 < MLSTM_HEADS // 2:
                yield

        def out_piece():
            o = _dot(mix_ref[r0:r0 + L, :].astype(BF16), wout_ref[...])
            ms_o = jnp.mean(o * o, axis=-1, keepdims=True)
            out_ref[0, r0:r0 + L, :] = x_ref[0, r0:r0 + L, :] + o * lax.rsqrt(ms_o + EPS) * postn_ref[...]

        pending.insert(0, out_piece)
        yield

    n_stages = 3 + SSD_GROUPS + MLSTM_HEADS // 2
    gens = [chunk_stages(c) for c in range(TT // L)]
    for _ in range(n_stages):
        for gen in gens:
            next(gen)
            between()
    while pending:
        pending.pop(0)()

    hph_ref[...] = hp_ref[TT - n_hph:TT, :]
    for lv, b in ps_carried:
        cs_ = slice(b * LANES, (b + 1) * LANES)
        psh_ref[lv, :, cs_] = ps_ref[lv, TT - n_psh:TT, cs_]


def ffn_kernel(x_ref, nw_ref, wup_ref, cw_ref, cb_ref, wdn_ref, postn_ref, out_ref, halo_ref, a_ref):
    TT = x_ref.shape[1]
    FT = FFN_FT
    L = CHUNK
    n_tail = (FFN_CONV - 1) * SUBLANES
    i = pl.program_id(1)

    @pl.when(i == 0)
    def _():
        halo_ref[...] = jnp.zeros(halo_ref.shape, F32)

    x = x_ref[0]
    ms = jnp.mean(x * x, axis=-1, keepdims=True)
    h = (x * lax.rsqrt(ms + EPS) * nw_ref[...]).astype(BF16)

    def conv_cols(col):
        u = _dot(h, wup_ref[:, col:col + FT])
        outs = []
        for c in range(TT // L):
            cur = u[c * L:(c + 1) * L]
            prev_tail = halo_ref[:, col:col + FT] if c == 0 else u[c * L - n_tail:c * L]
            outs.append(_causal_conv(prev_tail, cur, cw_ref, cb_ref[:, col:col + FT], col, FT, FFN_CONV))
        halo_ref[:, col:col + FT] = u[TT - n_tail:TT]
        return outs

    for j in range(D_FF // FT):
        gts = conv_cols(j * FT)
        vals = conv_cols(D_FF + j * FT)
        for c in range(TT // L):
            gt = gts[c]
            gelu = 0.5 * gt * (1.0 + jnp.tanh(math.sqrt(2.0 / math.pi) * (gt + 0.044715 * (gt * gt * gt))))
            a_ref[c * L:(c + 1) * L, j * FT:(j + 1) * FT] = (gelu * vals[c]).astype(BF16)

    f = _dot(a_ref[...], wdn_ref[...])
    ms_f = jnp.mean(f * f, axis=-1, keepdims=True)
    out_ref[0] = x_ref[0] + f * lax.rsqrt(ms_f + EPS) * postn_ref[...]


def _const_spec(shape):
    nd = len(shape)
    return pl.BlockSpec(shape, lambda b, i: (0,) * nd, pipeline_mode=pl.Buffered(1))


def _mix_layer(x, nw, win, gbias, alog, xcw, xcb, qcw, qcb, poolw, poolb, pools, dskip, snorm, mnorm,
               wout, postn):
    B, T, D = x.shape
    TT = MIX_TT
    consts = (nw, win, gbias, alog, xcw, xcb, qcw, qcb, poolw, poolb, pools, dskip, snorm, mnorm, wout, postn)
    n_t = T // TT
    cur_spec = pl.BlockSpec((1, TT, D), lambda b, i: (b, jnp.minimum(i, n_t - 1), 0))
    prev_spec = pl.BlockSpec((1, TT, D), lambda b, i: (b, jnp.maximum(i - 1, 0), 0))
    max_conv_tail = (max(SSD_CONV, MLSTM_CONV) - 1) * SUBLANES
    max_pool_tail = (POOL_WINDOWS[-1] // 2) * SUBLANES
    return pl.pallas_call(
        mix_kernel,
        grid=(B, n_t + 1),
        in_specs=[cur_spec, prev_spec] + [_const_spec(c.shape) for c in consts],
        out_specs=prev_spec,
        out_shape=jax.ShapeDtypeStruct(x.shape, x.dtype),
        scratch_shapes=[
            pltpu.VMEM((TT, D_MODEL), BF16),
            pltpu.VMEM((2, TT, N_HALO_COLS), F32),
            pltpu.VMEM((2, TT, N_REST_COLS), F32),
            pltpu.VMEM((max_conv_tail, N_HALO_COLS), F32),
            pltpu.VMEM((3, max_pool_tail, D_POOL), F32),
            pltpu.VMEM((TT, D_SSD_XBC + 2 * D_MLSTM), F32),
            pltpu.VMEM((TT, D_MODEL), F32),
            pltpu.VMEM((3, TT, D_POOL), F32),
            pltpu.VMEM((SSD_GROUPS, SSD_STATE, 4 * SSD_HEAD_DIM), F32),
            pltpu.VMEM((MLSTM_HEADS, LANES, LANES), F32),
            pltpu.VMEM((SUBLANES, LANES), F32),
        ],
        compiler_params=pltpu.CompilerParams(
            dimension_semantics=("arbitrary", "arbitrary"), vmem_limit_bytes=VMEM_LIMIT),
        name="mix_layer",
    )(x, x, *consts)


def _ffn_layer(x, nw, wup, cw, cb, wdn, postn):
    B, T, D = x.shape
    TT = FFN_TT
    consts = (nw, wup, cw, cb, wdn, postn)
    x_spec = pl.BlockSpec((1, TT, D), lambda b, i: (b, i, 0))
    return pl.pallas_call(
        ffn_kernel,
        grid=(B, T // TT),
        in_specs=[x_spec] + [_const_spec(c.shape) for c in consts],
        out_specs=x_spec,
        out_shape=jax.ShapeDtypeStruct(x.shape, x.dtype),
        scratch_shapes=[
            pltpu.VMEM(((FFN_CONV - 1) * SUBLANES, 2 * D_FF), F32),
            pltpu.VMEM((TT, D_FF), BF16),
        ],
        compiler_params=pltpu.CompilerParams(
            dimension_semantics=("arbitrary", "arbitrary"), vmem_limit_bytes=VMEM_LIMIT),
        name="ffn_layer",
    )(x, *consts)


def _row(v):
    return v.reshape(1, -1).astype(F32)


def _pad_lanes(v):
    return jnp.pad(v.astype(F32), (0, LANES - v.shape[0])).reshape(1, LANES)


def _prep_w_in(w):
    sizes = (D_POOL, D_SSD, D_SSD_XBC, SSD_HEADS, 2 * D_MLSTM, D_MLSTM, D_MLSTM, MLSTM_HEADS, MLSTM_HEADS)
    offs = [0]
    for s in sizes:
        offs.append(offs[-1] + s)
    u_pool, z, xbc, dt, qk, v, o, ig, fg = [w[:, offs[k]:offs[k + 1]] for k in range(len(sizes))]
    pad = jnp.zeros((w.shape[0], LANES - SSD_HEADS - 2 * MLSTM_HEADS), w.dtype)
    return jnp.concatenate([u_pool, xbc, qk, z, v, o, dt, ig, fg, pad], axis=1).astype(BF16)


def _prep_pool_w(w):
    out = jnp.zeros((D_POOL, D_POOL), F32)
    for g in range(len(POOL_WINDOWS)):
        s = g * POOL_GROUP_DIM
        out = lax.dynamic_update_slice(out, w[g].astype(F32), (s, s))
    return out.astype(BF16)


def _permute_chunks(x, to_kernel_order):
    B, T, D = x.shape
    a, b = (SUBLANES, VROWS) if to_kernel_order else (VROWS, SUBLANES)
    return x.reshape(B, T // CHUNK, a, b, D).transpose(0, 1, 3, 2, 4).reshape(B, T, D)


def kernel(x, pre_mix_norm, w_in, pool_w, pool_b, pool_scale, ssd_conv_w, ssd_conv_b, ssd_dt_bias, ssd_a_log, ssd_d, ssd_norm, mlstm_conv_w, mlstm_conv_b, mlstm_i_bias, mlstm_f_bias, mlstm_norm, w_out, post_mix_norm, pre_ffn_norm, ffn_w_up, ffn_conv_w, ffn_conv_b, ffn_w_down, post_ffn_norm):
    depth = w_in.shape[0]
    x = _permute_chunks(x, True)
    for l in range(depth):
        gbias = _pad_lanes(jnp.concatenate([ssd_dt_bias[l], mlstm_i_bias[l], mlstm_f_bias[l]]))
        x = _mix_layer(
            x, _row(pre_mix_norm[l]), _prep_w_in(w_in[l]), gbias, _pad_lanes(ssd_a_log[l]),
            ssd_conv_w[l].astype(F32), _row(ssd_conv_b[l]), mlstm_conv_w[l].astype(F32), _row(mlstm_conv_b[l]),
            _prep_pool_w(pool_w[l]), _row(pool_b[l]), _row(pool_scale[l]),
            _row(jnp.repeat(ssd_d[l], SSD_HEAD_DIM)), _row(ssd_norm[l]), _row(mlstm_norm[l]),
            w_out[l].astype(BF16), _row(post_mix_norm[l]))
        x = _ffn_layer(
            x, _row(pre_ffn_norm[l]), ffn_w_up[l].astype(BF16), ffn_conv_w[l].astype(F32),
            _row(ffn_conv_b[l]), ffn_w_down[l].astype(BF16), _row(post_ffn_norm[l]))
    return _permute_chunks(x, False)
```

```python
import functools
import math

import jax
import jax.numpy as jnp
from jax import lax
from jax.experimental import pallas as pl
from jax.experimental.pallas import tpu as pltpu

F32 = jnp.float32
BF16 = jnp.bfloat16

D_MODEL = 1024
EPS = 1e-6

D_POOL = 256
POOL_GROUP_DIM = 64
POOL_WINDOWS = (2, 4, 8, 16)

D_SSD = 512
SSD_HEADS = 8
SSD_HEAD_DIM = 64
SSD_GROUPS = 2
SSD_STATE = 128
SSD_CONV = 4
D_SSD_XBC = D_SSD + 2 * SSD_GROUPS * SSD_STATE

D_MLSTM = 256
MLSTM_HEADS = 4
MLSTM_HEAD_DIM = 64
MLSTM_CONV = 4

D_FF = 2816
FFN_CONV = 3

CHUNK = 128
LANES = 128
SUBLANES = 8
VROWS = CHUNK // SUBLANES

C_POOL = 0
C_XBC = C_POOL + D_POOL
C_QK = C_XBC + D_SSD_XBC
N_HALO_COLS = C_QK + 2 * D_MLSTM
R_Z = 0
R_V = R_Z + D_SSD
R_O = R_V + D_MLSTM
R_G = R_O + D_MLSTM
N_REST_COLS = R_G + LANES
N_IN_COLS = N_HALO_COLS + N_REST_COLS
G_DT = 0
G_I = SSD_HEADS
G_F = G_I + MLSTM_HEADS

MIX_TT = 256
PROJ_COLS = 256
STAGE_COLS = 256
FFN_TT = 512
FFN_FT = 256
VMEM_LIMIT = 56 * 1024 * 1024


def _dot(a, b):
    return jnp.dot(a, b, preferred_element_type=F32)


def _dot_nt(a, b):
    return lax.dot_general(a, b, (((1,), (1,)), ((), ())), preferred_element_type=F32)


def _silu(x):
    return x * jax.nn.sigmoid(x)


def _split3(a):
    hi = a.astype(BF16)
    r = a - hi.astype(F32)
    mid = r.astype(BF16)
    lo = (r - mid.astype(F32)).astype(BF16)
    return hi, mid, lo


def _bcast_lane(a, j, shape):
    return jnp.broadcast_to(a[:, j:j + 1], shape)


def _pair_expand(a, h_even, shape, lo_half):
    return jnp.where(lo_half, _bcast_lane(a, h_even, shape), _bcast_lane(a, h_even + 1, shape))


def _stage_weights(blocks, stage_ref, sem_ref):
    def copy(n):
        src, dst = blocks[n]
        rows, cols = src.shape
        return pltpu.make_async_copy(src, stage_ref.at[n % 2, 0:rows, 0:cols], sem_ref.at[n % 2])

    copy(0).start()
    for n, (src, dst) in enumerate(blocks):
        if n + 1 < len(blocks):
            copy(n + 1).start()
        copy(n).wait()
        rows, cols = src.shape
        dst[...] = stage_ref[n % 2, 0:rows, 0:cols].astype(BF16)


def _col_blocks(src_ref, src0, dst_ref, dst0, ncols, step):
    return [(src_ref.at[:, src0 + k:src0 + min(k + step, ncols)],
             dst_ref.at[:, dst0 + k:dst0 + min(k + step, ncols)]) for k in range(0, ncols, step)]


def _tau(p):
    return (p % SUBLANES) * VROWS + p // SUBLANES


def _ext_rows(prev_tail, cur_tail):
    n = cur_tail.shape[0] // SUBLANES
    sub0 = lax.broadcasted_iota(jnp.int32, (SUBLANES, cur_tail.shape[1]), 0) == 0
    out = []
    for j in range(n):
        sl = slice(j * SUBLANES, (j + 1) * SUBLANES)
        out.append(jnp.where(sub0, pltpu.roll(prev_tail[sl], 1, 0), pltpu.roll(cur_tail[sl], 1, 0)))
    return out


def _shifted(ext, cur, k):
    if k == 0:
        return cur
    return jnp.concatenate(ext[len(ext) - k:] + [cur[0:CHUNK - SUBLANES * k]], axis=0)


def _causal_conv(prev_tail, cur, w_ref, b_row, wcol, ncols, k_taps):
    n = k_taps - 1
    ext = _ext_rows(prev_tail, cur[CHUNK - n * SUBLANES:CHUNK])
    acc = b_row
    for k in range(k_taps):
        acc = acc + _shifted(ext, cur, n - k) * w_ref[k:k + 1, wcol:wcol + ncols]
    return acc


def mix_kernel(layer, xc_ref, xp_ref, win_hbm, wtail_hbm, wout_hbm, nw_ref, gbias_ref, alog_ref,
               xcw_ref, xcb_ref, qcw_ref, qcb_ref,
               poolw_ref, poolb_ref, pools_ref, dskip_ref, snorm_ref, mnorm_ref, postn_ref,
               out_ref, win_ref, wout_ref, stage_ref, stage_sem,
               h_ref, hp_ref, rest_ref, hph_ref, psh_ref, act_ref, mix_ref, ps_ref,
               sstate_ref, mstate_ref, mm_ref):
    TT = xc_ref.shape[1]
    i = pl.program_id(1)

    @pl.when((pl.program_id(0) == 0) & (i == 0))
    def _():
        wmain = win_hbm.at[layer]
        blocks = (_col_blocks(wmain, 0, win_ref, C_POOL, D_POOL, STAGE_COLS)
                  + _col_blocks(wmain, D_POOL + D_SSD, win_ref, C_XBC, D_SSD_XBC, STAGE_COLS)
                  + _col_blocks(wmain, D_POOL, win_ref, N_HALO_COLS + R_Z, D_SSD, STAGE_COLS)
                  + _col_blocks(wtail_hbm, 0, win_ref, C_QK, 2 * D_MLSTM, STAGE_COLS)
                  + _col_blocks(wtail_hbm, 2 * D_MLSTM, win_ref, N_HALO_COLS + R_V, 2 * D_MLSTM + LANES,
                                STAGE_COLS)
                  + _col_blocks(wout_hbm.at[layer], 0, wout_ref, 0, D_MODEL, STAGE_COLS))
        _stage_weights(blocks, stage_ref, stage_sem)

    @pl.when(i == 0)
    def _():
        hp_ref[1] = jnp.zeros(hp_ref.shape[1:], F32)
        rest_ref[1] = jnp.zeros(rest_ref.shape[1:], F32)

    @pl.when(i <= 1)
    def _():
        hph_ref[...] = jnp.zeros(hph_ref.shape, F32)
        psh_ref[...] = jnp.zeros(psh_ref.shape, F32)
        sstate_ref[...] = jnp.zeros(sstate_ref.shape, F32)
        mstate_ref[...] = jnp.zeros(mstate_ref.shape, F32)
        mm_ref[...] = jnp.zeros(mm_ref.shape, F32)

    def step(slot_proj, slot_mix):
        x = xc_ref[0]
        ms = jnp.mean(x * x, axis=-1, keepdims=True)
        h_ref[...] = (x * lax.rsqrt(ms + EPS) * nw_ref[...]).astype(BF16)

        def proj_piece(dst_ref, c0, c1, w0):
            def piece():
                dst_ref[slot_proj, :, c0:c1] = _dot(h_ref[...], win_ref[:, w0 + c0:w0 + c1])
            return piece

        pending = [proj_piece(hp_ref, c0, min(c0 + PROJ_COLS, N_HALO_COLS), 0)
                   for c0 in range(0, N_HALO_COLS, PROJ_COLS)]
        pending += [proj_piece(rest_ref, c0, min(c0 + PROJ_COLS, N_REST_COLS), N_HALO_COLS)
                    for c0 in range(0, N_REST_COLS, PROJ_COLS)]
        _mixers(jnp.maximum(i - 1, 0), xp_ref, hp_ref.at[slot_mix], rest_ref.at[slot_mix], gbias_ref, alog_ref,
                xcw_ref, xcb_ref, qcw_ref, qcb_ref, poolw_ref, poolb_ref, pools_ref, dskip_ref, snorm_ref,
                mnorm_ref, wout_ref, postn_ref, out_ref, hph_ref, psh_ref, act_ref, mix_ref, ps_ref,
                sstate_ref, mstate_ref, mm_ref, pending)

    @pl.when(i % 2 == 0)
    def _():
        step(0, 1)

    @pl.when(i % 2 == 1)
    def _():
        step(1, 0)


def _mixers(tile, x_ref, hp_ref, rest_ref, gbias_ref, alog_ref, xcw_ref, xcb_ref, qcw_ref, qcb_ref,
            poolw_ref, poolb_ref, pools_ref, dskip_ref, snorm_ref, mnorm_ref, wout_ref, postn_ref,
            out_ref, hph_ref, psh_ref, act_ref, mix_ref, ps_ref, sstate_ref, mstate_ref, mm_ref, pending):
    def between():
        if pending:
            pending.pop(0)()

    TT = x_ref.shape[1]
    L = CHUNK
    n_hph = hph_ref.shape[0]
    n_psh = psh_ref.shape[1]
    ps_carried = ((0, 0), (0, 1), (1, 1), (2, 1))

    def tail(cur_ref, halo, n_halo, r0, n_rows, cols):
        if r0 == 0:
            return halo[n_halo - n_rows:n_halo, cols]
        return cur_ref[r0 - n_rows:r0, cols]

    lane = lax.broadcasted_iota(jnp.int32, (L, LANES), 1)
    row = lax.broadcasted_iota(jnp.int32, (L, LANES), 0)
    lo_half = lane < 64
    tau_row = _tau(row)
    causal = _tau(lane) <= tau_row
    tril = jnp.where(causal, 1.0, 0.0).astype(BF16)
    lane_row = lax.broadcasted_iota(jnp.int32, (1, LANES), 1)
    a_row = -jnp.exp(alog_ref[...])
    neg_inf = -jnp.inf
    win_blk = [jnp.where(lo_half, float(POOL_WINDOWS[2 * b]), float(POOL_WINDOWS[2 * b + 1])) for b in range(2)]
    tau_f = tau_row.astype(F32)

    def chunk_stages(c):
        r0 = c * L

        def conv_block(col, cw_ref, cb_ref, wcol, k_taps):
            cols = slice(col, col + LANES)
            cur = hp_ref[r0:r0 + L, cols]
            prev_tail = tail(hp_ref, hph_ref, n_hph, r0, (k_taps - 1) * SUBLANES, cols)
            return _silu(_causal_conv(prev_tail, cur, cw_ref, cb_ref[:, wcol:wcol + LANES], wcol, LANES, k_taps))

        for blk in range(D_SSD_XBC // LANES):
            act_ref[r0:r0 + L, blk * LANES:(blk + 1) * LANES] = conv_block(
                C_XBC + blk * LANES, xcw_ref, xcb_ref, blk * LANES, SSD_CONV)
        for blk in range(2 * D_MLSTM // LANES):
            act_ref[r0:r0 + L, D_SSD_XBC + blk * LANES:D_SSD_XBC + (blk + 1) * LANES] = conv_block(
                C_QK + blk * LANES, qcw_ref, qcb_ref, blk * LANES, MLSTM_CONV)
        yield

        pos = tau_f + (tile * TT + r0 + 1).astype(F32)
        pooled_blocks = []
        for b in range(2):
            cs_ = slice(b * LANES, (b + 1) * LANES)
            u_cur = hp_ref[r0:r0 + L, cs_]
            lvl = u_cur
            sums = []
            for li, sh in enumerate((1, 2, 4, 8)):
                if li == 0:
                    prev_tail = tail(hp_ref, hph_ref, n_hph, r0, sh * SUBLANES, cs_)
                else:
                    prev_tail = tail(ps_ref.at[li - 1], psh_ref.at[li - 1], n_psh, r0, sh * SUBLANES, cs_)
                ext = _ext_rows(prev_tail, lvl[L - sh * SUBLANES:L])
                lvl = lvl + _shifted(ext, lvl, sh)
                sums.append(lvl)
                if (li, b) in ps_carried:
                    ps_ref[li, r0:r0 + L, cs_] = lvl
                if b == 0 and li == 1:
                    break
            wsum = jnp.where(lo_half, sums[0], sums[1]) if b == 0 else jnp.where(lo_half, sums[2], sums[3])
            pooled_blocks.append((wsum / jnp.minimum(pos, win_blk[b]) - u_cur).astype(BF16))
        mix_ref[r0:r0 + L, 0:D_POOL] = (
            (_dot(jnp.concatenate(pooled_blocks, axis=1), poolw_ref[...]) + poolb_ref[...]) * pools_ref[...])
        yield

        gb = rest_ref[r0:r0 + L, R_G:R_G + LANES] + gbias_ref[...]
        sp_term = jnp.log1p(jnp.exp(-jnp.abs(gb)))
        dt = jnp.maximum(gb, 0.0) + sp_term
        log_f = jnp.minimum(gb, 0.0) - sp_term
        is_dt = lane < G_I
        is_f = (lane >= G_F) & (lane < G_F + MLSTM_HEADS)
        v_cum = jnp.where(is_dt, dt * a_row, jnp.where(is_f, log_f, 0.0))
        hi, mid, lo = _split3(v_cum)
        cs = _dot(tril, hi) + _dot(tril, mid) + _dot(tril, lo)
        u_gate = jnp.where(is_dt, dt, gb)
        cs_t = cs.T
        ug_t = u_gate.T
        cs_last = cs[L - 1:L, :]
        e_col = jnp.exp(cs)
        w_col = jnp.exp(cs_last - cs) * dt
        e_last = jnp.exp(cs_last)
        yield

        for g in range(SSD_GROUPS):
            b_g = act_ref[r0:r0 + L, D_SSD + g * SSD_STATE:D_SSD + (g + 1) * SSD_STATE].astype(BF16)
            c_g = act_ref[r0:r0 + L, D_SSD + (SSD_GROUPS + g) * SSD_STATE:
                          D_SSD + (SSD_GROUPS + g + 1) * SSD_STATE].astype(BF16)
            s_g = _dot_nt(c_g, b_g)
            state_g = sstate_ref[g]
            y_off = _dot(c_g, state_g.astype(BF16))
            xd_blocks = []
            cd_blocks = []
            for pr in range(2):
                h_even = 4 * g + 2 * pr
                col = h_even * SSD_HEAD_DIM
                xs = act_ref[r0:r0 + L, col:col + LANES]
                xs_b = xs.astype(BF16)
                yd = []
                for hh in range(2):
                    hd = h_even + hh
                    seg = jnp.where(causal, cs[:, hd:hd + 1] - cs_t[hd:hd + 1, :], neg_inf)
                    m_h = (s_g * (jnp.exp(seg) * ug_t[hd:hd + 1, :])).astype(BF16)
                    yd.append(_dot(m_h, xs_b))
                y_diag = jnp.where(lo_half, yd[0], yd[1])
                e_exp = _pair_expand(e_col, h_even, (L, LANES), lo_half)
                w_exp = _pair_expand(w_col, h_even, (L, LANES), lo_half)
                y = (y_diag + y_off[:, pr * LANES:(pr + 1) * LANES] * e_exp
                     + xs * dskip_ref[:, col:col + LANES])
                z = rest_ref[r0:r0 + L, R_Z + col:R_Z + col + LANES]
                mix_ref[r0:r0 + L, D_POOL + col:D_POOL + col + LANES] = y * _silu(z)
                xd_blocks.append((xs * w_exp).astype(BF16))
                cd_blocks.append(_pair_expand(e_last, h_even, (1, LANES), lane_row < 64))
            xd_g = jnp.concatenate(xd_blocks, axis=1)
            cd_g = jnp.concatenate(cd_blocks, axis=1)
            new_states = lax.dot_general(b_g, xd_g, (((0,), (0,)), ((), ())),
                                         preferred_element_type=F32)
            sstate_ref[g] = state_g * cd_g + new_states
            yield
        y_all = mix_ref[r0:r0 + L, D_POOL:D_POOL + D_SSD]
        ms_y = jnp.mean(y_all * y_all, axis=-1, keepdims=True)
        mix_ref[r0:r0 + L, D_POOL:D_POOL + D_SSD] = y_all * lax.rsqrt(ms_y + EPS) * snorm_ref[...]

        for pr in range(MLSTM_HEADS // 2):
            qcol = D_SSD_XBC + pr * LANES
            kcol = D_SSD_XBC + D_MLSTM + pr * LANES
            q_b = act_ref[r0:r0 + L, qcol:qcol + LANES] * (MLSTM_HEAD_DIM ** -0.5)
            k_t = act_ref[r0:r0 + L, kcol:kcol + LANES].T
            k_tb = k_t.astype(BF16)
            v_b = rest_ref[r0:r0 + L, R_V + pr * LANES:R_V + (pr + 1) * LANES]
            o_b = rest_ref[r0:r0 + L, R_O + pr * LANES:R_O + (pr + 1) * LANES]
            hv = []
            for hh in range(2):
                hd = 2 * pr + hh
                in_half = lo_half if hh == 0 else jnp.logical_not(lo_half)
                row_in_half = (row < 64) if hh == 0 else (row >= 64)
                ones_lane = (lane == 64) if hh == 0 else (lane == 0)
                ol = 64 if hh == 0 else 0
                qm = jnp.where(in_half, q_b, 0.0).astype(BF16)
                s = _dot(qm, k_tb)
                b_row = cs_t[G_F + hd:G_F + hd + 1, :]
                r_row = ug_t[G_I + hd:G_I + hd + 1, :] - b_row
                b_last = jnp.sum(jnp.where(lane_row == L - 1, b_row, 0.0), axis=-1, keepdims=True)
                al_row = b_last + r_row
                m_loc = jnp.max(al_row, axis=-1, keepdims=True)
                prev_m = mm_ref[hd:hd + 1, 0:1]
                rmask = jnp.where(causal, r_row, neg_inf)
                g_col = jnp.maximum(jnp.max(rmask, axis=-1, keepdims=True), prev_m)
                p = (s * jnp.exp(rmask - g_col)).astype(BF16)
                v_ext = jnp.where(in_half, v_b, jnp.where(ones_lane, 1.0, 0.0)).astype(BF16)
                cn = mstate_ref[hd]
                res = _dot(p, v_ext) + _dot(qm, cn.astype(BF16)) * jnp.exp(prev_m - g_col)
                den = jnp.maximum(jnp.abs(res[:, ol:ol + 1]),
                                  jnp.exp(-(cs[:, G_F + hd:G_F + hd + 1] + g_col)))
                hv.append(res / den)
                w_row = jnp.exp(al_row - m_loc)
                ktw = jnp.where(row_in_half, k_t * w_row, 0.0).astype(BF16)
                c_loc = _dot(ktw, v_ext)
                m_new = jnp.maximum(b_last + prev_m, m_loc)
                mstate_ref[hd] = (jnp.exp(b_last + prev_m - m_new) * cn
                                  + jnp.exp(m_loc - m_new) * c_loc)
                mm_ref[hd:hd + 1, :] = jnp.broadcast_to(m_new, (1, LANES))
            hcat = jax.nn.sigmoid(o_b) * jnp.where(lo_half, hv[0], hv[1])
            sq = hcat * hcat
            ss_lo = jnp.sum(jnp.where(lo_half, sq, 0.0), axis=-1, keepdims=True)
            ss_hi = jnp.sum(jnp.where(lo_half, 0.0, sq), axis=-1, keepdims=True)
            inv = jnp.where(lo_half, lax.rsqrt(ss_lo * (1.0 / MLSTM_HEAD_DIM) + EPS),
                            lax.rsqrt(ss_hi * (1.0 / MLSTM_HEAD_DIM) + EPS))
            mcol = D_POOL + D_SSD + pr * LANES
            mix_ref[r0:r0 + L, mcol:mcol + LANES] = hcat * inv * mnorm_ref[:, pr * LANES:(pr + 1) * LANES]
            if pr + 1 < MLSTM_HEADS // 2:
                yield

        def out_piece():
            o = _dot(mix_ref[r0:r0 + L, :].astype(BF16), wout_ref[...])
            ms_o = jnp.mean(o * o, axis=-1, keepdims=True)
            out_ref[0, r0:r0 + L, :] = x_ref[0, r0:r0 + L, :] + o * lax.rsqrt(ms_o + EPS) * postn_ref[...]

        pending.insert(0, out_piece)
        yield

    n_stages = 3 + SSD_GROUPS + MLSTM_HEADS // 2
    gens = [chunk_stages(c) for c in range(TT // L)]
    for _ in range(n_stages):
        for gen in gens:
            next(gen)
            between()
    while pending:
        pending.pop(0)()

    hph_ref[...] = hp_ref[TT - n_hph:TT, :]
    for lv, b in ps_carried:
        cs_ = slice(b * LANES, (b + 1) * LANES)
        psh_ref[lv, :, cs_] = ps_ref[lv, TT - n_psh:TT, cs_]


def ffn_kernel(layer, x_ref, wup_hbm, wdn_hbm, nw_ref, cw_ref, cb_ref, postn_ref, out_ref,
               wup_ref, wdn_ref, stage_ref, stage_dn_ref, stage_sem, halo_ref, a_ref):
    TT = x_ref.shape[1]
    FT = FFN_FT
    L = CHUNK
    n_tail = (FFN_CONV - 1) * SUBLANES
    i = pl.program_id(1)

    @pl.when((pl.program_id(0) == 0) & (i == 0))
    def _():
        _stage_weights(_col_blocks(wup_hbm.at[layer], 0, wup_ref, 0, 2 * D_FF, STAGE_COLS),
                       stage_ref, stage_sem)
        _stage_weights([(wdn_hbm.at[layer, r:r + STAGE_COLS, :], wdn_ref.at[r:r + STAGE_COLS, :])
                        for r in range(0, D_FF, STAGE_COLS)], stage_dn_ref, stage_sem)

    @pl.when(i == 0)
    def _():
        halo_ref[...] = jnp.zeros(halo_ref.shape, F32)

    x = x_ref[0]
    ms = jnp.mean(x * x, axis=-1, keepdims=True)
    h = (x * lax.rsqrt(ms + EPS) * nw_ref[...]).astype(BF16)

    def conv_cols(col):
        u = _dot(h, wup_ref[:, col:col + FT])
        outs = []
        for c in range(TT // L):
            cur = u[c * L:(c + 1) * L]
            prev_tail = halo_ref[:, col:col + FT] if c == 0 else u[c * L - n_tail:c * L]
            outs.append(_causal_conv(prev_tail, cur, cw_ref, cb_ref[:, col:col + FT], col, FT, FFN_CONV))
        halo_ref[:, col:col + FT] = u[TT - n_tail:TT]
        return outs

    for j in range(D_FF // FT):
        gts = conv_cols(j * FT)
        vals = conv_cols(D_FF + j * FT)
        for c in range(TT // L):
            gt = gts[c]
            gelu = 0.5 * gt * (1.0 + jnp.tanh(math.sqrt(2.0 / math.pi) * (gt + 0.044715 * (gt * gt * gt))))
            a_ref[c * L:(c + 1) * L, j * FT:(j + 1) * FT] = (gelu * vals[c]).astype(BF16)

    f = _dot(a_ref[...], wdn_ref[...])
    ms_f = jnp.mean(f * f, axis=-1, keepdims=True)
    out_ref[0] = x_ref[0] + f * lax.rsqrt(ms_f + EPS) * postn_ref[...]


def _const_spec(shape):
    nd = len(shape)
    return pl.BlockSpec(shape, lambda b, i: (0,) * nd, pipeline_mode=pl.Buffered(1))


def _mix_layer(layer, x, win, wtail, wout, nw, gbias, alog, xcw, xcb, qcw, qcb, poolw, poolb, pools, dskip,
               snorm, mnorm, postn):
    B, T, D = x.shape
    TT = MIX_TT
    weights = (win, wtail, wout)
    consts = (nw, gbias, alog, xcw, xcb, qcw, qcb, poolw, poolb, pools, dskip, snorm, mnorm, postn)
    n_t = T // TT
    cur_spec = pl.BlockSpec((1, TT, D), lambda b, i: (b, jnp.minimum(i, n_t - 1), 0))
    prev_spec = pl.BlockSpec((1, TT, D), lambda b, i: (b, jnp.maximum(i - 1, 0), 0))
    max_conv_tail = (max(SSD_CONV, MLSTM_CONV) - 1) * SUBLANES
    max_pool_tail = (POOL_WINDOWS[-1] // 2) * SUBLANES
    return pl.pallas_call(
        functools.partial(mix_kernel, layer),
        grid=(B, n_t + 1),
        in_specs=([cur_spec, prev_spec] + [pl.BlockSpec(memory_space=pltpu.HBM) for _ in weights]
                  + [_const_spec(c.shape) for c in consts]),
        out_specs=prev_spec,
        out_shape=jax.ShapeDtypeStruct(x.shape, x.dtype),
        scratch_shapes=[
            pltpu.VMEM((D_MODEL, N_IN_COLS), BF16),
            pltpu.VMEM((D_MODEL, D_MODEL), BF16),
            pltpu.VMEM((2, D_MODEL, STAGE_COLS), F32),
            pltpu.SemaphoreType.DMA((2,)),
            pltpu.VMEM((TT, D_MODEL), BF16),
            pltpu.VMEM((2, TT, N_HALO_COLS), F32),
            pltpu.VMEM((2, TT, N_REST_COLS), F32),
            pltpu.VMEM((max_conv_tail, N_HALO_COLS), F32),
            pltpu.VMEM((3, max_pool_tail, D_POOL), F32),
            pltpu.VMEM((TT, D_SSD_XBC + 2 * D_MLSTM), F32),
            pltpu.VMEM((TT, D_MODEL), F32),
            pltpu.VMEM((3, TT, D_POOL), F32),
            pltpu.VMEM((SSD_GROUPS, SSD_STATE, 4 * SSD_HEAD_DIM), F32),
            pltpu.VMEM((MLSTM_HEADS, LANES, LANES), F32),
            pltpu.VMEM((SUBLANES, LANES), F32),
        ],
        compiler_params=pltpu.CompilerParams(
            dimension_semantics=("arbitrary", "arbitrary"), vmem_limit_bytes=VMEM_LIMIT),
        name="mix_layer",
    )(x, x, *weights, *consts)


def _ffn_layer(layer, x, wup, wdn, nw, cw, cb, postn):
    B, T, D = x.shape
    TT = FFN_TT
    weights = (wup, wdn)
    consts = (nw, cw, cb, postn)
    x_spec = pl.BlockSpec((1, TT, D), lambda b, i: (b, i, 0))
    return pl.pallas_call(
        functools.partial(ffn_kernel, layer),
        grid=(B, T // TT),
        in_specs=([x_spec] + [pl.BlockSpec(memory_space=pltpu.HBM) for _ in weights]
                  + [_const_spec(c.shape) for c in consts]),
        out_specs=x_spec,
        out_shape=jax.ShapeDtypeStruct(x.shape, x.dtype),
        scratch_shapes=[
            pltpu.VMEM((D_MODEL, 2 * D_FF), BF16),
            pltpu.VMEM((D_FF, D_MODEL), BF16),
            pltpu.VMEM((2, D_MODEL, STAGE_COLS), F32),
            pltpu.VMEM((2, STAGE_COLS, D_MODEL), F32),
            pltpu.SemaphoreType.DMA((2,)),
            pltpu.VMEM(((FFN_CONV - 1) * SUBLANES, 2 * D_FF), F32),
            pltpu.VMEM((TT, D_FF), BF16),
        ],
        compiler_params=pltpu.CompilerParams(
            dimension_semantics=("arbitrary", "arbitrary"), vmem_limit_bytes=VMEM_LIMIT),
        name="ffn_layer",
    )(x, *weights, *consts)


def _row(v):
    return v.reshape(1, -1).astype(F32)


def _pad_lanes(v):
    return jnp.pad(v.astype(F32), (0, LANES - v.shape[0])).reshape(1, LANES)


def _prep_w_tail(w):
    c_dt = D_POOL + D_SSD + D_SSD_XBC
    c_qk = c_dt + SSD_HEADS
    c_if = c_qk + 4 * D_MLSTM
    pad = jnp.zeros((w.shape[0], LANES - SSD_HEADS - 2 * MLSTM_HEADS), w.dtype)
    return jnp.concatenate([w[:, c_qk:c_if], w[:, c_dt:c_qk], w[:, c_if:], pad], axis=1)


def _prep_pool_w(w):
    out = jnp.zeros((D_POOL, D_POOL), F32)
    for g in range(len(POOL_WINDOWS)):
        s = g * POOL_GROUP_DIM
        out = lax.dynamic_update_slice(out, w[g].astype(F32), (s, s))
    return out.astype(BF16)


def _permute_chunks(x, to_kernel_order):
    B, T, D = x.shape
    a, b = (SUBLANES, VROWS) if to_kernel_order else (VROWS, SUBLANES)
    return x.reshape(B, T // CHUNK, a, b, D).transpose(0, 1, 3, 2, 4).reshape(B, T, D)


def kernel(x, pre_mix_norm, w_in, pool_w, pool_b, pool_scale, ssd_conv_w, ssd_conv_b, ssd_dt_bias, ssd_a_log, ssd_d, ssd_norm, mlstm_conv_w, mlstm_conv_b, mlstm_i_bias, mlstm_f_bias, mlstm_norm, w_out, post_mix_norm, pre_ffn_norm, ffn_w_up, ffn_conv_w, ffn_conv_b, ffn_w_down, post_ffn_norm):
    depth = w_in.shape[0]
    x = _permute_chunks(x, True)
    for l in range(depth):
        gbias = _pad_lanes(jnp.concatenate([ssd_dt_bias[l], mlstm_i_bias[l], mlstm_f_bias[l]]))
        x = _mix_layer(
            l, x, w_in, _prep_w_tail(w_in[l]), w_out, _row(pre_mix_norm[l]), gbias, _pad_lanes(ssd_a_log[l]),
            ssd_conv_w[l].astype(F32), _row(ssd_conv_b[l]), mlstm_conv_w[l].astype(F32), _row(mlstm_conv_b[l]),
            _prep_pool_w(pool_w[l]), _row(pool_b[l]), _row(pool_scale[l]),
            _row(jnp.repeat(ssd_d[l], SSD_HEAD_DIM)), _row(ssd_norm[l]), _row(mlstm_norm[l]),
            _row(post_mix_norm[l]))
        x = _ffn_layer(
            l, x, ffn_w_up, ffn_w_down, _row(pre_ffn_norm[l]), ffn_conv_w[l].astype(F32),
            _row(ffn_conv_b[l]), _row(post_ffn_norm[l]))
    return _permute_chunks(x, False)
```

```python
import functools
import math

import jax
import jax.numpy as jnp
from jax import lax
from jax.experimental import pallas as pl
from jax.experimental.pallas import tpu as pltpu

F32 = jnp.float32
BF16 = jnp.bfloat16

D_MODEL = 1024
EPS = 1e-6

D_POOL = 256
POOL_GROUP_DIM = 64
POOL_WINDOWS = (2, 4, 8, 16)

D_SSD = 512
SSD_HEADS = 8
SSD_HEAD_DIM = 64
SSD_GROUPS = 2
SSD_STATE = 128
SSD_CONV = 4
D_SSD_XBC = D_SSD + 2 * SSD_GROUPS * SSD_STATE

D_MLSTM = 256
MLSTM_HEADS = 4
MLSTM_HEAD_DIM = 64
MLSTM_CONV = 4

D_FF = 2816
FFN_CONV = 3

CHUNK = 128
LANES = 128
SUBLANES = 8
VROWS = CHUNK // SUBLANES

C_POOL = 0
C_XBC = C_POOL + D_POOL
C_QK = C_XBC + D_SSD_XBC
N_HALO_COLS = C_QK + 2 * D_MLSTM
R_Z = 0
R_V = R_Z + D_SSD
R_O = R_V + D_MLSTM
R_G = R_O + D_MLSTM
N_REST_COLS = R_G + LANES
N_IN_COLS = N_HALO_COLS + N_REST_COLS
G_DT = 0
G_I = SSD_HEADS
G_F = G_I + MLSTM_HEADS

MIX_TT = 256
PROJ_COLS = 256
STAGE_COLS = 256
STAGE_SLOTS = 4
FFN_TT = 512
FFN_FT = 256
VMEM_LIMIT = 56 * 1024 * 1024


def _dot(a, b):
    return jnp.dot(a, b, preferred_element_type=F32)


def _dot_nt(a, b):
    return lax.dot_general(a, b, (((1,), (1,)), ((), ())), preferred_element_type=F32)


def _silu(x):
    return x * jax.nn.sigmoid(x)


def _split3(a):
    hi = a.astype(BF16)
    r = a - hi.astype(F32)
    mid = r.astype(BF16)
    lo = (r - mid.astype(F32)).astype(BF16)
    return hi, mid, lo


def _bcast_lane(a, j, shape):
    return jnp.broadcast_to(a[:, j:j + 1], shape)


def _pair_expand(a, h_even, shape, lo_half):
    return jnp.where(lo_half, _bcast_lane(a, h_even, shape), _bcast_lane(a, h_even + 1, shape))


def _stage_weights(blocks, stage_ref, sem_ref):
    n_slots = stage_ref.shape[0]

    def copy(n):
        src, dst = blocks[n]
        rows, cols = src.shape
        return pltpu.make_async_copy(src, stage_ref.at[n % n_slots, 0:rows, 0:cols], sem_ref.at[n % n_slots])

    for n in range(min(n_slots - 1, len(blocks))):
        copy(n).start()
    for n, (src, dst) in enumerate(blocks):
        if n + n_slots - 1 < len(blocks):
            copy(n + n_slots - 1).start()
        copy(n).wait()
        rows, cols = src.shape
        dst[...] = stage_ref[n % n_slots, 0:rows, 0:cols].astype(BF16)


def _col_blocks(src_ref, src0, dst_ref, dst0, ncols, step):
    return [(src_ref.at[:, src0 + k:src0 + min(k + step, ncols)],
             dst_ref.at[:, dst0 + k:dst0 + min(k + step, ncols)]) for k in range(0, ncols, step)]


def _tau(p):
    return (p % SUBLANES) * VROWS + p // SUBLANES


def _ext_rows(prev_tail, cur_tail):
    n = cur_tail.shape[0] // SUBLANES
    sub0 = lax.broadcasted_iota(jnp.int32, (SUBLANES, cur_tail.shape[1]), 0) == 0
    out = []
    for j in range(n):
        sl = slice(j * SUBLANES, (j + 1) * SUBLANES)
        out.append(jnp.where(sub0, pltpu.roll(prev_tail[sl], 1, 0), pltpu.roll(cur_tail[sl], 1, 0)))
    return out


def _shifted(ext, cur, k):
    if k == 0:
        return cur
    return jnp.concatenate(ext[len(ext) - k:] + [cur[0:CHUNK - SUBLANES * k]], axis=0)


def _causal_conv(prev_tail, cur, w_ref, b_row, wcol, ncols, k_taps):
    n = k_taps - 1
    ext = _ext_rows(prev_tail, cur[CHUNK - n * SUBLANES:CHUNK])
    acc = b_row
    for k in range(k_taps):
        acc = acc + _shifted(ext, cur, n - k) * w_ref[k:k + 1, wcol:wcol + ncols]
    return acc


def mix_kernel(layer, xc_ref, xp_ref, wout_hbm, win_ref, nw_ref, gbias_ref, alog_ref,
               xcw_ref, xcb_ref, qcw_ref, qcb_ref,
               poolw_ref, poolb_ref, pools_ref, dskip_ref, snorm_ref, mnorm_ref, postn_ref,
               out_ref, wout_ref, stage_ref, stage_sem,
               h_ref, hp_ref, rest_ref, hph_ref, psh_ref, act_ref, mix_ref, ps_ref,
               sstate_ref, mstate_ref, mm_ref):
    TT = xc_ref.shape[1]
    i = pl.program_id(1)

    @pl.when((pl.program_id(0) == 0) & (i == 0))
    def _():
        _stage_weights(_col_blocks(wout_hbm.at[layer], 0, wout_ref, 0, D_MODEL, STAGE_COLS),
                       stage_ref, stage_sem)

    @pl.when(i == 0)
    def _():
        hp_ref[1] = jnp.zeros(hp_ref.shape[1:], F32)
        rest_ref[1] = jnp.zeros(rest_ref.shape[1:], F32)

    @pl.when(i <= 1)
    def _():
        hph_ref[...] = jnp.zeros(hph_ref.shape, F32)
        psh_ref[...] = jnp.zeros(psh_ref.shape, F32)
        sstate_ref[...] = jnp.zeros(sstate_ref.shape, F32)
        mstate_ref[...] = jnp.zeros(mstate_ref.shape, F32)
        mm_ref[...] = jnp.zeros(mm_ref.shape, F32)

    def step(slot_proj, slot_mix):
        x = xc_ref[0]
        ms = jnp.mean(x * x, axis=-1, keepdims=True)
        h_ref[...] = (x * lax.rsqrt(ms + EPS) * nw_ref[...]).astype(BF16)

        def proj_piece(dst_ref, c0, c1, w0):
            def piece():
                dst_ref[slot_proj, :, c0:c1] = _dot(h_ref[...], win_ref[:, w0 + c0:w0 + c1])
            return piece

        pending = [proj_piece(hp_ref, c0, min(c0 + PROJ_COLS, N_HALO_COLS), 0)
                   for c0 in range(0, N_HALO_COLS, PROJ_COLS)]
        pending += [proj_piece(rest_ref, c0, min(c0 + PROJ_COLS, N_REST_COLS), N_HALO_COLS)
                    for c0 in range(0, N_REST_COLS, PROJ_COLS)]
        _mixers(jnp.maximum(i - 1, 0), xp_ref, hp_ref.at[slot_mix], rest_ref.at[slot_mix], gbias_ref, alog_ref,
                xcw_ref, xcb_ref, qcw_ref, qcb_ref, poolw_ref, poolb_ref, pools_ref, dskip_ref, snorm_ref,
                mnorm_ref, wout_ref, postn_ref, out_ref, hph_ref, psh_ref, act_ref, mix_ref, ps_ref,
                sstate_ref, mstate_ref, mm_ref, pending)

    @pl.when(i % 2 == 0)
    def _():
        step(0, 1)

    @pl.when(i % 2 == 1)
    def _():
        step(1, 0)


def _mixers(tile, x_ref, hp_ref, rest_ref, gbias_ref, alog_ref, xcw_ref, xcb_ref, qcw_ref, qcb_ref,
            poolw_ref, poolb_ref, pools_ref, dskip_ref, snorm_ref, mnorm_ref, wout_ref, postn_ref,
            out_ref, hph_ref, psh_ref, act_ref, mix_ref, ps_ref, sstate_ref, mstate_ref, mm_ref, pending):
    def between():
        if pending:
            pending.pop(0)()

    TT = x_ref.shape[1]
    L = CHUNK
    n_hph = hph_ref.shape[0]
    n_psh = psh_ref.shape[1]
    ps_carried = ((0, 0), (0, 1), (1, 1), (2, 1))

    def tail(cur_ref, halo, n_halo, r0, n_rows, cols):
        if r0 == 0:
            return halo[n_halo - n_rows:n_halo, cols]
        return cur_ref[r0 - n_rows:r0, cols]

    lane = lax.broadcasted_iota(jnp.int32, (L, LANES), 1)
    row = lax.broadcasted_iota(jnp.int32, (L, LANES), 0)
    lo_half = lane < 64
    tau_row = _tau(row)
    causal = _tau(lane) <= tau_row
    tril = jnp.where(causal, 1.0, 0.0).astype(BF16)
    lane_row = lax.broadcasted_iota(jnp.int32, (1, LANES), 1)
    a_row = -jnp.exp(alog_ref[...])
    neg_inf = -jnp.inf
    win_blk = [jnp.where(lo_half, float(POOL_WINDOWS[2 * b]), float(POOL_WINDOWS[2 * b + 1])) for b in range(2)]
    tau_f = tau_row.astype(F32)

    def chunk_stages(c):
        r0 = c * L

        def conv_block(col, cw_ref, cb_ref, wcol, k_taps):
            cols = slice(col, col + LANES)
            cur = hp_ref[r0:r0 + L, cols]
            prev_tail = tail(hp_ref, hph_ref, n_hph, r0, (k_taps - 1) * SUBLANES, cols)
            return _silu(_causal_conv(prev_tail, cur, cw_ref, cb_ref[:, wcol:wcol + LANES], wcol, LANES, k_taps))

        for blk in range(D_SSD_XBC // LANES):
            act_ref[r0:r0 + L, blk * LANES:(blk + 1) * LANES] = conv_block(
                C_XBC + blk * LANES, xcw_ref, xcb_ref, blk * LANES, SSD_CONV)
        for blk in range(2 * D_MLSTM // LANES):
            act_ref[r0:r0 + L, D_SSD_XBC + blk * LANES:D_SSD_XBC + (blk + 1) * LANES] = conv_block(
                C_QK + blk * LANES, qcw_ref, qcb_ref, blk * LANES, MLSTM_CONV)
        yield

        pos = tau_f + (tile * TT + r0 + 1).astype(F32)
        pooled_blocks = []
        for b in range(2):
            cs_ = slice(b * LANES, (b + 1) * LANES)
            u_cur = hp_ref[r0:r0 + L, cs_]
            lvl = u_cur
            sums = []
            for li, sh in enumerate((1, 2, 4, 8)):
                if li == 0:
                    prev_tail = tail(hp_ref, hph_ref, n_hph, r0, sh * SUBLANES, cs_)
                else:
                    prev_tail = tail(ps_ref.at[li - 1], psh_ref.at[li - 1], n_psh, r0, sh * SUBLANES, cs_)
                ext = _ext_rows(prev_tail, lvl[L - sh * SUBLANES:L])
                lvl = lvl + _shifted(ext, lvl, sh)
                sums.append(lvl)
                if (li, b) in ps_carried:
                    ps_ref[li, r0:r0 + L, cs_] = lvl
                if b == 0 and li == 1:
                    break
            wsum = jnp.where(lo_half, sums[0], sums[1]) if b == 0 else jnp.where(lo_half, sums[2], sums[3])
            pooled_blocks.append((wsum / jnp.minimum(pos, win_blk[b]) - u_cur).astype(BF16))
        mix_ref[r0:r0 + L, 0:D_POOL] = (
            (_dot(jnp.concatenate(pooled_blocks, axis=1), poolw_ref[...]) + poolb_ref[...]) * pools_ref[...])
        yield

        gb = rest_ref[r0:r0 + L, R_G:R_G + LANES] + gbias_ref[...]
        sp_term = jnp.log1p(jnp.exp(-jnp.abs(gb)))
        dt = jnp.maximum(gb, 0.0) + sp_term
        log_f = jnp.minimum(gb, 0.0) - sp_term
        is_dt = lane < G_I
        is_f = (lane >= G_F) & (lane < G_F + MLSTM_HEADS)
        v_cum = jnp.where(is_dt, dt * a_row, jnp.where(is_f, log_f, 0.0))
        hi, mid, lo = _split3(v_cum)
        cs = _dot(tril, hi) + _dot(tril, mid) + _dot(tril, lo)
        u_gate = jnp.where(is_dt, dt, gb)
        cs_t = cs.T
        ug_t = u_gate.T
        cs_last = cs[L - 1:L, :]
        e_col = jnp.exp(cs)
        w_col = jnp.exp(cs_last - cs) * dt
        e_last = jnp.exp(cs_last)
        yield

        for g in range(SSD_GROUPS):
            b_g = act_ref[r0:r0 + L, D_SSD + g * SSD_STATE:D_SSD + (g + 1) * SSD_STATE].astype(BF16)
            c_g = act_ref[r0:r0 + L, D_SSD + (SSD_GROUPS + g) * SSD_STATE:
                          D_SSD + (SSD_GROUPS + g + 1) * SSD_STATE].astype(BF16)
            s_g = _dot_nt(c_g, b_g)
            state_g = sstate_ref[g]
            y_off = _dot(c_g, state_g.astype(BF16))
            xd_blocks = []
            cd_blocks = []
            for pr in range(2):
                h_even = 4 * g + 2 * pr
                col = h_even * SSD_HEAD_DIM
                xs = act_ref[r0:r0 + L, col:col + LANES]
                xs_b = xs.astype(BF16)
                yd = []
                for hh in range(2):
                    hd = h_even + hh
                    seg = jnp.where(causal, cs[:, hd:hd + 1] - cs_t[hd:hd + 1, :], neg_inf)
                    m_h = (s_g * (jnp.exp(seg) * ug_t[hd:hd + 1, :])).astype(BF16)
                    yd.append(_dot(m_h, xs_b))
                y_diag = jnp.where(lo_half, yd[0], yd[1])
                e_exp = _pair_expand(e_col, h_even, (L, LANES), lo_half)
                w_exp = _pair_expand(w_col, h_even, (L, LANES), lo_half)
                y = (y_diag + y_off[:, pr * LANES:(pr + 1) * LANES] * e_exp
                     + xs * dskip_ref[:, col:col + LANES])
                z = rest_ref[r0:r0 + L, R_Z + col:R_Z + col + LANES]
                mix_ref[r0:r0 + L, D_POOL + col:D_POOL + col + LANES] = y * _silu(z)
                xd_blocks.append((xs * w_exp).astype(BF16))
                cd_blocks.append(_pair_expand(e_last, h_even, (1, LANES), lane_row < 64))
            xd_g = jnp.concatenate(xd_blocks, axis=1)
            cd_g = jnp.concatenate(cd_blocks, axis=1)
            new_states = lax.dot_general(b_g, xd_g, (((0,), (0,)), ((), ())),
                                         preferred_element_type=F32)
            sstate_ref[g] = state_g * cd_g + new_states
            yield
        y_all = mix_ref[r0:r0 + L, D_POOL:D_POOL + D_SSD]
        ms_y = jnp.mean(y_all * y_all, axis=-1, keepdims=True)
        mix_ref[r0:r0 + L, D_POOL:D_POOL + D_SSD] = y_all * lax.rsqrt(ms_y + EPS) * snorm_ref[...]

        for pr in range(MLSTM_HEADS // 2):
            qcol = D_SSD_XBC + pr * LANES
            kcol = D_SSD_XBC + D_MLSTM + pr * LANES
            q_b = act_ref[r0:r0 + L, qcol:qcol + LANES] * (MLSTM_HEAD_DIM ** -0.5)
            k_t = act_ref[r0:r0 + L, kcol:kcol + LANES].T
            k_tb = k_t.astype(BF16)
            v_b = rest_ref[r0:r0 + L, R_V + pr * LANES:R_V + (pr + 1) * LANES]
            o_b = rest_ref[r0:r0 + L, R_O + pr * LANES:R_O + (pr + 1) * LANES]
            hv = []
            for hh in range(2):
                hd = 2 * pr + hh
                in_half = lo_half if hh == 0 else jnp.logical_not(lo_half)
                row_in_half = (row < 64) if hh == 0 else (row >= 64)
                ones_lane = (lane == 64) if hh == 0 else (lane == 0)
                ol = 64 if hh == 0 else 0
                qm = jnp.where(in_half, q_b, 0.0).astype(BF16)
                s = _dot(qm, k_tb)
                b_row = cs_t[G_F + hd:G_F + hd + 1, :]
                r_row = ug_t[G_I + hd:G_I + hd + 1, :] - b_row
                b_last = jnp.sum(jnp.where(lane_row == L - 1, b_row, 0.0), axis=-1, keepdims=True)
                al_row = b_last + r_row
                m_loc = jnp.max(al_row, axis=-1, keepdims=True)
                prev_m = mm_ref[hd:hd + 1, 0:1]
                rmask = jnp.where(causal, r_row, neg_inf)
                g_col = jnp.maximum(jnp.max(rmask, axis=-1, keepdims=True), prev_m)
                p = (s * jnp.exp(rmask - g_col)).astype(BF16)
                v_ext = jnp.where(in_half, v_b, jnp.where(ones_lane, 1.0, 0.0)).astype(BF16)
                cn = mstate_ref[hd]
                res = _dot(p, v_ext) + _dot(qm, cn.astype(BF16)) * jnp.exp(prev_m - g_col)
                den = jnp.maximum(jnp.abs(res[:, ol:ol + 1]),
                                  jnp.exp(-(cs[:, G_F + hd:G_F + hd + 1] + g_col)))
                hv.append(res / den)
                w_row = jnp.exp(al_row - m_loc)
                ktw = jnp.where(row_in_half, k_t * w_row, 0.0).astype(BF16)
                c_loc = _dot(ktw, v_ext)
                m_new = jnp.maximum(b_last + prev_m, m_loc)
                mstate_ref[hd] = (jnp.exp(b_last + prev_m - m_new) * cn
                                  + jnp.exp(m_loc - m_new) * c_loc)
                mm_ref[hd:hd + 1, :] = jnp.broadcast_to(m_new, (1, LANES))
            hcat = jax.nn.sigmoid(o_b) * jnp.where(lo_half, hv[0], hv[1])
            sq = hcat * hcat
            ss_lo = jnp.sum(jnp.where(lo_half, sq, 0.0), axis=-1, keepdims=True)
            ss_hi = jnp.sum(jnp.where(lo_half, 0.0, sq), axis=-1, keepdims=True)
            inv = jnp.where(lo_half, lax.rsqrt(ss_lo * (1.0 / MLSTM_HEAD_DIM) + EPS),
                            lax.rsqrt(ss_hi * (1.0 / MLSTM_HEAD_DIM) + EPS))
            mcol = D_POOL + D_SSD + pr * LANES
            mix_ref[r0:r0 + L, mcol:mcol + LANES] = hcat * inv * mnorm_ref[:, pr * LANES:(pr + 1) * LANES]
            if pr + 1 < MLSTM_HEADS // 2:
                yield

        def out_piece():
            o = _dot(mix_ref[r0:r0 + L, :].astype(BF16), wout_ref[...])
            ms_o = jnp.mean(o * o, axis=-1, keepdims=True)
            out_ref[0, r0:r0 + L, :] = x_ref[0, r0:r0 + L, :] + o * lax.rsqrt(ms_o + EPS) * postn_ref[...]

        pending.insert(0, out_piece)
        yield

    n_stages = 3 + SSD_GROUPS + MLSTM_HEADS // 2
    gens = [chunk_stages(c) for c in range(TT // L)]
    for _ in range(n_stages):
        for gen in gens:
            next(gen)
            between()
    while pending:
        pending.pop(0)()

    hph_ref[...] = hp_ref[TT - n_hph:TT, :]
    for lv, b in ps_carried:
        cs_ = slice(b * LANES, (b + 1) * LANES)
        psh_ref[lv, :, cs_] = ps_ref[lv, TT - n_psh:TT, cs_]


def ffn_kernel(layer, x_ref, wup_hbm, wdn_hbm, nw_ref, cw_ref, cb_ref, postn_ref, out_ref,
               wup_ref, wdn_ref, stage_ref, stage_dn_ref, stage_sem, halo_ref, a_ref):
    TT = x_ref.shape[1]
    FT = FFN_FT
    L = CHUNK
    n_tail = (FFN_CONV - 1) * SUBLANES
    i = pl.program_id(1)

    @pl.when((pl.program_id(0) == 0) & (i == 0))
    def _():
        _stage_weights(_col_blocks(wup_hbm.at[layer], 0, wup_ref, 0, 2 * D_FF, STAGE_COLS),
                       stage_ref, stage_sem)
        _stage_weights([(wdn_hbm.at[layer, r:r + STAGE_COLS, :], wdn_ref.at[r:r + STAGE_COLS, :])
                        for r in range(0, D_FF, STAGE_COLS)], stage_dn_ref, stage_sem)

    @pl.when(i == 0)
    def _():
        halo_ref[...] = jnp.zeros(halo_ref.shape, F32)

    x = x_ref[0]
    ms = jnp.mean(x * x, axis=-1, keepdims=True)
    h = (x * lax.rsqrt(ms + EPS) * nw_ref[...]).astype(BF16)

    def conv_cols(col):
        u = _dot(h, wup_ref[:, col:col + FT])
        outs = []
        for c in range(TT // L):
            cur = u[c * L:(c + 1) * L]
            prev_tail = halo_ref[:, col:col + FT] if c == 0 else u[c * L - n_tail:c * L]
            outs.append(_causal_conv(prev_tail, cur, cw_ref, cb_ref[:, col:col + FT], col, FT, FFN_CONV))
        halo_ref[:, col:col + FT] = u[TT - n_tail:TT]
        return outs

    for j in range(D_FF // FT):
        gts = conv_cols(j * FT)
        vals = conv_cols(D_FF + j * FT)
        for c in range(TT // L):
            gt = gts[c]
            gelu = 0.5 * gt * (1.0 + jnp.tanh(math.sqrt(2.0 / math.pi) * (gt + 0.044715 * (gt * gt * gt))))
            a_ref[c * L:(c + 1) * L, j * FT:(j + 1) * FT] = (gelu * vals[c]).astype(BF16)

    f = _dot(a_ref[...], wdn_ref[...])
    ms_f = jnp.mean(f * f, axis=-1, keepdims=True)
    out_ref[0] = x_ref[0] + f * lax.rsqrt(ms_f + EPS) * postn_ref[...]


def _const_spec(shape):
    nd = len(shape)
    return pl.BlockSpec(shape, lambda b, i: (0,) * nd, pipeline_mode=pl.Buffered(1))


def _mix_layer(layer, x, wout, win, nw, gbias, alog, xcw, xcb, qcw, qcb, poolw, poolb, pools, dskip,
               snorm, mnorm, postn):
    B, T, D = x.shape
    TT = MIX_TT
    weights = (wout,)
    consts = (win, nw, gbias, alog, xcw, xcb, qcw, qcb, poolw, poolb, pools, dskip, snorm, mnorm, postn)
    n_t = T // TT
    cur_spec = pl.BlockSpec((1, TT, D), lambda b, i: (b, jnp.minimum(i, n_t - 1), 0))
    prev_spec = pl.BlockSpec((1, TT, D), lambda b, i: (b, jnp.maximum(i - 1, 0), 0))
    max_conv_tail = (max(SSD_CONV, MLSTM_CONV) - 1) * SUBLANES
    max_pool_tail = (POOL_WINDOWS[-1] // 2) * SUBLANES
    return pl.pallas_call(
        functools.partial(mix_kernel, layer),
        grid=(B, n_t + 1),
        in_specs=([cur_spec, prev_spec] + [pl.BlockSpec(memory_space=pltpu.HBM) for _ in weights]
                  + [_const_spec(c.shape) for c in consts]),
        out_specs=prev_spec,
        out_shape=jax.ShapeDtypeStruct(x.shape, x.dtype),
        scratch_shapes=[
            pltpu.VMEM((D_MODEL, D_MODEL), BF16),
            pltpu.VMEM((STAGE_SLOTS, D_MODEL, STAGE_COLS), F32),
            pltpu.SemaphoreType.DMA((STAGE_SLOTS,)),
            pltpu.VMEM((TT, D_MODEL), BF16),
            pltpu.VMEM((2, TT, N_HALO_COLS), F32),
            pltpu.VMEM((2, TT, N_REST_COLS), F32),
            pltpu.VMEM((max_conv_tail, N_HALO_COLS), F32),
            pltpu.VMEM((3, max_pool_tail, D_POOL), F32),
            pltpu.VMEM((TT, D_SSD_XBC + 2 * D_MLSTM), F32),
            pltpu.VMEM((TT, D_MODEL), F32),
            pltpu.VMEM((3, TT, D_POOL), F32),
            pltpu.VMEM((SSD_GROUPS, SSD_STATE, 4 * SSD_HEAD_DIM), F32),
            pltpu.VMEM((MLSTM_HEADS, LANES, LANES), F32),
            pltpu.VMEM((SUBLANES, LANES), F32),
        ],
        compiler_params=pltpu.CompilerParams(
            dimension_semantics=("arbitrary", "arbitrary"), vmem_limit_bytes=VMEM_LIMIT),
        name="mix_layer",
    )(x, x, *weights, *consts)


def _ffn_layer(layer, x, wup, wdn, nw, cw, cb, postn):
    B, T, D = x.shape
    TT = FFN_TT
    weights = (wup, wdn)
    consts = (nw, cw, cb, postn)
    x_spec = pl.BlockSpec((1, TT, D), lambda b, i: (b, i, 0))
    return pl.pallas_call(
        functools.partial(ffn_kernel, layer),
        grid=(B, T // TT),
        in_specs=([x_spec] + [pl.BlockSpec(memory_space=pltpu.HBM) for _ in weights]
                  + [_const_spec(c.shape) for c in consts]),
        out_specs=x_spec,
        out_shape=jax.ShapeDtypeStruct(x.shape, x.dtype),
        scratch_shapes=[
            pltpu.VMEM((D_MODEL, 2 * D_FF), BF16),
            pltpu.VMEM((D_FF, D_MODEL), BF16),
            pltpu.VMEM((STAGE_SLOTS, D_MODEL, STAGE_COLS), F32),
            pltpu.VMEM((STAGE_SLOTS, STAGE_COLS, D_MODEL), F32),
            pltpu.SemaphoreType.DMA((STAGE_SLOTS,)),
            pltpu.VMEM(((FFN_CONV - 1) * SUBLANES, 2 * D_FF), F32),
            pltpu.VMEM((TT, D_FF), BF16),
        ],
        compiler_params=pltpu.CompilerParams(
            dimension_semantics=("arbitrary", "arbitrary"), vmem_limit_bytes=VMEM_LIMIT),
        name="ffn_layer",
    )(x, *weights, *consts)


def _row(v):
    return v.reshape(1, -1).astype(F32)


def _pad_lanes(v):
    return jnp.pad(v.astype(F32), (0, LANES - v.shape[0])).reshape(1, LANES)


def _prep_w_in(w):
    sizes = (D_POOL, D_SSD, D_SSD_XBC, SSD_HEADS, 2 * D_MLSTM, D_MLSTM, D_MLSTM, MLSTM_HEADS, MLSTM_HEADS)
    offs = [0]
    for s in sizes:
        offs.append(offs[-1] + s)
    u_pool, z, xbc, dt, qk, v, o, ig, fg = [w[:, offs[k]:offs[k + 1]] for k in range(len(sizes))]
    pad = jnp.zeros((w.shape[0], LANES - SSD_HEADS - 2 * MLSTM_HEADS), w.dtype)
    return jnp.concatenate([u_pool, xbc, qk, z, v, o, dt, ig, fg, pad], axis=1).astype(BF16)


def _prep_pool_w(w):
    out = jnp.zeros((D_POOL, D_POOL), F32)
    for g in range(len(POOL_WINDOWS)):
        s = g * POOL_GROUP_DIM
        out = lax.dynamic_update_slice(out, w[g].astype(F32), (s, s))
    return out.astype(BF16)


def _permute_chunks(x, to_kernel_order):
    B, T, D = x.shape
    a, b = (SUBLANES, VROWS) if to_kernel_order else (VROWS, SUBLANES)
    return x.reshape(B, T // CHUNK, a, b, D).transpose(0, 1, 3, 2, 4).reshape(B, T, D)


def kernel(x, pre_mix_norm, w_in, pool_w, pool_b, pool_scale, ssd_conv_w, ssd_conv_b, ssd_dt_bias, ssd_a_log, ssd_d, ssd_norm, mlstm_conv_w, mlstm_conv_b, mlstm_i_bias, mlstm_f_bias, mlstm_norm, w_out, post_mix_norm, pre_ffn_norm, ffn_w_up, ffn_conv_w, ffn_conv_b, ffn_w_down, post_ffn_norm):
    depth = w_in.shape[0]
    x = _permute_chunks(x, True)
    for l in range(depth):
        gbias = _pad_lanes(jnp.concatenate([ssd_dt_bias[l], mlstm_i_bias[l], mlstm_f_bias[l]]))
        x = _mix_layer(
            l, x, w_out, _prep_w_in(w_in[l]), _row(pre_mix_norm[l]), gbias, _pad_lanes(ssd_a_log[l]),
            ssd_conv_w[l].astype(F32), _row(ssd_conv_b[l]), mlstm_conv_w[l].astype(F32), _row(mlstm_conv_b[l]),
            _prep_pool_w(pool_w[l]), _row(pool_b[l]), _row(pool_scale[l]),
            _row(jnp.repeat(ssd_d[l], SSD_HEAD_DIM)), _row(ssd_norm[l]), _row(mlstm_norm[l]),
            _row(post_mix_norm[l]))
        x = _ffn_layer(
            l, x, ffn_w_up, ffn_w_down, _row(pre_ffn_norm[l]), ffn_conv_w[l].astype(F32),
            _row(ffn_conv_b[l]), _row(post_ffn_norm[l]))
    return _permute_chunks(x, False)
```

```python
import functools
import math

import jax
import jax.numpy as jnp
from jax import lax
from jax.experimental import pallas as pl
from jax.experimental.pallas import tpu as pltpu

F32 = jnp.float32
BF16 = jnp.bfloat16

D_MODEL = 1024
EPS = 1e-6

D_POOL = 256
POOL_GROUP_DIM = 64
POOL_WINDOWS = (2, 4, 8, 16)

D_SSD = 512
SSD_HEADS = 8
SSD_HEAD_DIM = 64
SSD_GROUPS = 2
SSD_STATE = 128
SSD_CONV = 4
D_SSD_XBC = D_SSD + 2 * SSD_GROUPS * SSD_STATE

D_MLSTM = 256
MLSTM_HEADS = 4
MLSTM_HEAD_DIM = 64
MLSTM_CONV = 4

D_FF = 2816
FFN_CONV = 3

CHUNK = 128
LANES = 128
SUBLANES = 8
VROWS = CHUNK // SUBLANES

C_POOL = 0
C_XBC = C_POOL + D_POOL
C_QK = C_XBC + D_SSD_XBC
N_HALO_COLS = C_QK + 2 * D_MLSTM
R_Z = 0
R_V = R_Z + D_SSD
R_O = R_V + D_MLSTM
R_G = R_O + D_MLSTM
N_REST_COLS = R_G + LANES
N_IN_COLS = N_HALO_COLS + N_REST_COLS
G_DT = 0
G_I = SSD_HEADS
G_F = G_I + MLSTM_HEADS

MIX_TT = 256
PROJ_COLS = 256
STAGE_COLS = 256
STAGE_SLOTS = 4
FFN_TT = 512
FFN_FT = 256
VMEM_LIMIT = 56 * 1024 * 1024


def _dot(a, b):
    return jnp.dot(a, b, preferred_element_type=F32)


def _dot_nt(a, b):
    return lax.dot_general(a, b, (((1,), (1,)), ((), ())), preferred_element_type=F32)


def _silu(x):
    return x * jax.nn.sigmoid(x)


def _split3(a):
    hi = a.astype(BF16)
    r = a - hi.astype(F32)
    mid = r.astype(BF16)
    lo = (r - mid.astype(F32)).astype(BF16)
    return hi, mid, lo


def _bcast_lane(a, j, shape):
    return jnp.broadcast_to(a[:, j:j + 1], shape)


def _pair_expand(a, h_even, shape, lo_half):
    return jnp.where(lo_half, _bcast_lane(a, h_even, shape), _bcast_lane(a, h_even + 1, shape))


def _stage_weights(blocks, stage_ref, sem_ref):
    n_slots = stage_ref.shape[0]

    def copy(n):
        src, dst = blocks[n]
        rows, cols = src.shape
        return pltpu.make_async_copy(src, stage_ref.at[n % n_slots, 0:rows, 0:cols], sem_ref.at[n % n_slots])

    for n in range(min(n_slots - 1, len(blocks))):
        copy(n).start()
    for n, (src, dst) in enumerate(blocks):
        if n + n_slots - 1 < len(blocks):
            copy(n + n_slots - 1).start()
        copy(n).wait()
        rows, cols = src.shape
        dst[...] = stage_ref[n % n_slots, 0:rows, 0:cols].astype(BF16)


def _col_blocks(src_ref, src0, dst_ref, dst0, ncols, step):
    return [(src_ref.at[:, src0 + k:src0 + min(k + step, ncols)],
             dst_ref.at[:, dst0 + k:dst0 + min(k + step, ncols)]) for k in range(0, ncols, step)]


def _tau(p):
    return (p % SUBLANES) * VROWS + p // SUBLANES


def _ext_rows(prev_tail, cur_tail):
    n = cur_tail.shape[0] // SUBLANES
    sub0 = lax.broadcasted_iota(jnp.int32, (SUBLANES, cur_tail.shape[1]), 0) == 0
    out = []
    for j in range(n):
        sl = slice(j * SUBLANES, (j + 1) * SUBLANES)
        out.append(jnp.where(sub0, pltpu.roll(prev_tail[sl], 1, 0), pltpu.roll(cur_tail[sl], 1, 0)))
    return out


def _shifted(ext, cur, k):
    if k == 0:
        return cur
    return jnp.concatenate(ext[len(ext) - k:] + [cur[0:CHUNK - SUBLANES * k]], axis=0)


def _causal_conv(prev_tail, cur, w_ref, b_row, wcol, ncols, k_taps):
    n = k_taps - 1
    ext = _ext_rows(prev_tail, cur[CHUNK - n * SUBLANES:CHUNK])
    acc = b_row
    for k in range(k_taps):
        acc = acc + _shifted(ext, cur, n - k) * w_ref[k:k + 1, wcol:wcol + ncols]
    return acc


def mix_kernel(layer, xc_ref, xp_ref, wout_hbm, win_ref, nw_ref, gbias_ref, alog_ref,
               xcw_ref, xcb_ref, qcw_ref, qcb_ref,
               poolw_ref, poolb_ref, pools_ref, dskip_ref, snorm_ref, mnorm_ref, postn_ref,
               out_ref, wout_ref, stage_ref, stage_sem,
               h_ref, hp_ref, rest_ref, hph_ref, psh_ref, act_ref, mix_ref, ps_ref,
               sstate_ref, mstate_ref, mm_ref):
    TT = xc_ref.shape[1]
    i = pl.program_id(1)

    @pl.when((pl.program_id(0) == 0) & (i == 0))
    def _():
        _stage_weights(_col_blocks(wout_hbm.at[layer], 0, wout_ref, 0, D_MODEL, STAGE_COLS),
                       stage_ref, stage_sem)

    @pl.when(i == 0)
    def _():
        hp_ref[1] = jnp.zeros(hp_ref.shape[1:], F32)
        rest_ref[1] = jnp.zeros(rest_ref.shape[1:], F32)

    @pl.when(i <= 1)
    def _():
        hph_ref[...] = jnp.zeros(hph_ref.shape, F32)
        psh_ref[...] = jnp.zeros(psh_ref.shape, F32)
        sstate_ref[...] = jnp.zeros(sstate_ref.shape, F32)
        mstate_ref[...] = jnp.zeros(mstate_ref.shape, F32)
        mm_ref[...] = jnp.zeros(mm_ref.shape, F32)

    def step(slot_proj, slot_mix):
        x = xc_ref[0]
        ms = jnp.mean(x * x, axis=-1, keepdims=True)
        h_ref[...] = (x * lax.rsqrt(ms + EPS) * nw_ref[...]).astype(BF16)

        def proj_piece(dst_ref, c0, c1, w0):
            def piece():
                dst_ref[slot_proj, :, c0:c1] = _dot(h_ref[...], win_ref[:, w0 + c0:w0 + c1])
            return piece

        pending = [proj_piece(hp_ref, c0, min(c0 + PROJ_COLS, N_HALO_COLS), 0)
                   for c0 in range(0, N_HALO_COLS, PROJ_COLS)]
        pending += [proj_piece(rest_ref, c0, min(c0 + PROJ_COLS, N_REST_COLS), N_HALO_COLS)
                    for c0 in range(0, N_REST_COLS, PROJ_COLS)]
        _mixers(jnp.maximum(i - 1, 0), xp_ref, hp_ref.at[slot_mix], rest_ref.at[slot_mix], gbias_ref, alog_ref,
                xcw_ref, xcb_ref, qcw_ref, qcb_ref, poolw_ref, poolb_ref, pools_ref, dskip_ref, snorm_ref,
                mnorm_ref, wout_ref, postn_ref, out_ref, hph_ref, psh_ref, act_ref, mix_ref, ps_ref,
                sstate_ref, mstate_ref, mm_ref, pending)

    @pl.when(i % 2 == 0)
    def _():
        step(0, 1)

    @pl.when(i % 2 == 1)
    def _():
        step(1, 0)


def _mixers(tile, x_ref, hp_ref, rest_ref, gbias_ref, alog_ref, xcw_ref, xcb_ref, qcw_ref, qcb_ref,
            poolw_ref, poolb_ref, pools_ref, dskip_ref, snorm_ref, mnorm_ref, wout_ref, postn_ref,
            out_ref, hph_ref, psh_ref, act_ref, mix_ref, ps_ref, sstate_ref, mstate_ref, mm_ref, pending):
    def between():
        if pending:
            pending.pop(0)()

    TT = x_ref.shape[1]
    L = CHUNK
    n_hph = hph_ref.shape[0]
    n_psh = psh_ref.shape[1]
    ps_carried = ((0, 0), (0, 1), (1, 1), (2, 1))

    def tail(cur_ref, halo, n_halo, r0, n_rows, cols):
        if r0 == 0:
            return halo[n_halo - n_rows:n_halo, cols]
        return cur_ref[r0 - n_rows:r0, cols]

    lane = lax.broadcasted_iota(jnp.int32, (L, LANES), 1)
    row = lax.broadcasted_iota(jnp.int32, (L, LANES), 0)
    lo_half = lane < 64
    tau_row = _tau(row)
    causal = _tau(lane) <= tau_row
    tril = jnp.where(causal, 1.0, 0.0).astype(BF16)
    lane_row = lax.broadcasted_iota(jnp.int32, (1, LANES), 1)
    a_row = -jnp.exp(alog_ref[...])
    neg_inf = -jnp.inf
    win_blk = [jnp.where(lo_half, float(POOL_WINDOWS[2 * b]), float(POOL_WINDOWS[2 * b + 1])) for b in range(2)]
    tau_f = tau_row.astype(F32)

    def chunk_stages(c):
        r0 = c * L

        def conv_block(col, cw_ref, cb_ref, wcol, k_taps):
            cols = slice(col, col + LANES)
            cur = hp_ref[r0:r0 + L, cols]
            prev_tail = tail(hp_ref, hph_ref, n_hph, r0, (k_taps - 1) * SUBLANES, cols)
            return _silu(_causal_conv(prev_tail, cur, cw_ref, cb_ref[:, wcol:wcol + LANES], wcol, LANES, k_taps))

        for blk in range(D_SSD_XBC // LANES):
            act_ref[r0:r0 + L, blk * LANES:(blk + 1) * LANES] = conv_block(
                C_XBC + blk * LANES, xcw_ref, xcb_ref, blk * LANES, SSD_CONV)
        for blk in range(2 * D_MLSTM // LANES):
            act_ref[r0:r0 + L, D_SSD_XBC + blk * LANES:D_SSD_XBC + (blk + 1) * LANES] = conv_block(
                C_QK + blk * LANES, qcw_ref, qcb_ref, blk * LANES, MLSTM_CONV)
        yield

        pos = tau_f + (tile * TT + r0 + 1).astype(F32)
        pooled_blocks = []
        for b in range(2):
            cs_ = slice(b * LANES, (b + 1) * LANES)
            u_cur = hp_ref[r0:r0 + L, cs_]
            lvl = u_cur
            sums = []
            for li, sh in enumerate((1, 2, 4, 8)):
                if li == 0:
                    prev_tail = tail(hp_ref, hph_ref, n_hph, r0, sh * SUBLANES, cs_)
                else:
                    prev_tail = tail(ps_ref.at[li - 1], psh_ref.at[li - 1], n_psh, r0, sh * SUBLANES, cs_)
                ext = _ext_rows(prev_tail, lvl[L - sh * SUBLANES:L])
                lvl = lvl + _shifted(ext, lvl, sh)
                sums.append(lvl)
                if (li, b) in ps_carried:
                    ps_ref[li, r0:r0 + L, cs_] = lvl
                if b == 0 and li == 1:
                    break
            wsum = jnp.where(lo_half, sums[0], sums[1]) if b == 0 else jnp.where(lo_half, sums[2], sums[3])
            pooled_blocks.append((wsum / jnp.minimum(pos, win_blk[b]) - u_cur).astype(BF16))
        mix_ref[r0:r0 + L, 0:D_POOL] = (
            (_dot(jnp.concatenate(pooled_blocks, axis=1), poolw_ref[...]) + poolb_ref[...]) * pools_ref[...])
        yield

        gb = rest_ref[r0:r0 + L, R_G:R_G + LANES] + gbias_ref[...]
        sp_term = jnp.log1p(jnp.exp(-jnp.abs(gb)))
        dt = jnp.maximum(gb, 0.0) + sp_term
        log_f = jnp.minimum(gb, 0.0) - sp_term
        is_dt = lane < G_I
        is_f = (lane >= G_F) & (lane < G_F + MLSTM_HEADS)
        v_cum = jnp.where(is_dt, dt * a_row, jnp.where(is_f, log_f, 0.0))
        hi, mid, lo = _split3(v_cum)
        cs = _dot(tril, hi) + _dot(tril, mid) + _dot(tril, lo)
        u_gate = jnp.where(is_dt, dt, gb)
        cs_t = cs.T
        ug_t = u_gate.T
        cs_last = cs[L - 1:L, :]
        e_col = jnp.exp(cs)
        w_col = jnp.exp(cs_last - cs) * dt
        e_last = jnp.exp(cs_last)
        yield

        for g in range(SSD_GROUPS):
            b_g = act_ref[r0:r0 + L, D_SSD + g * SSD_STATE:D_SSD + (g + 1) * SSD_STATE].astype(BF16)
            c_g = act_ref[r0:r0 + L, D_SSD + (SSD_GROUPS + g) * SSD_STATE:
                          D_SSD + (SSD_GROUPS + g + 1) * SSD_STATE].astype(BF16)
            s_g = _dot_nt(c_g, b_g)
            state_g = sstate_ref[g]
            y_off = _dot(c_g, state_g.astype(BF16))
            xd_blocks = []
            cd_blocks = []
            for pr in range(2):
                h_even = 4 * g + 2 * pr
                col = h_even * SSD_HEAD_DIM
                xs = act_ref[r0:r0 + L, col:col + LANES]
                xs_b = xs.astype(BF16)
                yd = []
                for hh in range(2):
                    hd = h_even + hh
                    seg = jnp.where(causal, cs[:, hd:hd + 1] - cs_t[hd:hd + 1, :], neg_inf)
                    m_h = (s_g * (jnp.exp(seg) * ug_t[hd:hd + 1, :])).astype(BF16)
                    yd.append(_dot(m_h, xs_b))
                y_diag = jnp.where(lo_half, yd[0], yd[1])
                e_exp = _pair_expand(e_col, h_even, (L, LANES), lo_half)
                w_exp = _pair_expand(w_col, h_even, (L, LANES), lo_half)
                y = (y_diag + y_off[:, pr * LANES:(pr + 1) * LANES] * e_exp
                     + xs * dskip_ref[:, col:col + LANES])
                z = rest_ref[r0:r0 + L, R_Z + col:R_Z + col + LANES]
                mix_ref[r0:r0 + L, D_POOL + col:D_POOL + col + LANES] = y * _silu(z)
                xd_blocks.append((xs * w_exp).astype(BF16))
                cd_blocks.append(_pair_expand(e_last, h_even, (1, LANES), lane_row < 64))
            xd_g = jnp.concatenate(xd_blocks, axis=1)
            cd_g = jnp.concatenate(cd_blocks, axis=1)
            new_states = lax.dot_general(b_g, xd_g, (((0,), (0,)), ((), ())),
                                         preferred_element_type=F32)
            sstate_ref[g] = state_g * cd_g + new_states
            yield
        y_all = mix_ref[r0:r0 + L, D_POOL:D_POOL + D_SSD]
        ms_y = jnp.mean(y_all * y_all, axis=-1, keepdims=True)
        mix_ref[r0:r0 + L, D_POOL:D_POOL + D_SSD] = y_all * lax.rsqrt(ms_y + EPS) * snorm_ref[...]

        for pr in range(MLSTM_HEADS // 2):
            qcol = D_SSD_XBC + pr * LANES
            kcol = D_SSD_XBC + D_MLSTM + pr * LANES
            q_b = act_ref[r0:r0 + L, qcol:qcol + LANES] * (MLSTM_HEAD_DIM ** -0.5)
            k_t = act_ref[r0:r0 + L, kcol:kcol + LANES].T
            k_tb = k_t.astype(BF16)
            v_b = rest_ref[r0:r0 + L, R_V + pr * LANES:R_V + (pr + 1) * LANES]
            o_b = rest_ref[r0:r0 + L, R_O + pr * LANES:R_O + (pr + 1) * LANES]
            hv = []
            for hh in range(2):
                hd = 2 * pr + hh
                in_half = lo_half if hh == 0 else jnp.logical_not(lo_half)
                row_in_half = (row < 64) if hh == 0 else (row >= 64)
                ones_lane = (lane == 64) if hh == 0 else (lane == 0)
                ol = 64 if hh == 0 else 0
                qm = jnp.where(in_half, q_b, 0.0).astype(BF16)
                s = _dot(qm, k_tb)
                b_row = cs_t[G_F + hd:G_F + hd + 1, :]
                r_row = ug_t[G_I + hd:G_I + hd + 1, :] - b_row
                b_last = jnp.sum(jnp.where(lane_row == L - 1, b_row, 0.0), axis=-1, keepdims=True)
                al_row = b_last + r_row
                m_loc = jnp.max(al_row, axis=-1, keepdims=True)
                prev_m = mm_ref[hd:hd + 1, 0:1]
                rmask = jnp.where(causal, r_row, neg_inf)
                g_col = jnp.maximum(jnp.max(rmask, axis=-1, keepdims=True), prev_m)
                p = (s * jnp.exp(rmask - g_col)).astype(BF16)
                v_ext = jnp.where(in_half, v_b, jnp.where(ones_lane, 1.0, 0.0)).astype(BF16)
                cn = mstate_ref[hd]
                res = _dot(p, v_ext) + _dot(qm, cn.astype(BF16)) * jnp.exp(prev_m - g_col)
                den = jnp.maximum(jnp.abs(res[:, ol:ol + 1]),
                                  jnp.exp(-(cs[:, G_F + hd:G_F + hd + 1] + g_col)))
                hv.append(res / den)
                w_row = jnp.exp(al_row - m_loc)
                ktw = jnp.where(row_in_half, k_t * w_row, 0.0).astype(BF16)
                c_loc = _dot(ktw, v_ext)
                m_new = jnp.maximum(b_last + prev_m, m_loc)
                mstate_ref[hd] = (jnp.exp(b_last + prev_m - m_new) * cn
                                  + jnp.exp(m_loc - m_new) * c_loc)
                mm_ref[hd:hd + 1, :] = jnp.broadcast_to(m_new, (1, LANES))
            hcat = jax.nn.sigmoid(o_b) * jnp.where(lo_half, hv[0], hv[1])
            sq = hcat * hcat
            ss_lo = jnp.sum(jnp.where(lo_half, sq, 0.0), axis=-1, keepdims=True)
            ss_hi = jnp.sum(jnp.where(lo_half, 0.0, sq), axis=-1, keepdims=True)
            inv = jnp.where(lo_half, lax.rsqrt(ss_lo * (1.0 / MLSTM_HEAD_DIM) + EPS),
                            lax.rsqrt(ss_hi * (1.0 / MLSTM_HEAD_DIM) + EPS))
            mcol = D_POOL + D_SSD + pr * LANES
            mix_ref[r0:r0 + L, mcol:mcol + LANES] = hcat * inv * mnorm_ref[:, pr * LANES:(pr + 1) * LANES]
            if pr + 1 < MLSTM_HEADS // 2:
                yield

        def out_piece():
            o = _dot(mix_ref[r0:r0 + L, :].astype(BF16), wout_ref[...])
            ms_o = jnp.mean(o * o, axis=-1, keepdims=True)
            out_ref[0, r0:r0 + L, :] = x_ref[0, r0:r0 + L, :] + o * lax.rsqrt(ms_o + EPS) * postn_ref[...]

        pending.insert(0, out_piece)
        yield

    n_stages = 3 + SSD_GROUPS + MLSTM_HEADS // 2
    gens = [chunk_stages(c) for c in range(TT // L)]
    for _ in range(n_stages):
        for gen in gens:
            next(gen)
            between()
    while pending:
        pending.pop(0)()

    hph_ref[...] = hp_ref[TT - n_hph:TT, :]
    for lv, b in ps_carried:
        cs_ = slice(b * LANES, (b + 1) * LANES)
        psh_ref[lv, :, cs_] = ps_ref[lv, TT - n_psh:TT, cs_]


def ffn_kernel(layer, to_time_order, x_ref, wup_hbm, wdn_hbm, nw_ref, cw_ref, cb_ref, postn_ref, out_ref,
               wup_ref, wdn_ref, stage_ref, stage_dn_ref, stage_sem, halo_ref, a_ref, *out_scratch):
    TT = x_ref.shape[1]
    FT = FFN_FT
    L = CHUNK
    n_tail = (FFN_CONV - 1) * SUBLANES
    i = pl.program_id(1)

    @pl.when((pl.program_id(0) == 0) & (i == 0))
    def _():
        _stage_weights(_col_blocks(wup_hbm.at[layer], 0, wup_ref, 0, 2 * D_FF, STAGE_COLS),
                       stage_ref, stage_sem)
        _stage_weights([(wdn_hbm.at[layer, r:r + STAGE_COLS, :], wdn_ref.at[r:r + STAGE_COLS, :])
                        for r in range(0, D_FF, STAGE_COLS)], stage_dn_ref, stage_sem)

    @pl.when(i == 0)
    def _():
        halo_ref[...] = jnp.zeros(halo_ref.shape, F32)

    x = x_ref[0]
    ms = jnp.mean(x * x, axis=-1, keepdims=True)
    h = (x * lax.rsqrt(ms + EPS) * nw_ref[...]).astype(BF16)

    def conv_cols(col):
        u = _dot(h, wup_ref[:, col:col + FT])
        outs = []
        for c in range(TT // L):
            cur = u[c * L:(c + 1) * L]
            prev_tail = halo_ref[:, col:col + FT] if c == 0 else u[c * L - n_tail:c * L]
            outs.append(_causal_conv(prev_tail, cur, cw_ref, cb_ref[:, col:col + FT], col, FT, FFN_CONV))
        halo_ref[:, col:col + FT] = u[TT - n_tail:TT]
        return outs

    for j in range(D_FF // FT):
        gts = conv_cols(j * FT)
        vals = conv_cols(D_FF + j * FT)
        for c in range(TT // L):
            gt = gts[c]
            gelu = 0.5 * gt * (1.0 + jnp.tanh(math.sqrt(2.0 / math.pi) * (gt + 0.044715 * (gt * gt * gt))))
            a_ref[c * L:(c + 1) * L, j * FT:(j + 1) * FT] = (gelu * vals[c]).astype(BF16)

    f = _dot(a_ref[...], wdn_ref[...])
    ms_f = jnp.mean(f * f, axis=-1, keepdims=True)
    res = x_ref[0] + f * lax.rsqrt(ms_f + EPS) * postn_ref[...]
    if not to_time_order:
        out_ref[0] = res
        return

    obuf_ref, out_sem = out_scratch
    n_c = TT // L
    b = pl.program_id(0)
    step = b * pl.num_programs(1) + i
    n_steps = pl.num_programs(0) * pl.num_programs(1)
    slot = step % 2

    def out_copies(sl):
        return [pltpu.make_async_copy(obuf_ref.at[sl, c, :, s, :], out_ref.at[b, i * n_c + c, s], out_sem.at[sl])
                for c in range(n_c) for s in range(SUBLANES)]

    @pl.when(step >= 2)
    def _():
        for cp in out_copies(slot):
            cp.wait()

    obuf_ref[slot] = res.reshape(n_c, VROWS, SUBLANES, res.shape[1])
    for cp in out_copies(slot):
        cp.start()

    @pl.when(step == n_steps - 1)
    def _():
        for cp in out_copies(slot):
            cp.wait()

    @pl.when((step == n_steps - 1) & (n_steps >= 2))
    def _():
        for cp in out_copies(1 - slot):
            cp.wait()


def _const_spec(shape):
    nd = len(shape)
    return pl.BlockSpec(shape, lambda b, i: (0,) * nd, pipeline_mode=pl.Buffered(1))


def _mix_layer(layer, x, wout, win, nw, gbias, alog, xcw, xcb, qcw, qcb, poolw, poolb, pools, dskip,
               snorm, mnorm, postn):
    B, T, D = x.shape
    TT = MIX_TT
    weights = (wout,)
    consts = (win, nw, gbias, alog, xcw, xcb, qcw, qcb, poolw, poolb, pools, dskip, snorm, mnorm, postn)
    n_t = T // TT
    cur_spec = pl.BlockSpec((1, TT, D), lambda b, i: (b, jnp.minimum(i, n_t - 1), 0))
    prev_spec = pl.BlockSpec((1, TT, D), lambda b, i: (b, jnp.maximum(i - 1, 0), 0))
    max_conv_tail = (max(SSD_CONV, MLSTM_CONV) - 1) * SUBLANES
    max_pool_tail = (POOL_WINDOWS[-1] // 2) * SUBLANES
    return pl.pallas_call(
        functools.partial(mix_kernel, layer),
        grid=(B, n_t + 1),
        in_specs=([cur_spec, prev_spec] + [pl.BlockSpec(memory_space=pltpu.HBM) for _ in weights]
                  + [_const_spec(c.shape) for c in consts]),
        out_specs=prev_spec,
        out_shape=jax.ShapeDtypeStruct(x.shape, x.dtype),
        scratch_shapes=[
            pltpu.VMEM((D_MODEL, D_MODEL), BF16),
            pltpu.VMEM((STAGE_SLOTS, D_MODEL, STAGE_COLS), F32),
            pltpu.SemaphoreType.DMA((STAGE_SLOTS,)),
            pltpu.VMEM((TT, D_MODEL), BF16),
            pltpu.VMEM((2, TT, N_HALO_COLS), F32),
            pltpu.VMEM((2, TT, N_REST_COLS), F32),
            pltpu.VMEM((max_conv_tail, N_HALO_COLS), F32),
            pltpu.VMEM((3, max_pool_tail, D_POOL), F32),
            pltpu.VMEM((TT, D_SSD_XBC + 2 * D_MLSTM), F32),
            pltpu.VMEM((TT, D_MODEL), F32),
            pltpu.VMEM((3, TT, D_POOL), F32),
            pltpu.VMEM((SSD_GROUPS, SSD_STATE, 4 * SSD_HEAD_DIM), F32),
            pltpu.VMEM((MLSTM_HEADS, LANES, LANES), F32),
            pltpu.VMEM((SUBLANES, LANES), F32),
        ],
        compiler_params=pltpu.CompilerParams(
            dimension_semantics=("arbitrary", "arbitrary"), vmem_limit_bytes=VMEM_LIMIT),
        name="mix_layer",
    )(x, x, *weights, *consts)


def _ffn_layer(layer, to_time_order, x, wup, wdn, nw, cw, cb, postn):
    B, T, D = x.shape
    TT = FFN_TT
    weights = (wup, wdn)
    consts = (nw, cw, cb, postn)
    x_spec = pl.BlockSpec((1, TT, D), lambda b, i: (b, i, 0))
    if to_time_order:
        out_spec = pl.BlockSpec(memory_space=pltpu.HBM)
        out_shape = jax.ShapeDtypeStruct((B, T // CHUNK, SUBLANES, VROWS, D), x.dtype)
        out_scratch = [pltpu.VMEM((2, TT // CHUNK, VROWS, SUBLANES, D), F32),
                       pltpu.SemaphoreType.DMA((2,))]
    else:
        out_spec, out_shape, out_scratch = x_spec, jax.ShapeDtypeStruct(x.shape, x.dtype), []
    return pl.pallas_call(
        functools.partial(ffn_kernel, layer, to_time_order),
        grid=(B, T // TT),
        in_specs=([x_spec] + [pl.BlockSpec(memory_space=pltpu.HBM) for _ in weights]
                  + [_const_spec(c.shape) for c in consts]),
        out_specs=out_spec,
        out_shape=out_shape,
        scratch_shapes=[
            pltpu.VMEM((D_MODEL, 2 * D_FF), BF16),
            pltpu.VMEM((D_FF, D_MODEL), BF16),
            pltpu.VMEM((STAGE_SLOTS, D_MODEL, STAGE_COLS), F32),
            pltpu.VMEM((STAGE_SLOTS, STAGE_COLS, D_MODEL), F32),
            pltpu.SemaphoreType.DMA((STAGE_SLOTS,)),
            pltpu.VMEM(((FFN_CONV - 1) * SUBLANES, 2 * D_FF), F32),
            pltpu.VMEM((TT, D_FF), BF16),
        ] + out_scratch,
        compiler_params=pltpu.CompilerParams(
            dimension_semantics=("arbitrary", "arbitrary"), vmem_limit_bytes=VMEM_LIMIT),
        name="ffn_layer",
    )(x, *weights, *consts).reshape(B, T, D)


def _row(v):
    return v.reshape(1, -1).astype(F32)


def _pad_lanes(v):
    return jnp.pad(v.astype(F32), (0, LANES - v.shape[0])).reshape(1, LANES)


def _prep_w_in(w):
    sizes = (D_POOL, D_SSD, D_SSD_XBC, SSD_HEADS, 2 * D_MLSTM, D_MLSTM, D_MLSTM, MLSTM_HEADS, MLSTM_HEADS)
    offs = [0]
    for s in sizes:
        offs.append(offs[-1] + s)
    u_pool, z, xbc, dt, qk, v, o, ig, fg = [w[:, offs[k]:offs[k + 1]] for k in range(len(sizes))]
    pad = jnp.zeros((w.shape[0], LANES - SSD_HEADS - 2 * MLSTM_HEADS), w.dtype)
    return jnp.concatenate([u_pool, xbc, qk, z, v, o, dt, ig, fg, pad], axis=1).astype(BF16)


def _prep_pool_w(w):
    out = jnp.zeros((D_POOL, D_POOL), F32)
    for g in range(len(POOL_WINDOWS)):
        s = g * POOL_GROUP_DIM
        out = lax.dynamic_update_slice(out, w[g].astype(F32), (s, s))
    return out.astype(BF16)


def _permute_chunks(x, to_kernel_order):
    B, T, D = x.shape
    a, b = (SUBLANES, VROWS) if to_kernel_order else (VROWS, SUBLANES)
    return x.reshape(B, T // CHUNK, a, b, D).transpose(0, 1, 3, 2, 4).reshape(B, T, D)


def kernel(x, pre_mix_norm, w_in, pool_w, pool_b, pool_scale, ssd_conv_w, ssd_conv_b, ssd_dt_bias, ssd_a_log, ssd_d, ssd_norm, mlstm_conv_w, mlstm_conv_b, mlstm_i_bias, mlstm_f_bias, mlstm_norm, w_out, post_mix_norm, pre_ffn_norm, ffn_w_up, ffn_conv_w, ffn_conv_b, ffn_w_down, post_ffn_norm):
    depth = w_in.shape[0]
    x = _permute_chunks(x, True)
    for l in range(depth):
        gbias = _pad_lanes(jnp.concatenate([ssd_dt_bias[l], mlstm_i_bias[l], mlstm_f_bias[l]]))
        x = _mix_layer(
            l, x, w_out, _prep_w_in(w_in[l]), _row(pre_mix_norm[l]), gbias, _pad_lanes(ssd_a_log[l]),
            ssd_conv_w[l].astype(F32), _row(ssd_conv_b[l]), mlstm_conv_w[l].astype(F32), _row(mlstm_conv_b[l]),
            _prep_pool_w(pool_w[l]), _row(pool_b[l]), _row(pool_scale[l]),
            _row(jnp.repeat(ssd_d[l], SSD_HEAD_DIM)), _row(ssd_norm[l]), _row(mlstm_norm[l]),
            _row(post_mix_norm[l]))
        x = _ffn_layer(
            l, l == depth - 1, x, ffn_w_up, ffn_w_down, _row(pre_ffn_norm[l]), ffn_conv_w[l].astype(F32),
            _row(ffn_conv_b[l]), _row(post_ffn_norm[l]))
    return x
```

```python
import functools
import math

import jax
import jax.numpy as jnp
from jax import lax
from jax.experimental import pallas as pl
from jax.experimental.pallas import tpu as pltpu

F32 = jnp.float32
BF16 = jnp.bfloat16

D_MODEL = 1024
EPS = 1e-6

D_POOL = 256
POOL_GROUP_DIM = 64
POOL_WINDOWS = (2, 4, 8, 16)

D_SSD = 512
SSD_HEADS = 8
SSD_HEAD_DIM = 64
SSD_GROUPS = 2
SSD_STATE = 128
SSD_CONV = 4
D_SSD_XBC = D_SSD + 2 * SSD_GROUPS * SSD_STATE

D_MLSTM = 256
MLSTM_HEADS = 4
MLSTM_HEAD_DIM = 64
MLSTM_CONV = 4

D_FF = 2816
FFN_CONV = 3

CHUNK = 128
LANES = 128
SUBLANES = 8
VROWS = CHUNK // SUBLANES

C_POOL = 0
C_XBC = C_POOL + D_POOL
C_QK = C_XBC + D_SSD_XBC
N_HALO_COLS = C_QK + 2 * D_MLSTM
R_Z = 0
R_V = R_Z + D_SSD
R_O = R_V + D_MLSTM
R_G = R_O + D_MLSTM
N_REST_COLS = R_G + LANES
N_IN_COLS = N_HALO_COLS + N_REST_COLS
G_DT = 0
G_I = SSD_HEADS
G_F = G_I + MLSTM_HEADS

MIX_TT = 256
PROJ_COLS = 256
STAGE_COLS = 256
STAGE_SLOTS = 4
FFN_TT = 512
FFN_FT = 256
VMEM_LIMIT = 56 * 1024 * 1024


def _dot(a, b):
    return jnp.dot(a, b, preferred_element_type=F32)


def _dot_nt(a, b):
    return lax.dot_general(a, b, (((1,), (1,)), ((), ())), preferred_element_type=F32)


def _silu(x):
    return x * jax.nn.sigmoid(x)


def _split3(a):
    hi = a.astype(BF16)
    r = a - hi.astype(F32)
    mid = r.astype(BF16)
    lo = (r - mid.astype(F32)).astype(BF16)
    return hi, mid, lo


def _bcast_lane(a, j, shape):
    return jnp.broadcast_to(a[:, j:j + 1], shape)


def _pair_expand(a, h_even, shape, lo_half):
    return jnp.where(lo_half, _bcast_lane(a, h_even, shape), _bcast_lane(a, h_even + 1, shape))


def _stage_weights(blocks, stage_ref, sem_ref):
    n_slots = stage_ref.shape[0]

    def copy(n):
        src, dst = blocks[n]
        rows, cols = src.shape
        return pltpu.make_async_copy(src, stage_ref.at[n % n_slots, 0:rows, 0:cols], sem_ref.at[n % n_slots])

    for n in range(min(n_slots - 1, len(blocks))):
        copy(n).start()
    for n, (src, dst) in enumerate(blocks):
        if n + n_slots - 1 < len(blocks):
            copy(n + n_slots - 1).start()
        copy(n).wait()
        rows, cols = src.shape
        dst[...] = stage_ref[n % n_slots, 0:rows, 0:cols].astype(BF16)


def _col_blocks(src_ref, src0, dst_ref, dst0, ncols, step):
    return [(src_ref.at[:, src0 + k:src0 + min(k + step, ncols)],
             dst_ref.at[:, dst0 + k:dst0 + min(k + step, ncols)]) for k in range(0, ncols, step)]


def _tau(p):
    return (p % SUBLANES) * VROWS + p // SUBLANES


def _ext_rows(prev_tail, cur_tail):
    n = cur_tail.shape[0] // SUBLANES
    sub0 = lax.broadcasted_iota(jnp.int32, (SUBLANES, cur_tail.shape[1]), 0) == 0
    out = []
    for j in range(n):
        sl = slice(j * SUBLANES, (j + 1) * SUBLANES)
        out.append(jnp.where(sub0, pltpu.roll(prev_tail[sl], 1, 0), pltpu.roll(cur_tail[sl], 1, 0)))
    return out


def _shifted(ext, cur, k):
    if k == 0:
        return cur
    return jnp.concatenate(ext[len(ext) - k:] + [cur[0:CHUNK - SUBLANES * k]], axis=0)


def _causal_conv(prev_tail, cur, w_ref, b_row, wcol, ncols, k_taps):
    n = k_taps - 1
    ext = _ext_rows(prev_tail, cur[CHUNK - n * SUBLANES:CHUNK])
    acc = b_row
    for k in range(k_taps):
        acc = acc + _shifted(ext, cur, n - k) * w_ref[k:k + 1, wcol:wcol + ncols]
    return acc


def _relayout_w_in(wraw_ref, win_ref):
    c_z = D_POOL
    c_xbc = c_z + D_SSD
    c_dt = c_xbc + D_SSD_XBC
    n_tail = SSD_HEADS + 4 * D_MLSTM + 2 * MLSTM_HEADS
    o_qk = SSD_HEADS
    o_v = o_qk + 2 * D_MLSTM
    o_if = o_v + 2 * D_MLSTM
    for r in range(0, D_MODEL, CHUNK):
        rows = slice(r, r + CHUNK)
        win_ref[rows, C_POOL:C_POOL + D_POOL] = wraw_ref[rows, 0:D_POOL]
        win_ref[rows, C_XBC:C_XBC + D_SSD_XBC] = wraw_ref[rows, c_xbc:c_dt]
        win_ref[rows, N_HALO_COLS + R_Z:N_HALO_COLS + R_Z + D_SSD] = wraw_ref[rows, c_z:c_xbc]
        tail = wraw_ref[rows, c_dt:c_dt + n_tail]
        win_ref[rows, C_QK:C_QK + 2 * D_MLSTM] = tail[:, o_qk:o_v]
        win_ref[rows, N_HALO_COLS + R_V:N_HALO_COLS + R_V + 2 * D_MLSTM] = tail[:, o_v:o_if]
        pad = jnp.zeros((CHUNK, LANES - SSD_HEADS - 2 * MLSTM_HEADS), tail.dtype)
        win_ref[rows, N_HALO_COLS + R_G:N_HALO_COLS + R_G + LANES] = jnp.concatenate(
            [tail[:, 0:o_qk], tail[:, o_if:n_tail], pad], axis=1)


def mix_kernel(layer, from_time_order, xc_ref, xp_ref, wout_hbm, wraw_ref, nw_ref, gbias_ref, alog_ref,
               xcw_ref, xcb_ref, qcw_ref, qcb_ref,
               poolw_ref, poolb_ref, pools_ref, dskip_ref, snorm_ref, mnorm_ref, postn_ref,
               out_ref, win_ref, wout_ref, stage_ref, stage_sem,
               h_ref, hp_ref, rest_ref, hph_ref, psh_ref, act_ref, mix_ref, ps_ref,
               sstate_ref, mstate_ref, mm_ref, *in_scratch):
    TT = hp_ref.shape[1]
    L = CHUNK
    n_c = TT // L
    b = pl.program_id(0)
    i = pl.program_id(1)
    n_t = pl.num_programs(1) - 1

    if from_time_order:
        xbuf_ref, in_sem = in_scratch
        n_slots = xbuf_ref.shape[0]

        def in_copies(tile):
            sl = tile % n_slots
            return [pltpu.make_async_copy(xc_ref.at[b, tile * n_c + c, s], xbuf_ref.at[sl, c, :, s, :],
                                          in_sem.at[sl]) for c in range(n_c) for s in range(SUBLANES)]

        @pl.when(i == 0)
        def _():
            for cp in in_copies(0):
                cp.start()

        @pl.when(i + 1 < n_t)
        def _():
            for cp in in_copies(i + 1):
                cp.start()

        @pl.when(i < n_t)
        def _():
            for cp in in_copies(i):
                cp.wait()

        slot_cur = jnp.minimum(i, n_t - 1) % n_slots
        slot_prev = jnp.maximum(i - 1, 0) % n_slots

        def x_cur():
            return xbuf_ref[slot_cur].reshape(TT, xbuf_ref.shape[-1])

        def x_rows(r0):
            return xbuf_ref[slot_prev, r0 // L].reshape(L, xbuf_ref.shape[-1])
    else:
        def x_cur():
            return xc_ref[0]

        def x_rows(r0):
            return xp_ref[0, r0:r0 + L, :]

    @pl.when((pl.program_id(0) == 0) & (i == 0))
    def _():
        _relayout_w_in(wraw_ref, win_ref)
        _stage_weights(_col_blocks(wout_hbm.at[layer], 0, wout_ref, 0, D_MODEL, STAGE_COLS),
                       stage_ref, stage_sem)

    @pl.when(i == 0)
    def _():
        hp_ref[1] = jnp.zeros(hp_ref.shape[1:], F32)
        rest_ref[1] = jnp.zeros(rest_ref.shape[1:], F32)

    @pl.when(i <= 1)
    def _():
        hph_ref[...] = jnp.zeros(hph_ref.shape, F32)
        psh_ref[...] = jnp.zeros(psh_ref.shape, F32)
        sstate_ref[...] = jnp.zeros(sstate_ref.shape, F32)
        mstate_ref[...] = jnp.zeros(mstate_ref.shape, F32)
        mm_ref[...] = jnp.zeros(mm_ref.shape, F32)

    def step(slot_proj, slot_mix):
        x = x_cur()
        ms = jnp.mean(x * x, axis=-1, keepdims=True)
        h_ref[...] = (x * lax.rsqrt(ms + EPS) * nw_ref[...]).astype(BF16)

        def proj_piece(dst_ref, c0, c1, w0):
            def piece():
                dst_ref[slot_proj, :, c0:c1] = _dot(h_ref[...], win_ref[:, w0 + c0:w0 + c1])
            return piece

        pending = [proj_piece(hp_ref, c0, min(c0 + PROJ_COLS, N_HALO_COLS), 0)
                   for c0 in range(0, N_HALO_COLS, PROJ_COLS)]
        pending += [proj_piece(rest_ref, c0, min(c0 + PROJ_COLS, N_REST_COLS), N_HALO_COLS)
                    for c0 in range(0, N_REST_COLS, PROJ_COLS)]
        _mixers(jnp.maximum(i - 1, 0), x_rows, hp_ref.at[slot_mix], rest_ref.at[slot_mix], gbias_ref, alog_ref,
                xcw_ref, xcb_ref, qcw_ref, qcb_ref, poolw_ref, poolb_ref, pools_ref, dskip_ref, snorm_ref,
                mnorm_ref, wout_ref, postn_ref, out_ref, hph_ref, psh_ref, act_ref, mix_ref, ps_ref,
                sstate_ref, mstate_ref, mm_ref, pending)

    @pl.when(i % 2 == 0)
    def _():
        step(0, 1)

    @pl.when(i % 2 == 1)
    def _():
        step(1, 0)


def _mixers(tile, x_rows, hp_ref, rest_ref, gbias_ref, alog_ref, xcw_ref, xcb_ref, qcw_ref, qcb_ref,
            poolw_ref, poolb_ref, pools_ref, dskip_ref, snorm_ref, mnorm_ref, wout_ref, postn_ref,
            out_ref, hph_ref, psh_ref, act_ref, mix_ref, ps_ref, sstate_ref, mstate_ref, mm_ref, pending):
    def between():
        if pending:
            pending.pop(0)()

    TT = hp_ref.shape[0]
    L = CHUNK
    n_hph = hph_ref.shape[0]
    n_psh = psh_ref.shape[1]
    ps_carried = ((0, 0), (0, 1), (1, 1), (2, 1))

    def tail(cur_ref, halo, n_halo, r0, n_rows, cols):
        if r0 == 0:
            return halo[n_halo - n_rows:n_halo, cols]
        return cur_ref[r0 - n_rows:r0, cols]

    lane = lax.broadcasted_iota(jnp.int32, (L, LANES), 1)
    row = lax.broadcasted_iota(jnp.int32, (L, LANES), 0)
    lo_half = lane < 64
    tau_row = _tau(row)
    causal = _tau(lane) <= tau_row
    tril = jnp.where(causal, 1.0, 0.0).astype(BF16)
    lane_row = lax.broadcasted_iota(jnp.int32, (1, LANES), 1)
    a_row = -jnp.exp(alog_ref[...])
    neg_inf = -jnp.inf
    win_blk = [jnp.where(lo_half, float(POOL_WINDOWS[2 * b]), float(POOL_WINDOWS[2 * b + 1])) for b in range(2)]
    tau_f = tau_row.astype(F32)

    def chunk_stages(c):
        r0 = c * L

        def conv_block(col, cw_ref, cb_ref, wcol, k_taps):
            cols = slice(col, col + LANES)
            cur = hp_ref[r0:r0 + L, cols]
            prev_tail = tail(hp_ref, hph_ref, n_hph, r0, (k_taps - 1) * SUBLANES, cols)
            return _silu(_causal_conv(prev_tail, cur, cw_ref, cb_ref[:, wcol:wcol + LANES], wcol, LANES, k_taps))

        for blk in range(D_SSD_XBC // LANES):
            act_ref[r0:r0 + L, blk * LANES:(blk + 1) * LANES] = conv_block(
                C_XBC + blk * LANES, xcw_ref, xcb_ref, blk * LANES, SSD_CONV)
        for blk in range(2 * D_MLSTM // LANES):
            act_ref[r0:r0 + L, D_SSD_XBC + blk * LANES:D_SSD_XBC + (blk + 1) * LANES] = conv_block(
                C_QK + blk * LANES, qcw_ref, qcb_ref, blk * LANES, MLSTM_CONV)
        yield

        pos = tau_f + (tile * TT + r0 + 1).astype(F32)
        pooled_blocks = []
        for b in range(2):
            cs_ = slice(b * LANES, (b + 1) * LANES)
            u_cur = hp_ref[r0:r0 + L, cs_]
            lvl = u_cur
            sums = []
            for li, sh in enumerate((1, 2, 4, 8)):
                if li == 0:
                    prev_tail = tail(hp_ref, hph_ref, n_hph, r0, sh * SUBLANES, cs_)
                else:
                    prev_tail = tail(ps_ref.at[li - 1], psh_ref.at[li - 1], n_psh, r0, sh * SUBLANES, cs_)
                ext = _ext_rows(prev_tail, lvl[L - sh * SUBLANES:L])
                lvl = lvl + _shifted(ext, lvl, sh)
                sums.append(lvl)
                if (li, b) in ps_carried:
                    ps_ref[li, r0:r0 + L, cs_] = lvl
                if b == 0 and li == 1:
                    break
            wsum = jnp.where(lo_half, sums[0], sums[1]) if b == 0 else jnp.where(lo_half, sums[2], sums[3])
            pooled_blocks.append((wsum / jnp.minimum(pos, win_blk[b]) - u_cur).astype(BF16))
        mix_ref[r0:r0 + L, 0:D_POOL] = (
            (_dot(jnp.concatenate(pooled_blocks, axis=1), poolw_ref[...]) + poolb_ref[...]) * pools_ref[...])
        yield

        gb = rest_ref[r0:r0 + L, R_G:R_G + LANES] + gbias_ref[...]
        sp_term = jnp.log1p(jnp.exp(-jnp.abs(gb)))
        dt = jnp.maximum(gb, 0.0) + sp_term
        log_f = jnp.minimum(gb, 0.0) - sp_term
        is_dt = lane < G_I
        is_f = (lane >= G_F) & (lane < G_F + MLSTM_HEADS)
        v_cum = jnp.where(is_dt, dt * a_row, jnp.where(is_f, log_f, 0.0))
        hi, mid, lo = _split3(v_cum)
        cs = _dot(tril, hi) + _dot(tril, mid) + _dot(tril, lo)
        u_gate = jnp.where(is_dt, dt, gb)
        cs_t = cs.T
        ug_t = u_gate.T
        cs_last = cs[L - 1:L, :]
        e_col = jnp.exp(cs)
        w_col = jnp.exp(cs_last - cs) * dt
        e_last = jnp.exp(cs_last)
        yield

        for g in range(SSD_GROUPS):
            b_g = act_ref[r0:r0 + L, D_SSD + g * SSD_STATE:D_SSD + (g + 1) * SSD_STATE].astype(BF16)
            c_g = act_ref[r0:r0 + L, D_SSD + (SSD_GROUPS + g) * SSD_STATE:
                          D_SSD + (SSD_GROUPS + g + 1) * SSD_STATE].astype(BF16)
            s_g = _dot_nt(c_g, b_g)
            state_g = sstate_ref[g]
            y_off = _dot(c_g, state_g.astype(BF16))
            xd_blocks = []
            cd_blocks = []
            for pr in range(2):
                h_even = 4 * g + 2 * pr
                col = h_even * SSD_HEAD_DIM
                xs = act_ref[r0:r0 + L, col:col + LANES]
                xs_b = xs.astype(BF16)
                yd = []
                for hh in range(2):
                    hd = h_even + hh
                    seg = jnp.where(causal, cs[:, hd:hd + 1] - cs_t[hd:hd + 1, :], neg_inf)
                    m_h = (s_g * (jnp.exp(seg) * ug_t[hd:hd + 1, :])).astype(BF16)
                    yd.append(_dot(m_h, xs_b))
                y_diag = jnp.where(lo_half, yd[0], yd[1])
                e_exp = _pair_expand(e_col, h_even, (L, LANES), lo_half)
                w_exp = _pair_expand(w_col, h_even, (L, LANES), lo_half)
                y = (y_diag + y_off[:, pr * LANES:(pr + 1) * LANES] * e_exp
                     + xs * dskip_ref[:, col:col + LANES])
                z = rest_ref[r0:r0 + L, R_Z + col:R_Z + col + LANES]
                mix_ref[r0:r0 + L, D_POOL + col:D_POOL + col + LANES] = y * _silu(z)
                xd_blocks.append((xs * w_exp).astype(BF16))
                cd_blocks.append(_pair_expand(e_last, h_even, (1, LANES), lane_row < 64))
            xd_g = jnp.concatenate(xd_blocks, axis=1)
            cd_g = jnp.concatenate(cd_blocks, axis=1)
            new_states = lax.dot_general(b_g, xd_g, (((0,), (0,)), ((), ())),
                                         preferred_element_type=F32)
            sstate_ref[g] = state_g * cd_g + new_states
            yield
        y_all = mix_ref[r0:r0 + L, D_POOL:D_POOL + D_SSD]
        ms_y = jnp.mean(y_all * y_all, axis=-1, keepdims=True)
        mix_ref[r0:r0 + L, D_POOL:D_POOL + D_SSD] = y_all * lax.rsqrt(ms_y + EPS) * snorm_ref[...]

        for pr in range(MLSTM_HEADS // 2):
            qcol = D_SSD_XBC + pr * LANES
            kcol = D_SSD_XBC + D_MLSTM + pr * LANES
            q_b = act_ref[r0:r0 + L, qcol:qcol + LANES] * (MLSTM_HEAD_DIM ** -0.5)
            k_t = act_ref[r0:r0 + L, kcol:kcol + LANES].T
            k_tb = k_t.astype(BF16)
            v_b = rest_ref[r0:r0 + L, R_V + pr * LANES:R_V + (pr + 1) * LANES]
            o_b = rest_ref[r0:r0 + L, R_O + pr * LANES:R_O + (pr + 1) * LANES]
            hv = []
            for hh in range(2):
                hd = 2 * pr + hh
                in_half = lo_half if hh == 0 else jnp.logical_not(lo_half)
                row_in_half = (row < 64) if hh == 0 else (row >= 64)
                ones_lane = (lane == 64) if hh == 0 else (lane == 0)
                ol = 64 if hh == 0 else 0
                qm = jnp.where(in_half, q_b, 0.0).astype(BF16)
                s = _dot(qm, k_tb)
                b_row = cs_t[G_F + hd:G_F + hd + 1, :]
                r_row = ug_t[G_I + hd:G_I + hd + 1, :] - b_row
                b_last = jnp.sum(jnp.where(lane_row == L - 1, b_row, 0.0), axis=-1, keepdims=True)
                al_row = b_last + r_row
                m_loc = jnp.max(al_row, axis=-1, keepdims=True)
                prev_m = mm_ref[hd:hd + 1, 0:1]
                rmask = jnp.where(causal, r_row, neg_inf)
                g_col = jnp.maximum(jnp.max(rmask, axis=-1, keepdims=True), prev_m)
                p = (s * jnp.exp(rmask - g_col)).astype(BF16)
                v_ext = jnp.where(in_half, v_b, jnp.where(ones_lane, 1.0, 0.0)).astype(BF16)
                cn = mstate_ref[hd]
                res = _dot(p, v_ext) + _dot(qm, cn.astype(BF16)) * jnp.exp(prev_m - g_col)
                den = jnp.maximum(jnp.abs(res[:, ol:ol + 1]),
                                  jnp.exp(-(cs[:, G_F + hd:G_F + hd + 1] + g_col)))
                hv.append(res / den)
                w_row = jnp.exp(al_row - m_loc)
                ktw = jnp.where(row_in_half, k_t * w_row, 0.0).astype(BF16)
                c_loc = _dot(ktw, v_ext)
                m_new = jnp.maximum(b_last + prev_m, m_loc)
                mstate_ref[hd] = (jnp.exp(b_last + prev_m - m_new) * cn
                                  + jnp.exp(m_loc - m_new) * c_loc)
                mm_ref[hd:hd + 1, :] = jnp.broadcast_to(m_new, (1, LANES))
            hcat = jax.nn.sigmoid(o_b) * jnp.where(lo_half, hv[0], hv[1])
            sq = hcat * hcat
            ss_lo = jnp.sum(jnp.where(lo_half, sq, 0.0), axis=-1, keepdims=True)
            ss_hi = jnp.sum(jnp.where(lo_half, 0.0, sq), axis=-1, keepdims=True)
            inv = jnp.where(lo_half, lax.rsqrt(ss_lo * (1.0 / MLSTM_HEAD_DIM) + EPS),
                            lax.rsqrt(ss_hi * (1.0 / MLSTM_HEAD_DIM) + EPS))
            mcol = D_POOL + D_SSD + pr * LANES
            mix_ref[r0:r0 + L, mcol:mcol + LANES] = hcat * inv * mnorm_ref[:, pr * LANES:(pr + 1) * LANES]
            if pr + 1 < MLSTM_HEADS // 2:
                yield

        def out_piece():
            o = _dot(mix_ref[r0:r0 + L, :].astype(BF16), wout_ref[...])
            ms_o = jnp.mean(o * o, axis=-1, keepdims=True)
            out_ref[0, r0:r0 + L, :] = x_rows(r0) + o * lax.rsqrt(ms_o + EPS) * postn_ref[...]

        pending.insert(0, out_piece)
        yield

    n_stages = 3 + SSD_GROUPS + MLSTM_HEADS // 2
    gens = [chunk_stages(c) for c in range(TT // L)]
    for _ in range(n_stages):
        for gen in gens:
            next(gen)
            between()
    while pending:
        pending.pop(0)()

    hph_ref[...] = hp_ref[TT - n_hph:TT, :]
    for lv, b in ps_carried:
        cs_ = slice(b * LANES, (b + 1) * LANES)
        psh_ref[lv, :, cs_] = ps_ref[lv, TT - n_psh:TT, cs_]


def ffn_kernel(layer, to_time_order, x_ref, wup_hbm, wdn_hbm, nw_ref, cw_ref, cb_ref, postn_ref, out_ref,
               wup_ref, wdn_ref, stage_ref, stage_dn_ref, stage_sem, halo_ref, a_ref, *out_scratch):
    TT = x_ref.shape[1]
    FT = FFN_FT
    L = CHUNK
    n_tail = (FFN_CONV - 1) * SUBLANES
    i = pl.program_id(1)

    @pl.when((pl.program_id(0) == 0) & (i == 0))
    def _():
        _stage_weights(_col_blocks(wup_hbm.at[layer], 0, wup_ref, 0, 2 * D_FF, STAGE_COLS),
                       stage_ref, stage_sem)
        _stage_weights([(wdn_hbm.at[layer, r:r + STAGE_COLS, :], wdn_ref.at[r:r + STAGE_COLS, :])
                        for r in range(0, D_FF, STAGE_COLS)], stage_dn_ref, stage_sem)

    @pl.when(i == 0)
    def _():
        halo_ref[...] = jnp.zeros(halo_ref.shape, F32)

    x = x_ref[0]
    ms = jnp.mean(x * x, axis=-1, keepdims=True)
    h = (x * lax.rsqrt(ms + EPS) * nw_ref[...]).astype(BF16)

    def conv_cols(col):
        u = _dot(h, wup_ref[:, col:col + FT])
        outs = []
        for c in range(TT // L):
            cur = u[c * L:(c + 1) * L]
            prev_tail = halo_ref[:, col:col + FT] if c == 0 else u[c * L - n_tail:c * L]
            outs.append(_causal_conv(prev_tail, cur, cw_ref, cb_ref[:, col:col + FT], col, FT, FFN_CONV))
        halo_ref[:, col:col + FT] = u[TT - n_tail:TT]
        return outs

    for j in range(D_FF // FT):
        gts = conv_cols(j * FT)
        vals = conv_cols(D_FF + j * FT)
        for c in range(TT // L):
            gt = gts[c]
            gelu = 0.5 * gt * (1.0 + jnp.tanh(math.sqrt(2.0 / math.pi) * (gt + 0.044715 * (gt * gt * gt))))
            a_ref[c * L:(c + 1) * L, j * FT:(j + 1) * FT] = (gelu * vals[c]).astype(BF16)

    f = _dot(a_ref[...], wdn_ref[...])
    ms_f = jnp.mean(f * f, axis=-1, keepdims=True)
    res = x_ref[0] + f * lax.rsqrt(ms_f + EPS) * postn_ref[...]
    if not to_time_order:
        out_ref[0] = res
        return

    obuf_ref, out_sem = out_scratch
    n_c = TT // L
    b = pl.program_id(0)
    step = b * pl.num_programs(1) + i
    n_steps = pl.num_programs(0) * pl.num_programs(1)
    slot = step % 2

    def out_copies(sl):
        return [pltpu.make_async_copy(obuf_ref.at[sl, c, :, s, :], out_ref.at[b, i * n_c + c, s], out_sem.at[sl])
                for c in range(n_c) for s in range(SUBLANES)]

    @pl.when(step >= 2)
    def _():
        for cp in out_copies(slot):
            cp.wait()

    obuf_ref[slot] = res.reshape(n_c, VROWS, SUBLANES, res.shape[1])
    for cp in out_copies(slot):
        cp.start()

    @pl.when(step == n_steps - 1)
    def _():
        for cp in out_copies(slot):
            cp.wait()

    @pl.when((step == n_steps - 1) & (n_steps >= 2))
    def _():
        for cp in out_copies(1 - slot):
            cp.wait()


def _const_spec(shape):
    nd = len(shape)
    return pl.BlockSpec(shape, lambda b, i: (0,) * nd, pipeline_mode=pl.Buffered(1))


def _mix_layer(layer, from_time_order, x, wout, win, nw, gbias, alog, xcw, xcb, qcw, qcb, poolw, poolb, pools,
               dskip, snorm, mnorm, postn):
    B, T, D = x.shape
    TT = MIX_TT
    weights = (wout,)
    consts = (win, nw, gbias, alog, xcw, xcb, qcw, qcb, poolw, poolb, pools, dskip, snorm, mnorm, postn)
    n_t = T // TT
    cur_spec = pl.BlockSpec((1, TT, D), lambda b, i: (b, jnp.minimum(i, n_t - 1), 0))
    prev_spec = pl.BlockSpec((1, TT, D), lambda b, i: (b, jnp.maximum(i - 1, 0), 0))
    max_conv_tail = (max(SSD_CONV, MLSTM_CONV) - 1) * SUBLANES
    max_pool_tail = (POOL_WINDOWS[-1] // 2) * SUBLANES
    if from_time_order:
        x_in = x.reshape(B, T // CHUNK, SUBLANES, VROWS, D)
        x_specs = [pl.BlockSpec(memory_space=pltpu.HBM)] * 2
        in_scratch = [pltpu.VMEM((3, TT // CHUNK, VROWS, SUBLANES, D), F32),
                      pltpu.SemaphoreType.DMA((3,))]
    else:
        x_in, x_specs, in_scratch = x, [cur_spec, prev_spec], []
    return pl.pallas_call(
        functools.partial(mix_kernel, layer, from_time_order),
        grid=(B, n_t + 1),
        in_specs=(x_specs + [pl.BlockSpec(memory_space=pltpu.HBM) for _ in weights]
                  + [_const_spec(c.shape) for c in consts]),
        out_specs=prev_spec,
        out_shape=jax.ShapeDtypeStruct(x.shape, x.dtype),
        scratch_shapes=[
            pltpu.VMEM((D_MODEL, N_IN_COLS), BF16),
            pltpu.VMEM((D_MODEL, D_MODEL), BF16),
            pltpu.VMEM((STAGE_SLOTS, D_MODEL, STAGE_COLS), F32),
            pltpu.SemaphoreType.DMA((STAGE_SLOTS,)),
            pltpu.VMEM((TT, D_MODEL), BF16),
            pltpu.VMEM((2, TT, N_HALO_COLS), F32),
            pltpu.VMEM((2, TT, N_REST_COLS), F32),
            pltpu.VMEM((max_conv_tail, N_HALO_COLS), F32),
            pltpu.VMEM((3, max_pool_tail, D_POOL), F32),
            pltpu.VMEM((TT, D_SSD_XBC + 2 * D_MLSTM), F32),
            pltpu.VMEM((TT, D_MODEL), F32),
            pltpu.VMEM((3, TT, D_POOL), F32),
            pltpu.VMEM((SSD_GROUPS, SSD_STATE, 4 * SSD_HEAD_DIM), F32),
            pltpu.VMEM((MLSTM_HEADS, LANES, LANES), F32),
            pltpu.VMEM((SUBLANES, LANES), F32),
        ] + in_scratch,
        compiler_params=pltpu.CompilerParams(
            dimension_semantics=("arbitrary", "arbitrary"), vmem_limit_bytes=VMEM_LIMIT),
        name="mix_layer",
    )(x_in, x_in, *weights, *consts)


def _ffn_layer(layer, to_time_order, x, wup, wdn, nw, cw, cb, postn):
    B, T, D = x.shape
    TT = FFN_TT
    weights = (wup, wdn)
    consts = (nw, cw, cb, postn)
    x_spec = pl.BlockSpec((1, TT, D), lambda b, i: (b, i, 0))
    if to_time_order:
        out_spec = pl.BlockSpec(memory_space=pltpu.HBM)
        out_shape = jax.ShapeDtypeStruct((B, T // CHUNK, SUBLANES, VROWS, D), x.dtype)
        out_scratch = [pltpu.VMEM((2, TT // CHUNK, VROWS, SUBLANES, D), F32),
                       pltpu.SemaphoreType.DMA((2,))]
    else:
        out_spec, out_shape, out_scratch = x_spec, jax.ShapeDtypeStruct(x.shape, x.dtype), []
    return pl.pallas_call(
        functools.partial(ffn_kernel, layer, to_time_order),
        grid=(B, T // TT),
        in_specs=([x_spec] + [pl.BlockSpec(memory_space=pltpu.HBM) for _ in weights]
                  + [_const_spec(c.shape) for c in consts]),
        out_specs=out_spec,
        out_shape=out_shape,
        scratch_shapes=[
            pltpu.VMEM((D_MODEL, 2 * D_FF), BF16),
            pltpu.VMEM((D_FF, D_MODEL), BF16),
            pltpu.VMEM((STAGE_SLOTS, D_MODEL, STAGE_COLS), F32),
            pltpu.VMEM((STAGE_SLOTS, STAGE_COLS, D_MODEL), F32),
            pltpu.SemaphoreType.DMA((STAGE_SLOTS,)),
            pltpu.VMEM(((FFN_CONV - 1) * SUBLANES, 2 * D_FF), F32),
            pltpu.VMEM((TT, D_FF), BF16),
        ] + out_scratch,
        compiler_params=pltpu.CompilerParams(
            dimension_semantics=("arbitrary", "arbitrary"), vmem_limit_bytes=VMEM_LIMIT),
        name="ffn_layer",
    )(x, *weights, *consts).reshape(B, T, D)


def _row(v):
    return v.reshape(1, -1).astype(F32)


def _pad_lanes(v):
    return jnp.pad(v.astype(F32), (0, LANES - v.shape[0])).reshape(1, LANES)


def _prep_pool_w(w):
    out = jnp.zeros((D_POOL, D_POOL), F32)
    for g in range(len(POOL_WINDOWS)):
        s = g * POOL_GROUP_DIM
        out = lax.dynamic_update_slice(out, w[g].astype(F32), (s, s))
    return out.astype(BF16)


def kernel(x, pre_mix_norm, w_in, pool_w, pool_b, pool_scale, ssd_conv_w, ssd_conv_b, ssd_dt_bias, ssd_a_log, ssd_d, ssd_norm, mlstm_conv_w, mlstm_conv_b, mlstm_i_bias, mlstm_f_bias, mlstm_norm, w_out, post_mix_norm, pre_ffn_norm, ffn_w_up, ffn_conv_w, ffn_conv_b, ffn_w_down, post_ffn_norm):
    depth = w_in.shape[0]
    for l in range(depth):
        gbias = _pad_lanes(jnp.concatenate([ssd_dt_bias[l], mlstm_i_bias[l], mlstm_f_bias[l]]))
        x = _mix_layer(
            l, l == 0, x, w_out, w_in[l].astype(BF16), _row(pre_mix_norm[l]), gbias, _pad_lanes(ssd_a_log[l]),
            ssd_conv_w[l].astype(F32), _row(ssd_conv_b[l]), mlstm_conv_w[l].astype(F32), _row(mlstm_conv_b[l]),
            _prep_pool_w(pool_w[l]), _row(pool_b[l]), _row(pool_scale[l]),
            _row(jnp.repeat(ssd_d[l], SSD_HEAD_DIM)), _row(ssd_norm[l]), _row(mlstm_norm[l]),
            _row(post_mix_norm[l]))
        x = _ffn_layer(
            l, l == depth - 1, x, ffn_w_up, ffn_w_down, _row(pre_ffn_norm[l]), ffn_conv_w[l].astype(F32),
            _row(ffn_conv_b[l]), _row(post_ffn_norm[l]))
    return x
```

```python
import functools
import math

import jax
import jax.numpy as jnp
from jax import lax
from jax.experimental import pallas as pl
from jax.experimental.pallas import tpu as pltpu

F32 = jnp.float32
BF16 = jnp.bfloat16

D_MODEL = 1024
EPS = 1e-6

D_POOL = 256
POOL_GROUP_DIM = 64
POOL_WINDOWS = (2, 4, 8, 16)

D_SSD = 512
SSD_HEADS = 8
SSD_HEAD_DIM = 64
SSD_GROUPS = 2
SSD_STATE = 128
SSD_CONV = 4
D_SSD_XBC = D_SSD + 2 * SSD_GROUPS * SSD_STATE

D_MLSTM = 256
MLSTM_HEADS = 4
MLSTM_HEAD_DIM = 64
MLSTM_CONV = 4

D_FF = 2816
FFN_CONV = 3

CHUNK = 128
LANES = 128
SUBLANES = 8
VROWS = CHUNK // SUBLANES

C_POOL = 0
C_XBC = C_POOL + D_POOL
C_QK = C_XBC + D_SSD_XBC
N_HALO_COLS = C_QK + 2 * D_MLSTM
R_Z = 0
R_V = R_Z + D_SSD
R_O = R_V + D_MLSTM
R_G = R_O + D_MLSTM
N_REST_COLS = R_G + LANES
N_IN_COLS = N_HALO_COLS + N_REST_COLS
G_DT = 0
G_I = SSD_HEADS
G_F = G_I + MLSTM_HEADS

MIX_TT = 256
PROJ_COLS = 256
PROJ_ROWS = 128
STAGE_COLS = 256
STAGE_SLOTS = 4
FFN_TT = 512
FFN_FT = 256
VMEM_LIMIT = 56 * 1024 * 1024


def _dot(a, b):
    return jnp.dot(a, b, preferred_element_type=F32)


def _dot_nt(a, b):
    return lax.dot_general(a, b, (((1,), (1,)), ((), ())), preferred_element_type=F32)


def _silu(x):
    return x * jax.nn.sigmoid(x)


def _split3(a):
    hi = a.astype(BF16)
    r = a - hi.astype(F32)
    mid = r.astype(BF16)
    lo = (r - mid.astype(F32)).astype(BF16)
    return hi, mid, lo


def _bcast_lane(a, j, shape):
    return jnp.broadcast_to(a[:, j:j + 1], shape)


def _pair_expand(a, h_even, shape, lo_half):
    return jnp.where(lo_half, _bcast_lane(a, h_even, shape), _bcast_lane(a, h_even + 1, shape))


def _stage_weights(blocks, stage_ref, sem_ref):
    n_slots = stage_ref.shape[0]

    def copy(n):
        src, dst = blocks[n]
        rows, cols = src.shape
        return pltpu.make_async_copy(src, stage_ref.at[n % n_slots, 0:rows, 0:cols], sem_ref.at[n % n_slots])

    for n in range(min(n_slots - 1, len(blocks))):
        copy(n).start()
    for n, (src, dst) in enumerate(blocks):
        if n + n_slots - 1 < len(blocks):
            copy(n + n_slots - 1).start()
        copy(n).wait()
        rows, cols = src.shape
        dst[...] = stage_ref[n % n_slots, 0:rows, 0:cols].astype(BF16)


def _col_blocks(src_ref, src0, dst_ref, dst0, ncols, step):
    return [(src_ref.at[:, src0 + k:src0 + min(k + step, ncols)],
             dst_ref.at[:, dst0 + k:dst0 + min(k + step, ncols)]) for k in range(0, ncols, step)]


def _tau(p):
    return (p % SUBLANES) * VROWS + p // SUBLANES


def _ext_rows(prev_tail, cur_tail):
    n = cur_tail.shape[0] // SUBLANES
    sub0 = lax.broadcasted_iota(jnp.int32, (SUBLANES, cur_tail.shape[1]), 0) == 0
    out = []
    for j in range(n):
        sl = slice(j * SUBLANES, (j + 1) * SUBLANES)
        out.append(jnp.where(sub0, pltpu.roll(prev_tail[sl], 1, 0), pltpu.roll(cur_tail[sl], 1, 0)))
    return out


def _shifted(ext, cur, k):
    if k == 0:
        return cur
    return jnp.concatenate(ext[len(ext) - k:] + [cur[0:CHUNK - SUBLANES * k]], axis=0)


def _causal_conv(prev_tail, cur, w_ref, b_row, wcol, ncols, k_taps):
    n = k_taps - 1
    ext = _ext_rows(prev_tail, cur[CHUNK - n * SUBLANES:CHUNK])
    acc = b_row
    for k in range(k_taps):
        acc = acc + _shifted(ext, cur, n - k) * w_ref[k:k + 1, wcol:wcol + ncols]
    return acc


def _relayout_w_in(wraw_ref, win_ref):
    c_z = D_POOL
    c_xbc = c_z + D_SSD
    c_dt = c_xbc + D_SSD_XBC
    n_tail = SSD_HEADS + 4 * D_MLSTM + 2 * MLSTM_HEADS
    o_qk = SSD_HEADS
    o_v = o_qk + 2 * D_MLSTM
    o_if = o_v + 2 * D_MLSTM
    for r in range(0, D_MODEL, CHUNK):
        rows = slice(r, r + CHUNK)
        win_ref[rows, C_POOL:C_POOL + D_POOL] = wraw_ref[rows, 0:D_POOL]
        win_ref[rows, C_XBC:C_XBC + D_SSD_XBC] = wraw_ref[rows, c_xbc:c_dt]
        win_ref[rows, N_HALO_COLS + R_Z:N_HALO_COLS + R_Z + D_SSD] = wraw_ref[rows, c_z:c_xbc]
        tail = wraw_ref[rows, c_dt:c_dt + n_tail]
        win_ref[rows, C_QK:C_QK + 2 * D_MLSTM] = tail[:, o_qk:o_v]
        win_ref[rows, N_HALO_COLS + R_V:N_HALO_COLS + R_V + 2 * D_MLSTM] = tail[:, o_v:o_if]
        pad = jnp.zeros((CHUNK, LANES - SSD_HEADS - 2 * MLSTM_HEADS), tail.dtype)
        win_ref[rows, N_HALO_COLS + R_G:N_HALO_COLS + R_G + LANES] = jnp.concatenate(
            [tail[:, 0:o_qk], tail[:, o_if:n_tail], pad], axis=1)


def mix_kernel(layer, from_time_order, xc_ref, xp_ref, wout_hbm, wraw_ref, nw_ref, gbias_ref, alog_ref,
               xcw_ref, xcb_ref, qcw_ref, qcb_ref,
               poolw_ref, poolb_ref, pools_ref, dskip_ref, snorm_ref, mnorm_ref, postn_ref,
               out_ref, win_ref, wout_ref, stage_ref, stage_sem,
               h_ref, hp_ref, rest_ref, hph_ref, psh_ref, act_ref, mix_ref, ps_ref,
               sstate_ref, mstate_ref, mm_ref, *in_scratch):
    TT = hp_ref.shape[1]
    L = CHUNK
    n_c = TT // L
    b = pl.program_id(0)
    i = pl.program_id(1)
    n_t = pl.num_programs(1) - 1

    if from_time_order:
        xbuf_ref, in_sem = in_scratch
        n_slots = xbuf_ref.shape[0]

        def in_copies(tile):
            sl = tile % n_slots
            return [pltpu.make_async_copy(xc_ref.at[b, pl.ds(tile * n_c, n_c), s], xbuf_ref.at[sl, :, :, s, :],
                                          in_sem.at[sl]) for s in range(SUBLANES)]

        @pl.when(i == 0)
        def _():
            for cp in in_copies(0):
                cp.start()

        @pl.when(i + 1 < n_t)
        def _():
            for cp in in_copies(i + 1):
                cp.start()

        @pl.when(i < n_t)
        def _():
            for cp in in_copies(i):
                cp.wait()

        slot_cur = jnp.minimum(i, n_t - 1) % n_slots
        slot_prev = jnp.maximum(i - 1, 0) % n_slots

        def x_cur():
            return xbuf_ref[slot_cur].reshape(TT, xbuf_ref.shape[-1])

        def x_rows(r0):
            return xbuf_ref[slot_prev, r0 // L].reshape(L, xbuf_ref.shape[-1])
    else:
        def x_cur():
            return xc_ref[0]

        def x_rows(r0):
            return xp_ref[0, r0:r0 + L, :]

    @pl.when((pl.program_id(0) == 0) & (i == 0))
    def _():
        _relayout_w_in(wraw_ref, win_ref)
        _stage_weights(_col_blocks(wout_hbm.at[layer], 0, wout_ref, 0, D_MODEL, STAGE_COLS),
                       stage_ref, stage_sem)

    @pl.when(i == 0)
    def _():
        hp_ref[1] = jnp.zeros(hp_ref.shape[1:], F32)
        rest_ref[1] = jnp.zeros(rest_ref.shape[1:], F32)

    @pl.when(i <= 1)
    def _():
        hph_ref[...] = jnp.zeros(hph_ref.shape, F32)
        psh_ref[...] = jnp.zeros(psh_ref.shape, F32)
        sstate_ref[...] = jnp.zeros(sstate_ref.shape, F32)
        mstate_ref[...] = jnp.zeros(mstate_ref.shape, F32)
        mm_ref[...] = jnp.zeros(mm_ref.shape, F32)

    def step(slot_proj, slot_mix):
        x = x_cur()
        ms = jnp.mean(x * x, axis=-1, keepdims=True)
        h_ref[...] = (x * lax.rsqrt(ms + EPS) * nw_ref[...]).astype(BF16)

        def proj_piece(dst_ref, r0, c0, c1, w0):
            def piece():
                dst_ref[slot_proj, r0:r0 + PROJ_ROWS, c0:c1] = _dot(
                    h_ref[r0:r0 + PROJ_ROWS, :], win_ref[:, w0 + c0:w0 + c1])
            return piece

        pending = []
        for r0 in range(0, TT, PROJ_ROWS):
            pending += [proj_piece(hp_ref, r0, c0, min(c0 + PROJ_COLS, N_HALO_COLS), 0)
                        for c0 in range(0, N_HALO_COLS, PROJ_COLS)]
            pending += [proj_piece(rest_ref, r0, c0, min(c0 + PROJ_COLS, N_REST_COLS), N_HALO_COLS)
                        for c0 in range(0, N_REST_COLS, PROJ_COLS)]
        _mixers(jnp.maximum(i - 1, 0), x_rows, hp_ref.at[slot_mix], rest_ref.at[slot_mix], gbias_ref, alog_ref,
                xcw_ref, xcb_ref, qcw_ref, qcb_ref, poolw_ref, poolb_ref, pools_ref, dskip_ref, snorm_ref,
                mnorm_ref, wout_ref, postn_ref, out_ref, hph_ref, psh_ref, act_ref, mix_ref, ps_ref,
                sstate_ref, mstate_ref, mm_ref, pending)

    @pl.when(i % 2 == 0)
    def _():
        step(0, 1)

    @pl.when(i % 2 == 1)
    def _():
        step(1, 0)


def _mixers(tile, x_rows, hp_ref, rest_ref, gbias_ref, alog_ref, xcw_ref, xcb_ref, qcw_ref, qcb_ref,
            poolw_ref, poolb_ref, pools_ref, dskip_ref, snorm_ref, mnorm_ref, wout_ref, postn_ref,
            out_ref, hph_ref, psh_ref, act_ref, mix_ref, ps_ref, sstate_ref, mstate_ref, mm_ref, pending):
    def between():
        if pending:
            pending.pop(0)()

    TT = hp_ref.shape[0]
    L = CHUNK
    n_hph = hph_ref.shape[0]
    n_psh = psh_ref.shape[1]
    ps_carried = ((0, 0), (0, 1), (1, 1), (2, 1))

    def tail(cur_ref, halo, n_halo, r0, n_rows, cols):
        if r0 == 0:
            return halo[n_halo - n_rows:n_halo, cols]
        return cur_ref[r0 - n_rows:r0, cols]

    lane = lax.broadcasted_iota(jnp.int32, (L, LANES), 1)
    row = lax.broadcasted_iota(jnp.int32, (L, LANES), 0)
    lo_half = lane < 64
    tau_row = _tau(row)
    causal = _tau(lane) <= tau_row
    tril = jnp.where(causal, 1.0, 0.0).astype(BF16)
    lane_row = lax.broadcasted_iota(jnp.int32, (1, LANES), 1)
    a_row = -jnp.exp(alog_ref[...])
    neg_inf = -jnp.inf
    win_blk = [jnp.where(lo_half, float(POOL_WINDOWS[2 * b]), float(POOL_WINDOWS[2 * b + 1])) for b in range(2)]
    tau_f = tau_row.astype(F32)

    def chunk_stages(c):
        r0 = c * L

        def conv_block(col, cw_ref, cb_ref, wcol, k_taps):
            cols = slice(col, col + LANES)
            cur = hp_ref[r0:r0 + L, cols]
            prev_tail = tail(hp_ref, hph_ref, n_hph, r0, (k_taps - 1) * SUBLANES, cols)
            return _silu(_causal_conv(prev_tail, cur, cw_ref, cb_ref[:, wcol:wcol + LANES], wcol, LANES, k_taps))

        for blk in range(D_SSD_XBC // LANES):
            act_ref[r0:r0 + L, blk * LANES:(blk + 1) * LANES] = conv_block(
                C_XBC + blk * LANES, xcw_ref, xcb_ref, blk * LANES, SSD_CONV)
        between()
        for blk in range(2 * D_MLSTM // LANES):
            act_ref[r0:r0 + L, D_SSD_XBC + blk * LANES:D_SSD_XBC + (blk + 1) * LANES] = conv_block(
                C_QK + blk * LANES, qcw_ref, qcb_ref, blk * LANES, MLSTM_CONV)
        yield

        pos = tau_f + (tile * TT + r0 + 1).astype(F32)
        pooled_blocks = []
        for b in range(2):
            cs_ = slice(b * LANES, (b + 1) * LANES)
            u_cur = hp_ref[r0:r0 + L, cs_]
            lvl = u_cur
            sums = []
            for li, sh in enumerate((1, 2, 4, 8)):
                if li == 0:
                    prev_tail = tail(hp_ref, hph_ref, n_hph, r0, sh * SUBLANES, cs_)
                else:
                    prev_tail = tail(ps_ref.at[li - 1], psh_ref.at[li - 1], n_psh, r0, sh * SUBLANES, cs_)
                ext = _ext_rows(prev_tail, lvl[L - sh * SUBLANES:L])
                lvl = lvl + _shifted(ext, lvl, sh)
                sums.append(lvl)
                if (li, b) in ps_carried:
                    ps_ref[li, r0:r0 + L, cs_] = lvl
                if b == 0 and li == 1:
                    break
            wsum = jnp.where(lo_half, sums[0], sums[1]) if b == 0 else jnp.where(lo_half, sums[2], sums[3])
            pooled_blocks.append((wsum / jnp.minimum(pos, win_blk[b]) - u_cur).astype(BF16))
        mix_ref[r0:r0 + L, 0:D_POOL] = (
            (_dot(jnp.concatenate(pooled_blocks, axis=1), poolw_ref[...]) + poolb_ref[...]) * pools_ref[...])
        yield

        gb = rest_ref[r0:r0 + L, R_G:R_G + LANES] + gbias_ref[...]
        sp_term = jnp.log1p(jnp.exp(-jnp.abs(gb)))
        dt = jnp.maximum(gb, 0.0) + sp_term
        log_f = jnp.minimum(gb, 0.0) - sp_term
        is_dt = lane < G_I
        is_f = (lane >= G_F) & (lane < G_F + MLSTM_HEADS)
        v_cum = jnp.where(is_dt, dt * a_row, jnp.where(is_f, log_f, 0.0))
        hi, mid, lo = _split3(v_cum)
        cs = _dot(tril, hi) + _dot(tril, mid) + _dot(tril, lo)
        u_gate = jnp.where(is_dt, dt, gb)
        cs_t = cs.T
        ug_t = u_gate.T
        cs_last = cs[L - 1:L, :]
        e_col = jnp.exp(cs)
        w_col = jnp.exp(cs_last - cs) * dt
        e_last = jnp.exp(cs_last)
        yield

        for g in range(SSD_GROUPS):
            b_g = act_ref[r0:r0 + L, D_SSD + g * SSD_STATE:D_SSD + (g + 1) * SSD_STATE].astype(BF16)
            c_g = act_ref[r0:r0 + L, D_SSD + (SSD_GROUPS + g) * SSD_STATE:
                          D_SSD + (SSD_GROUPS + g + 1) * SSD_STATE].astype(BF16)
            s_g = _dot_nt(c_g, b_g)
            state_g = sstate_ref[g]
            y_off = _dot(c_g, state_g.astype(BF16))
            xd_blocks = []
            cd_blocks = []
            for pr in range(2):
                h_even = 4 * g + 2 * pr
                col = h_even * SSD_HEAD_DIM
                xs = act_ref[r0:r0 + L, col:col + LANES]
                xs_b = xs.astype(BF16)
                yd = []
                for hh in range(2):
                    hd = h_even + hh
                    seg = jnp.where(causal, cs[:, hd:hd + 1] - cs_t[hd:hd + 1, :], neg_inf)
                    m_h = (s_g * (jnp.exp(seg) * ug_t[hd:hd + 1, :])).astype(BF16)
                    yd.append(_dot(m_h, xs_b))
                y_diag = jnp.where(lo_half, yd[0], yd[1])
                e_exp = _pair_expand(e_col, h_even, (L, LANES), lo_half)
                w_exp = _pair_expand(w_col, h_even, (L, LANES), lo_half)
                y = (y_diag + y_off[:, pr * LANES:(pr + 1) * LANES] * e_exp
                     + xs * dskip_ref[:, col:col + LANES])
                z = rest_ref[r0:r0 + L, R_Z + col:R_Z + col + LANES]
                mix_ref[r0:r0 + L, D_POOL + col:D_POOL + col + LANES] = y * _silu(z)
                xd_blocks.append((xs * w_exp).astype(BF16))
                cd_blocks.append(_pair_expand(e_last, h_even, (1, LANES), lane_row < 64))
                if pr == 0:
                    between()
            xd_g = jnp.concatenate(xd_blocks, axis=1)
            cd_g = jnp.concatenate(cd_blocks, axis=1)
            new_states = lax.dot_general(b_g, xd_g, (((0,), (0,)), ((), ())),
                                         preferred_element_type=F32)
            sstate_ref[g] = state_g * cd_g + new_states
            yield
        y_all = mix_ref[r0:r0 + L, D_POOL:D_POOL + D_SSD]
        ms_y = jnp.mean(y_all * y_all, axis=-1, keepdims=True)
        mix_ref[r0:r0 + L, D_POOL:D_POOL + D_SSD] = y_all * lax.rsqrt(ms_y + EPS) * snorm_ref[...]

        for pr in range(MLSTM_HEADS // 2):
            qcol = D_SSD_XBC + pr * LANES
            kcol = D_SSD_XBC + D_MLSTM + pr * LANES
            q_b = act_ref[r0:r0 + L, qcol:qcol + LANES] * (MLSTM_HEAD_DIM ** -0.5)
            k_t = act_ref[r0:r0 + L, kcol:kcol + LANES].T
            k_tb = k_t.astype(BF16)
            v_b = rest_ref[r0:r0 + L, R_V + pr * LANES:R_V + (pr + 1) * LANES]
            o_b = rest_ref[r0:r0 + L, R_O + pr * LANES:R_O + (pr + 1) * LANES]
            hv = []
            for hh in range(2):
                hd = 2 * pr + hh
                in_half = lo_half if hh == 0 else jnp.logical_not(lo_half)
                row_in_half = (row < 64) if hh == 0 else (row >= 64)
                ones_lane = (lane == 64) if hh == 0 else (lane == 0)
                ol = 64 if hh == 0 else 0
                qm = jnp.where(in_half, q_b, 0.0).astype(BF16)
                s = _dot(qm, k_tb)
                b_row = cs_t[G_F + hd:G_F + hd + 1, :]
                r_row = ug_t[G_I + hd:G_I + hd + 1, :] - b_row
                b_last = jnp.sum(jnp.where(lane_row == L - 1, b_row, 0.0), axis=-1, keepdims=True)
                al_row = b_last + r_row
                m_loc = jnp.max(al_row, axis=-1, keepdims=True)
                prev_m = mm_ref[hd:hd + 1, 0:1]
                rmask = jnp.where(causal, r_row, neg_inf)
                g_col = jnp.maximum(jnp.max(rmask, axis=-1, keepdims=True), prev_m)
                p = (s * jnp.exp(rmask - g_col)).astype(BF16)
                v_ext = jnp.where(in_half, v_b, jnp.where(ones_lane, 1.0, 0.0)).astype(BF16)
                cn = mstate_ref[hd]
                res = _dot(p, v_ext) + _dot(qm, cn.astype(BF16)) * jnp.exp(prev_m - g_col)
                den = jnp.maximum(jnp.abs(res[:, ol:ol + 1]),
                                  jnp.exp(-(cs[:, G_F + hd:G_F + hd + 1] + g_col)))
                hv.append(res / den)
                w_row = jnp.exp(al_row - m_loc)
                ktw = jnp.where(row_in_half, k_t * w_row, 0.0).astype(BF16)
                c_loc = _dot(ktw, v_ext)
                m_new = jnp.maximum(b_last + prev_m, m_loc)
                mstate_ref[hd] = (jnp.exp(b_last + prev_m - m_new) * cn
                                  + jnp.exp(m_loc - m_new) * c_loc)
                mm_ref[hd:hd + 1, :] = jnp.broadcast_to(m_new, (1, LANES))
                if hh == 0:
                    between()
            hcat = jax.nn.sigmoid(o_b) * jnp.where(lo_half, hv[0], hv[1])
            sq = hcat * hcat
            ss_lo = jnp.sum(jnp.where(lo_half, sq, 0.0), axis=-1, keepdims=True)
            ss_hi = jnp.sum(jnp.where(lo_half, 0.0, sq), axis=-1, keepdims=True)
            inv = jnp.where(lo_half, lax.rsqrt(ss_lo * (1.0 / MLSTM_HEAD_DIM) + EPS),
                            lax.rsqrt(ss_hi * (1.0 / MLSTM_HEAD_DIM) + EPS))
            mcol = D_POOL + D_SSD + pr * LANES
            mix_ref[r0:r0 + L, mcol:mcol + LANES] = hcat * inv * mnorm_ref[:, pr * LANES:(pr + 1) * LANES]
            if pr + 1 < MLSTM_HEADS // 2:
                yield

        def out_piece():
            o = _dot(mix_ref[r0:r0 + L, :].astype(BF16), wout_ref[...])
            ms_o = jnp.mean(o * o, axis=-1, keepdims=True)
            out_ref[0, r0:r0 + L, :] = x_rows(r0) + o * lax.rsqrt(ms_o + EPS) * postn_ref[...]

        pending.insert(0, out_piece)
        yield

    n_stages = 3 + SSD_GROUPS + MLSTM_HEADS // 2
    gens = [chunk_stages(c) for c in range(TT // L)]
    for _ in range(n_stages):
        for gen in gens:
            next(gen)
            between()
    while pending:
        pending.pop(0)()

    hph_ref[...] = hp_ref[TT - n_hph:TT, :]
    for lv, b in ps_carried:
        cs_ = slice(b * LANES, (b + 1) * LANES)
        psh_ref[lv, :, cs_] = ps_ref[lv, TT - n_psh:TT, cs_]


def ffn_kernel(layer, to_time_order, x_ref, wup_hbm, wdn_hbm, nw_ref, cw_ref, cb_ref, postn_ref, out_ref,
               wup_ref, wdn_ref, stage_ref, stage_dn_ref, stage_sem, halo_ref, a_ref, *out_scratch):
    TT = x_ref.shape[1]
    FT = FFN_FT
    L = CHUNK
    n_tail = (FFN_CONV - 1) * SUBLANES
    i = pl.program_id(1)

    @pl.when((pl.program_id(0) == 0) & (i == 0))
    def _():
        _stage_weights(_col_blocks(wup_hbm.at[layer], 0, wup_ref, 0, 2 * D_FF, STAGE_COLS),
                       stage_ref, stage_sem)
        _stage_weights([(wdn_hbm.at[layer, r:r + STAGE_COLS, :], wdn_ref.at[r:r + STAGE_COLS, :])
                        for r in range(0, D_FF, STAGE_COLS)], stage_dn_ref, stage_sem)

    @pl.when(i == 0)
    def _():
        halo_ref[...] = jnp.zeros(halo_ref.shape, F32)

    x = x_ref[0]
    ms = jnp.mean(x * x, axis=-1, keepdims=True)
    h = (x * lax.rsqrt(ms + EPS) * nw_ref[...]).astype(BF16)

    def conv_cols(col):
        u = _dot(h, wup_ref[:, col:col + FT])
        outs = []
        for c in range(TT // L):
            cur = u[c * L:(c + 1) * L]
            prev_tail = halo_ref[:, col:col + FT] if c == 0 else u[c * L - n_tail:c * L]
            outs.append(_causal_conv(prev_tail, cur, cw_ref, cb_ref[:, col:col + FT], col, FT, FFN_CONV))
        halo_ref[:, col:col + FT] = u[TT - n_tail:TT]
        return outs

    for j in range(D_FF // FT):
        gts = conv_cols(j * FT)
        vals = conv_cols(D_FF + j * FT)
        for c in range(TT // L):
            gt = gts[c]
            gelu = 0.5 * gt * (1.0 + jnp.tanh(math.sqrt(2.0 / math.pi) * (gt + 0.044715 * (gt * gt * gt))))
            a_ref[c * L:(c + 1) * L, j * FT:(j + 1) * FT] = (gelu * vals[c]).astype(BF16)

    f = _dot(a_ref[...], wdn_ref[...])
    ms_f = jnp.mean(f * f, axis=-1, keepdims=True)
    res = x_ref[0] + f * lax.rsqrt(ms_f + EPS) * postn_ref[...]
    if not to_time_order:
        out_ref[0] = res
        return

    obuf_ref, out_sem = out_scratch
    n_c = TT // L
    b = pl.program_id(0)
    step = b * pl.num_programs(1) + i
    n_steps = pl.num_programs(0) * pl.num_programs(1)
    slot = step % 2

    def out_copies(sl):
        return [pltpu.make_async_copy(obuf_ref.at[sl, :, :, s, :], out_ref.at[b, pl.ds(i * n_c, n_c), s],
                                      out_sem.at[sl]) for s in range(SUBLANES)]

    @pl.when(step >= 2)
    def _():
        for cp in out_copies(slot):
            cp.wait()

    obuf_ref[slot] = res.reshape(n_c, VROWS, SUBLANES, res.shape[1])
    for cp in out_copies(slot):
        cp.start()

    @pl.when(step == n_steps - 1)
    def _():
        for cp in out_copies(slot):
            cp.wait()

    @pl.when((step == n_steps - 1) & (n_steps >= 2))
    def _():
        for cp in out_copies(1 - slot):
            cp.wait()


def _const_spec(shape):
    nd = len(shape)
    return pl.BlockSpec(shape, lambda b, i: (0,) * nd, pipeline_mode=pl.Buffered(1))


def _mix_layer(layer, from_time_order, x, wout, win, nw, gbias, alog, xcw, xcb, qcw, qcb, poolw, poolb, pools,
               dskip, snorm, mnorm, postn):
    B, T, D = x.shape
    TT = MIX_TT
    weights = (wout,)
    consts = (win, nw, gbias, alog, xcw, xcb, qcw, qcb, poolw, poolb, pools, dskip, snorm, mnorm, postn)
    n_t = T // TT
    cur_spec = pl.BlockSpec((1, TT, D), lambda b, i: (b, jnp.minimum(i, n_t - 1), 0))
    prev_spec = pl.BlockSpec((1, TT, D), lambda b, i: (b, jnp.maximum(i - 1, 0), 0))
    max_conv_tail = (max(SSD_CONV, MLSTM_CONV) - 1) * SUBLANES
    max_pool_tail = (POOL_WINDOWS[-1] // 2) * SUBLANES
    if from_time_order:
        x_in = x.reshape(B, T // CHUNK, SUBLANES, VROWS, D)
        x_specs = [pl.BlockSpec(memory_space=pltpu.HBM)] * 2
        in_scratch = [pltpu.VMEM((3, TT // CHUNK, VROWS, SUBLANES, D), F32),
                      pltpu.SemaphoreType.DMA((3,))]
    else:
        x_in, x_specs, in_scratch = x, [cur_spec, prev_spec], []
    return pl.pallas_call(
        functools.partial(mix_kernel, layer, from_time_order),
        grid=(B, n_t + 1),
        in_specs=(x_specs + [pl.BlockSpec(memory_space=pltpu.HBM) for _ in weights]
                  + [_const_spec(c.shape) for c in consts]),
        out_specs=prev_spec,
        out_shape=jax.ShapeDtypeStruct(x.shape, x.dtype),
        scratch_shapes=[
            pltpu.VMEM((D_MODEL, N_IN_COLS), BF16),
            pltpu.VMEM((D_MODEL, D_MODEL), BF16),
            pltpu.VMEM((STAGE_SLOTS, D_MODEL, STAGE_COLS), F32),
            pltpu.SemaphoreType.DMA((STAGE_SLOTS,)),
            pltpu.VMEM((TT, D_MODEL), BF16),
            pltpu.VMEM((2, TT, N_HALO_COLS), F32),
            pltpu.VMEM((2, TT, N_REST_COLS), F32),
            pltpu.VMEM((max_conv_tail, N_HALO_COLS), F32),
            pltpu.VMEM((3, max_pool_tail, D_POOL), F32),
            pltpu.VMEM((TT, D_SSD_XBC + 2 * D_MLSTM), F32),
            pltpu.VMEM((TT, D_MODEL), F32),
            pltpu.VMEM((3, TT, D_POOL), F32),
            pltpu.VMEM((SSD_GROUPS, SSD_STATE, 4 * SSD_HEAD_DIM), F32),
            pltpu.VMEM((MLSTM_HEADS, LANES, LANES), F32),
            pltpu.VMEM((SUBLANES, LANES), F32),
        ] + in_scratch,
        compiler_params=pltpu.CompilerParams(
            dimension_semantics=("arbitrary", "arbitrary"), vmem_limit_bytes=VMEM_LIMIT),
        name="mix_layer",
    )(x_in, x_in, *weights, *consts)


def _ffn_layer(layer, to_time_order, x, wup, wdn, nw, cw, cb, postn):
    B, T, D = x.shape
    TT = FFN_TT
    weights = (wup, wdn)
    consts = (nw, cw, cb, postn)
    x_spec = pl.BlockSpec((1, TT, D), lambda b, i: (b, i, 0))
    if to_time_order:
        out_spec = pl.BlockSpec(memory_space=pltpu.HBM)
        out_shape = jax.ShapeDtypeStruct((B, T // CHUNK, SUBLANES, VROWS, D), x.dtype)
        out_scratch = [pltpu.VMEM((2, TT // CHUNK, VROWS, SUBLANES, D), F32),
                       pltpu.SemaphoreType.DMA((2,))]
    else:
        out_spec, out_shape, out_scratch = x_spec, jax.ShapeDtypeStruct(x.shape, x.dtype), []
    return pl.pallas_call(
        functools.partial(ffn_kernel, layer, to_time_order),
        grid=(B, T // TT),
        in_specs=([x_spec] + [pl.BlockSpec(memory_space=pltpu.HBM) for _ in weights]
                  + [_const_spec(c.shape) for c in consts]),
        out_specs=out_spec,
        out_shape=out_shape,
        scratch_shapes=[
            pltpu.VMEM((D_MODEL, 2 * D_FF), BF16),
            pltpu.VMEM((D_FF, D_MODEL), BF16),
            pltpu.VMEM((STAGE_SLOTS, D_MODEL, STAGE_COLS), F32),
            pltpu.VMEM((STAGE_SLOTS, STAGE_COLS, D_MODEL), F32),
            pltpu.SemaphoreType.DMA((STAGE_SLOTS,)),
            pltpu.VMEM(((FFN_CONV - 1) * SUBLANES, 2 * D_FF), F32),
            pltpu.VMEM((TT, D_FF), BF16),
        ] + out_scratch,
        compiler_params=pltpu.CompilerParams(
            dimension_semantics=("arbitrary", "arbitrary"), vmem_limit_bytes=VMEM_LIMIT),
        name="ffn_layer",
    )(x, *weights, *consts).reshape(B, T, D)


def _row(v):
    return v.reshape(1, -1).astype(F32)


def _pad_lanes(v):
    return jnp.pad(v.astype(F32), (0, LANES - v.shape[0])).reshape(1, LANES)


def _prep_pool_w(w):
    out = jnp.zeros((D_POOL, D_POOL), F32)
    for g in range(len(POOL_WINDOWS)):
        s = g * POOL_GROUP_DIM
        out = lax.dynamic_update_slice(out, w[g].astype(F32), (s, s))
    return out.astype(BF16)


def kernel(x, pre_mix_norm, w_in, pool_w, pool_b, pool_scale, ssd_conv_w, ssd_conv_b, ssd_dt_bias, ssd_a_log, ssd_d, ssd_norm, mlstm_conv_w, mlstm_conv_b, mlstm_i_bias, mlstm_f_bias, mlstm_norm, w_out, post_mix_norm, pre_ffn_norm, ffn_w_up, ffn_conv_w, ffn_conv_b, ffn_w_down, post_ffn_norm):
    depth = w_in.shape[0]
    for l in range(depth):
        gbias = _pad_lanes(jnp.concatenate([ssd_dt_bias[l], mlstm_i_bias[l], mlstm_f_bias[l]]))
        x = _mix_layer(
            l, l == 0, x, w_out, w_in[l].astype(BF16), _row(pre_mix_norm[l]), gbias, _pad_lanes(ssd_a_log[l]),
            ssd_conv_w[l].astype(F32), _row(ssd_conv_b[l]), mlstm_conv_w[l].astype(F32), _row(mlstm_conv_b[l]),
            _prep_pool_w(pool_w[l]), _row(pool_b[l]), _row(pool_scale[l]),
            _row(jnp.repeat(ssd_d[l], SSD_HEAD_DIM)), _row(ssd_norm[l]), _row(mlstm_norm[l]),
            _row(post_mix_norm[l]))
        x = _ffn_layer(
            l, l == depth - 1, x, ffn_w_up, ffn_w_down, _row(pre_ffn_norm[l]), ffn_conv_w[l].astype(F32),
            _row(ffn_conv_b[l]), _row(post_ffn_norm[l]))
    return x
```

```python
import functools
import math

import jax
import jax.numpy as jnp
from jax import lax
from jax.experimental import pallas as pl
from jax.experimental.pallas import tpu as pltpu

F32 = jnp.float32
BF16 = jnp.bfloat16

D_MODEL = 1024
EPS = 1e-6

D_POOL = 256
POOL_GROUP_DIM = 64
POOL_WINDOWS = (2, 4, 8, 16)

D_SSD = 512
SSD_HEADS = 8
SSD_HEAD_DIM = 64
SSD_GROUPS = 2
SSD_STATE = 128
SSD_CONV = 4
D_SSD_XBC = D_SSD + 2 * SSD_GROUPS * SSD_STATE

D_MLSTM = 256
MLSTM_HEADS = 4
MLSTM_HEAD_DIM = 64
MLSTM_CONV = 4

D_FF = 2816
FFN_CONV = 3

CHUNK = 128
LANES = 128
SUBLANES = 8
VROWS = CHUNK // SUBLANES

C_POOL = 0
C_XBC = C_POOL + D_POOL
C_QK = C_XBC + D_SSD_XBC
N_HALO_COLS = C_QK + 2 * D_MLSTM
R_Z = 0
R_V = R_Z + D_SSD
R_O = R_V + D_MLSTM
R_G = R_O + D_MLSTM
N_REST_COLS = R_G + LANES
N_IN_COLS = N_HALO_COLS + N_REST_COLS
G_DT = 0
G_I = SSD_HEADS
G_F = G_I + MLSTM_HEADS

MIX_TT = 256
PROJ_COLS = 256
STAGE_COLS = 256
STAGE_SLOTS = 4
FFN_TT = 512
FFN_FT = 256
VMEM_LIMIT = 56 * 1024 * 1024


def _dot(a, b):
    return jnp.dot(a, b, preferred_element_type=F32)


def _dot_nt(a, b):
    return lax.dot_general(a, b, (((1,), (1,)), ((), ())), preferred_element_type=F32)


def _silu(x):
    return x * jax.nn.sigmoid(x)


def _split3(a):
    hi = a.astype(BF16)
    r = a - hi.astype(F32)
    mid = r.astype(BF16)
    lo = (r - mid.astype(F32)).astype(BF16)
    return hi, mid, lo


def _bcast_lane(a, j, shape):
    return jnp.broadcast_to(a[:, j:j + 1], shape)


def _pair_expand(a, h_even, shape, lo_half):
    return jnp.where(lo_half, _bcast_lane(a, h_even, shape), _bcast_lane(a, h_even + 1, shape))


def _stage_weights(blocks, stage_ref, sem_ref):
    n_slots = stage_ref.shape[0]

    def copy(n):
        src, dst = blocks[n]
        rows, cols = src.shape
        return pltpu.make_async_copy(src, stage_ref.at[n % n_slots, 0:rows, 0:cols], sem_ref.at[n % n_slots])

    for n in range(min(n_slots - 1, len(blocks))):
        copy(n).start()
    for n, (src, dst) in enumerate(blocks):
        if n + n_slots - 1 < len(blocks):
            copy(n + n_slots - 1).start()
        copy(n).wait()
        rows, cols = src.shape
        dst[...] = stage_ref[n % n_slots, 0:rows, 0:cols].astype(BF16)


def _col_blocks(src_ref, src0, dst_ref, dst0, ncols, step):
    return [(src_ref.at[:, src0 + k:src0 + min(k + step, ncols)],
             dst_ref.at[:, dst0 + k:dst0 + min(k + step, ncols)]) for k in range(0, ncols, step)]


def _tau(p):
    return (p % SUBLANES) * VROWS + p // SUBLANES


def _ext_rows(prev_tail, cur_tail):
    n = cur_tail.shape[0] // SUBLANES
    sub0 = lax.broadcasted_iota(jnp.int32, (SUBLANES, cur_tail.shape[1]), 0) == 0
    out = []
    for j in range(n):
        sl = slice(j * SUBLANES, (j + 1) * SUBLANES)
        out.append(jnp.where(sub0, pltpu.roll(prev_tail[sl], 1, 0), pltpu.roll(cur_tail[sl], 1, 0)))
    return out


def _shifted(ext, cur, k):
    if k == 0:
        return cur
    return jnp.concatenate(ext[len(ext) - k:] + [cur[0:CHUNK - SUBLANES * k]], axis=0)


def _causal_conv(prev_tail, cur, w_ref, b_row, wcol, ncols, k_taps):
    n = k_taps - 1
    ext = _ext_rows(prev_tail, cur[CHUNK - n * SUBLANES:CHUNK])
    acc = b_row
    for k in range(k_taps):
        acc = acc + _shifted(ext, cur, n - k) * w_ref[k:k + 1, wcol:wcol + ncols]
    return acc


def _relayout_w_in(wraw_ref, win_ref):
    c_z = D_POOL
    c_xbc = c_z + D_SSD
    c_dt = c_xbc + D_SSD_XBC
    n_tail = SSD_HEADS + 4 * D_MLSTM + 2 * MLSTM_HEADS
    o_qk = SSD_HEADS
    o_v = o_qk + 2 * D_MLSTM
    o_if = o_v + 2 * D_MLSTM
    for r in range(0, D_MODEL, CHUNK):
        rows = slice(r, r + CHUNK)
        win_ref[rows, C_POOL:C_POOL + D_POOL] = wraw_ref[rows, 0:D_POOL]
        win_ref[rows, C_XBC:C_XBC + D_SSD_XBC] = wraw_ref[rows, c_xbc:c_dt]
        win_ref[rows, N_HALO_COLS + R_Z:N_HALO_COLS + R_Z + D_SSD] = wraw_ref[rows, c_z:c_xbc]
        tail = wraw_ref[rows, c_dt:c_dt + n_tail]
        win_ref[rows, C_QK:C_QK + 2 * D_MLSTM] = tail[:, o_qk:o_v]
        win_ref[rows, N_HALO_COLS + R_V:N_HALO_COLS + R_V + 2 * D_MLSTM] = tail[:, o_v:o_if]
        pad = jnp.zeros((CHUNK, LANES - SSD_HEADS - 2 * MLSTM_HEADS), tail.dtype)
        win_ref[rows, N_HALO_COLS + R_G:N_HALO_COLS + R_G + LANES] = jnp.concatenate(
            [tail[:, 0:o_qk], tail[:, o_if:n_tail], pad], axis=1)


def mix_kernel(layer, from_time_order, n_tiles, xc_ref, xp_ref, wout_hbm, wraw_ref, nw_ref, gbias_ref, alog_ref,
               xcw_ref, xcb_ref, qcw_ref, qcb_ref,
               poolw_ref, poolb_ref, pools_ref, dskip_ref, snorm_ref, mnorm_ref, postn_ref,
               out_ref, win_ref, wout_ref, stage_ref, stage_sem,
               h_ref, hp_ref, rest_ref, hph_ref, psh_ref, act_ref, mix_ref, ps_ref,
               sstate_ref, mstate_ref, mm_ref, *in_scratch):
    TT = hp_ref.shape[1]
    L = CHUNK
    n_c = TT // L
    b = pl.program_id(0)
    i = pl.program_id(1)
    n_t = n_tiles

    if from_time_order:
        xbuf_ref, in_sem = in_scratch
        n_slots = xbuf_ref.shape[0]

        def in_copies(tile):
            sl = tile % n_slots
            return [pltpu.make_async_copy(xc_ref.at[b, pl.ds(tile * n_c, n_c), s], xbuf_ref.at[sl, :, :, s, :],
                                          in_sem.at[sl]) for s in range(SUBLANES)]

        @pl.when(i == 0)
        def _():
            for cp in in_copies(0):
                cp.start()

        @pl.when(i + 1 < n_t)
        def _():
            for cp in in_copies(i + 1):
                cp.start()

        @pl.when(i < n_t)
        def _():
            for cp in in_copies(i):
                cp.wait()

        slot_cur = jnp.minimum(i, n_t - 1) % n_slots
        slot_prev = jnp.maximum(i - 1, 0) % n_slots

        def x_cur():
            return xbuf_ref[slot_cur].reshape(TT, xbuf_ref.shape[-1])

        def x_rows(r0):
            return xbuf_ref[slot_prev, r0 // L].reshape(L, xbuf_ref.shape[-1])
    else:
        def x_cur():
            return xc_ref[0]

        def x_rows(r0):
            return xp_ref[0, r0:r0 + L, :]

    @pl.when((pl.program_id(0) == 0) & (i == 0))
    def _():
        _relayout_w_in(wraw_ref, win_ref)
        _stage_weights(_col_blocks(wout_hbm.at[layer], 0, wout_ref, 0, D_MODEL, STAGE_COLS),
                       stage_ref, stage_sem)

    @pl.when(i == 0)
    def _():
        hph_ref[...] = jnp.zeros(hph_ref.shape, F32)
        psh_ref[...] = jnp.zeros(psh_ref.shape, F32)
        sstate_ref[...] = jnp.zeros(sstate_ref.shape, F32)
        mstate_ref[...] = jnp.zeros(mstate_ref.shape, F32)
        mm_ref[...] = jnp.zeros(mm_ref.shape, F32)

    def step(slot_proj, slot_mix):
        pending = []
        if slot_proj is not None:
            x = x_cur()
            ms = jnp.mean(x * x, axis=-1, keepdims=True)
            h_ref[...] = (x * lax.rsqrt(ms + EPS) * nw_ref[...]).astype(BF16)

            def proj_piece(dst_ref, c0, c1, w0):
                def piece():
                    dst_ref[slot_proj, :, c0:c1] = _dot(h_ref[...], win_ref[:, w0 + c0:w0 + c1])
                return piece

            pending += [proj_piece(hp_ref, c0, min(c0 + PROJ_COLS, N_HALO_COLS), 0)
                        for c0 in range(0, N_HALO_COLS, PROJ_COLS)]
            pending += [proj_piece(rest_ref, c0, min(c0 + PROJ_COLS, N_REST_COLS), N_HALO_COLS)
                        for c0 in range(0, N_REST_COLS, PROJ_COLS)]
        if slot_mix is not None:
            _mixers(i - 1, x_rows, hp_ref.at[slot_mix], rest_ref.at[slot_mix], gbias_ref, alog_ref,
                    xcw_ref, xcb_ref, qcw_ref, qcb_ref, poolw_ref, poolb_ref, pools_ref, dskip_ref, snorm_ref,
                    mnorm_ref, wout_ref, postn_ref, out_ref, hph_ref, psh_ref, act_ref, mix_ref, ps_ref,
                    sstate_ref, mstate_ref, mm_ref, pending)
        while pending:
            pending.pop(0)()

    last_slot = (n_tiles - 1) % 2

    @pl.when(i == 0)
    def _():
        step(0, None)

    @pl.when((i > 0) & (i < n_tiles) & (i % 2 == 0))
    def _():
        step(0, 1)

    @pl.when((i > 0) & (i < n_tiles) & (i % 2 == 1))
    def _():
        step(1, 0)

    @pl.when(i == n_tiles)
    def _():
        step(None, last_slot)


def _mixers(tile, x_rows, hp_ref, rest_ref, gbias_ref, alog_ref, xcw_ref, xcb_ref, qcw_ref, qcb_ref,
            poolw_ref, poolb_ref, pools_ref, dskip_ref, snorm_ref, mnorm_ref, wout_ref, postn_ref,
            out_ref, hph_ref, psh_ref, act_ref, mix_ref, ps_ref, sstate_ref, mstate_ref, mm_ref, pending):
    def between():
        if pending:
            pending.pop(0)()

    TT = hp_ref.shape[0]
    L = CHUNK
    n_hph = hph_ref.shape[0]
    n_psh = psh_ref.shape[1]
    ps_carried = ((0, 0), (0, 1), (1, 1), (2, 1))

    def tail(cur_ref, halo, n_halo, r0, n_rows, cols):
        if r0 == 0:
            return halo[n_halo - n_rows:n_halo, cols]
        return cur_ref[r0 - n_rows:r0, cols]

    lane = lax.broadcasted_iota(jnp.int32, (L, LANES), 1)
    row = lax.broadcasted_iota(jnp.int32, (L, LANES), 0)
    lo_half = lane < 64
    tau_row = _tau(row)
    causal = _tau(lane) <= tau_row
    tril = jnp.where(causal, 1.0, 0.0).astype(BF16)
    lane_row = lax.broadcasted_iota(jnp.int32, (1, LANES), 1)
    a_row = -jnp.exp(alog_ref[...])
    neg_inf = -jnp.inf
    win_blk = [jnp.where(lo_half, float(POOL_WINDOWS[2 * b]), float(POOL_WINDOWS[2 * b + 1])) for b in range(2)]
    tau_f = tau_row.astype(F32)

    def chunk_stages(c):
        r0 = c * L

        def conv_block(col, cw_ref, cb_ref, wcol, k_taps):
            cols = slice(col, col + LANES)
            cur = hp_ref[r0:r0 + L, cols]
            prev_tail = tail(hp_ref, hph_ref, n_hph, r0, (k_taps - 1) * SUBLANES, cols)
            return _silu(_causal_conv(prev_tail, cur, cw_ref, cb_ref[:, wcol:wcol + LANES], wcol, LANES, k_taps))

        for blk in range(D_SSD_XBC // LANES):
            act_ref[r0:r0 + L, blk * LANES:(blk + 1) * LANES] = conv_block(
                C_XBC + blk * LANES, xcw_ref, xcb_ref, blk * LANES, SSD_CONV)
        for blk in range(2 * D_MLSTM // LANES):
            act_ref[r0:r0 + L, D_SSD_XBC + blk * LANES:D_SSD_XBC + (blk + 1) * LANES] = conv_block(
                C_QK + blk * LANES, qcw_ref, qcb_ref, blk * LANES, MLSTM_CONV)
        yield

        pos = tau_f + (tile * TT + r0 + 1).astype(F32)
        pooled_blocks = []
        for b in range(2):
            cs_ = slice(b * LANES, (b + 1) * LANES)
            u_cur = hp_ref[r0:r0 + L, cs_]
            lvl = u_cur
            sums = []
            for li, sh in enumerate((1, 2, 4, 8)):
                if li == 0:
                    prev_tail = tail(hp_ref, hph_ref, n_hph, r0, sh * SUBLANES, cs_)
                else:
                    prev_tail = tail(ps_ref.at[li - 1], psh_ref.at[li - 1], n_psh, r0, sh * SUBLANES, cs_)
                ext = _ext_rows(prev_tail, lvl[L - sh * SUBLANES:L])
                lvl = lvl + _shifted(ext, lvl, sh)
                sums.append(lvl)
                if (li, b) in ps_carried:
                    ps_ref[li, r0:r0 + L, cs_] = lvl
                if b == 0 and li == 1:
                    break
            wsum = jnp.where(lo_half, sums[0], sums[1]) if b == 0 else jnp.where(lo_half, sums[2], sums[3])
            pooled_blocks.append((wsum / jnp.minimum(pos, win_blk[b]) - u_cur).astype(BF16))
        mix_ref[r0:r0 + L, 0:D_POOL] = (
            (_dot(jnp.concatenate(pooled_blocks, axis=1), poolw_ref[...]) + poolb_ref[...]) * pools_ref[...])
        yield

        gb = rest_ref[r0:r0 + L, R_G:R_G + LANES] + gbias_ref[...]
        sp_term = jnp.log1p(jnp.exp(-jnp.abs(gb)))
        dt = jnp.maximum(gb, 0.0) + sp_term
        log_f = jnp.minimum(gb, 0.0) - sp_term
        is_dt = lane < G_I
        is_f = (lane >= G_F) & (lane < G_F + MLSTM_HEADS)
        v_cum = jnp.where(is_dt, dt * a_row, jnp.where(is_f, log_f, 0.0))
        hi, mid, lo = _split3(v_cum)
        cs = _dot(tril, hi) + _dot(tril, mid) + _dot(tril, lo)
        u_gate = jnp.where(is_dt, dt, gb)
        cs_t = cs.T
        ug_t = u_gate.T
        cs_last = cs[L - 1:L, :]
        e_col = jnp.exp(cs)
        w_col = jnp.exp(cs_last - cs) * dt
        e_last = jnp.exp(cs_last)
        yield

        for g in range(SSD_GROUPS):
            b_g = act_ref[r0:r0 + L, D_SSD + g * SSD_STATE:D_SSD + (g + 1) * SSD_STATE].astype(BF16)
            c_g = act_ref[r0:r0 + L, D_SSD + (SSD_GROUPS + g) * SSD_STATE:
                          D_SSD + (SSD_GROUPS + g + 1) * SSD_STATE].astype(BF16)
            s_g = _dot_nt(c_g, b_g)
            state_g = sstate_ref[g]
            y_off = _dot(c_g, state_g.astype(BF16))
            xd_blocks = []
            cd_blocks = []
            for pr in range(2):
                h_even = 4 * g + 2 * pr
                col = h_even * SSD_HEAD_DIM
                xs = act_ref[r0:r0 + L, col:col + LANES]
                xs_b = xs.astype(BF16)
                yd = []
                for hh in range(2):
                    hd = h_even + hh
                    seg = jnp.where(causal, cs[:, hd:hd + 1] - cs_t[hd:hd + 1, :], neg_inf)
                    m_h = (s_g * (jnp.exp(seg) * ug_t[hd:hd + 1, :])).astype(BF16)
                    yd.append(_dot(m_h, xs_b))
                y_diag = jnp.where(lo_half, yd[0], yd[1])
                e_exp = _pair_expand(e_col, h_even, (L, LANES), lo_half)
                w_exp = _pair_expand(w_col, h_even, (L, LANES), lo_half)
                y = (y_diag + y_off[:, pr * LANES:(pr + 1) * LANES] * e_exp
                     + xs * dskip_ref[:, col:col + LANES])
                z = rest_ref[r0:r0 + L, R_Z + col:R_Z + col + LANES]
                mix_ref[r0:r0 + L, D_POOL + col:D_POOL + col + LANES] = y * _silu(z)
                xd_blocks.append((xs * w_exp).astype(BF16))
                cd_blocks.append(_pair_expand(e_last, h_even, (1, LANES), lane_row < 64))
            xd_g = jnp.concatenate(xd_blocks, axis=1)
            cd_g = jnp.concatenate(cd_blocks, axis=1)
            new_states = lax.dot_general(b_g, xd_g, (((0,), (0,)), ((), ())),
                                         preferred_element_type=F32)
            sstate_ref[g] = state_g * cd_g + new_states
            yield
        y_all = mix_ref[r0:r0 + L, D_POOL:D_POOL + D_SSD]
        ms_y = jnp.mean(y_all * y_all, axis=-1, keepdims=True)
        mix_ref[r0:r0 + L, D_POOL:D_POOL + D_SSD] = y_all * lax.rsqrt(ms_y + EPS) * snorm_ref[...]

        for pr in range(MLSTM_HEADS // 2):
            qcol = D_SSD_XBC + pr * LANES
            kcol = D_SSD_XBC + D_MLSTM + pr * LANES
            q_b = act_ref[r0:r0 + L, qcol:qcol + LANES] * (MLSTM_HEAD_DIM ** -0.5)
            k_t = act_ref[r0:r0 + L, kcol:kcol + LANES].T
            k_tb = k_t.astype(BF16)
            v_b = rest_ref[r0:r0 + L, R_V + pr * LANES:R_V + (pr + 1) * LANES]
            o_b = rest_ref[r0:r0 + L, R_O + pr * LANES:R_O + (pr + 1) * LANES]
            hv = []
            for hh in range(2):
                hd = 2 * pr + hh
                in_half = lo_half if hh == 0 else jnp.logical_not(lo_half)
                row_in_half = (row < 64) if hh == 0 else (row >= 64)
                ones_lane = (lane == 64) if hh == 0 else (lane == 0)
                ol = 64 if hh == 0 else 0
                qm = jnp.where(in_half, q_b, 0.0).astype(BF16)
                s = _dot(qm, k_tb)
                b_row = cs_t[G_F + hd:G_F + hd + 1, :]
                r_row = ug_t[G_I + hd:G_I + hd + 1, :] - b_row
                b_last = jnp.sum(jnp.where(lane_row == L - 1, b_row, 0.0), axis=-1, keepdims=True)
                al_row = b_last + r_row
                m_loc = jnp.max(al_row, axis=-1, keepdims=True)
                prev_m = mm_ref[hd:hd + 1, 0:1]
                rmask = jnp.where(causal, r_row, neg_inf)
                g_col = jnp.maximum(jnp.max(rmask, axis=-1, keepdims=True), prev_m)
                p = (s * jnp.exp(rmask - g_col)).astype(BF16)
                v_ext = jnp.where(in_half, v_b, jnp.where(ones_lane, 1.0, 0.0)).astype(BF16)
                cn = mstate_ref[hd]
                res = _dot(p, v_ext) + _dot(qm, cn.astype(BF16)) * jnp.exp(prev_m - g_col)
                den = jnp.maximum(jnp.abs(res[:, ol:ol + 1]),
                                  jnp.exp(-(cs[:, G_F + hd:G_F + hd + 1] + g_col)))
                hv.append(res / den)
                w_row = jnp.exp(al_row - m_loc)
                ktw = jnp.where(row_in_half, k_t * w_row, 0.0).astype(BF16)
                c_loc = _dot(ktw, v_ext)
                m_new = jnp.maximum(b_last + prev_m, m_loc)
                mstate_ref[hd] = (jnp.exp(b_last + prev_m - m_new) * cn
                                  + jnp.exp(m_loc - m_new) * c_loc)
                mm_ref[hd:hd + 1, :] = jnp.broadcast_to(m_new, (1, LANES))
            hcat = jax.nn.sigmoid(o_b) * jnp.where(lo_half, hv[0], hv[1])
            sq = hcat * hcat
            ss_lo = jnp.sum(jnp.where(lo_half, sq, 0.0), axis=-1, keepdims=True)
            ss_hi = jnp.sum(jnp.where(lo_half, 0.0, sq), axis=-1, keepdims=True)
            inv = jnp.where(lo_half, lax.rsqrt(ss_lo * (1.0 / MLSTM_HEAD_DIM) + EPS),
                            lax.rsqrt(ss_hi * (1.0 / MLSTM_HEAD_DIM) + EPS))
            mcol = D_POOL + D_SSD + pr * LANES
            mix_ref[r0:r0 + L, mcol:mcol + LANES] = hcat * inv * mnorm_ref[:, pr * LANES:(pr + 1) * LANES]
            if pr + 1 < MLSTM_HEADS // 2:
                yield

        def out_piece():
            o = _dot(mix_ref[r0:r0 + L, :].astype(BF16), wout_ref[...])
            ms_o = jnp.mean(o * o, axis=-1, keepdims=True)
            out_ref[0, r0:r0 + L, :] = x_rows(r0) + o * lax.rsqrt(ms_o + EPS) * postn_ref[...]

        pending.insert(0, out_piece)
        yield

    n_stages = 3 + SSD_GROUPS + MLSTM_HEADS // 2
    gens = [chunk_stages(c) for c in range(TT // L)]
    for _ in range(n_stages):
        for gen in gens:
            next(gen)
            between()
    while pending:
        pending.pop(0)()

    hph_ref[...] = hp_ref[TT - n_hph:TT, :]
    for lv, b in ps_carried:
        cs_ = slice(b * LANES, (b + 1) * LANES)
        psh_ref[lv, :, cs_] = ps_ref[lv, TT - n_psh:TT, cs_]


def ffn_kernel(layer, to_time_order, x_ref, wup_hbm, wdn_hbm, nw_ref, cw_ref, cb_ref, postn_ref, out_ref,
               wup_ref, wdn_ref, stage_ref, stage_dn_ref, stage_sem, halo_ref, a_ref, *out_scratch):
    TT = x_ref.shape[1]
    FT = FFN_FT
    L = CHUNK
    n_tail = (FFN_CONV - 1) * SUBLANES
    i = pl.program_id(1)

    @pl.when((pl.program_id(0) == 0) & (i == 0))
    def _():
        _stage_weights(_col_blocks(wup_hbm.at[layer], 0, wup_ref, 0, 2 * D_FF, STAGE_COLS),
                       stage_ref, stage_sem)
        _stage_weights([(wdn_hbm.at[layer, r:r + STAGE_COLS, :], wdn_ref.at[r:r + STAGE_COLS, :])
                        for r in range(0, D_FF, STAGE_COLS)], stage_dn_ref, stage_sem)

    @pl.when(i == 0)
    def _():
        halo_ref[...] = jnp.zeros(halo_ref.shape, F32)

    x = x_ref[0]
    ms = jnp.mean(x * x, axis=-1, keepdims=True)
    h = (x * lax.rsqrt(ms + EPS) * nw_ref[...]).astype(BF16)

    def conv_cols(col):
        u = _dot(h, wup_ref[:, col:col + FT])
        outs = []
        for c in range(TT // L):
            cur = u[c * L:(c + 1) * L]
            prev_tail = halo_ref[:, col:col + FT] if c == 0 else u[c * L - n_tail:c * L]
            outs.append(_causal_conv(prev_tail, cur, cw_ref, cb_ref[:, col:col + FT], col, FT, FFN_CONV))
        halo_ref[:, col:col + FT] = u[TT - n_tail:TT]
        return outs

    for j in range(D_FF // FT):
        gts = conv_cols(j * FT)
        vals = conv_cols(D_FF + j * FT)
        for c in range(TT // L):
            gt = gts[c]
            gelu = 0.5 * gt * (1.0 + jnp.tanh(math.sqrt(2.0 / math.pi) * (gt + 0.044715 * (gt * gt * gt))))
            a_ref[c * L:(c + 1) * L, j * FT:(j + 1) * FT] = (gelu * vals[c]).astype(BF16)

    f = _dot(a_ref[...], wdn_ref[...])
    ms_f = jnp.mean(f * f, axis=-1, keepdims=True)
    res = x_ref[0] + f * lax.rsqrt(ms_f + EPS) * postn_ref[...]
    if not to_time_order:
        out_ref[0] = res
        return

    obuf_ref, out_sem = out_scratch
    n_c = TT // L
    b = pl.program_id(0)
    step = b * pl.num_programs(1) + i
    n_steps = pl.num_programs(0) * pl.num_programs(1)
    slot = step % 2

    def out_copies(sl):
        return [pltpu.make_async_copy(obuf_ref.at[sl, :, :, s, :], out_ref.at[b, pl.ds(i * n_c, n_c), s],
                                      out_sem.at[sl]) for s in range(SUBLANES)]

    @pl.when(step >= 2)
    def _():
        for cp in out_copies(slot):
            cp.wait()

    obuf_ref[slot] = res.reshape(n_c, VROWS, SUBLANES, res.shape[1])
    for cp in out_copies(slot):
        cp.start()

    @pl.when(step == n_steps - 1)
    def _():
        for cp in out_copies(slot):
            cp.wait()

    @pl.when((step == n_steps - 1) & (n_steps >= 2))
    def _():
        for cp in out_copies(1 - slot):
            cp.wait()


def _const_spec(shape):
    nd = len(shape)
    return pl.BlockSpec(shape, lambda b, i: (0,) * nd, pipeline_mode=pl.Buffered(1))


def _mix_layer(layer, from_time_order, x, wout, win, nw, gbias, alog, xcw, xcb, qcw, qcb, poolw, poolb, pools,
               dskip, snorm, mnorm, postn):
    B, T, D = x.shape
    TT = MIX_TT
    weights = (wout,)
    consts = (win, nw, gbias, alog, xcw, xcb, qcw, qcb, poolw, poolb, pools, dskip, snorm, mnorm, postn)
    n_t = T // TT
    cur_spec = pl.BlockSpec((1, TT, D), lambda b, i: (b, jnp.minimum(i, n_t - 1), 0))
    prev_spec = pl.BlockSpec((1, TT, D), lambda b, i: (b, jnp.maximum(i - 1, 0), 0))
    max_conv_tail = (max(SSD_CONV, MLSTM_CONV) - 1) * SUBLANES
    max_pool_tail = (POOL_WINDOWS[-1] // 2) * SUBLANES
    if from_time_order:
        x_in = x.reshape(B, T // CHUNK, SUBLANES, VROWS, D)
        x_specs = [pl.BlockSpec(memory_space=pltpu.HBM)] * 2
        in_scratch = [pltpu.VMEM((3, TT // CHUNK, VROWS, SUBLANES, D), F32),
                      pltpu.SemaphoreType.DMA((3,))]
    else:
        x_in, x_specs, in_scratch = x, [cur_spec, prev_spec], []
    return pl.pallas_call(
        functools.partial(mix_kernel, layer, from_time_order, n_t),
        grid=(B, n_t + 1),
        in_specs=(x_specs + [pl.BlockSpec(memory_space=pltpu.HBM) for _ in weights]
                  + [_const_spec(c.shape) for c in consts]),
        out_specs=prev_spec,
        out_shape=jax.ShapeDtypeStruct(x.shape, x.dtype),
        scratch_shapes=[
            pltpu.VMEM((D_MODEL, N_IN_COLS), BF16),
            pltpu.VMEM((D_MODEL, D_MODEL), BF16),
            pltpu.VMEM((STAGE_SLOTS, D_MODEL, STAGE_COLS), F32),
            pltpu.SemaphoreType.DMA((STAGE_SLOTS,)),
            pltpu.VMEM((TT, D_MODEL), BF16),
            pltpu.VMEM((2, TT, N_HALO_COLS), F32),
            pltpu.VMEM((2, TT, N_REST_COLS), F32),
            pltpu.VMEM((max_conv_tail, N_HALO_COLS), F32),
            pltpu.VMEM((3, max_pool_tail, D_POOL), F32),
            pltpu.VMEM((TT, D_SSD_XBC + 2 * D_MLSTM), F32),
            pltpu.VMEM((TT, D_MODEL), F32),
            pltpu.VMEM((3, TT, D_POOL), F32),
            pltpu.VMEM((SSD_GROUPS, SSD_STATE, 4 * SSD_HEAD_DIM), F32),
            pltpu.VMEM((MLSTM_HEADS, LANES, LANES), F32),
            pltpu.VMEM((SUBLANES, LANES), F32),
        ] + in_scratch,
        compiler_params=pltpu.CompilerParams(
            dimension_semantics=("arbitrary", "arbitrary"), vmem_limit_bytes=VMEM_LIMIT),
        name="mix_layer",
    )(x_in, x_in, *weights, *consts)


def _ffn_layer(layer, to_time_order, x, wup, wdn, nw, cw, cb, postn):
    B, T, D = x.shape
    TT = FFN_TT
    weights = (wup, wdn)
    consts = (nw, cw, cb, postn)
    x_spec = pl.BlockSpec((1, TT, D), lambda b, i: (b, i, 0))
    if to_time_order:
        out_spec = pl.BlockSpec(memory_space=pltpu.HBM)
        out_shape = jax.ShapeDtypeStruct((B, T // CHUNK, SUBLANES, VROWS, D), x.dtype)
        out_scratch = [pltpu.VMEM((2, TT // CHUNK, VROWS, SUBLANES, D), F32),
                       pltpu.SemaphoreType.DMA((2,))]
    else:
        out_spec, out_shape, out_scratch = x_spec, jax.ShapeDtypeStruct(x.shape, x.dtype), []
    return pl.pallas_call(
        functools.partial(ffn_kernel, layer, to_time_order),
        grid=(B, T // TT),
        in_specs=([x_spec] + [pl.BlockSpec(memory_space=pltpu.HBM) for _ in weights]
                  + [_const_spec(c.shape) for c in consts]),
        out_specs=out_spec,
        out_shape=out_shape,
        scratch_shapes=[
            pltpu.VMEM((D_MODEL, 2 * D_FF), BF16),
            pltpu.VMEM((D_FF, D_MODEL), BF16),
            pltpu.VMEM((STAGE_SLOTS, D_MODEL, STAGE_COLS), F32),
            pltpu.VMEM((STAGE_SLOTS, STAGE_COLS, D_MODEL), F32),
            pltpu.SemaphoreType.DMA((STAGE_SLOTS,)),
            pltpu.VMEM(((FFN_CONV - 1) * SUBLANES, 2 * D_FF), F32),
            pltpu.VMEM((TT, D_FF), BF16),
        ] + out_scratch,
        compiler_params=pltpu.CompilerParams(
            dimension_semantics=("arbitrary", "arbitrary"), vmem_limit_bytes=VMEM_LIMIT),
        name="ffn_layer",
    )(x, *weights, *consts).reshape(B, T, D)


def _row(v):
    return v.reshape(1, -1).astype(F32)


def _pad_lanes(v):
    return jnp.pad(v.astype(F32), (0, LANES - v.shape[0])).reshape(1, LANES)


def _prep_pool_w(w):
    out = jnp.zeros((D_POOL, D_POOL), F32)
    for g in range(len(POOL_WINDOWS)):
        s = g * POOL_GROUP_DIM
        out = lax.dynamic_update_slice(out, w[g].astype(F32), (s, s))
    return out.astype(BF16)


def kernel(x, pre_mix_norm, w_in, pool_w, pool_b, pool_scale, ssd_conv_w, ssd_conv_b, ssd_dt_bias, ssd_a_log, ssd_d, ssd_norm, mlstm_conv_w, mlstm_conv_b, mlstm_i_bias, mlstm_f_bias, mlstm_norm, w_out, post_mix_norm, pre_ffn_norm, ffn_w_up, ffn_conv_w, ffn_conv_b, ffn_w_down, post_ffn_norm):
    depth = w_in.shape[0]
    for l in range(depth):
        gbias = _pad_lanes(jnp.concatenate([ssd_dt_bias[l], mlstm_i_bias[l], mlstm_f_bias[l]]))
        x = _mix_layer(
            l, l == 0, x, w_out, w_in[l].astype(BF16), _row(pre_mix_norm[l]), gbias, _pad_lanes(ssd_a_log[l]),
            ssd_conv_w[l].astype(F32), _row(ssd_conv_b[l]), mlstm_conv_w[l].astype(F32), _row(mlstm_conv_b[l]),
            _prep_pool_w(pool_w[l]), _row(pool_b[l]), _row(pool_scale[l]),
            _row(jnp.repeat(ssd_d[l], SSD_HEAD_DIM)), _row(ssd_norm[l]), _row(mlstm_norm[l]),
            _row(post_mix_norm[l]))
        x = _ffn_layer(
            l, l == depth - 1, x, ffn_w_up, ffn_w_down, _row(pre_ffn_norm[l]), ffn_conv_w[l].astype(F32),
            _row(ffn_conv_b[l]), _row(post_ffn_norm[l]))
    return x
```

```python
import functools
import math

import jax
import jax.numpy as jnp
from jax import lax
from jax.experimental import pallas as pl
from jax.experimental.pallas import tpu as pltpu

F32 = jnp.float32
BF16 = jnp.bfloat16

D_MODEL = 1024
EPS = 1e-6

D_POOL = 256
POOL_GROUP_DIM = 64
POOL_WINDOWS = (2, 4, 8, 16)

D_SSD = 512
SSD_HEADS = 8
SSD_HEAD_DIM = 64
SSD_GROUPS = 2
SSD_STATE = 128
SSD_CONV = 4
D_SSD_XBC = D_SSD + 2 * SSD_GROUPS * SSD_STATE

D_MLSTM = 256
MLSTM_HEADS = 4
MLSTM_HEAD_DIM = 64
MLSTM_CONV = 4

D_FF = 2816
FFN_CONV = 3

CHUNK = 128
LANES = 128
SUBLANES = 8
VROWS = CHUNK // SUBLANES

C_POOL = 0
C_XBC = C_POOL + D_POOL
C_QK = C_XBC + D_SSD_XBC
N_HALO_COLS = C_QK + 2 * D_MLSTM
R_Z = 0
R_V = R_Z + D_SSD
R_O = R_V + D_MLSTM
R_G = R_O + D_MLSTM
N_REST_COLS = R_G + LANES
N_IN_COLS = N_HALO_COLS + N_REST_COLS
G_DT = 0
G_I = SSD_HEADS
G_F = G_I + MLSTM_HEADS

MIX_TT = 256
PROJ_COLS = 256
STAGE_COLS = 256
STAGE_SLOTS = 4
FFN_TT = 1024
FFN_FT = 256
VMEM_LIMIT = 56 * 1024 * 1024


def _dot(a, b):
    return jnp.dot(a, b, preferred_element_type=F32)


def _dot_nt(a, b):
    return lax.dot_general(a, b, (((1,), (1,)), ((), ())), preferred_element_type=F32)


def _silu(x):
    return x * jax.nn.sigmoid(x)


def _split3(a):
    hi = a.astype(BF16)
    r = a - hi.astype(F32)
    mid = r.astype(BF16)
    lo = (r - mid.astype(F32)).astype(BF16)
    return hi, mid, lo


def _bcast_lane(a, j, shape):
    return jnp.broadcast_to(a[:, j:j + 1], shape)


def _pair_expand(a, h_even, shape, lo_half):
    return jnp.where(lo_half, _bcast_lane(a, h_even, shape), _bcast_lane(a, h_even + 1, shape))


def _stage_weights(blocks, stage_ref, sem_ref):
    n_slots = stage_ref.shape[0]

    def copy(n):
        src, dst = blocks[n]
        rows, cols = src.shape
        return pltpu.make_async_copy(src, stage_ref.at[n % n_slots, 0:rows, 0:cols], sem_ref.at[n % n_slots])

    for n in range(min(n_slots - 1, len(blocks))):
        copy(n).start()
    for n, (src, dst) in enumerate(blocks):
        if n + n_slots - 1 < len(blocks):
            copy(n + n_slots - 1).start()
        copy(n).wait()
        rows, cols = src.shape
        dst[...] = stage_ref[n % n_slots, 0:rows, 0:cols].astype(BF16)


def _col_blocks(src_ref, src0, dst_ref, dst0, ncols, step):
    return [(src_ref.at[:, src0 + k:src0 + min(k + step, ncols)],
             dst_ref.at[:, dst0 + k:dst0 + min(k + step, ncols)]) for k in range(0, ncols, step)]


def _tau(p):
    return (p % SUBLANES) * VROWS + p // SUBLANES


def _ext_rows(prev_tail, cur_tail):
    n = cur_tail.shape[0] // SUBLANES
    sub0 = lax.broadcasted_iota(jnp.int32, (SUBLANES, cur_tail.shape[1]), 0) == 0
    out = []
    for j in range(n):
        sl = slice(j * SUBLANES, (j + 1) * SUBLANES)
        out.append(jnp.where(sub0, pltpu.roll(prev_tail[sl], 1, 0), pltpu.roll(cur_tail[sl], 1, 0)))
    return out


def _shifted(ext, cur, k):
    if k == 0:
        return cur
    return jnp.concatenate(ext[len(ext) - k:] + [cur[0:CHUNK - SUBLANES * k]], axis=0)


def _causal_conv(prev_tail, cur, w_ref, b_row, wcol, ncols, k_taps):
    n = k_taps - 1
    ext = _ext_rows(prev_tail, cur[CHUNK - n * SUBLANES:CHUNK])
    acc = b_row
    for k in range(k_taps):
        acc = acc + _shifted(ext, cur, n - k) * w_ref[k:k + 1, wcol:wcol + ncols]
    return acc


def _relayout_w_in(wraw_ref, win_ref):
    c_z = D_POOL
    c_xbc = c_z + D_SSD
    c_dt = c_xbc + D_SSD_XBC
    n_tail = SSD_HEADS + 4 * D_MLSTM + 2 * MLSTM_HEADS
    o_qk = SSD_HEADS
    o_v = o_qk + 2 * D_MLSTM
    o_if = o_v + 2 * D_MLSTM
    for r in range(0, D_MODEL, CHUNK):
        rows = slice(r, r + CHUNK)
        win_ref[rows, C_POOL:C_POOL + D_POOL] = wraw_ref[rows, 0:D_POOL]
        win_ref[rows, C_XBC:C_XBC + D_SSD_XBC] = wraw_ref[rows, c_xbc:c_dt]
        win_ref[rows, N_HALO_COLS + R_Z:N_HALO_COLS + R_Z + D_SSD] = wraw_ref[rows, c_z:c_xbc]
        tail = wraw_ref[rows, c_dt:c_dt + n_tail]
        win_ref[rows, C_QK:C_QK + 2 * D_MLSTM] = tail[:, o_qk:o_v]
        win_ref[rows, N_HALO_COLS + R_V:N_HALO_COLS + R_V + 2 * D_MLSTM] = tail[:, o_v:o_if]
        pad = jnp.zeros((CHUNK, LANES - SSD_HEADS - 2 * MLSTM_HEADS), tail.dtype)
        win_ref[rows, N_HALO_COLS + R_G:N_HALO_COLS + R_G + LANES] = jnp.concatenate(
            [tail[:, 0:o_qk], tail[:, o_if:n_tail], pad], axis=1)


def mix_kernel(layer, from_time_order, n_tiles, xc_ref, xp_ref, wout_hbm, wraw_ref, nw_ref, gbias_ref, alog_ref,
               xcw_ref, xcb_ref, qcw_ref, qcb_ref,
               poolw_ref, poolb_ref, pools_ref, dskip_ref, snorm_ref, mnorm_ref, postn_ref,
               out_ref, win_ref, wout_ref, stage_ref, stage_sem,
               h_ref, hp_ref, rest_ref, hph_ref, psh_ref, act_ref, mix_ref, ps_ref,
               sstate_ref, mstate_ref, mm_ref, *in_scratch):
    TT = hp_ref.shape[1]
    L = CHUNK
    n_c = TT // L
    b = pl.program_id(0)
    i = pl.program_id(1)
    n_t = n_tiles

    if from_time_order:
        xbuf_ref, in_sem = in_scratch
        n_slots = xbuf_ref.shape[0]

        def in_copies(tile):
            sl = tile % n_slots
            return [pltpu.make_async_copy(xc_ref.at[b, pl.ds(tile * n_c, n_c), s], xbuf_ref.at[sl, :, :, s, :],
                                          in_sem.at[sl]) for s in range(SUBLANES)]

        @pl.when(i == 0)
        def _():
            for cp in in_copies(0):
                cp.start()

        @pl.when(i + 1 < n_t)
        def _():
            for cp in in_copies(i + 1):
                cp.start()

        @pl.when(i < n_t)
        def _():
            for cp in in_copies(i):
                cp.wait()

        slot_cur = jnp.minimum(i, n_t - 1) % n_slots
        slot_prev = jnp.maximum(i - 1, 0) % n_slots

        def x_cur():
            return xbuf_ref[slot_cur].reshape(TT, xbuf_ref.shape[-1])

        def x_rows(r0):
            return xbuf_ref[slot_prev, r0 // L].reshape(L, xbuf_ref.shape[-1])
    else:
        def x_cur():
            return xc_ref[0]

        def x_rows(r0):
            return xp_ref[0, r0:r0 + L, :]

    @pl.when((pl.program_id(0) == 0) & (i == 0))
    def _():
        _relayout_w_in(wraw_ref, win_ref)
        _stage_weights(_col_blocks(wout_hbm.at[layer], 0, wout_ref, 0, D_MODEL, STAGE_COLS),
                       stage_ref, stage_sem)

    @pl.when(i == 0)
    def _():
        hph_ref[...] = jnp.zeros(hph_ref.shape, F32)
        psh_ref[...] = jnp.zeros(psh_ref.shape, F32)
        sstate_ref[...] = jnp.zeros(sstate_ref.shape, F32)
        mstate_ref[...] = jnp.zeros(mstate_ref.shape, F32)
        mm_ref[...] = jnp.zeros(mm_ref.shape, F32)

    def step(slot_proj, slot_mix):
        pending = []
        if slot_proj is not None:
            x = x_cur()
            ms = jnp.mean(x * x, axis=-1, keepdims=True)
            h_ref[...] = (x * lax.rsqrt(ms + EPS) * nw_ref[...]).astype(BF16)

            def proj_piece(dst_ref, c0, c1, w0):
                def piece():
                    dst_ref[slot_proj, :, c0:c1] = _dot(h_ref[...], win_ref[:, w0 + c0:w0 + c1])
                return piece

            pending += [proj_piece(hp_ref, c0, min(c0 + PROJ_COLS, N_HALO_COLS), 0)
                        for c0 in range(0, N_HALO_COLS, PROJ_COLS)]
            pending += [proj_piece(rest_ref, c0, min(c0 + PROJ_COLS, N_REST_COLS), N_HALO_COLS)
                        for c0 in range(0, N_REST_COLS, PROJ_COLS)]
        if slot_mix is not None:
            _mixers(i - 1, x_rows, hp_ref.at[slot_mix], rest_ref.at[slot_mix], gbias_ref, alog_ref,
                    xcw_ref, xcb_ref, qcw_ref, qcb_ref, poolw_ref, poolb_ref, pools_ref, dskip_ref, snorm_ref,
                    mnorm_ref, wout_ref, postn_ref, out_ref, hph_ref, psh_ref, act_ref, mix_ref, ps_ref,
                    sstate_ref, mstate_ref, mm_ref, pending)
        while pending:
            pending.pop(0)()

    last_slot = (n_tiles - 1) % 2

    @pl.when(i == 0)
    def _():
        step(0, None)

    @pl.when((i > 0) & (i < n_tiles) & (i % 2 == 0))
    def _():
        step(0, 1)

    @pl.when((i > 0) & (i < n_tiles) & (i % 2 == 1))
    def _():
        step(1, 0)

    @pl.when(i == n_tiles)
    def _():
        step(None, last_slot)


def _mixers(tile, x_rows, hp_ref, rest_ref, gbias_ref, alog_ref, xcw_ref, xcb_ref, qcw_ref, qcb_ref,
            poolw_ref, poolb_ref, pools_ref, dskip_ref, snorm_ref, mnorm_ref, wout_ref, postn_ref,
            out_ref, hph_ref, psh_ref, act_ref, mix_ref, ps_ref, sstate_ref, mstate_ref, mm_ref, pending):
    def between():
        if pending:
            pending.pop(0)()

    TT = hp_ref.shape[0]
    L = CHUNK
    n_hph = hph_ref.shape[0]
    n_psh = psh_ref.shape[1]
    ps_carried = ((0, 0), (0, 1), (1, 1), (2, 1))

    def tail(cur_ref, halo, n_halo, r0, n_rows, cols):
        if r0 == 0:
            return halo[n_halo - n_rows:n_halo, cols]
        return cur_ref[r0 - n_rows:r0, cols]

    lane = lax.broadcasted_iota(jnp.int32, (L, LANES), 1)
    row = lax.broadcasted_iota(jnp.int32, (L, LANES), 0)
    lo_half = lane < 64
    tau_row = _tau(row)
    causal = _tau(lane) <= tau_row
    tril = jnp.where(causal, 1.0, 0.0).astype(BF16)
    lane_row = lax.broadcasted_iota(jnp.int32, (1, LANES), 1)
    a_row = -jnp.exp(alog_ref[...])
    neg_inf = -jnp.inf
    win_blk = [jnp.where(lo_half, float(POOL_WINDOWS[2 * b]), float(POOL_WINDOWS[2 * b + 1])) for b in range(2)]
    tau_f = tau_row.astype(F32)

    def chunk_stages(c):
        r0 = c * L

        def conv_block(col, cw_ref, cb_ref, wcol, k_taps):
            cols = slice(col, col + LANES)
            cur = hp_ref[r0:r0 + L, cols]
            prev_tail = tail(hp_ref, hph_ref, n_hph, r0, (k_taps - 1) * SUBLANES, cols)
            return _silu(_causal_conv(prev_tail, cur, cw_ref, cb_ref[:, wcol:wcol + LANES], wcol, LANES, k_taps))

        for blk in range(D_SSD_XBC // LANES):
            act_ref[r0:r0 + L, blk * LANES:(blk + 1) * LANES] = conv_block(
                C_XBC + blk * LANES, xcw_ref, xcb_ref, blk * LANES, SSD_CONV)
        for blk in range(2 * D_MLSTM // LANES):
            act_ref[r0:r0 + L, D_SSD_XBC + blk * LANES:D_SSD_XBC + (blk + 1) * LANES] = conv_block(
                C_QK + blk * LANES, qcw_ref, qcb_ref, blk * LANES, MLSTM_CONV)
        yield

        pos = tau_f + (tile * TT + r0 + 1).astype(F32)
        pooled_blocks = []
        for b in range(2):
            cs_ = slice(b * LANES, (b + 1) * LANES)
            u_cur = hp_ref[r0:r0 + L, cs_]
            lvl = u_cur
            sums = []
            for li, sh in enumerate((1, 2, 4, 8)):
                if li == 0:
                    prev_tail = tail(hp_ref, hph_ref, n_hph, r0, sh * SUBLANES, cs_)
                else:
                    prev_tail = tail(ps_ref.at[li - 1], psh_ref.at[li - 1], n_psh, r0, sh * SUBLANES, cs_)
                ext = _ext_rows(prev_tail, lvl[L - sh * SUBLANES:L])
                lvl = lvl + _shifted(ext, lvl, sh)
                sums.append(lvl)
                if (li, b) in ps_carried:
                    ps_ref[li, r0:r0 + L, cs_] = lvl
                if b == 0 and li == 1:
                    break
            wsum = jnp.where(lo_half, sums[0], sums[1]) if b == 0 else jnp.where(lo_half, sums[2], sums[3])
            pooled_blocks.append((wsum / jnp.minimum(pos, win_blk[b]) - u_cur).astype(BF16))
        mix_ref[r0:r0 + L, 0:D_POOL] = (
            (_dot(jnp.concatenate(pooled_blocks, axis=1), poolw_ref[...]) + poolb_ref[...]) * pools_ref[...])
        yield

        gb = rest_ref[r0:r0 + L, R_G:R_G + LANES] + gbias_ref[...]
        sp_term = jnp.log1p(jnp.exp(-jnp.abs(gb)))
        dt = jnp.maximum(gb, 0.0) + sp_term
        log_f = jnp.minimum(gb, 0.0) - sp_term
        is_dt = lane < G_I
        is_f = (lane >= G_F) & (lane < G_F + MLSTM_HEADS)
        v_cum = jnp.where(is_dt, dt * a_row, jnp.where(is_f, log_f, 0.0))
        hi, mid, lo = _split3(v_cum)
        cs = _dot(tril, hi) + _dot(tril, mid) + _dot(tril, lo)
        u_gate = jnp.where(is_dt, dt, gb)
        cs_t = cs.T
        ug_t = u_gate.T
        cs_last = cs[L - 1:L, :]
        e_col = jnp.exp(cs)
        w_col = jnp.exp(cs_last - cs) * dt
        e_last = jnp.exp(cs_last)
        yield

        for g in range(SSD_GROUPS):
            b_g = act_ref[r0:r0 + L, D_SSD + g * SSD_STATE:D_SSD + (g + 1) * SSD_STATE].astype(BF16)
            c_g = act_ref[r0:r0 + L, D_SSD + (SSD_GROUPS + g) * SSD_STATE:
                          D_SSD + (SSD_GROUPS + g + 1) * SSD_STATE].astype(BF16)
            s_g = _dot_nt(c_g, b_g)
            state_g = sstate_ref[g]
            y_off = _dot(c_g, state_g.astype(BF16))
            xd_blocks = []
            cd_blocks = []
            for pr in range(2):
                h_even = 4 * g + 2 * pr
                col = h_even * SSD_HEAD_DIM
                xs = act_ref[r0:r0 + L, col:col + LANES]
                xs_b = xs.astype(BF16)
                yd = []
                for hh in range(2):
                    hd = h_even + hh
                    seg = jnp.where(causal, cs[:, hd:hd + 1] - cs_t[hd:hd + 1, :], neg_inf)
                    m_h = (s_g * (jnp.exp(seg) * ug_t[hd:hd + 1, :])).astype(BF16)
                    yd.append(_dot(m_h, xs_b))
                y_diag = jnp.where(lo_half, yd[0], yd[1])
                e_exp = _pair_expand(e_col, h_even, (L, LANES), lo_half)
                w_exp = _pair_expand(w_col, h_even, (L, LANES), lo_half)
                y = (y_diag + y_off[:, pr * LANES:(pr + 1) * LANES] * e_exp
                     + xs * dskip_ref[:, col:col + LANES])
                z = rest_ref[r0:r0 + L, R_Z + col:R_Z + col + LANES]
                mix_ref[r0:r0 + L, D_POOL + col:D_POOL + col + LANES] = y * _silu(z)
                xd_blocks.append((xs * w_exp).astype(BF16))
                cd_blocks.append(_pair_expand(e_last, h_even, (1, LANES), lane_row < 64))
            xd_g = jnp.concatenate(xd_blocks, axis=1)
            cd_g = jnp.concatenate(cd_blocks, axis=1)
            new_states = lax.dot_general(b_g, xd_g, (((0,), (0,)), ((), ())),
                                         preferred_element_type=F32)
            sstate_ref[g] = state_g * cd_g + new_states
            yield
        y_all = mix_ref[r0:r0 + L, D_POOL:D_POOL + D_SSD]
        ms_y = jnp.mean(y_all * y_all, axis=-1, keepdims=True)
        mix_ref[r0:r0 + L, D_POOL:D_POOL + D_SSD] = y_all * lax.rsqrt(ms_y + EPS) * snorm_ref[...]

        for pr in range(MLSTM_HEADS // 2):
            qcol = D_SSD_XBC + pr * LANES
            kcol = D_SSD_XBC + D_MLSTM + pr * LANES
            q_b = act_ref[r0:r0 + L, qcol:qcol + LANES] * (MLSTM_HEAD_DIM ** -0.5)
            k_t = act_ref[r0:r0 + L, kcol:kcol + LANES].T
            k_tb = k_t.astype(BF16)
            v_b = rest_ref[r0:r0 + L, R_V + pr * LANES:R_V + (pr + 1) * LANES]
            o_b = rest_ref[r0:r0 + L, R_O + pr * LANES:R_O + (pr + 1) * LANES]
            hv = []
            for hh in range(2):
                hd = 2 * pr + hh
                in_half = lo_half if hh == 0 else jnp.logical_not(lo_half)
                row_in_half = (row < 64) if hh == 0 else (row >= 64)
                ones_lane = (lane == 64) if hh == 0 else (lane == 0)
                ol = 64 if hh == 0 else 0
                qm = jnp.where(in_half, q_b, 0.0).astype(BF16)
                s = _dot(qm, k_tb)
                b_row = cs_t[G_F + hd:G_F + hd + 1, :]
                r_row = ug_t[G_I + hd:G_I + hd + 1, :] - b_row
                b_last = jnp.sum(jnp.where(lane_row == L - 1, b_row, 0.0), axis=-1, keepdims=True)
                al_row = b_last + r_row
                m_loc = jnp.max(al_row, axis=-1, keepdims=True)
                prev_m_row = mm_ref[hd:hd + 1, :]
                rmask = jnp.where(causal, r_row, neg_inf)
                g_t = jnp.maximum(jnp.broadcast_to(jnp.max(rmask, axis=-1, keepdims=True), (L, LANES)),
                                  prev_m_row)
                p = (s * jnp.exp(rmask - g_t)).astype(BF16)
                v_ext = jnp.where(in_half, v_b, jnp.where(ones_lane, 1.0, 0.0)).astype(BF16)
                cn = mstate_ref[hd]
                res = _dot(p, v_ext) + _dot(qm, cn.astype(BF16)) * jnp.exp(prev_m_row - g_t)
                den = jnp.maximum(jnp.abs(_bcast_lane(res, ol, (L, LANES))),
                                  jnp.exp(-(_bcast_lane(cs, G_F + hd, (L, LANES)) + g_t)))
                hv.append(res / den)
                w_row = jnp.exp(al_row - m_loc)
                ktw = jnp.where(row_in_half, k_t * w_row, 0.0).astype(BF16)
                c_loc = _dot(ktw, v_ext)
                m_new = jnp.maximum(b_last + prev_m_row, m_loc)
                mstate_ref[hd] = (jnp.exp(b_last + prev_m_row - m_new) * cn
                                  + jnp.exp(m_loc - m_new) * c_loc)
                mm_ref[hd:hd + 1, :] = m_new
            hcat = jax.nn.sigmoid(o_b) * jnp.where(lo_half, hv[0], hv[1])
            sq = hcat * hcat
            ss_lo = jnp.sum(jnp.where(lo_half, sq, 0.0), axis=-1, keepdims=True)
            ss_hi = jnp.sum(jnp.where(lo_half, 0.0, sq), axis=-1, keepdims=True)
            inv = jnp.where(lo_half, lax.rsqrt(ss_lo * (1.0 / MLSTM_HEAD_DIM) + EPS),
                            lax.rsqrt(ss_hi * (1.0 / MLSTM_HEAD_DIM) + EPS))
            mcol = D_POOL + D_SSD + pr * LANES
            mix_ref[r0:r0 + L, mcol:mcol + LANES] = hcat * inv * mnorm_ref[:, pr * LANES:(pr + 1) * LANES]
            if pr + 1 < MLSTM_HEADS // 2:
                yield

        def out_piece():
            o = _dot(mix_ref[r0:r0 + L, :].astype(BF16), wout_ref[...])
            ms_o = jnp.mean(o * o, axis=-1, keepdims=True)
            out_ref[0, r0:r0 + L, :] = x_rows(r0) + o * lax.rsqrt(ms_o + EPS) * postn_ref[...]

        pending.insert(0, out_piece)
        yield

    n_stages = 3 + SSD_GROUPS + MLSTM_HEADS // 2
    gens = [chunk_stages(c) for c in range(TT // L)]
    for _ in range(n_stages):
        for gen in gens:
            next(gen)
            between()
    while pending:
        pending.pop(0)()

    hph_ref[...] = hp_ref[TT - n_hph:TT, :]
    for lv, b in ps_carried:
        cs_ = slice(b * LANES, (b + 1) * LANES)
        psh_ref[lv, :, cs_] = ps_ref[lv, TT - n_psh:TT, cs_]


def ffn_kernel(layer, to_time_order, x_ref, wup_hbm, wdn_hbm, nw_ref, cw_ref, cb_ref, postn_ref, out_ref,
               wup_ref, wdn_ref, stage_ref, stage_dn_ref, stage_sem, halo_ref, a_ref, *out_scratch):
    TT = x_ref.shape[1]
    FT = FFN_FT
    L = CHUNK
    n_tail = (FFN_CONV - 1) * SUBLANES
    i = pl.program_id(1)

    @pl.when((pl.program_id(0) == 0) & (i == 0))
    def _():
        _stage_weights(_col_blocks(wup_hbm.at[layer], 0, wup_ref, 0, 2 * D_FF, STAGE_COLS),
                       stage_ref, stage_sem)
        _stage_weights([(wdn_hbm.at[layer, r:r + STAGE_COLS, :], wdn_ref.at[r:r + STAGE_COLS, :])
                        for r in range(0, D_FF, STAGE_COLS)], stage_dn_ref, stage_sem)

    @pl.when(i == 0)
    def _():
        halo_ref[...] = jnp.zeros(halo_ref.shape, F32)

    x = x_ref[0]
    ms = jnp.mean(x * x, axis=-1, keepdims=True)
    h = (x * lax.rsqrt(ms + EPS) * nw_ref[...]).astype(BF16)

    def conv_cols(col):
        u = _dot(h, wup_ref[:, col:col + FT])
        outs = []
        for c in range(TT // L):
            cur = u[c * L:(c + 1) * L]
            prev_tail = halo_ref[:, col:col + FT] if c == 0 else u[c * L - n_tail:c * L]
            outs.append(_causal_conv(prev_tail, cur, cw_ref, cb_ref[:, col:col + FT], col, FT, FFN_CONV))
        halo_ref[:, col:col + FT] = u[TT - n_tail:TT]
        return outs

    for j in range(D_FF // FT):
        gts = conv_cols(j * FT)
        vals = conv_cols(D_FF + j * FT)
        for c in range(TT // L):
            gt = gts[c]
            gelu = 0.5 * gt * (1.0 + jnp.tanh(math.sqrt(2.0 / math.pi) * (gt + 0.044715 * (gt * gt * gt))))
            a_ref[c * L:(c + 1) * L, j * FT:(j + 1) * FT] = (gelu * vals[c]).astype(BF16)

    f = _dot(a_ref[...], wdn_ref[...])
    ms_f = jnp.mean(f * f, axis=-1, keepdims=True)
    res = x_ref[0] + f * lax.rsqrt(ms_f + EPS) * postn_ref[...]
    if not to_time_order:
        out_ref[0] = res
        return

    obuf_ref, out_sem = out_scratch
    n_c = TT // L
    b = pl.program_id(0)
    step = b * pl.num_programs(1) + i
    n_steps = pl.num_programs(0) * pl.num_programs(1)
    slot = step % 2

    def out_copies(sl):
        return [pltpu.make_async_copy(obuf_ref.at[sl, :, :, s, :], out_ref.at[b, pl.ds(i * n_c, n_c), s],
                                      out_sem.at[sl]) for s in range(SUBLANES)]

    @pl.when(step >= 2)
    def _():
        for cp in out_copies(slot):
            cp.wait()

    obuf_ref[slot] = res.reshape(n_c, VROWS, SUBLANES, res.shape[1])
    for cp in out_copies(slot):
        cp.start()

    @pl.when(step == n_steps - 1)
    def _():
        for cp in out_copies(slot):
            cp.wait()

    @pl.when((step == n_steps - 1) & (n_steps >= 2))
    def _():
        for cp in out_copies(1 - slot):
            cp.wait()


def _const_spec(shape):
    nd = len(shape)
    return pl.BlockSpec(shape, lambda b, i: (0,) * nd, pipeline_mode=pl.Buffered(1))


def _mix_layer(layer, from_time_order, x, wout, win, nw, gbias, alog, xcw, xcb, qcw, qcb, poolw, poolb, pools,
               dskip, snorm, mnorm, postn):
    B, T, D = x.shape
    TT = MIX_TT
    weights = (wout,)
    consts = (win, nw, gbias, alog, xcw, xcb, qcw, qcb, poolw, poolb, pools, dskip, snorm, mnorm, postn)
    n_t = T // TT
    cur_spec = pl.BlockSpec((1, TT, D), lambda b, i: (b, jnp.minimum(i, n_t - 1), 0))
    prev_spec = pl.BlockSpec((1, TT, D), lambda b, i: (b, jnp.maximum(i - 1, 0), 0))
    max_conv_tail = (max(SSD_CONV, MLSTM_CONV) - 1) * SUBLANES
    max_pool_tail = (POOL_WINDOWS[-1] // 2) * SUBLANES
    if from_time_order:
        x_in = x.reshape(B, T // CHUNK, SUBLANES, VROWS, D)
        x_specs = [pl.BlockSpec(memory_space=pltpu.HBM)] * 2
        in_scratch = [pltpu.VMEM((3, TT // CHUNK, VROWS, SUBLANES, D), F32),
                      pltpu.SemaphoreType.DMA((3,))]
    else:
        x_in, x_specs, in_scratch = x, [cur_spec, prev_spec], []
    return pl.pallas_call(
        functools.partial(mix_kernel, layer, from_time_order, n_t),
        grid=(B, n_t + 1),
        in_specs=(x_specs + [pl.BlockSpec(memory_space=pltpu.HBM) for _ in weights]
                  + [_const_spec(c.shape) for c in consts]),
        out_specs=prev_spec,
        out_shape=jax.ShapeDtypeStruct(x.shape, x.dtype),
        scratch_shapes=[
            pltpu.VMEM((D_MODEL, N_IN_COLS), BF16),
            pltpu.VMEM((D_MODEL, D_MODEL), BF16),
            pltpu.VMEM((STAGE_SLOTS, D_MODEL, STAGE_COLS), F32),
            pltpu.SemaphoreType.DMA((STAGE_SLOTS,)),
            pltpu.VMEM((TT, D_MODEL), BF16),
            pltpu.VMEM((2, TT, N_HALO_COLS), F32),
            pltpu.VMEM((2, TT, N_REST_COLS), F32),
            pltpu.VMEM((max_conv_tail, N_HALO_COLS), F32),
            pltpu.VMEM((3, max_pool_tail, D_POOL), F32),
            pltpu.VMEM((TT, D_SSD_XBC + 2 * D_MLSTM), F32),
            pltpu.VMEM((TT, D_MODEL), F32),
            pltpu.VMEM((3, TT, D_POOL), F32),
            pltpu.VMEM((SSD_GROUPS, SSD_STATE, 4 * SSD_HEAD_DIM), F32),
            pltpu.VMEM((MLSTM_HEADS, LANES, LANES), F32),
            pltpu.VMEM((SUBLANES, LANES), F32),
        ] + in_scratch,
        compiler_params=pltpu.CompilerParams(
            dimension_semantics=("arbitrary", "arbitrary"), vmem_limit_bytes=VMEM_LIMIT),
        name="mix_layer",
    )(x_in, x_in, *weights, *consts)


def _ffn_layer(layer, to_time_order, x, wup, wdn, nw, cw, cb, postn):
    B, T, D = x.shape
    TT = FFN_TT
    weights = (wup, wdn)
    consts = (nw, cw, cb, postn)
    x_spec = pl.BlockSpec((1, TT, D), lambda b, i: (b, i, 0))
    if to_time_order:
        out_spec = pl.BlockSpec(memory_space=pltpu.HBM)
        out_shape = jax.ShapeDtypeStruct((B, T // CHUNK, SUBLANES, VROWS, D), x.dtype)
        out_scratch = [pltpu.VMEM((2, TT // CHUNK, VROWS, SUBLANES, D), F32),
                       pltpu.SemaphoreType.DMA((2,))]
    else:
        out_spec, out_shape, out_scratch = x_spec, jax.ShapeDtypeStruct(x.shape, x.dtype), []
    return pl.pallas_call(
        functools.partial(ffn_kernel, layer, to_time_order),
        grid=(B, T // TT),
        in_specs=([x_spec] + [pl.BlockSpec(memory_space=pltpu.HBM) for _ in weights]
                  + [_const_spec(c.shape) for c in consts]),
        out_specs=out_spec,
        out_shape=out_shape,
        scratch_shapes=[
            pltpu.VMEM((D_MODEL, 2 * D_FF), BF16),
            pltpu.VMEM((D_FF, D_MODEL), BF16),
            pltpu.VMEM((STAGE_SLOTS, D_MODEL, STAGE_COLS), F32),
            pltpu.VMEM((STAGE_SLOTS, STAGE_COLS, D_MODEL), F32),
            pltpu.SemaphoreType.DMA((STAGE_SLOTS,)),
            pltpu.VMEM(((FFN_CONV - 1) * SUBLANES, 2 * D_FF), F32),
            pltpu.VMEM((TT, D_FF), BF16),
        ] + out_scratch,
        compiler_params=pltpu.CompilerParams(
            dimension_semantics=("arbitrary", "arbitrary"), vmem_limit_bytes=VMEM_LIMIT),
        name="ffn_layer",
    )(x, *weights, *consts).reshape(B, T, D)


def _row(v):
    return v.reshape(1, -1).astype(F32)


def _pad_lanes(v):
    return jnp.pad(v.astype(F32), (0, LANES - v.shape[0])).reshape(1, LANES)


def _prep_pool_w(w):
    out = jnp.zeros((D_POOL, D_POOL), F32)
    for g in range(len(POOL_WINDOWS)):
        s = g * POOL_GROUP_DIM
        out = lax.dynamic_update_slice(out, w[g].astype(F32), (s, s))
    return out.astype(BF16)


def kernel(x, pre_mix_norm, w_in, pool_w, pool_b, pool_scale, ssd_conv_w, ssd_conv_b, ssd_dt_bias, ssd_a_log, ssd_d, ssd_norm, mlstm_conv_w, mlstm_conv_b, mlstm_i_bias, mlstm_f_bias, mlstm_norm, w_out, post_mix_norm, pre_ffn_norm, ffn_w_up, ffn_conv_w, ffn_conv_b, ffn_w_down, post_ffn_norm):
    depth = w_in.shape[0]
    for l in range(depth):
        gbias = _pad_lanes(jnp.concatenate([ssd_dt_bias[l], mlstm_i_bias[l], mlstm_f_bias[l]]))
        x = _mix_layer(
            l, l == 0, x, w_out, w_in[l].astype(BF16), _row(pre_mix_norm[l]), gbias, _pad_lanes(ssd_a_log[l]),
            ssd_conv_w[l].astype(F32), _row(ssd_conv_b[l]), mlstm_conv_w[l].astype(F32), _row(mlstm_conv_b[l]),
            _prep_pool_w(pool_w[l]), _row(pool_b[l]), _row(pool_scale[l]),
            _row(jnp.repeat(ssd_d[l], SSD_HEAD_DIM)), _row(ssd_norm[l]), _row(mlstm_norm[l]),
            _row(post_mix_norm[l]))
        x = _ffn_layer(
            l, l == depth - 1, x, ffn_w_up, ffn_w_down, _row(pre_ffn_norm[l]), ffn_conv_w[l].astype(F32),
            _row(ffn_conv_b[l]), _row(post_ffn_norm[l]))
    return x
```

```python
import functools
import math

import jax
import jax.numpy as jnp
from jax import lax
from jax.experimental import pallas as pl
from jax.experimental.pallas import tpu as pltpu

F32 = jnp.float32
BF16 = jnp.bfloat16

D_MODEL = 1024
EPS = 1e-6

D_POOL = 256
POOL_GROUP_DIM = 64
POOL_WINDOWS = (2, 4, 8, 16)

D_SSD = 512
SSD_HEADS = 8
SSD_HEAD_DIM = 64
SSD_GROUPS = 2
SSD_STATE = 128
SSD_CONV = 4
D_SSD_XBC = D_SSD + 2 * SSD_GROUPS * SSD_STATE

D_MLSTM = 256
MLSTM_HEADS = 4
MLSTM_HEAD_DIM = 64
MLSTM_CONV = 4

D_FF = 2816
FFN_CONV = 3

CHUNK = 128
LANES = 128
SUBLANES = 8
VROWS = CHUNK // SUBLANES

C_POOL = 0
C_XBC = C_POOL + D_POOL
C_QK = C_XBC + D_SSD_XBC
N_HALO_COLS = C_QK + 2 * D_MLSTM
R_Z = 0
R_V = R_Z + D_SSD
R_O = R_V + D_MLSTM
R_G = R_O + D_MLSTM
N_REST_COLS = R_G + LANES
N_IN_COLS = N_HALO_COLS + N_REST_COLS
G_DT = 0
G_I = SSD_HEADS
G_F = G_I + MLSTM_HEADS

MIX_TT = 256
PROJ_COLS = 256
STAGE_COLS = 256
STAGE_SLOTS = 4
FFN_TT = 512
FFN_FT = 256
VMEM_LIMIT = 56 * 1024 * 1024


def _dot(a, b):
    return jnp.dot(a, b, preferred_element_type=F32)


def _dot_nt(a, b):
    return lax.dot_general(a, b, (((1,), (1,)), ((), ())), preferred_element_type=F32)


def _silu(x):
    return x * jax.nn.sigmoid(x)


def _split3(a):
    hi = a.astype(BF16)
    r = a - hi.astype(F32)
    mid = r.astype(BF16)
    lo = (r - mid.astype(F32)).astype(BF16)
    return hi, mid, lo


def _bcast_lane(a, j, shape):
    return jnp.broadcast_to(a[:, j:j + 1], shape)


def _pair_expand(a, h_even, shape, lo_half):
    return jnp.where(lo_half, _bcast_lane(a, h_even, shape), _bcast_lane(a, h_even + 1, shape))


def _stage_weights(blocks, stage_ref, sem_ref):
    n_slots = stage_ref.shape[0]

    def copy(n):
        src, dst = blocks[n]
        rows, cols = src.shape
        return pltpu.make_async_copy(src, stage_ref.at[n % n_slots, 0:rows, 0:cols], sem_ref.at[n % n_slots])

    for n in range(min(n_slots - 1, len(blocks))):
        copy(n).start()
    for n, (src, dst) in enumerate(blocks):
        if n + n_slots - 1 < len(blocks):
            copy(n + n_slots - 1).start()
        copy(n).wait()
        rows, cols = src.shape
        dst[...] = stage_ref[n % n_slots, 0:rows, 0:cols].astype(BF16)


def _col_blocks(src_ref, src0, dst_ref, dst0, ncols, step):
    return [(src_ref.at[:, src0 + k:src0 + min(k + step, ncols)],
             dst_ref.at[:, dst0 + k:dst0 + min(k + step, ncols)]) for k in range(0, ncols, step)]


def _tau(p):
    return (p % SUBLANES) * VROWS + p // SUBLANES


def _ext_rows(prev_tail, cur_tail):
    n = cur_tail.shape[0] // SUBLANES
    sub0 = lax.broadcasted_iota(jnp.int32, (SUBLANES, cur_tail.shape[1]), 0) == 0
    out = []
    for j in range(n):
        sl = slice(j * SUBLANES, (j + 1) * SUBLANES)
        out.append(jnp.where(sub0, pltpu.roll(prev_tail[sl], 1, 0), pltpu.roll(cur_tail[sl], 1, 0)))
    return out


def _shifted(ext, cur, k):
    if k == 0:
        return cur
    return jnp.concatenate(ext[len(ext) - k:] + [cur[0:CHUNK - SUBLANES * k]], axis=0)


def _causal_conv(prev_tail, cur, w_ref, b_row, wcol, ncols, k_taps):
    n = k_taps - 1
    ext = _ext_rows(prev_tail, cur[CHUNK - n * SUBLANES:CHUNK])
    acc = b_row
    for k in range(k_taps):
        acc = acc + _shifted(ext, cur, n - k) * w_ref[k:k + 1, wcol:wcol + ncols]
    return acc


def _relayout_w_in(wraw_ref, win_ref):
    c_z = D_POOL
    c_xbc = c_z + D_SSD
    c_dt = c_xbc + D_SSD_XBC
    n_tail = SSD_HEADS + 4 * D_MLSTM + 2 * MLSTM_HEADS
    o_qk = SSD_HEADS
    o_v = o_qk + 2 * D_MLSTM
    o_if = o_v + 2 * D_MLSTM
    for r in range(0, D_MODEL, CHUNK):
        rows = slice(r, r + CHUNK)
        win_ref[rows, C_POOL:C_POOL + D_POOL] = wraw_ref[rows, 0:D_POOL]
        win_ref[rows, C_XBC:C_XBC + D_SSD_XBC] = wraw_ref[rows, c_xbc:c_dt]
        win_ref[rows, N_HALO_COLS + R_Z:N_HALO_COLS + R_Z + D_SSD] = wraw_ref[rows, c_z:c_xbc]
        tail = wraw_ref[rows, c_dt:c_dt + n_tail]
        win_ref[rows, C_QK:C_QK + 2 * D_MLSTM] = tail[:, o_qk:o_v]
        win_ref[rows, N_HALO_COLS + R_V:N_HALO_COLS + R_V + 2 * D_MLSTM] = tail[:, o_v:o_if]
        pad = jnp.zeros((CHUNK, LANES - SSD_HEADS - 2 * MLSTM_HEADS), tail.dtype)
        win_ref[rows, N_HALO_COLS + R_G:N_HALO_COLS + R_G + LANES] = jnp.concatenate(
            [tail[:, 0:o_qk], tail[:, o_if:n_tail], pad], axis=1)


def mix_kernel(layer, from_time_order, n_tiles, xc_ref, xp_ref, wout_hbm, wraw_ref, nw_ref, gbias_ref, alog_ref,
               xcw_ref, xcb_ref, qcw_ref, qcb_ref,
               poolw_ref, poolb_ref, pools_ref, dskip_ref, snorm_ref, mnorm_ref, postn_ref,
               out_ref, win_ref, wout_ref, stage_ref, stage_sem,
               h_ref, hp_ref, rest_ref, hph_ref, psh_ref, act_ref, mix_ref, ps_ref,
               sstate_ref, mstate_ref, mm_ref, *in_scratch):
    TT = hp_ref.shape[1]
    L = CHUNK
    n_c = TT // L
    b = pl.program_id(0)
    i = pl.program_id(1)
    n_t = n_tiles

    if from_time_order:
        xbuf_ref, in_sem = in_scratch
        n_slots = xbuf_ref.shape[0]

        def in_copies(tile):
            sl = tile % n_slots
            return [pltpu.make_async_copy(xc_ref.at[b, pl.ds(tile * n_c, n_c), s], xbuf_ref.at[sl, :, :, s, :],
                                          in_sem.at[sl]) for s in range(SUBLANES)]

        @pl.when(i == 0)
        def _():
            for cp in in_copies(0):
                cp.start()

        @pl.when(i + 1 < n_t)
        def _():
            for cp in in_copies(i + 1):
                cp.start()

        @pl.when(i < n_t)
        def _():
            for cp in in_copies(i):
                cp.wait()

        slot_cur = jnp.minimum(i, n_t - 1) % n_slots
        slot_prev = jnp.maximum(i - 1, 0) % n_slots

        def x_cur():
            return xbuf_ref[slot_cur].reshape(TT, xbuf_ref.shape[-1])

        def x_rows(r0):
            return xbuf_ref[slot_prev, r0 // L].reshape(L, xbuf_ref.shape[-1])
    else:
        def x_cur():
            return xc_ref[0]

        def x_rows(r0):
            return xp_ref[0, r0:r0 + L, :]

    @pl.when((pl.program_id(0) == 0) & (i == 0))
    def _():
        _relayout_w_in(wraw_ref, win_ref)
        _stage_weights(_col_blocks(wout_hbm.at[layer], 0, wout_ref, 0, D_MODEL, STAGE_COLS),
                       stage_ref, stage_sem)

    @pl.when(i == 0)
    def _():
        hph_ref[...] = jnp.zeros(hph_ref.shape, F32)
        psh_ref[...] = jnp.zeros(psh_ref.shape, F32)
        sstate_ref[...] = jnp.zeros(sstate_ref.shape, F32)
        mstate_ref[...] = jnp.zeros(mstate_ref.shape, F32)
        mm_ref[...] = jnp.zeros(mm_ref.shape, F32)

    def step(slot_proj, slot_mix):
        pending = []
        if slot_proj is not None:
            x = x_cur()
            ms = jnp.mean(x * x, axis=-1, keepdims=True)
            h_ref[...] = (x * lax.rsqrt(ms + EPS) * nw_ref[...]).astype(BF16)

            def proj_piece(dst_ref, c0, c1, w0):
                def piece():
                    dst_ref[slot_proj, :, c0:c1] = _dot(h_ref[...], win_ref[:, w0 + c0:w0 + c1])
                return piece

            pending += [proj_piece(hp_ref, c0, min(c0 + PROJ_COLS, N_HALO_COLS), 0)
                        for c0 in range(0, N_HALO_COLS, PROJ_COLS)]
            pending += [proj_piece(rest_ref, c0, min(c0 + PROJ_COLS, N_REST_COLS), N_HALO_COLS)
                        for c0 in range(0, N_REST_COLS, PROJ_COLS)]
        if slot_mix is not None:
            _mixers(i - 1, x_rows, hp_ref.at[slot_mix], rest_ref.at[slot_mix], gbias_ref, alog_ref,
                    xcw_ref, xcb_ref, qcw_ref, qcb_ref, poolw_ref, poolb_ref, pools_ref, dskip_ref, snorm_ref,
                    mnorm_ref, wout_ref, postn_ref, out_ref, hph_ref, psh_ref, act_ref, mix_ref, ps_ref,
                    sstate_ref, mstate_ref, mm_ref, pending)
        while pending:
            pending.pop(0)()

    last_slot = (n_tiles - 1) % 2

    @pl.when(i == 0)
    def _():
        step(0, None)

    @pl.when((i > 0) & (i < n_tiles) & (i % 2 == 0))
    def _():
        step(0, 1)

    @pl.when((i > 0) & (i < n_tiles) & (i % 2 == 1))
    def _():
        step(1, 0)

    @pl.when(i == n_tiles)
    def _():
        step(None, last_slot)


def _mixers(tile, x_rows, hp_ref, rest_ref, gbias_ref, alog_ref, xcw_ref, xcb_ref, qcw_ref, qcb_ref,
            poolw_ref, poolb_ref, pools_ref, dskip_ref, snorm_ref, mnorm_ref, wout_ref, postn_ref,
            out_ref, hph_ref, psh_ref, act_ref, mix_ref, ps_ref, sstate_ref, mstate_ref, mm_ref, pending):
    def between():
        if pending:
            pending.pop(0)()

    TT = hp_ref.shape[0]
    L = CHUNK
    n_hph = hph_ref.shape[0]
    n_psh = psh_ref.shape[1]
    ps_carried = ((0, 0), (0, 1), (1, 1), (2, 1))

    def tail(cur_ref, halo, n_halo, r0, n_rows, cols):
        if r0 == 0:
            return halo[n_halo - n_rows:n_halo, cols]
        return cur_ref[r0 - n_rows:r0, cols]

    lane = lax.broadcasted_iota(jnp.int32, (L, LANES), 1)
    row = lax.broadcasted_iota(jnp.int32, (L, LANES), 0)
    lo_half = lane < 64
    tau_row = _tau(row)
    causal = _tau(lane) <= tau_row
    tril = jnp.where(causal, 1.0, 0.0).astype(BF16)
    lane_row = lax.broadcasted_iota(jnp.int32, (1, LANES), 1)
    a_row = -jnp.exp(alog_ref[...])
    neg_inf = -jnp.inf
    win_blk = [jnp.where(lo_half, float(POOL_WINDOWS[2 * b]), float(POOL_WINDOWS[2 * b + 1])) for b in range(2)]
    tau_f = tau_row.astype(F32)

    def chunk_stages(c):
        r0 = c * L

        def conv_block(col, cw_ref, cb_ref, wcol, k_taps):
            cols = slice(col, col + LANES)
            cur = hp_ref[r0:r0 + L, cols]
            prev_tail = tail(hp_ref, hph_ref, n_hph, r0, (k_taps - 1) * SUBLANES, cols)
            return _silu(_causal_conv(prev_tail, cur, cw_ref, cb_ref[:, wcol:wcol + LANES], wcol, LANES, k_taps))

        for blk in range(D_SSD_XBC // LANES):
            act_ref[r0:r0 + L, blk * LANES:(blk + 1) * LANES] = conv_block(
                C_XBC + blk * LANES, xcw_ref, xcb_ref, blk * LANES, SSD_CONV)
        for blk in range(2 * D_MLSTM // LANES):
            act_ref[r0:r0 + L, D_SSD_XBC + blk * LANES:D_SSD_XBC + (blk + 1) * LANES] = conv_block(
                C_QK + blk * LANES, qcw_ref, qcb_ref, blk * LANES, MLSTM_CONV)
        yield

        pos = tau_f + (tile * TT + r0 + 1).astype(F32)
        pooled_blocks = []
        for b in range(2):
            cs_ = slice(b * LANES, (b + 1) * LANES)
            u_cur = hp_ref[r0:r0 + L, cs_]
            lvl = u_cur
            sums = []
            for li, sh in enumerate((1, 2, 4, 8)):
                if li == 0:
                    prev_tail = tail(hp_ref, hph_ref, n_hph, r0, sh * SUBLANES, cs_)
                else:
                    prev_tail = tail(ps_ref.at[li - 1], psh_ref.at[li - 1], n_psh, r0, sh * SUBLANES, cs_)
                ext = _ext_rows(prev_tail, lvl[L - sh * SUBLANES:L])
                lvl = lvl + _shifted(ext, lvl, sh)
                sums.append(lvl)
                if (li, b) in ps_carried:
                    ps_ref[li, r0:r0 + L, cs_] = lvl
                if b == 0 and li == 1:
                    break
            wsum = jnp.where(lo_half, sums[0], sums[1]) if b == 0 else jnp.where(lo_half, sums[2], sums[3])
            pooled_blocks.append((wsum / jnp.minimum(pos, win_blk[b]) - u_cur).astype(BF16))
        mix_ref[r0:r0 + L, 0:D_POOL] = (
            (_dot(jnp.concatenate(pooled_blocks, axis=1), poolw_ref[...]) + poolb_ref[...]) * pools_ref[...])
        yield

        gb = rest_ref[r0:r0 + L, R_G:R_G + LANES] + gbias_ref[...]
        sp_term = jnp.log(1.0 + jnp.exp(-jnp.abs(gb)))
        dt = jnp.maximum(gb, 0.0) + sp_term
        log_f = jnp.minimum(gb, 0.0) - sp_term
        is_dt = lane < G_I
        is_f = (lane >= G_F) & (lane < G_F + MLSTM_HEADS)
        v_cum = jnp.where(is_dt, dt * a_row, jnp.where(is_f, log_f, 0.0))
        hi, mid, lo = _split3(v_cum)
        cs = _dot(tril, hi) + _dot(tril, mid) + _dot(tril, lo)
        u_gate = jnp.where(is_dt, dt, gb)
        cs_t = cs.T
        ug_t = u_gate.T
        cs_last = cs[L - 1:L, :]
        e_col = jnp.exp(cs)
        w_col = jnp.exp(cs_last - cs) * dt
        e_last = jnp.exp(cs_last)
        yield

        for g in range(SSD_GROUPS):
            b_g = act_ref[r0:r0 + L, D_SSD + g * SSD_STATE:D_SSD + (g + 1) * SSD_STATE].astype(BF16)
            c_g = act_ref[r0:r0 + L, D_SSD + (SSD_GROUPS + g) * SSD_STATE:
                          D_SSD + (SSD_GROUPS + g + 1) * SSD_STATE].astype(BF16)
            s_g = _dot_nt(c_g, b_g)
            state_g = sstate_ref[g]
            y_off = _dot(c_g, state_g.astype(BF16))
            xd_blocks = []
            cd_blocks = []
            for pr in range(2):
                h_even = 4 * g + 2 * pr
                col = h_even * SSD_HEAD_DIM
                xs = act_ref[r0:r0 + L, col:col + LANES]
                xs_b = xs.astype(BF16)
                m_pair = []
                for hh in range(2):
                    hd = h_even + hh
                    seg = jnp.where(causal, cs[:, hd:hd + 1] - cs_t[hd:hd + 1, :], neg_inf)
                    m_pair.append((s_g * (jnp.exp(seg) * ug_t[hd:hd + 1, :])).astype(BF16))
                yd = _dot(jnp.concatenate(m_pair, axis=0), xs_b)
                y_diag = jnp.where(lo_half, yd[0:L], yd[L:2 * L])
                e_exp = _pair_expand(e_col, h_even, (L, LANES), lo_half)
                w_exp = _pair_expand(w_col, h_even, (L, LANES), lo_half)
                y = (y_diag + y_off[:, pr * LANES:(pr + 1) * LANES] * e_exp
                     + xs * dskip_ref[:, col:col + LANES])
                z = rest_ref[r0:r0 + L, R_Z + col:R_Z + col + LANES]
                mix_ref[r0:r0 + L, D_POOL + col:D_POOL + col + LANES] = y * _silu(z)
                xd_blocks.append((xs * w_exp).astype(BF16))
                cd_blocks.append(_pair_expand(e_last, h_even, (1, LANES), lane_row < 64))
            xd_g = jnp.concatenate(xd_blocks, axis=1)
            cd_g = jnp.concatenate(cd_blocks, axis=1)
            new_states = lax.dot_general(b_g, xd_g, (((0,), (0,)), ((), ())),
                                         preferred_element_type=F32)
            sstate_ref[g] = state_g * cd_g + new_states
            yield
        y_all = mix_ref[r0:r0 + L, D_POOL:D_POOL + D_SSD]
        ms_y = jnp.mean(y_all * y_all, axis=-1, keepdims=True)
        mix_ref[r0:r0 + L, D_POOL:D_POOL + D_SSD] = y_all * lax.rsqrt(ms_y + EPS) * snorm_ref[...]

        for pr in range(MLSTM_HEADS // 2):
            qcol = D_SSD_XBC + pr * LANES
            kcol = D_SSD_XBC + D_MLSTM + pr * LANES
            q_b = act_ref[r0:r0 + L, qcol:qcol + LANES] * (MLSTM_HEAD_DIM ** -0.5)
            k_t = act_ref[r0:r0 + L, kcol:kcol + LANES].T
            k_tb = k_t.astype(BF16)
            v_b = rest_ref[r0:r0 + L, R_V + pr * LANES:R_V + (pr + 1) * LANES]
            o_b = rest_ref[r0:r0 + L, R_O + pr * LANES:R_O + (pr + 1) * LANES]
            hv = []
            qms = [jnp.where(lo_half, q_b, 0.0).astype(BF16), jnp.where(lo_half, 0.0, q_b).astype(BF16)]
            s_pair = _dot(jnp.concatenate(qms, axis=0), k_tb)
            for hh in range(2):
                hd = 2 * pr + hh
                in_half = lo_half if hh == 0 else jnp.logical_not(lo_half)
                row_in_half = (row < 64) if hh == 0 else (row >= 64)
                ones_lane = (lane == 64) if hh == 0 else (lane == 0)
                ol = 64 if hh == 0 else 0
                qm = qms[hh]
                s = s_pair[hh * L:(hh + 1) * L]
                b_row = cs_t[G_F + hd:G_F + hd + 1, :]
                r_row = ug_t[G_I + hd:G_I + hd + 1, :] - b_row
                b_last = jnp.sum(jnp.where(lane_row == L - 1, b_row, 0.0), axis=-1, keepdims=True)
                al_row = b_last + r_row
                m_loc = jnp.max(al_row, axis=-1, keepdims=True)
                prev_m_row = mm_ref[hd:hd + 1, :]
                rmask = jnp.where(causal, r_row, neg_inf)
                g_t = jnp.maximum(jnp.broadcast_to(jnp.max(rmask, axis=-1, keepdims=True), (L, LANES)),
                                  prev_m_row)
                p = (s * jnp.exp(rmask - g_t)).astype(BF16)
                v_ext = jnp.where(in_half, v_b, jnp.where(ones_lane, 1.0, 0.0)).astype(BF16)
                cn = mstate_ref[hd]
                res = _dot(p, v_ext) + _dot(qm, cn.astype(BF16)) * jnp.exp(prev_m_row - g_t)
                den = jnp.maximum(jnp.abs(_bcast_lane(res, ol, (L, LANES))),
                                  jnp.exp(-(_bcast_lane(cs, G_F + hd, (L, LANES)) + g_t)))
                hv.append(res / den)
                w_row = jnp.exp(al_row - m_loc)
                ktw = jnp.where(row_in_half, k_t * w_row, 0.0).astype(BF16)
                c_loc = _dot(ktw, v_ext)
                m_new = jnp.maximum(b_last + prev_m_row, m_loc)
                mstate_ref[hd] = (jnp.exp(b_last + prev_m_row - m_new) * cn
                                  + jnp.exp(m_loc - m_new) * c_loc)
                mm_ref[hd:hd + 1, :] = m_new
            hcat = jax.nn.sigmoid(o_b) * jnp.where(lo_half, hv[0], hv[1])
            sq = hcat * hcat
            ss_lo = jnp.sum(jnp.where(lo_half, sq, 0.0), axis=-1, keepdims=True)
            ss_hi = jnp.sum(jnp.where(lo_half, 0.0, sq), axis=-1, keepdims=True)
            inv = jnp.where(lo_half, lax.rsqrt(ss_lo * (1.0 / MLSTM_HEAD_DIM) + EPS),
                            lax.rsqrt(ss_hi * (1.0 / MLSTM_HEAD_DIM) + EPS))
            mcol = D_POOL + D_SSD + pr * LANES
            mix_ref[r0:r0 + L, mcol:mcol + LANES] = hcat * inv * mnorm_ref[:, pr * LANES:(pr + 1) * LANES]
            if pr + 1 < MLSTM_HEADS // 2:
                yield

        def out_piece():
            o = _dot(mix_ref[r0:r0 + L, :].astype(BF16), wout_ref[...])
            ms_o = jnp.mean(o * o, axis=-1, keepdims=True)
            out_ref[0, r0:r0 + L, :] = x_rows(r0) + o * lax.rsqrt(ms_o + EPS) * postn_ref[...]

        pending.insert(0, out_piece)
        yield

    n_stages = 3 + SSD_GROUPS + MLSTM_HEADS // 2
    gens = [chunk_stages(c) for c in range(TT // L)]
    for _ in range(n_stages):
        for gen in gens:
            next(gen)
            between()
    while pending:
        pending.pop(0)()

    hph_ref[...] = hp_ref[TT - n_hph:TT, :]
    for lv, b in ps_carried:
        cs_ = slice(b * LANES, (b + 1) * LANES)
        psh_ref[lv, :, cs_] = ps_ref[lv, TT - n_psh:TT, cs_]


def ffn_kernel(layer, to_time_order, x_ref, wup_hbm, wdn_hbm, nw_ref, cw_ref, cb_ref, postn_ref, out_ref,
               wup_ref, wdn_ref, stage_ref, stage_dn_ref, stage_sem, halo_ref, a_ref, *out_scratch):
    TT = x_ref.shape[1]
    FT = FFN_FT
    L = CHUNK
    n_tail = (FFN_CONV - 1) * SUBLANES
    i = pl.program_id(1)

    @pl.when((pl.program_id(0) == 0) & (i == 0))
    def _():
        _stage_weights(_col_blocks(wup_hbm.at[layer], 0, wup_ref, 0, 2 * D_FF, STAGE_COLS),
                       stage_ref, stage_sem)
        _stage_weights([(wdn_hbm.at[layer, r:r + STAGE_COLS, :], wdn_ref.at[r:r + STAGE_COLS, :])
                        for r in range(0, D_FF, STAGE_COLS)], stage_dn_ref, stage_sem)

    @pl.when(i == 0)
    def _():
        halo_ref[...] = jnp.zeros(halo_ref.shape, F32)

    x = x_ref[0]
    ms = jnp.mean(x * x, axis=-1, keepdims=True)
    h = (x * lax.rsqrt(ms + EPS) * nw_ref[...]).astype(BF16)

    def conv_cols(col):
        u = _dot(h, wup_ref[:, col:col + FT])
        outs = []
        for c in range(TT // L):
            cur = u[c * L:(c + 1) * L]
            prev_tail = halo_ref[:, col:col + FT] if c == 0 else u[c * L - n_tail:c * L]
            outs.append(_causal_conv(prev_tail, cur, cw_ref, cb_ref[:, col:col + FT], col, FT, FFN_CONV))
        halo_ref[:, col:col + FT] = u[TT - n_tail:TT]
        return outs

    for j in range(D_FF // FT):
        gts = conv_cols(j * FT)
        vals = conv_cols(D_FF + j * FT)
        for c in range(TT // L):
            gt = gts[c]
            gelu = 0.5 * gt * (1.0 + jnp.tanh(math.sqrt(2.0 / math.pi) * (gt + 0.044715 * (gt * gt * gt))))
            a_ref[c * L:(c + 1) * L, j * FT:(j + 1) * FT] = (gelu * vals[c]).astype(BF16)

    f = _dot(a_ref[...], wdn_ref[...])
    ms_f = jnp.mean(f * f, axis=-1, keepdims=True)
    res = x_ref[0] + f * lax.rsqrt(ms_f + EPS) * postn_ref[...]
    if not to_time_order:
        out_ref[0] = res
        return

    obuf_ref, out_sem = out_scratch
    n_c = TT // L
    b = pl.program_id(0)
    step = b * pl.num_programs(1) + i
    n_steps = pl.num_programs(0) * pl.num_programs(1)
    slot = step % 2

    def out_copies(sl):
        return [pltpu.make_async_copy(obuf_ref.at[sl, :, :, s, :], out_ref.at[b, pl.ds(i * n_c, n_c), s],
                                      out_sem.at[sl]) for s in range(SUBLANES)]

    @pl.when(step >= 2)
    def _():
        for cp in out_copies(slot):
            cp.wait()

    obuf_ref[slot] = res.reshape(n_c, VROWS, SUBLANES, res.shape[1])
    for cp in out_copies(slot):
        cp.start()

    @pl.when(step == n_steps - 1)
    def _():
        for cp in out_copies(slot):
            cp.wait()

    @pl.when((step == n_steps - 1) & (n_steps >= 2))
    def _():
        for cp in out_copies(1 - slot):
            cp.wait()


def _const_spec(shape):
    nd = len(shape)
    return pl.BlockSpec(shape, lambda b, i: (0,) * nd, pipeline_mode=pl.Buffered(1))


def _mix_layer(layer, from_time_order, x, wout, win, nw, gbias, alog, xcw, xcb, qcw, qcb, poolw, poolb, pools,
               dskip, snorm, mnorm, postn):
    B, T, D = x.shape
    TT = MIX_TT
    weights = (wout,)
    consts = (win, nw, gbias, alog, xcw, xcb, qcw, qcb, poolw, poolb, pools, dskip, snorm, mnorm, postn)
    n_t = T // TT
    cur_spec = pl.BlockSpec((1, TT, D), lambda b, i: (b, jnp.minimum(i, n_t - 1), 0))
    prev_spec = pl.BlockSpec((1, TT, D), lambda b, i: (b, jnp.maximum(i - 1, 0), 0))
    max_conv_tail = (max(SSD_CONV, MLSTM_CONV) - 1) * SUBLANES
    max_pool_tail = (POOL_WINDOWS[-1] // 2) * SUBLANES
    if from_time_order:
        x_in = x.reshape(B, T // CHUNK, SUBLANES, VROWS, D)
        x_specs = [pl.BlockSpec(memory_space=pltpu.HBM)] * 2
        in_scratch = [pltpu.VMEM((3, TT // CHUNK, VROWS, SUBLANES, D), F32),
                      pltpu.SemaphoreType.DMA((3,))]
    else:
        x_in, x_specs, in_scratch = x, [cur_spec, prev_spec], []
    return pl.pallas_call(
        functools.partial(mix_kernel, layer, from_time_order, n_t),
        grid=(B, n_t + 1),
        in_specs=(x_specs + [pl.BlockSpec(memory_space=pltpu.HBM) for _ in weights]
                  + [_const_spec(c.shape) for c in consts]),
        out_specs=prev_spec,
        out_shape=jax.ShapeDtypeStruct(x.shape, x.dtype),
        scratch_shapes=[
            pltpu.VMEM((D_MODEL, N_IN_COLS), BF16),
            pltpu.VMEM((D_MODEL, D_MODEL), BF16),
            pltpu.VMEM((STAGE_SLOTS, D_MODEL, STAGE_COLS), F32),
            pltpu.SemaphoreType.DMA((STAGE_SLOTS,)),
            pltpu.VMEM((TT, D_MODEL), BF16),
            pltpu.VMEM((2, TT, N_HALO_COLS), F32),
            pltpu.VMEM((2, TT, N_REST_COLS), F32),
            pltpu.VMEM((max_conv_tail, N_HALO_COLS), F32),
            pltpu.VMEM((3, max_pool_tail, D_POOL), F32),
            pltpu.VMEM((TT, D_SSD_XBC + 2 * D_MLSTM), F32),
            pltpu.VMEM((TT, D_MODEL), F32),
            pltpu.VMEM((3, TT, D_POOL), F32),
            pltpu.VMEM((SSD_GROUPS, SSD_STATE, 4 * SSD_HEAD_DIM), F32),
            pltpu.VMEM((MLSTM_HEADS, LANES, LANES), F32),
            pltpu.VMEM((SUBLANES, LANES), F32),
        ] + in_scratch,
        compiler_params=pltpu.CompilerParams(
            dimension_semantics=("arbitrary", "arbitrary"), vmem_limit_bytes=VMEM_LIMIT),
        name="mix_layer",
    )(x_in, x_in, *weights, *consts)


def _ffn_layer(layer, to_time_order, x, wup, wdn, nw, cw, cb, postn):
    B, T, D = x.shape
    TT = FFN_TT
    weights = (wup, wdn)
    consts = (nw, cw, cb, postn)
    x_spec = pl.BlockSpec((1, TT, D), lambda b, i: (b, i, 0))
    if to_time_order:
        out_spec = pl.BlockSpec(memory_space=pltpu.HBM)
        out_shape = jax.ShapeDtypeStruct((B, T // CHUNK, SUBLANES, VROWS, D), x.dtype)
        out_scratch = [pltpu.VMEM((2, TT // CHUNK, VROWS, SUBLANES, D), F32),
                       pltpu.SemaphoreType.DMA((2,))]
    else:
        out_spec, out_shape, out_scratch = x_spec, jax.ShapeDtypeStruct(x.shape, x.dtype), []
    return pl.pallas_call(
        functools.partial(ffn_kernel, layer, to_time_order),
        grid=(B, T // TT),
        in_specs=([x_spec] + [pl.BlockSpec(memory_space=pltpu.HBM) for _ in weights]
                  + [_const_spec(c.shape) for c in consts]),
        out_specs=out_spec,
        out_shape=out_shape,
        scratch_shapes=[
            pltpu.VMEM((D_MODEL, 2 * D_FF), BF16),
            pltpu.VMEM((D_FF, D_MODEL), BF16),
            pltpu.VMEM((STAGE_SLOTS, D_MODEL, STAGE_COLS), F32),
            pltpu.VMEM((STAGE_SLOTS, STAGE_COLS, D_MODEL), F32),
            pltpu.SemaphoreType.DMA((STAGE_SLOTS,)),
            pltpu.VMEM(((FFN_CONV - 1) * SUBLANES, 2 * D_FF), F32),
            pltpu.VMEM((TT, D_FF), BF16),
        ] + out_scratch,
        compiler_params=pltpu.CompilerParams(
            dimension_semantics=("arbitrary", "arbitrary"), vmem_limit_bytes=VMEM_LIMIT),
        name="ffn_layer",
    )(x, *weights, *consts).reshape(B, T, D)


def _row(v):
    return v.reshape(1, -1).astype(F32)


def _pad_lanes(v):
    return jnp.pad(v.astype(F32), (0, LANES - v.shape[0])).reshape(1, LANES)


def _prep_pool_w(w):
    out = jnp.zeros((D_POOL, D_POOL), F32)
    for g in range(len(POOL_WINDOWS)):
        s = g * POOL_GROUP_DIM
        out = lax.dynamic_update_slice(out, w[g].astype(F32), (s, s))
    return out.astype(BF16)


def kernel(x, pre_mix_norm, w_in, pool_w, pool_b, pool_scale, ssd_conv_w, ssd_conv_b, ssd_dt_bias, ssd_a_log, ssd_d, ssd_norm, mlstm_conv_w, mlstm_conv_b, mlstm_i_bias, mlstm_f_bias, mlstm_norm, w_out, post_mix_norm, pre_ffn_norm, ffn_w_up, ffn_conv_w, ffn_conv_b, ffn_w_down, post_ffn_norm):
    depth = w_in.shape[0]
    for l in range(depth):
        gbias = _pad_lanes(jnp.concatenate([ssd_dt_bias[l], mlstm_i_bias[l], mlstm_f_bias[l]]))
        x = _mix_layer(
            l, l == 0, x, w_out, w_in[l].astype(BF16), _row(pre_mix_norm[l]), gbias, _pad_lanes(ssd_a_log[l]),
            ssd_conv_w[l].astype(F32), _row(ssd_conv_b[l]), mlstm_conv_w[l].astype(F32), _row(mlstm_conv_b[l]),
            _prep_pool_w(pool_w[l]), _row(pool_b[l]), _row(pool_scale[l]),
            _row(jnp.repeat(ssd_d[l], SSD_HEAD_DIM)), _row(ssd_norm[l]), _row(mlstm_norm[l]),
            _row(post_mix_norm[l]))
        x = _ffn_layer(
            l, l == depth - 1, x, ffn_w_up, ffn_w_down, _row(pre_ffn_norm[l]), ffn_conv_w[l].astype(F32),
            _row(ffn_conv_b[l]), _row(post_ffn_norm[l]))
    return x
```

```python
import functools
import math

import jax
import jax.numpy as jnp
from jax import lax
from jax.experimental import pallas as pl
from jax.experimental.pallas import tpu as pltpu

F32 = jnp.float32
BF16 = jnp.bfloat16

D_MODEL = 1024
EPS = 1e-6

D_POOL = 256
POOL_GROUP_DIM = 64
POOL_WINDOWS = (2, 4, 8, 16)

D_SSD = 512
SSD_HEADS = 8
SSD_HEAD_DIM = 64
SSD_GROUPS = 2
SSD_STATE = 128
SSD_CONV = 4
D_SSD_XBC = D_SSD + 2 * SSD_GROUPS * SSD_STATE

D_MLSTM = 256
MLSTM_HEADS = 4
MLSTM_HEAD_DIM = 64
MLSTM_CONV = 4

D_FF = 2816
FFN_CONV = 3

CHUNK = 128
LANES = 128
SUBLANES = 8
VROWS = CHUNK // SUBLANES

C_POOL = 0
C_XBC = C_POOL + D_POOL
C_QK = C_XBC + D_SSD_XBC
N_HALO_COLS = C_QK + 2 * D_MLSTM
R_Z = 0
R_V = R_Z + D_SSD
R_O = R_V + D_MLSTM
R_G = R_O + D_MLSTM
N_REST_COLS = R_G + LANES
N_IN_COLS = N_HALO_COLS + N_REST_COLS
G_DT = 0
G_I = SSD_HEADS
G_F = G_I + MLSTM_HEADS

MIX_TT = 256
PROJ_COLS = 256
STAGE_COLS = 256
STAGE_SLOTS = 4
FFN_TT = 512
FFN_FT = 256
VMEM_LIMIT = 56 * 1024 * 1024


def _dot(a, b):
    return jnp.dot(a, b, preferred_element_type=F32)


def _dot_nt(a, b):
    return lax.dot_general(a, b, (((1,), (1,)), ((), ())), preferred_element_type=F32)


def _silu(x):
    return x * jax.nn.sigmoid(x)


def _split3(a):
    hi = a.astype(BF16)
    r = a - hi.astype(F32)
    mid = r.astype(BF16)
    lo = (r - mid.astype(F32)).astype(BF16)
    return hi, mid, lo


def _bcast_lane(a, j, shape):
    return jnp.broadcast_to(a[:, j:j + 1], shape)


def _pair_expand(a, h_even, shape, lo_half):
    return jnp.where(lo_half, _bcast_lane(a, h_even, shape), _bcast_lane(a, h_even + 1, shape))


def _stage_weights(blocks, stage_ref, sem_ref):
    n_slots = stage_ref.shape[0]

    def copy(n):
        src, dst = blocks[n]
        rows, cols = src.shape
        return pltpu.make_async_copy(src, stage_ref.at[n % n_slots, 0:rows, 0:cols], sem_ref.at[n % n_slots])

    for n in range(min(n_slots - 1, len(blocks))):
        copy(n).start()
    for n, (src, dst) in enumerate(blocks):
        if n + n_slots - 1 < len(blocks):
            copy(n + n_slots - 1).start()
        copy(n).wait()
        rows, cols = src.shape
        dst[...] = stage_ref[n % n_slots, 0:rows, 0:cols].astype(BF16)


def _col_blocks(src_ref, src0, dst_ref, dst0, ncols, step):
    return [(src_ref.at[:, src0 + k:src0 + min(k + step, ncols)],
             dst_ref.at[:, dst0 + k:dst0 + min(k + step, ncols)]) for k in range(0, ncols, step)]


def _tau(p):
    return (p % SUBLANES) * VROWS + p // SUBLANES


def _ext_rows(prev_tail, cur_tail):
    n = cur_tail.shape[0] // SUBLANES
    sub0 = lax.broadcasted_iota(jnp.int32, (SUBLANES, cur_tail.shape[1]), 0) == 0
    out = []
    for j in range(n):
        sl = slice(j * SUBLANES, (j + 1) * SUBLANES)
        out.append(jnp.where(sub0, pltpu.roll(prev_tail[sl], 1, 0), pltpu.roll(cur_tail[sl], 1, 0)))
    return out


def _shifted(ext, cur, k):
    if k == 0:
        return cur
    return jnp.concatenate(ext[len(ext) - k:] + [cur[0:CHUNK - SUBLANES * k]], axis=0)


def _causal_conv(prev_tail, cur, w_ref, b_row, wcol, ncols, k_taps):
    n = k_taps - 1
    ext = _ext_rows(prev_tail, cur[CHUNK - n * SUBLANES:CHUNK])
    acc = b_row
    for k in range(k_taps):
        acc = acc + _shifted(ext, cur, n - k) * w_ref[k:k + 1, wcol:wcol + ncols]
    return acc


def _relayout_w_in(wraw_ref, win_ref):
    c_z = D_POOL
    c_xbc = c_z + D_SSD
    c_dt = c_xbc + D_SSD_XBC
    n_tail = SSD_HEADS + 4 * D_MLSTM + 2 * MLSTM_HEADS
    o_qk = SSD_HEADS
    o_v = o_qk + 2 * D_MLSTM
    o_if = o_v + 2 * D_MLSTM
    for r in range(0, D_MODEL, CHUNK):
        rows = slice(r, r + CHUNK)
        win_ref[rows, C_POOL:C_POOL + D_POOL] = wraw_ref[rows, 0:D_POOL]
        win_ref[rows, C_XBC:C_XBC + D_SSD_XBC] = wraw_ref[rows, c_xbc:c_dt]
        win_ref[rows, N_HALO_COLS + R_Z:N_HALO_COLS + R_Z + D_SSD] = wraw_ref[rows, c_z:c_xbc]
        tail = wraw_ref[rows, c_dt:c_dt + n_tail]
        win_ref[rows, C_QK:C_QK + 2 * D_MLSTM] = tail[:, o_qk:o_v]
        win_ref[rows, N_HALO_COLS + R_V:N_HALO_COLS + R_V + 2 * D_MLSTM] = tail[:, o_v:o_if]
        pad = jnp.zeros((CHUNK, LANES - SSD_HEADS - 2 * MLSTM_HEADS), tail.dtype)
        win_ref[rows, N_HALO_COLS + R_G:N_HALO_COLS + R_G + LANES] = jnp.concatenate(
            [tail[:, 0:o_qk], tail[:, o_if:n_tail], pad], axis=1)


def mix_kernel(layer, from_time_order, n_tiles, xc_ref, xp_ref, wout_hbm, wraw_ref, nw_ref, gbias_ref, alog_ref,
               xcw_ref, xcb_ref, qcw_ref, qcb_ref,
               poolw_ref, poolb_ref, pools_ref, dskip_ref, snorm_ref, mnorm_ref, postn_ref,
               out_ref, win_ref, wout_ref, stage_ref, stage_sem,
               h_ref, hp_ref, rest_ref, hph_ref, psh_ref, act_ref, mix_ref, ps_ref,
               sstate_ref, mstate_ref, mm_ref, *in_scratch):
    TT = hp_ref.shape[1]
    L = CHUNK
    n_c = TT // L
    b = pl.program_id(0)
    i = pl.program_id(1)
    n_t = n_tiles

    if from_time_order:
        xbuf_ref, in_sem = in_scratch
        n_slots = xbuf_ref.shape[0]

        def in_copies(tile):
            sl = tile % n_slots
            return [pltpu.make_async_copy(xc_ref.at[b, pl.ds(tile * n_c, n_c), s], xbuf_ref.at[sl, :, :, s, :],
                                          in_sem.at[sl]) for s in range(SUBLANES)]

        @pl.when(i == 0)
        def _():
            for cp in in_copies(0):
                cp.start()

        @pl.when(i + 1 < n_t)
        def _():
            for cp in in_copies(i + 1):
                cp.start()

        @pl.when(i < n_t)
        def _():
            for cp in in_copies(i):
                cp.wait()

        slot_cur = jnp.minimum(i, n_t - 1) % n_slots
        slot_prev = jnp.maximum(i - 1, 0) % n_slots

        def x_cur():
            return xbuf_ref[slot_cur].reshape(TT, xbuf_ref.shape[-1])

        def x_rows(r0):
            return xbuf_ref[slot_prev, r0 // L].reshape(L, xbuf_ref.shape[-1])
    else:
        def x_cur():
            return xc_ref[0]

        def x_rows(r0):
            return xp_ref[0, r0:r0 + L, :]

    @pl.when((pl.program_id(0) == 0) & (i == 0))
    def _():
        _relayout_w_in(wraw_ref, win_ref)
        _stage_weights(_col_blocks(wout_hbm.at[layer], 0, wout_ref, 0, D_MODEL, STAGE_COLS),
                       stage_ref, stage_sem)

    @pl.when(i == 0)
    def _():
        hph_ref[...] = jnp.zeros(hph_ref.shape, F32)
        psh_ref[...] = jnp.zeros(psh_ref.shape, F32)
        sstate_ref[...] = jnp.zeros(sstate_ref.shape, F32)
        mstate_ref[...] = jnp.zeros(mstate_ref.shape, F32)
        mm_ref[...] = jnp.zeros(mm_ref.shape, F32)

    def step(slot_proj, slot_mix):
        pending = []
        if slot_proj is not None:
            x = x_cur()
            ms = jnp.mean(x * x, axis=-1, keepdims=True)
            h_ref[...] = (x * lax.rsqrt(ms + EPS) * nw_ref[...]).astype(BF16)

            def proj_piece(dst_ref, c0, c1, w0):
                def piece():
                    dst_ref[slot_proj, :, c0:c1] = _dot(h_ref[...], win_ref[:, w0 + c0:w0 + c1])
                return piece

            pending += [proj_piece(hp_ref, c0, min(c0 + PROJ_COLS, N_HALO_COLS), 0)
                        for c0 in range(0, N_HALO_COLS, PROJ_COLS)]
            pending += [proj_piece(rest_ref, c0, min(c0 + PROJ_COLS, N_REST_COLS), N_HALO_COLS)
                        for c0 in range(0, N_REST_COLS, PROJ_COLS)]
        if slot_mix is not None:
            _mixers(i - 1, x_rows, hp_ref.at[slot_mix], rest_ref.at[slot_mix], gbias_ref, alog_ref,
                    xcw_ref, xcb_ref, qcw_ref, qcb_ref, poolw_ref, poolb_ref, pools_ref, dskip_ref, snorm_ref,
                    mnorm_ref, wout_ref, postn_ref, out_ref, hph_ref, psh_ref, act_ref, mix_ref, ps_ref,
                    sstate_ref, mstate_ref, mm_ref, pending)
        while pending:
            pending.pop(0)()

    last_slot = (n_tiles - 1) % 2

    @pl.when(i == 0)
    def _():
        step(0, None)

    @pl.when((i > 0) & (i < n_tiles) & (i % 2 == 0))
    def _():
        step(0, 1)

    @pl.when((i > 0) & (i < n_tiles) & (i % 2 == 1))
    def _():
        step(1, 0)

    @pl.when(i == n_tiles)
    def _():
        step(None, last_slot)


def _mixers(tile, x_rows, hp_ref, rest_ref, gbias_ref, alog_ref, xcw_ref, xcb_ref, qcw_ref, qcb_ref,
            poolw_ref, poolb_ref, pools_ref, dskip_ref, snorm_ref, mnorm_ref, wout_ref, postn_ref,
            out_ref, hph_ref, psh_ref, act_ref, mix_ref, ps_ref, sstate_ref, mstate_ref, mm_ref, pending):
    def between():
        if pending:
            pending.pop(0)()

    TT = hp_ref.shape[0]
    L = CHUNK
    n_hph = hph_ref.shape[0]
    n_psh = psh_ref.shape[1]
    ps_carried = ((0, 0), (0, 1), (1, 1), (2, 1))

    def tail(cur_ref, halo, n_halo, r0, n_rows, cols):
        if r0 == 0:
            return halo[n_halo - n_rows:n_halo, cols]
        return cur_ref[r0 - n_rows:r0, cols]

    lane = lax.broadcasted_iota(jnp.int32, (L, LANES), 1)
    row = lax.broadcasted_iota(jnp.int32, (L, LANES), 0)
    lo_half = lane < 64
    tau_row = _tau(row)
    causal = _tau(lane) <= tau_row
    tril = jnp.where(causal, 1.0, 0.0).astype(BF16)
    lane_row = lax.broadcasted_iota(jnp.int32, (1, LANES), 1)
    a_row = -jnp.exp(alog_ref[...])
    neg_inf = -jnp.inf
    win_blk = [jnp.where(lo_half, float(POOL_WINDOWS[2 * b]), float(POOL_WINDOWS[2 * b + 1])) for b in range(2)]
    tau_f = tau_row.astype(F32)

    def chunk_stages(c):
        r0 = c * L

        def conv_block(col, cw_ref, cb_ref, wcol, k_taps):
            cols = slice(col, col + LANES)
            cur = hp_ref[r0:r0 + L, cols]
            prev_tail = tail(hp_ref, hph_ref, n_hph, r0, (k_taps - 1) * SUBLANES, cols)
            return _silu(_causal_conv(prev_tail, cur, cw_ref, cb_ref[:, wcol:wcol + LANES], wcol, LANES, k_taps))

        for blk in range(D_SSD_XBC // LANES):
            act_ref[r0:r0 + L, blk * LANES:(blk + 1) * LANES] = conv_block(
                C_XBC + blk * LANES, xcw_ref, xcb_ref, blk * LANES, SSD_CONV)
        for blk in range(2 * D_MLSTM // LANES):
            act_ref[r0:r0 + L, D_SSD_XBC + blk * LANES:D_SSD_XBC + (blk + 1) * LANES] = conv_block(
                C_QK + blk * LANES, qcw_ref, qcb_ref, blk * LANES, MLSTM_CONV)
        yield

        pos = tau_f + (tile * TT + r0 + 1).astype(F32)
        pooled_blocks = []
        for b in range(2):
            cs_ = slice(b * LANES, (b + 1) * LANES)
            u_cur = hp_ref[r0:r0 + L, cs_]
            lvl = u_cur
            sums = []
            for li, sh in enumerate((1, 2, 4, 8)):
                if li == 0:
                    prev_tail = tail(hp_ref, hph_ref, n_hph, r0, sh * SUBLANES, cs_)
                else:
                    prev_tail = tail(ps_ref.at[li - 1], psh_ref.at[li - 1], n_psh, r0, sh * SUBLANES, cs_)
                ext = _ext_rows(prev_tail, lvl[L - sh * SUBLANES:L])
                lvl = lvl + _shifted(ext, lvl, sh)
                sums.append(lvl)
                if (li, b) in ps_carried:
                    ps_ref[li, r0:r0 + L, cs_] = lvl
                if b == 0 and li == 1:
                    break
            wsum = jnp.where(lo_half, sums[0], sums[1]) if b == 0 else jnp.where(lo_half, sums[2], sums[3])
            pooled_blocks.append((wsum / jnp.minimum(pos, win_blk[b]) - u_cur).astype(BF16))
        mix_ref[r0:r0 + L, 0:D_POOL] = (
            (_dot(jnp.concatenate(pooled_blocks, axis=1), poolw_ref[...]) + poolb_ref[...]) * pools_ref[...])
        yield

        gb = rest_ref[r0:r0 + L, R_G:R_G + LANES] + gbias_ref[...]
        sp_term = jnp.log(1.0 + jnp.exp(-jnp.abs(gb)))
        dt = jnp.maximum(gb, 0.0) + sp_term
        log_f = jnp.minimum(gb, 0.0) - sp_term
        is_dt = lane < G_I
        is_f = (lane >= G_F) & (lane < G_F + MLSTM_HEADS)
        v_cum = jnp.where(is_dt, dt * a_row, jnp.where(is_f, log_f, 0.0))
        hi, mid, lo = _split3(v_cum)
        cs = _dot(tril, hi) + _dot(tril, mid) + _dot(tril, lo)
        u_gate = jnp.where(is_dt, dt, gb)
        cs_t = cs.T
        ug_t = u_gate.T
        cs_last = cs[L - 1:L, :]
        e_col = jnp.exp(cs)
        w_col = jnp.exp(cs_last - cs) * dt
        e_last = jnp.exp(cs_last)
        yield

        for g in range(SSD_GROUPS):
            b_g = act_ref[r0:r0 + L, D_SSD + g * SSD_STATE:D_SSD + (g + 1) * SSD_STATE].astype(BF16)
            c_g = act_ref[r0:r0 + L, D_SSD + (SSD_GROUPS + g) * SSD_STATE:
                          D_SSD + (SSD_GROUPS + g + 1) * SSD_STATE].astype(BF16)
            s_g = _dot_nt(c_g, b_g)
            state_g = sstate_ref[g]
            y_off = _dot(c_g, state_g.astype(BF16))
            xd_blocks = []
            cd_blocks = []
            for pr in range(2):
                h_even = 4 * g + 2 * pr
                col = h_even * SSD_HEAD_DIM
                xs = act_ref[r0:r0 + L, col:col + LANES]
                xs_b = xs.astype(BF16)
                m_pair = []
                for hh in range(2):
                    hd = h_even + hh
                    seg = jnp.where(causal, cs[:, hd:hd + 1] - cs_t[hd:hd + 1, :], neg_inf)
                    m_pair.append((s_g * (jnp.exp(seg) * ug_t[hd:hd + 1, :])).astype(BF16))
                yd = _dot(jnp.concatenate(m_pair, axis=0), xs_b)
                y_diag = jnp.where(lo_half, yd[0:L], yd[L:2 * L])
                e_exp = _pair_expand(e_col, h_even, (L, LANES), lo_half)
                w_exp = _pair_expand(w_col, h_even, (L, LANES), lo_half)
                y = (y_diag + y_off[:, pr * LANES:(pr + 1) * LANES] * e_exp
                     + xs * dskip_ref[:, col:col + LANES])
                z = rest_ref[r0:r0 + L, R_Z + col:R_Z + col + LANES]
                mix_ref[r0:r0 + L, D_POOL + col:D_POOL + col + LANES] = y * _silu(z)
                xd_blocks.append((xs * w_exp).astype(BF16))
                cd_blocks.append(_pair_expand(e_last, h_even, (1, LANES), lane_row < 64))
            xd_g = jnp.concatenate(xd_blocks, axis=1)
            cd_g = jnp.concatenate(cd_blocks, axis=1)
            new_states = lax.dot_general(b_g, xd_g, (((0,), (0,)), ((), ())),
                                         preferred_element_type=F32)
            sstate_ref[g] = state_g * cd_g + new_states
            yield
        y_all = mix_ref[r0:r0 + L, D_POOL:D_POOL + D_SSD]
        ms_y = jnp.mean(y_all * y_all, axis=-1, keepdims=True)
        mix_ref[r0:r0 + L, D_POOL:D_POOL + D_SSD] = y_all * lax.rsqrt(ms_y + EPS) * snorm_ref[...]

        for pr in range(MLSTM_HEADS // 2):
            qcol = D_SSD_XBC + pr * LANES
            kcol = D_SSD_XBC + D_MLSTM + pr * LANES
            q_b = act_ref[r0:r0 + L, qcol:qcol + LANES] * (MLSTM_HEAD_DIM ** -0.5)
            k_t = act_ref[r0:r0 + L, kcol:kcol + LANES].T
            k_tb = k_t.astype(BF16)
            v_b = rest_ref[r0:r0 + L, R_V + pr * LANES:R_V + (pr + 1) * LANES]
            o_b = rest_ref[r0:r0 + L, R_O + pr * LANES:R_O + (pr + 1) * LANES]
            qms = [jnp.where(lo_half, q_b, 0.0), jnp.where(lo_half, 0.0, q_b)]
            s_pair = _dot(jnp.concatenate([q.astype(BF16) for q in qms], axis=0), k_tb)
            v_ones = jnp.concatenate([v_b, jnp.where(lane == 0, 1.0, 0.0)], axis=1).astype(BF16)
            state = mstate_ref[pr]
            lhs = []
            g_ts = []
            kw_rows = []
            decay_old = []
            decay_new = []
            for hh in range(2):
                hd = 2 * pr + hh
                s = s_pair[hh * L:(hh + 1) * L]
                b_row = cs_t[G_F + hd:G_F + hd + 1, :]
                r_row = ug_t[G_I + hd:G_I + hd + 1, :] - b_row
                b_last = jnp.sum(jnp.where(lane_row == L - 1, b_row, 0.0), axis=-1, keepdims=True)
                al_row = b_last + r_row
                m_loc = jnp.max(al_row, axis=-1, keepdims=True)
                prev_m_row = mm_ref[hd:hd + 1, :]
                rmask = jnp.where(causal, r_row, neg_inf)
                g_t = jnp.maximum(jnp.broadcast_to(jnp.max(rmask, axis=-1, keepdims=True), (L, LANES)),
                                  prev_m_row)
                p = (s * jnp.exp(rmask - g_t)).astype(BF16)
                q_inter = (qms[hh] * jnp.exp(prev_m_row - g_t)).astype(BF16)
                lhs.append(jnp.concatenate([p, q_inter], axis=1))
                g_ts.append(g_t)
                kw_rows.append(jnp.exp(al_row - m_loc))
                m_new = jnp.maximum(b_last + prev_m_row, m_loc)
                decay_old.append(jnp.exp(b_last + prev_m_row - m_new))
                decay_new.append(jnp.exp(m_loc - m_new))
                mm_ref[hd:hd + 1, :] = m_new
            rhs = jnp.concatenate([v_ones, state.astype(BF16)], axis=0)
            res_pair = _dot(jnp.concatenate(lhs, axis=0), rhs)
            hv = []
            for hh in range(2):
                hd = 2 * pr + hh
                res = res_pair[hh * L:(hh + 1) * L]
                den = jnp.maximum(jnp.abs(_bcast_lane(res, LANES, (L, LANES))),
                                  jnp.exp(-(_bcast_lane(cs, G_F + hd, (L, LANES)) + g_ts[hh])))
                hv.append(res[:, 0:LANES] / den)
            top = row < 64
            ktw = (k_t * jnp.where(top, kw_rows[0], kw_rows[1])).astype(BF16)
            c_loc = _dot(ktw, v_ones)
            own = top == lo_half
            d_old = jnp.where(top, decay_old[0], decay_old[1])
            d_new = jnp.where(top, decay_new[0], decay_new[1])
            mstate_ref[pr] = jnp.concatenate(
                [d_old * state[:, 0:LANES] + d_new * jnp.where(own, c_loc[:, 0:LANES], 0.0),
                 d_old * state[:, LANES:2 * LANES] + d_new * c_loc[:, LANES:2 * LANES]], axis=1)
            hcat = jax.nn.sigmoid(o_b) * jnp.where(lo_half, hv[0], hv[1])
            sq = hcat * hcat
            ss_lo = jnp.sum(jnp.where(lo_half, sq, 0.0), axis=-1, keepdims=True)
            ss_hi = jnp.sum(jnp.where(lo_half, 0.0, sq), axis=-1, keepdims=True)
            inv = jnp.where(lo_half, lax.rsqrt(ss_lo * (1.0 / MLSTM_HEAD_DIM) + EPS),
                            lax.rsqrt(ss_hi * (1.0 / MLSTM_HEAD_DIM) + EPS))
            mcol = D_POOL + D_SSD + pr * LANES
            mix_ref[r0:r0 + L, mcol:mcol + LANES] = hcat * inv * mnorm_ref[:, pr * LANES:(pr + 1) * LANES]
            if pr + 1 < MLSTM_HEADS // 2:
                yield

        def out_piece():
            o = _dot(mix_ref[r0:r0 + L, :].astype(BF16), wout_ref[...])
            ms_o = jnp.mean(o * o, axis=-1, keepdims=True)
            out_ref[0, r0:r0 + L, :] = x_rows(r0) + o * lax.rsqrt(ms_o + EPS) * postn_ref[...]

        pending.insert(0, out_piece)
        yield

    n_stages = 3 + SSD_GROUPS + MLSTM_HEADS // 2
    gens = [chunk_stages(c) for c in range(TT // L)]
    for _ in range(n_stages):
        for gen in gens:
            next(gen)
            between()
    while pending:
        pending.pop(0)()

    hph_ref[...] = hp_ref[TT - n_hph:TT, :]
    for lv, b in ps_carried:
        cs_ = slice(b * LANES, (b + 1) * LANES)
        psh_ref[lv, :, cs_] = ps_ref[lv, TT - n_psh:TT, cs_]


def ffn_kernel(layer, to_time_order, x_ref, wup_hbm, wdn_hbm, nw_ref, cw_ref, cb_ref, postn_ref, out_ref,
               wup_ref, wdn_ref, stage_ref, stage_dn_ref, stage_sem, halo_ref, a_ref, *out_scratch):
    TT = x_ref.shape[1]
    FT = FFN_FT
    L = CHUNK
    n_tail = (FFN_CONV - 1) * SUBLANES
    i = pl.program_id(1)

    @pl.when((pl.program_id(0) == 0) & (i == 0))
    def _():
        _stage_weights(_col_blocks(wup_hbm.at[layer], 0, wup_ref, 0, 2 * D_FF, STAGE_COLS),
                       stage_ref, stage_sem)
        _stage_weights([(wdn_hbm.at[layer, r:r + STAGE_COLS, :], wdn_ref.at[r:r + STAGE_COLS, :])
                        for r in range(0, D_FF, STAGE_COLS)], stage_dn_ref, stage_sem)

    @pl.when(i == 0)
    def _():
        halo_ref[...] = jnp.zeros(halo_ref.shape, F32)

    x = x_ref[0]
    ms = jnp.mean(x * x, axis=-1, keepdims=True)
    h = (x * lax.rsqrt(ms + EPS) * nw_ref[...]).astype(BF16)

    def conv_cols(col):
        u = _dot(h, wup_ref[:, col:col + FT])
        outs = []
        for c in range(TT // L):
            cur = u[c * L:(c + 1) * L]
            prev_tail = halo_ref[:, col:col + FT] if c == 0 else u[c * L - n_tail:c * L]
            outs.append(_causal_conv(prev_tail, cur, cw_ref, cb_ref[:, col:col + FT], col, FT, FFN_CONV))
        halo_ref[:, col:col + FT] = u[TT - n_tail:TT]
        return outs

    for j in range(D_FF // FT):
        gts = conv_cols(j * FT)
        vals = conv_cols(D_FF + j * FT)
        for c in range(TT // L):
            gt = gts[c]
            gelu = 0.5 * gt * (1.0 + jnp.tanh(math.sqrt(2.0 / math.pi) * (gt + 0.044715 * (gt * gt * gt))))
            a_ref[c * L:(c + 1) * L, j * FT:(j + 1) * FT] = (gelu * vals[c]).astype(BF16)

    f = _dot(a_ref[...], wdn_ref[...])
    ms_f = jnp.mean(f * f, axis=-1, keepdims=True)
    res = x_ref[0] + f * lax.rsqrt(ms_f + EPS) * postn_ref[...]
    if not to_time_order:
        out_ref[0] = res
        return

    obuf_ref, out_sem = out_scratch
    n_c = TT // L
    b = pl.program_id(0)
    step = b * pl.num_programs(1) + i
    n_steps = pl.num_programs(0) * pl.num_programs(1)
    slot = step % 2

    def out_copies(sl):
        return [pltpu.make_async_copy(obuf_ref.at[sl, :, :, s, :], out_ref.at[b, pl.ds(i * n_c, n_c), s],
                                      out_sem.at[sl]) for s in range(SUBLANES)]

    @pl.when(step >= 2)
    def _():
        for cp in out_copies(slot):
            cp.wait()

    obuf_ref[slot] = res.reshape(n_c, VROWS, SUBLANES, res.shape[1])
    for cp in out_copies(slot):
        cp.start()

    @pl.when(step == n_steps - 1)
    def _():
        for cp in out_copies(slot):
            cp.wait()

    @pl.when((step == n_steps - 1) & (n_steps >= 2))
    def _():
        for cp in out_copies(1 - slot):
            cp.wait()


def _const_spec(shape):
    nd = len(shape)
    return pl.BlockSpec(shape, lambda b, i: (0,) * nd, pipeline_mode=pl.Buffered(1))


def _mix_layer(layer, from_time_order, x, wout, win, nw, gbias, alog, xcw, xcb, qcw, qcb, poolw, poolb, pools,
               dskip, snorm, mnorm, postn):
    B, T, D = x.shape
    TT = MIX_TT
    weights = (wout,)
    consts = (win, nw, gbias, alog, xcw, xcb, qcw, qcb, poolw, poolb, pools, dskip, snorm, mnorm, postn)
    n_t = T // TT
    cur_spec = pl.BlockSpec((1, TT, D), lambda b, i: (b, jnp.minimum(i, n_t - 1), 0))
    prev_spec = pl.BlockSpec((1, TT, D), lambda b, i: (b, jnp.maximum(i - 1, 0), 0))
    max_conv_tail = (max(SSD_CONV, MLSTM_CONV) - 1) * SUBLANES
    max_pool_tail = (POOL_WINDOWS[-1] // 2) * SUBLANES
    if from_time_order:
        x_in = x.reshape(B, T // CHUNK, SUBLANES, VROWS, D)
        x_specs = [pl.BlockSpec(memory_space=pltpu.HBM)] * 2
        in_scratch = [pltpu.VMEM((3, TT // CHUNK, VROWS, SUBLANES, D), F32),
                      pltpu.SemaphoreType.DMA((3,))]
    else:
        x_in, x_specs, in_scratch = x, [cur_spec, prev_spec], []
    return pl.pallas_call(
        functools.partial(mix_kernel, layer, from_time_order, n_t),
        grid=(B, n_t + 1),
        in_specs=(x_specs + [pl.BlockSpec(memory_space=pltpu.HBM) for _ in weights]
                  + [_const_spec(c.shape) for c in consts]),
        out_specs=prev_spec,
        out_shape=jax.ShapeDtypeStruct(x.shape, x.dtype),
        scratch_shapes=[
            pltpu.VMEM((D_MODEL, N_IN_COLS), BF16),
            pltpu.VMEM((D_MODEL, D_MODEL), BF16),
            pltpu.VMEM((STAGE_SLOTS, D_MODEL, STAGE_COLS), F32),
            pltpu.SemaphoreType.DMA((STAGE_SLOTS,)),
            pltpu.VMEM((TT, D_MODEL), BF16),
            pltpu.VMEM((2, TT, N_HALO_COLS), F32),
            pltpu.VMEM((2, TT, N_REST_COLS), F32),
            pltpu.VMEM((max_conv_tail, N_HALO_COLS), F32),
            pltpu.VMEM((3, max_pool_tail, D_POOL), F32),
            pltpu.VMEM((TT, D_SSD_XBC + 2 * D_MLSTM), F32),
            pltpu.VMEM((TT, D_MODEL), F32),
            pltpu.VMEM((3, TT, D_POOL), F32),
            pltpu.VMEM((SSD_GROUPS, SSD_STATE, 4 * SSD_HEAD_DIM), F32),
            pltpu.VMEM((MLSTM_HEADS // 2, LANES, 2 * LANES), F32),
            pltpu.VMEM((SUBLANES, LANES), F32),
        ] + in_scratch,
        compiler_params=pltpu.CompilerParams(
            dimension_semantics=("arbitrary", "arbitrary"), vmem_limit_bytes=VMEM_LIMIT),
        name="mix_layer",
    )(x_in, x_in, *weights, *consts)


def _ffn_layer(layer, to_time_order, x, wup, wdn, nw, cw, cb, postn):
    B, T, D = x.shape
    TT = FFN_TT
    weights = (wup, wdn)
    consts = (nw, cw, cb, postn)
    x_spec = pl.BlockSpec((1, TT, D), lambda b, i: (b, i, 0))
    if to_time_order:
        out_spec = pl.BlockSpec(memory_space=pltpu.HBM)
        out_shape = jax.ShapeDtypeStruct((B, T // CHUNK, SUBLANES, VROWS, D), x.dtype)
        out_scratch = [pltpu.VMEM((2, TT // CHUNK, VROWS, SUBLANES, D), F32),
                       pltpu.SemaphoreType.DMA((2,))]
    else:
        out_spec, out_shape, out_scratch = x_spec, jax.ShapeDtypeStruct(x.shape, x.dtype), []
    return pl.pallas_call(
        functools.partial(ffn_kernel, layer, to_time_order),
        grid=(B, T // TT),
        in_specs=([x_spec] + [pl.BlockSpec(memory_space=pltpu.HBM) for _ in weights]
                  + [_const_spec(c.shape) for c in consts]),
        out_specs=out_spec,
        out_shape=out_shape,
        scratch_shapes=[
            pltpu.VMEM((D_MODEL, 2 * D_FF), BF16),
            pltpu.VMEM((D_FF, D_MODEL), BF16),
            pltpu.VMEM((STAGE_SLOTS, D_MODEL, STAGE_COLS), F32),
            pltpu.VMEM((STAGE_SLOTS, STAGE_COLS, D_MODEL), F32),
            pltpu.SemaphoreType.DMA((STAGE_SLOTS,)),
            pltpu.VMEM(((FFN_CONV - 1) * SUBLANES, 2 * D_FF), F32),
            pltpu.VMEM((TT, D_FF), BF16),
        ] + out_scratch,
        compiler_params=pltpu.CompilerParams(
            dimension_semantics=("arbitrary", "arbitrary"), vmem_limit_bytes=VMEM_LIMIT),
        name="ffn_layer",
    )(x, *weights, *consts).reshape(B, T, D)


def _row(v):
    return v.reshape(1, -1).astype(F32)


def _pad_lanes(v):
    return jnp.pad(v.astype(F32), (0, LANES - v.shape[0])).reshape(1, LANES)


def _prep_pool_w(w):
    out = jnp.zeros((D_POOL, D_POOL), F32)
    for g in range(len(POOL_WINDOWS)):
        s = g * POOL_GROUP_DIM
        out = lax.dynamic_update_slice(out, w[g].astype(F32), (s, s))
    return out.astype(BF16)


def kernel(x, pre_mix_norm, w_in, pool_w, pool_b, pool_scale, ssd_conv_w, ssd_conv_b, ssd_dt_bias, ssd_a_log, ssd_d, ssd_norm, mlstm_conv_w, mlstm_conv_b, mlstm_i_bias, mlstm_f_bias, mlstm_norm, w_out, post_mix_norm, pre_ffn_norm, ffn_w_up, ffn_conv_w, ffn_conv_b, ffn_w_down, post_ffn_norm):
    depth = w_in.shape[0]
    for l in range(depth):
        gbias = _pad_lanes(jnp.concatenate([ssd_dt_bias[l], mlstm_i_bias[l], mlstm_f_bias[l]]))
        x = _mix_layer(
            l, l == 0, x, w_out, w_in[l].astype(BF16), _row(pre_mix_norm[l]), gbias, _pad_lanes(ssd_a_log[l]),
            ssd_conv_w[l].astype(F32), _row(ssd_conv_b[l]), mlstm_conv_w[l].astype(F32), _row(mlstm_conv_b[l]),
            _prep_pool_w(pool_w[l]), _row(pool_b[l]), _row(pool_scale[l]),
            _row(jnp.repeat(ssd_d[l], SSD_HEAD_DIM)), _row(ssd_norm[l]), _row(mlstm_norm[l]),
            _row(post_mix_norm[l]))
        x = _ffn_layer(
            l, l == depth - 1, x, ffn_w_up, ffn_w_down, _row(pre_ffn_norm[l]), ffn_conv_w[l].astype(F32),
            _row(ffn_conv_b[l]), _row(post_ffn_norm[l]))
    return x
```

```python
import functools
import math

import jax
import jax.numpy as jnp
from jax import lax
from jax.experimental import pallas as pl
from jax.experimental.pallas import tpu as pltpu

F32 = jnp.float32
BF16 = jnp.bfloat16

D_MODEL = 1024
EPS = 1e-6

D_POOL = 256
POOL_GROUP_DIM = 64
POOL_WINDOWS = (2, 4, 8, 16)

D_SSD = 512
SSD_HEADS = 8
SSD_HEAD_DIM = 64
SSD_GROUPS = 2
SSD_STATE = 128
SSD_CONV = 4
D_SSD_XBC = D_SSD + 2 * SSD_GROUPS * SSD_STATE

D_MLSTM = 256
MLSTM_HEADS = 4
MLSTM_HEAD_DIM = 64
MLSTM_CONV = 4

D_FF = 2816
FFN_CONV = 3

CHUNK = 128
LANES = 128
SUBLANES = 8
VROWS = CHUNK // SUBLANES

C_POOL = 0
C_XBC = C_POOL + D_POOL
C_QK = C_XBC + D_SSD_XBC
N_HALO_COLS = C_QK + 2 * D_MLSTM
R_Z = 0
R_V = R_Z + D_SSD
R_O = R_V + D_MLSTM
R_G = R_O + D_MLSTM
N_REST_COLS = R_G + LANES
N_IN_COLS = N_HALO_COLS + N_REST_COLS
G_DT = 0
G_I = SSD_HEADS
G_F = G_I + MLSTM_HEADS

MIX_TT = 256
PROJ_COLS = 256
STAGE_COLS = 256
STAGE_SLOTS = 4
FFN_TT = 512
FFN_FT = 256
VMEM_LIMIT = 56 * 1024 * 1024


def _dot(a, b):
    return jnp.dot(a, b, preferred_element_type=F32)


def _silu(x):
    return x * jax.nn.sigmoid(x)


def _split3(a):
    hi = a.astype(BF16)
    r = a - hi.astype(F32)
    mid = r.astype(BF16)
    lo = (r - mid.astype(F32)).astype(BF16)
    return hi, mid, lo


def _bcast_lane(a, j, shape):
    return jnp.broadcast_to(a[:, j:j + 1], shape)


def _pair_expand(a, h_even, shape, lo_half):
    return jnp.where(lo_half, _bcast_lane(a, h_even, shape), _bcast_lane(a, h_even + 1, shape))


def _stage_weights(blocks, stage_ref, sem_ref):
    n_slots = stage_ref.shape[0]

    def copy(n):
        src, dst = blocks[n]
        rows, cols = src.shape
        return pltpu.make_async_copy(src, stage_ref.at[n % n_slots, 0:rows, 0:cols], sem_ref.at[n % n_slots])

    for n in range(min(n_slots - 1, len(blocks))):
        copy(n).start()
    for n, (src, dst) in enumerate(blocks):
        if n + n_slots - 1 < len(blocks):
            copy(n + n_slots - 1).start()
        copy(n).wait()
        rows, cols = src.shape
        dst[...] = stage_ref[n % n_slots, 0:rows, 0:cols].astype(BF16)


def _col_blocks(src_ref, src0, dst_ref, dst0, ncols, step):
    return [(src_ref.at[:, src0 + k:src0 + min(k + step, ncols)],
             dst_ref.at[:, dst0 + k:dst0 + min(k + step, ncols)]) for k in range(0, ncols, step)]


def _tau(p):
    return (p % SUBLANES) * VROWS + p // SUBLANES


def _ext_rows(prev_tail, cur_tail):
    n = cur_tail.shape[0] // SUBLANES
    sub0 = lax.broadcasted_iota(jnp.int32, (SUBLANES, cur_tail.shape[1]), 0) == 0
    out = []
    for j in range(n):
        sl = slice(j * SUBLANES, (j + 1) * SUBLANES)
        out.append(jnp.where(sub0, pltpu.roll(prev_tail[sl], 1, 0), pltpu.roll(cur_tail[sl], 1, 0)))
    return out


def _shifted(ext, cur, k):
    if k == 0:
        return cur
    return jnp.concatenate(ext[len(ext) - k:] + [cur[0:CHUNK - SUBLANES * k]], axis=0)


def _causal_conv(prev_tail, cur, w_ref, b_row, wcol, ncols, k_taps):
    n = k_taps - 1
    ext = _ext_rows(prev_tail, cur[CHUNK - n * SUBLANES:CHUNK])
    acc = b_row
    for k in range(k_taps):
        acc = acc + _shifted(ext, cur, n - k) * w_ref[k:k + 1, wcol:wcol + ncols]
    return acc


def _relayout_w_in(wraw_ref, win_ref):
    c_z = D_POOL
    c_xbc = c_z + D_SSD
    c_dt = c_xbc + D_SSD_XBC
    n_tail = SSD_HEADS + 4 * D_MLSTM + 2 * MLSTM_HEADS
    o_qk = SSD_HEADS
    o_v = o_qk + 2 * D_MLSTM
    o_if = o_v + 2 * D_MLSTM
    for r in range(0, D_MODEL, CHUNK):
        rows = slice(r, r + CHUNK)
        win_ref[rows, C_POOL:C_POOL + D_POOL] = wraw_ref[rows, 0:D_POOL]
        win_ref[rows, C_XBC:C_XBC + D_SSD_XBC] = wraw_ref[rows, c_xbc:c_dt]
        win_ref[rows, N_HALO_COLS + R_Z:N_HALO_COLS + R_Z + D_SSD] = wraw_ref[rows, c_z:c_xbc]
        tail = wraw_ref[rows, c_dt:c_dt + n_tail]
        win_ref[rows, C_QK:C_QK + 2 * D_MLSTM] = tail[:, o_qk:o_v]
        win_ref[rows, N_HALO_COLS + R_V:N_HALO_COLS + R_V + 2 * D_MLSTM] = tail[:, o_v:o_if]
        pad = jnp.zeros((CHUNK, LANES - SSD_HEADS - 2 * MLSTM_HEADS), tail.dtype)
        win_ref[rows, N_HALO_COLS + R_G:N_HALO_COLS + R_G + LANES] = jnp.concatenate(
            [tail[:, 0:o_qk], tail[:, o_if:n_tail], pad], axis=1)


def mix_kernel(layer, from_time_order, n_tiles, xc_ref, xp_ref, wout_hbm, wraw_ref, nw_ref, gbias_ref, alog_ref,
               xcw_ref, xcb_ref, qcw_ref, qcb_ref,
               poolw_ref, poolb_ref, pools_ref, dskip_ref, snorm_ref, mnorm_ref, postn_ref,
               out_ref, win_ref, wout_ref, stage_ref, stage_sem,
               h_ref, hp_ref, rest_ref, hph_ref, psh_ref, act_ref, mix_ref, ps_ref,
               sstate_ref, mstate_ref, mm_ref, *in_scratch):
    TT = hp_ref.shape[1]
    L = CHUNK
    n_c = TT // L
    b = pl.program_id(0)
    i = pl.program_id(1)
    n_t = n_tiles

    if from_time_order:
        xbuf_ref, in_sem = in_scratch
        n_slots = xbuf_ref.shape[0]

        def in_copies(tile):
            sl = tile % n_slots
            return [pltpu.make_async_copy(xc_ref.at[b, pl.ds(tile * n_c, n_c), s], xbuf_ref.at[sl, :, :, s, :],
                                          in_sem.at[sl]) for s in range(SUBLANES)]

        @pl.when(i == 0)
        def _():
            for cp in in_copies(0):
                cp.start()

        @pl.when(i + 1 < n_t)
        def _():
            for cp in in_copies(i + 1):
                cp.start()

        @pl.when(i < n_t)
        def _():
            for cp in in_copies(i):
                cp.wait()

        slot_cur = jnp.minimum(i, n_t - 1) % n_slots
        slot_prev = jnp.maximum(i - 1, 0) % n_slots

        def x_cur():
            return xbuf_ref[slot_cur].reshape(TT, xbuf_ref.shape[-1])

        def x_rows(r0):
            return xbuf_ref[slot_prev, r0 // L].reshape(L, xbuf_ref.shape[-1])
    else:
        def x_cur():
            return xc_ref[0]

        def x_rows(r0):
            return xp_ref[0, r0:r0 + L, :]

    @pl.when((pl.program_id(0) == 0) & (i == 0))
    def _():
        _relayout_w_in(wraw_ref, win_ref)
        _stage_weights(_col_blocks(wout_hbm.at[layer], 0, wout_ref, 0, D_MODEL, STAGE_COLS),
                       stage_ref, stage_sem)

    @pl.when(i == 0)
    def _():
        hph_ref[...] = jnp.zeros(hph_ref.shape, F32)
        psh_ref[...] = jnp.zeros(psh_ref.shape, F32)
        sstate_ref[...] = jnp.zeros(sstate_ref.shape, F32)
        mstate_ref[...] = jnp.zeros(mstate_ref.shape, F32)
        mm_ref[...] = jnp.zeros(mm_ref.shape, F32)

    def step(slot_proj, slot_mix):
        pending = []
        if slot_proj is not None:
            x = x_cur()
            ms = jnp.mean(x * x, axis=-1, keepdims=True)
            h_ref[...] = (x * lax.rsqrt(ms + EPS) * nw_ref[...]).astype(BF16)

            def proj_piece(dst_ref, c0, c1, w0):
                def piece():
                    dst_ref[slot_proj, :, c0:c1] = _dot(h_ref[...], win_ref[:, w0 + c0:w0 + c1])
                return piece

            pending += [proj_piece(hp_ref, c0, min(c0 + PROJ_COLS, N_HALO_COLS), 0)
                        for c0 in range(0, N_HALO_COLS, PROJ_COLS)]
            pending += [proj_piece(rest_ref, c0, min(c0 + PROJ_COLS, N_REST_COLS), N_HALO_COLS)
                        for c0 in range(0, N_REST_COLS, PROJ_COLS)]
        if slot_mix is not None:
            _mixers(i - 1, x_rows, hp_ref.at[slot_mix], rest_ref.at[slot_mix], gbias_ref, alog_ref,
                    xcw_ref, xcb_ref, qcw_ref, qcb_ref, poolw_ref, poolb_ref, pools_ref, dskip_ref, snorm_ref,
                    mnorm_ref, wout_ref, postn_ref, out_ref, hph_ref, psh_ref, act_ref, mix_ref, ps_ref,
                    sstate_ref, mstate_ref, mm_ref, pending)
        while pending:
            pending.pop(0)()

    last_slot = (n_tiles - 1) % 2

    @pl.when(i == 0)
    def _():
        step(0, None)

    @pl.when((i > 0) & (i < n_tiles) & (i % 2 == 0))
    def _():
        step(0, 1)

    @pl.when((i > 0) & (i < n_tiles) & (i % 2 == 1))
    def _():
        step(1, 0)

    @pl.when(i == n_tiles)
    def _():
        step(None, last_slot)


def _mixers(tile, x_rows, hp_ref, rest_ref, gbias_ref, alog_ref, xcw_ref, xcb_ref, qcw_ref, qcb_ref,
            poolw_ref, poolb_ref, pools_ref, dskip_ref, snorm_ref, mnorm_ref, wout_ref, postn_ref,
            out_ref, hph_ref, psh_ref, act_ref, mix_ref, ps_ref, sstate_ref, mstate_ref, mm_ref, pending):
    def between():
        if pending:
            pending.pop(0)()

    TT = hp_ref.shape[0]
    L = CHUNK
    n_hph = hph_ref.shape[0]
    n_psh = psh_ref.shape[1]
    ps_carried = ((0, 0), (0, 1), (1, 1), (2, 1))

    def tail(cur_ref, halo, n_halo, r0, n_rows, cols):
        if r0 == 0:
            return halo[n_halo - n_rows:n_halo, cols]
        return cur_ref[r0 - n_rows:r0, cols]

    lane = lax.broadcasted_iota(jnp.int32, (L, LANES), 1)
    row = lax.broadcasted_iota(jnp.int32, (L, LANES), 0)
    lo_half = lane < 64
    tau_row = _tau(row)
    causal = _tau(lane) <= tau_row
    tril = jnp.where(causal, 1.0, 0.0).astype(BF16)
    lane_row = lax.broadcasted_iota(jnp.int32, (1, LANES), 1)
    a_row = -jnp.exp(alog_ref[...])
    neg_inf = -jnp.inf
    win_blk = [jnp.where(lo_half, float(POOL_WINDOWS[2 * b]), float(POOL_WINDOWS[2 * b + 1])) for b in range(2)]
    tau_f = tau_row.astype(F32)

    def chunk_stages(c):
        r0 = c * L

        def conv_block(col, cw_ref, cb_ref, wcol, k_taps):
            cols = slice(col, col + LANES)
            cur = hp_ref[r0:r0 + L, cols]
            prev_tail = tail(hp_ref, hph_ref, n_hph, r0, (k_taps - 1) * SUBLANES, cols)
            return _silu(_causal_conv(prev_tail, cur, cw_ref, cb_ref[:, wcol:wcol + LANES], wcol, LANES, k_taps))

        for blk in range(D_SSD_XBC // LANES):
            act_ref[r0:r0 + L, blk * LANES:(blk + 1) * LANES] = conv_block(
                C_XBC + blk * LANES, xcw_ref, xcb_ref, blk * LANES, SSD_CONV)
        for blk in range(2 * D_MLSTM // LANES):
            act_ref[r0:r0 + L, D_SSD_XBC + blk * LANES:D_SSD_XBC + (blk + 1) * LANES] = conv_block(
                C_QK + blk * LANES, qcw_ref, qcb_ref, blk * LANES, MLSTM_CONV)
        yield

        pos = tau_f + (tile * TT + r0 + 1).astype(F32)
        pooled_blocks = []
        for b in range(2):
            cs_ = slice(b * LANES, (b + 1) * LANES)
            u_cur = hp_ref[r0:r0 + L, cs_]
            lvl = u_cur
            sums = []
            for li, sh in enumerate((1, 2, 4, 8)):
                if li == 0:
                    prev_tail = tail(hp_ref, hph_ref, n_hph, r0, sh * SUBLANES, cs_)
                else:
                    prev_tail = tail(ps_ref.at[li - 1], psh_ref.at[li - 1], n_psh, r0, sh * SUBLANES, cs_)
                ext = _ext_rows(prev_tail, lvl[L - sh * SUBLANES:L])
                lvl = lvl + _shifted(ext, lvl, sh)
                sums.append(lvl)
                if (li, b) in ps_carried:
                    ps_ref[li, r0:r0 + L, cs_] = lvl
                if b == 0 and li == 1:
                    break
            wsum = jnp.where(lo_half, sums[0], sums[1]) if b == 0 else jnp.where(lo_half, sums[2], sums[3])
            pooled_blocks.append((wsum / jnp.minimum(pos, win_blk[b]) - u_cur).astype(BF16))
        mix_ref[r0:r0 + L, 0:D_POOL] = (
            (_dot(jnp.concatenate(pooled_blocks, axis=1), poolw_ref[...]) + poolb_ref[...]) * pools_ref[...])
        yield

        gb = rest_ref[r0:r0 + L, R_G:R_G + LANES] + gbias_ref[...]
        sp_term = jnp.log(1.0 + jnp.exp(-jnp.abs(gb)))
        dt = jnp.maximum(gb, 0.0) + sp_term
        log_f = jnp.minimum(gb, 0.0) - sp_term
        is_dt = lane < G_I
        is_f = (lane >= G_F) & (lane < G_F + MLSTM_HEADS)
        v_cum = jnp.where(is_dt, dt * a_row, jnp.where(is_f, log_f, 0.0))
        hi, mid, lo = _split3(v_cum)
        cs3 = _dot(tril, jnp.concatenate([hi, mid, lo], axis=1))
        cs = cs3[:, 0:LANES] + cs3[:, LANES:2 * LANES] + cs3[:, 2 * LANES:3 * LANES]
        u_gate = jnp.where(is_dt, dt, gb)
        cs_t = cs.T
        ug_t = u_gate.T
        cs_last = cs[L - 1:L, :]
        e_col = jnp.exp(cs)
        w_col = jnp.exp(cs_last - cs) * dt
        e_last = jnp.exp(cs_last)
        yield

        for g in range(SSD_GROUPS):
            b_t = act_ref[r0:r0 + L, D_SSD + g * SSD_STATE:D_SSD + (g + 1) * SSD_STATE].T.astype(BF16)
            c_g = act_ref[r0:r0 + L, D_SSD + (SSD_GROUPS + g) * SSD_STATE:
                          D_SSD + (SSD_GROUPS + g + 1) * SSD_STATE].astype(BF16)
            state_g = sstate_ref[g]
            sc = _dot(c_g, jnp.concatenate([b_t, state_g.astype(BF16)], axis=1))
            s_g = sc[:, 0:L]
            y_off = sc[:, L:L + 4 * SSD_HEAD_DIM]
            xd_blocks = []
            cd_blocks = []
            for pr in range(2):
                h_even = 4 * g + 2 * pr
                col = h_even * SSD_HEAD_DIM
                xs = act_ref[r0:r0 + L, col:col + LANES]
                xs_b = xs.astype(BF16)
                m_pair = []
                for hh in range(2):
                    hd = h_even + hh
                    seg = jnp.where(causal, cs[:, hd:hd + 1] - cs_t[hd:hd + 1, :], neg_inf)
                    m_pair.append((s_g * (jnp.exp(seg) * ug_t[hd:hd + 1, :])).astype(BF16))
                yd = _dot(jnp.concatenate(m_pair, axis=0), xs_b)
                y_diag = jnp.where(lo_half, yd[0:L], yd[L:2 * L])
                e_exp = _pair_expand(e_col, h_even, (L, LANES), lo_half)
                w_exp = _pair_expand(w_col, h_even, (L, LANES), lo_half)
                y = (y_diag + y_off[:, pr * LANES:(pr + 1) * LANES] * e_exp
                     + xs * dskip_ref[:, col:col + LANES])
                z = rest_ref[r0:r0 + L, R_Z + col:R_Z + col + LANES]
                mix_ref[r0:r0 + L, D_POOL + col:D_POOL + col + LANES] = y * _silu(z)
                xd_blocks.append((xs * w_exp).astype(BF16))
                cd_blocks.append(_pair_expand(e_last, h_even, (1, LANES), lane_row < 64))
            xd_g = jnp.concatenate(xd_blocks, axis=1)
            cd_g = jnp.concatenate(cd_blocks, axis=1)
            new_states = _dot(b_t, xd_g)
            sstate_ref[g] = state_g * cd_g + new_states
            yield
        y_all = mix_ref[r0:r0 + L, D_POOL:D_POOL + D_SSD]
        ms_y = jnp.mean(y_all * y_all, axis=-1, keepdims=True)
        mix_ref[r0:r0 + L, D_POOL:D_POOL + D_SSD] = y_all * lax.rsqrt(ms_y + EPS) * snorm_ref[...]

        for pr in range(MLSTM_HEADS // 2):
            qcol = D_SSD_XBC + pr * LANES
            kcol = D_SSD_XBC + D_MLSTM + pr * LANES
            q_b = act_ref[r0:r0 + L, qcol:qcol + LANES] * (MLSTM_HEAD_DIM ** -0.5)
            k_t = act_ref[r0:r0 + L, kcol:kcol + LANES].T
            k_tb = k_t.astype(BF16)
            v_b = rest_ref[r0:r0 + L, R_V + pr * LANES:R_V + (pr + 1) * LANES]
            o_b = rest_ref[r0:r0 + L, R_O + pr * LANES:R_O + (pr + 1) * LANES]
            qms = [jnp.where(lo_half, q_b, 0.0), jnp.where(lo_half, 0.0, q_b)]
            s_pair = _dot(jnp.concatenate([q.astype(BF16) for q in qms], axis=0), k_tb)
            v_ones = jnp.concatenate([v_b, jnp.where(lane == 0, 1.0, 0.0)], axis=1).astype(BF16)
            state = mstate_ref[pr]
            lhs = []
            g_ts = []
            kw_rows = []
            decay_old = []
            decay_new = []
            for hh in range(2):
                hd = 2 * pr + hh
                s = s_pair[hh * L:(hh + 1) * L]
                b_row = cs_t[G_F + hd:G_F + hd + 1, :]
                r_row = ug_t[G_I + hd:G_I + hd + 1, :] - b_row
                b_last = jnp.sum(jnp.where(lane_row == L - 1, b_row, 0.0), axis=-1, keepdims=True)
                al_row = b_last + r_row
                m_loc = jnp.max(al_row, axis=-1, keepdims=True)
                prev_m_row = mm_ref[hd:hd + 1, :]
                rmask = jnp.where(causal, r_row, neg_inf)
                g_t = jnp.maximum(jnp.broadcast_to(jnp.max(rmask, axis=-1, keepdims=True), (L, LANES)),
                                  prev_m_row)
                p = (s * jnp.exp(rmask - g_t)).astype(BF16)
                q_inter = (qms[hh] * jnp.exp(prev_m_row - g_t)).astype(BF16)
                lhs.append(jnp.concatenate([p, q_inter], axis=1))
                g_ts.append(g_t)
                kw_rows.append(jnp.exp(al_row - m_loc))
                m_new = jnp.maximum(b_last + prev_m_row, m_loc)
                decay_old.append(jnp.exp(b_last + prev_m_row - m_new))
                decay_new.append(jnp.exp(m_loc - m_new))
                mm_ref[hd:hd + 1, :] = m_new
            rhs = jnp.concatenate([v_ones, state.astype(BF16)], axis=0)
            res_pair = _dot(jnp.concatenate(lhs, axis=0), rhs)
            hv = []
            for hh in range(2):
                hd = 2 * pr + hh
                res = res_pair[hh * L:(hh + 1) * L]
                den = jnp.maximum(jnp.abs(_bcast_lane(res, LANES, (L, LANES))),
                                  jnp.exp(-(_bcast_lane(cs, G_F + hd, (L, LANES)) + g_ts[hh])))
                hv.append(res[:, 0:LANES] / den)
            top = row < 64
            ktw = (k_t * jnp.where(top, kw_rows[0], kw_rows[1])).astype(BF16)
            c_loc = _dot(ktw, v_ones)
            own = top == lo_half
            d_old = jnp.where(top, decay_old[0], decay_old[1])
            d_new = jnp.where(top, decay_new[0], decay_new[1])
            mstate_ref[pr] = jnp.concatenate(
                [d_old * state[:, 0:LANES] + d_new * jnp.where(own, c_loc[:, 0:LANES], 0.0),
                 d_old * state[:, LANES:2 * LANES] + d_new * c_loc[:, LANES:2 * LANES]], axis=1)
            hcat = jax.nn.sigmoid(o_b) * jnp.where(lo_half, hv[0], hv[1])
            sq = hcat * hcat
            ss_lo = jnp.sum(jnp.where(lo_half, sq, 0.0), axis=-1, keepdims=True)
            ss_hi = jnp.sum(jnp.where(lo_half, 0.0, sq), axis=-1, keepdims=True)
            inv = jnp.where(lo_half, lax.rsqrt(ss_lo * (1.0 / MLSTM_HEAD_DIM) + EPS),
                            lax.rsqrt(ss_hi * (1.0 / MLSTM_HEAD_DIM) + EPS))
            mcol = D_POOL + D_SSD + pr * LANES
            mix_ref[r0:r0 + L, mcol:mcol + LANES] = hcat * inv * mnorm_ref[:, pr * LANES:(pr + 1) * LANES]
            if pr + 1 < MLSTM_HEADS // 2:
                yield

        def out_piece():
            o = _dot(mix_ref[r0:r0 + L, :].astype(BF16), wout_ref[...])
            ms_o = jnp.mean(o * o, axis=-1, keepdims=True)
            out_ref[0, r0:r0 + L, :] = x_rows(r0) + o * lax.rsqrt(ms_o + EPS) * postn_ref[...]

        pending.insert(0, out_piece)
        yield

    n_stages = 3 + SSD_GROUPS + MLSTM_HEADS // 2
    gens = [chunk_stages(c) for c in range(TT // L)]
    for _ in range(n_stages):
        for gen in gens:
            next(gen)
            between()
    while pending:
        pending.pop(0)()

    hph_ref[...] = hp_ref[TT - n_hph:TT, :]
    for lv, b in ps_carried:
        cs_ = slice(b * LANES, (b + 1) * LANES)
        psh_ref[lv, :, cs_] = ps_ref[lv, TT - n_psh:TT, cs_]


def ffn_kernel(layer, to_time_order, x_ref, wup_hbm, wdn_hbm, nw_ref, cw_ref, cb_ref, postn_ref, out_ref,
               wup_ref, wdn_ref, stage_ref, stage_dn_ref, stage_sem, halo_ref, a_ref, *out_scratch):
    TT = x_ref.shape[1]
    FT = FFN_FT
    L = CHUNK
    n_tail = (FFN_CONV - 1) * SUBLANES
    i = pl.program_id(1)

    @pl.when((pl.program_id(0) == 0) & (i == 0))
    def _():
        _stage_weights(_col_blocks(wup_hbm.at[layer], 0, wup_ref, 0, 2 * D_FF, STAGE_COLS),
                       stage_ref, stage_sem)
        _stage_weights([(wdn_hbm.at[layer, r:r + STAGE_COLS, :], wdn_ref.at[r:r + STAGE_COLS, :])
                        for r in range(0, D_FF, STAGE_COLS)], stage_dn_ref, stage_sem)

    @pl.when(i == 0)
    def _():
        halo_ref[...] = jnp.zeros(halo_ref.shape, F32)

    x = x_ref[0]
    ms = jnp.mean(x * x, axis=-1, keepdims=True)
    h = (x * lax.rsqrt(ms + EPS) * nw_ref[...]).astype(BF16)

    def conv_cols(col):
        u = _dot(h, wup_ref[:, col:col + FT])
        outs = []
        for c in range(TT // L):
            cur = u[c * L:(c + 1) * L]
            prev_tail = halo_ref[:, col:col + FT] if c == 0 else u[c * L - n_tail:c * L]
            outs.append(_causal_conv(prev_tail, cur, cw_ref, cb_ref[:, col:col + FT], col, FT, FFN_CONV))
        halo_ref[:, col:col + FT] = u[TT - n_tail:TT]
        return outs

    for j in range(D_FF // FT):
        gts = conv_cols(j * FT)
        vals = conv_cols(D_FF + j * FT)
        for c in range(TT // L):
            gt = gts[c]
            gelu = 0.5 * gt * (1.0 + jnp.tanh(math.sqrt(2.0 / math.pi) * (gt + 0.044715 * (gt * gt * gt))))
            a_ref[c * L:(c + 1) * L, j * FT:(j + 1) * FT] = (gelu * vals[c]).astype(BF16)

    f = _dot(a_ref[...], wdn_ref[...])
    ms_f = jnp.mean(f * f, axis=-1, keepdims=True)
    res = x_ref[0] + f * lax.rsqrt(ms_f + EPS) * postn_ref[...]
    if not to_time_order:
        out_ref[0] = res
        return

    obuf_ref, out_sem = out_scratch
    n_c = TT // L
    b = pl.program_id(0)
    step = b * pl.num_programs(1) + i
    n_steps = pl.num_programs(0) * pl.num_programs(1)
    slot = step % 2

    def out_copies(sl):
        return [pltpu.make_async_copy(obuf_ref.at[sl, :, :, s, :], out_ref.at[b, pl.ds(i * n_c, n_c), s],
                                      out_sem.at[sl]) for s in range(SUBLANES)]

    @pl.when(step >= 2)
    def _():
        for cp in out_copies(slot):
            cp.wait()

    obuf_ref[slot] = res.reshape(n_c, VROWS, SUBLANES, res.shape[1])
    for cp in out_copies(slot):
        cp.start()

    @pl.when(step == n_steps - 1)
    def _():
        for cp in out_copies(slot):
            cp.wait()

    @pl.when((step == n_steps - 1) & (n_steps >= 2))
    def _():
        for cp in out_copies(1 - slot):
            cp.wait()


def _const_spec(shape):
    nd = len(shape)
    return pl.BlockSpec(shape, lambda b, i: (0,) * nd, pipeline_mode=pl.Buffered(1))


def _mix_layer(layer, from_time_order, x, wout, win, nw, gbias, alog, xcw, xcb, qcw, qcb, poolw, poolb, pools,
               dskip, snorm, mnorm, postn):
    B, T, D = x.shape
    TT = MIX_TT
    weights = (wout,)
    consts = (win, nw, gbias, alog, xcw, xcb, qcw, qcb, poolw, poolb, pools, dskip, snorm, mnorm, postn)
    n_t = T // TT
    cur_spec = pl.BlockSpec((1, TT, D), lambda b, i: (b, jnp.minimum(i, n_t - 1), 0))
    prev_spec = pl.BlockSpec((1, TT, D), lambda b, i: (b, jnp.maximum(i - 1, 0), 0))
    max_conv_tail = (max(SSD_CONV, MLSTM_CONV) - 1) * SUBLANES
    max_pool_tail = (POOL_WINDOWS[-1] // 2) * SUBLANES
    if from_time_order:
        x_in = x.reshape(B, T // CHUNK, SUBLANES, VROWS, D)
        x_specs = [pl.BlockSpec(memory_space=pltpu.HBM)] * 2
        in_scratch = [pltpu.VMEM((3, TT // CHUNK, VROWS, SUBLANES, D), F32),
                      pltpu.SemaphoreType.DMA((3,))]
    else:
        x_in, x_specs, in_scratch = x, [cur_spec, prev_spec], []
    return pl.pallas_call(
        functools.partial(mix_kernel, layer, from_time_order, n_t),
        grid=(B, n_t + 1),
        in_specs=(x_specs + [pl.BlockSpec(memory_space=pltpu.HBM) for _ in weights]
                  + [_const_spec(c.shape) for c in consts]),
        out_specs=prev_spec,
        out_shape=jax.ShapeDtypeStruct(x.shape, x.dtype),
        scratch_shapes=[
            pltpu.VMEM((D_MODEL, N_IN_COLS), BF16),
            pltpu.VMEM((D_MODEL, D_MODEL), BF16),
            pltpu.VMEM((STAGE_SLOTS, D_MODEL, STAGE_COLS), F32),
            pltpu.SemaphoreType.DMA((STAGE_SLOTS,)),
            pltpu.VMEM((TT, D_MODEL), BF16),
            pltpu.VMEM((2, TT, N_HALO_COLS), F32),
            pltpu.VMEM((2, TT, N_REST_COLS), F32),
            pltpu.VMEM((max_conv_tail, N_HALO_COLS), F32),
            pltpu.VMEM((3, max_pool_tail, D_POOL), F32),
            pltpu.VMEM((TT, D_SSD_XBC + 2 * D_MLSTM), F32),
            pltpu.VMEM((TT, D_MODEL), F32),
            pltpu.VMEM((3, TT, D_POOL), F32),
            pltpu.VMEM((SSD_GROUPS, SSD_STATE, 4 * SSD_HEAD_DIM), F32),
            pltpu.VMEM((MLSTM_HEADS // 2, LANES, 2 * LANES), F32),
            pltpu.VMEM((SUBLANES, LANES), F32),
        ] + in_scratch,
        compiler_params=pltpu.CompilerParams(
            dimension_semantics=("arbitrary", "arbitrary"), vmem_limit_bytes=VMEM_LIMIT),
        name="mix_layer",
    )(x_in, x_in, *weights, *consts)


def _ffn_layer(layer, to_time_order, x, wup, wdn, nw, cw, cb, postn):
    B, T, D = x.shape
    TT = FFN_TT
    weights = (wup, wdn)
    consts = (nw, cw, cb, postn)
    x_spec = pl.BlockSpec((1, TT, D), lambda b, i: (b, i, 0))
    if to_time_order:
        out_spec = pl.BlockSpec(memory_space=pltpu.HBM)
        out_shape = jax.ShapeDtypeStruct((B, T // CHUNK, SUBLANES, VROWS, D), x.dtype)
        out_scratch = [pltpu.VMEM((2, TT // CHUNK, VROWS, SUBLANES, D), F32),
                       pltpu.SemaphoreType.DMA((2,))]
    else:
        out_spec, out_shape, out_scratch = x_spec, jax.ShapeDtypeStruct(x.shape, x.dtype), []
    return pl.pallas_call(
        functools.partial(ffn_kernel, layer, to_time_order),
        grid=(B, T // TT),
        in_specs=([x_spec] + [pl.BlockSpec(memory_space=pltpu.HBM) for _ in weights]
                  + [_const_spec(c.shape) for c in consts]),
        out_specs=out_spec,
        out_shape=out_shape,
        scratch_shapes=[
            pltpu.VMEM((D_MODEL, 2 * D_FF), BF16),
            pltpu.VMEM((D_FF, D_MODEL), BF16),
            pltpu.VMEM((STAGE_SLOTS, D_MODEL, STAGE_COLS), F32),
            pltpu.VMEM((STAGE_SLOTS, STAGE_COLS, D_MODEL), F32),
            pltpu.SemaphoreType.DMA((STAGE_SLOTS,)),
            pltpu.VMEM(((FFN_CONV - 1) * SUBLANES, 2 * D_FF), F32),
            pltpu.VMEM((TT, D_FF), BF16),
        ] + out_scratch,
        compiler_params=pltpu.CompilerParams(
            dimension_semantics=("arbitrary", "arbitrary"), vmem_limit_bytes=VMEM_LIMIT),
        name="ffn_layer",
    )(x, *weights, *consts).reshape(B, T, D)


def _row(v):
    return v.reshape(1, -1).astype(F32)


def _pad_lanes(v):
    return jnp.pad(v.astype(F32), (0, LANES - v.shape[0])).reshape(1, LANES)


def _prep_pool_w(w):
    out = jnp.zeros((D_POOL, D_POOL), F32)
    for g in range(len(POOL_WINDOWS)):
        s = g * POOL_GROUP_DIM
        out = lax.dynamic_update_slice(out, w[g].astype(F32), (s, s))
    return out.astype(BF16)


def kernel(x, pre_mix_norm, w_in, pool_w, pool_b, pool_scale, ssd_conv_w, ssd_conv_b, ssd_dt_bias, ssd_a_log, ssd_d, ssd_norm, mlstm_conv_w, mlstm_conv_b, mlstm_i_bias, mlstm_f_bias, mlstm_norm, w_out, post_mix_norm, pre_ffn_norm, ffn_w_up, ffn_conv_w, ffn_conv_b, ffn_w_down, post_ffn_norm):
    depth = w_in.shape[0]
    for l in range(depth):
        gbias = _pad_lanes(jnp.concatenate([ssd_dt_bias[l], mlstm_i_bias[l], mlstm_f_bias[l]]))
        x = _mix_layer(
            l, l == 0, x, w_out, w_in[l].astype(BF16), _row(pre_mix_norm[l]), gbias, _pad_lanes(ssd_a_log[l]),
            ssd_conv_w[l].astype(F32), _row(ssd_conv_b[l]), mlstm_conv_w[l].astype(F32), _row(mlstm_conv_b[l]),
            _prep_pool_w(pool_w[l]), _row(pool_b[l]), _row(pool_scale[l]),
            _row(jnp.repeat(ssd_d[l], SSD_HEAD_DIM)), _row(ssd_norm[l]), _row(mlstm_norm[l]),
            _row(post_mix_norm[l]))
        x = _ffn_layer(
            l, l == depth - 1, x, ffn_w_up, ffn_w_down, _row(pre_ffn_norm[l]), ffn_conv_w[l].astype(F32),
            _row(ffn_conv_b[l]), _row(post_ffn_norm[l]))
    return x
```

```python
import functools
import math

import jax
import jax.numpy as jnp
from jax import lax
from jax.experimental import pallas as pl
from jax.experimental.pallas import tpu as pltpu

F32 = jnp.float32
BF16 = jnp.bfloat16

D_MODEL = 1024
EPS = 1e-6

D_POOL = 256
POOL_GROUP_DIM = 64
POOL_WINDOWS = (2, 4, 8, 16)

D_SSD = 512
SSD_HEADS = 8
SSD_HEAD_DIM = 64
SSD_GROUPS = 2
SSD_STATE = 128
SSD_CONV = 4
D_SSD_XBC = D_SSD + 2 * SSD_GROUPS * SSD_STATE

D_MLSTM = 256
MLSTM_HEADS = 4
MLSTM_HEAD_DIM = 64
MLSTM_CONV = 4

D_FF = 2816
FFN_CONV = 3

CHUNK = 128
LANES = 128
SUBLANES = 8
VROWS = CHUNK // SUBLANES

C_POOL = 0
C_XBC = C_POOL + D_POOL
C_QK = C_XBC + D_SSD_XBC
N_HALO_COLS = C_QK + 2 * D_MLSTM
R_Z = 0
R_V = R_Z + D_SSD
R_O = R_V + D_MLSTM
R_G = R_O + D_MLSTM
N_REST_COLS = R_G + LANES
N_IN_COLS = N_HALO_COLS + N_REST_COLS
G_DT = 0
G_I = SSD_HEADS
G_F = G_I + MLSTM_HEADS

MIX_TT = 256
PROJ_COLS = 256
STAGE_COLS = 256
STAGE_SLOTS = 4
FFN_TT = 512
FFN_FT = 256
VMEM_LIMIT = 56 * 1024 * 1024


def _dot(a, b):
    return jnp.dot(a, b, preferred_element_type=F32)


def _silu(x):
    return x * jax.nn.sigmoid(x)


def _split3(a):
    hi = a.astype(BF16)
    r = a - hi.astype(F32)
    mid = r.astype(BF16)
    lo = (r - mid.astype(F32)).astype(BF16)
    return hi, mid, lo


def _bcast_lane(a, j, shape):
    return jnp.broadcast_to(a[:, j:j + 1], shape)


def _pair_expand(a, h_even, shape, lo_half):
    return jnp.where(lo_half, _bcast_lane(a, h_even, shape), _bcast_lane(a, h_even + 1, shape))


def _stage_weights(blocks, stage_ref, sem_ref):
    n_slots = stage_ref.shape[0]

    def copy(n):
        src, dst = blocks[n]
        rows, cols = src.shape
        return pltpu.make_async_copy(src, stage_ref.at[n % n_slots, 0:rows, 0:cols], sem_ref.at[n % n_slots])

    for n in range(min(n_slots - 1, len(blocks))):
        copy(n).start()
    for n, (src, dst) in enumerate(blocks):
        if n + n_slots - 1 < len(blocks):
            copy(n + n_slots - 1).start()
        copy(n).wait()
        rows, cols = src.shape
        dst[...] = stage_ref[n % n_slots, 0:rows, 0:cols].astype(BF16)


def _col_blocks(src_ref, src0, dst_ref, dst0, ncols, step):
    return [(src_ref.at[:, src0 + k:src0 + min(k + step, ncols)],
             dst_ref.at[:, dst0 + k:dst0 + min(k + step, ncols)]) for k in range(0, ncols, step)]


def _tau(p):
    return (p % SUBLANES) * VROWS + p // SUBLANES


def _ext_rows(prev_tail, cur_tail):
    n = cur_tail.shape[0] // SUBLANES
    sub0 = lax.broadcasted_iota(jnp.int32, (SUBLANES, cur_tail.shape[1]), 0) == 0
    out = []
    for j in range(n):
        sl = slice(j * SUBLANES, (j + 1) * SUBLANES)
        out.append(jnp.where(sub0, pltpu.roll(prev_tail[sl], 1, 0), pltpu.roll(cur_tail[sl], 1, 0)))
    return out


def _shifted(ext, cur, k):
    if k == 0:
        return cur
    return jnp.concatenate(ext[len(ext) - k:] + [cur[0:CHUNK - SUBLANES * k]], axis=0)


def _causal_conv(prev_tail, cur, w_ref, b_row, wcol, ncols, k_taps):
    n = k_taps - 1
    ext = _ext_rows(prev_tail, cur[CHUNK - n * SUBLANES:CHUNK])
    acc = b_row
    for k in range(k_taps):
        acc = acc + _shifted(ext, cur, n - k) * w_ref[k:k + 1, wcol:wcol + ncols]
    return acc


def _relayout_w_in(wraw_ref, win_ref):
    c_z = D_POOL
    c_xbc = c_z + D_SSD
    c_dt = c_xbc + D_SSD_XBC
    n_tail = SSD_HEADS + 4 * D_MLSTM + 2 * MLSTM_HEADS
    o_qk = SSD_HEADS
    o_v = o_qk + 2 * D_MLSTM
    o_if = o_v + 2 * D_MLSTM
    for r in range(0, D_MODEL, CHUNK):
        rows = slice(r, r + CHUNK)
        win_ref[rows, C_POOL:C_POOL + D_POOL] = wraw_ref[rows, 0:D_POOL]
        win_ref[rows, C_XBC:C_XBC + D_SSD_XBC] = wraw_ref[rows, c_xbc:c_dt]
        win_ref[rows, N_HALO_COLS + R_Z:N_HALO_COLS + R_Z + D_SSD] = wraw_ref[rows, c_z:c_xbc]
        tail = wraw_ref[rows, c_dt:c_dt + n_tail]
        win_ref[rows, C_QK:C_QK + 2 * D_MLSTM] = tail[:, o_qk:o_v]
        win_ref[rows, N_HALO_COLS + R_V:N_HALO_COLS + R_V + 2 * D_MLSTM] = tail[:, o_v:o_if]
        pad = jnp.zeros((CHUNK, LANES - SSD_HEADS - 2 * MLSTM_HEADS), tail.dtype)
        win_ref[rows, N_HALO_COLS + R_G:N_HALO_COLS + R_G + LANES] = jnp.concatenate(
            [tail[:, 0:o_qk], tail[:, o_if:n_tail], pad], axis=1)


def mix_kernel(layer, from_time_order, n_tiles, xc_ref, xp_ref, wout_hbm, wraw_ref, nw_ref, gbias_ref, alog_ref,
               xcw_ref, xcb_ref, qcw_ref, qcb_ref,
               poolw_ref, poolb_ref, pools_ref, dskip_ref, snorm_ref, mnorm_ref, postn_ref,
               out_ref, win_ref, wout_ref, stage_ref, stage_sem,
               h_ref, hp_ref, rest_ref, hph_ref, psh_ref, act_ref, mix_ref, ps_ref,
               sstate_ref, mstate_ref, mm_ref, *in_scratch):
    TT = hp_ref.shape[1]
    L = CHUNK
    n_c = TT // L
    b = pl.program_id(0)
    i = pl.program_id(1)
    n_t = n_tiles

    if from_time_order:
        xbuf_ref, in_sem = in_scratch
        n_slots = xbuf_ref.shape[0]

        def in_copies(tile):
            sl = tile % n_slots
            return [pltpu.make_async_copy(xc_ref.at[b, pl.ds(tile * n_c, n_c), s], xbuf_ref.at[sl, :, :, s, :],
                                          in_sem.at[sl]) for s in range(SUBLANES)]

        @pl.when(i == 0)
        def _():
            for cp in in_copies(0):
                cp.start()

        @pl.when(i + 1 < n_t)
        def _():
            for cp in in_copies(i + 1):
                cp.start()

        @pl.when(i < n_t)
        def _():
            for cp in in_copies(i):
                cp.wait()

        slot_cur = jnp.minimum(i, n_t - 1) % n_slots
        slot_prev = jnp.maximum(i - 1, 0) % n_slots

        def x_cur():
            return xbuf_ref[slot_cur].reshape(TT, xbuf_ref.shape[-1])

        def x_rows(r0):
            return xbuf_ref[slot_prev, r0 // L].reshape(L, xbuf_ref.shape[-1])
    else:
        def x_cur():
            return xc_ref[0]

        def x_rows(r0):
            return xp_ref[0, r0:r0 + L, :]

    @pl.when((pl.program_id(0) == 0) & (i == 0))
    def _():
        _relayout_w_in(wraw_ref, win_ref)
        _stage_weights(_col_blocks(wout_hbm.at[layer], 0, wout_ref, 0, D_MODEL, STAGE_COLS),
                       stage_ref, stage_sem)

    @pl.when(i == 0)
    def _():
        hph_ref[...] = jnp.zeros(hph_ref.shape, F32)
        psh_ref[...] = jnp.zeros(psh_ref.shape, F32)
        sstate_ref[...] = jnp.zeros(sstate_ref.shape, F32)
        mstate_ref[...] = jnp.zeros(mstate_ref.shape, F32)
        mm_ref[...] = jnp.zeros(mm_ref.shape, F32)

    def step(slot_proj, slot_mix):
        pending = []
        if slot_proj is not None:
            x = x_cur()
            ms = jnp.mean(x * x, axis=-1, keepdims=True)
            h_ref[...] = (x * lax.rsqrt(ms + EPS) * nw_ref[...]).astype(BF16)

            def proj_piece(dst_ref, c0, c1, w0):
                def piece():
                    dst_ref[slot_proj, :, c0:c1] = _dot(h_ref[...], win_ref[:, w0 + c0:w0 + c1])
                return piece

            pending += [proj_piece(hp_ref, c0, min(c0 + PROJ_COLS, N_HALO_COLS), 0)
                        for c0 in range(0, N_HALO_COLS, PROJ_COLS)]
            pending += [proj_piece(rest_ref, c0, min(c0 + PROJ_COLS, N_REST_COLS), N_HALO_COLS)
                        for c0 in range(0, N_REST_COLS, PROJ_COLS)]
        if slot_mix is not None:
            _mixers(i - 1, x_rows, hp_ref.at[slot_mix], rest_ref.at[slot_mix], gbias_ref, alog_ref,
                    xcw_ref, xcb_ref, qcw_ref, qcb_ref, poolw_ref, poolb_ref, pools_ref, dskip_ref, snorm_ref,
                    mnorm_ref, wout_ref, postn_ref, out_ref, hph_ref, psh_ref, act_ref, mix_ref, ps_ref,
                    sstate_ref, mstate_ref, mm_ref, pending)
        while pending:
            pending.pop(0)()

    last_slot = (n_tiles - 1) % 2

    @pl.when(i == 0)
    def _():
        step(0, None)

    @pl.when((i > 0) & (i < n_tiles) & (i % 2 == 0))
    def _():
        step(0, 1)

    @pl.when((i > 0) & (i < n_tiles) & (i % 2 == 1))
    def _():
        step(1, 0)

    @pl.when(i == n_tiles)
    def _():
        step(None, last_slot)


def _mixers(tile, x_rows, hp_ref, rest_ref, gbias_ref, alog_ref, xcw_ref, xcb_ref, qcw_ref, qcb_ref,
            poolw_ref, poolb_ref, pools_ref, dskip_ref, snorm_ref, mnorm_ref, wout_ref, postn_ref,
            out_ref, hph_ref, psh_ref, act_ref, mix_ref, ps_ref, sstate_ref, mstate_ref, mm_ref, pending):
    def between():
        if pending:
            pending.pop(0)()

    TT = hp_ref.shape[0]
    L = CHUNK
    n_hph = hph_ref.shape[0]
    n_psh = psh_ref.shape[1]
    ps_carried = ((0, 0), (0, 1), (1, 1), (2, 1))

    def tail(cur_ref, halo, n_halo, r0, n_rows, cols):
        if r0 == 0:
            return halo[n_halo - n_rows:n_halo, cols]
        return cur_ref[r0 - n_rows:r0, cols]

    lane = lax.broadcasted_iota(jnp.int32, (L, LANES), 1)
    row = lax.broadcasted_iota(jnp.int32, (L, LANES), 0)
    lo_half = lane < 64
    tau_row = _tau(row)
    causal = _tau(lane) <= tau_row
    tril = jnp.where(causal, 1.0, 0.0).astype(BF16)
    lane_row = lax.broadcasted_iota(jnp.int32, (1, LANES), 1)
    a_row = -jnp.exp(alog_ref[...])
    neg_inf = -jnp.inf
    win_blk = [jnp.where(lo_half, float(POOL_WINDOWS[2 * b]), float(POOL_WINDOWS[2 * b + 1])) for b in range(2)]
    tau_f = tau_row.astype(F32)

    def chunk_stages(c):
        r0 = c * L

        def conv_block(col, cw_ref, cb_ref, wcol, k_taps):
            cols = slice(col, col + LANES)
            cur = hp_ref[r0:r0 + L, cols]
            prev_tail = tail(hp_ref, hph_ref, n_hph, r0, (k_taps - 1) * SUBLANES, cols)
            return _silu(_causal_conv(prev_tail, cur, cw_ref, cb_ref[:, wcol:wcol + LANES], wcol, LANES, k_taps))

        for blk in range(D_SSD_XBC // LANES):
            act_ref[r0:r0 + L, blk * LANES:(blk + 1) * LANES] = conv_block(
                C_XBC + blk * LANES, xcw_ref, xcb_ref, blk * LANES, SSD_CONV)
        for blk in range(2 * D_MLSTM // LANES):
            act_ref[r0:r0 + L, D_SSD_XBC + blk * LANES:D_SSD_XBC + (blk + 1) * LANES] = conv_block(
                C_QK + blk * LANES, qcw_ref, qcb_ref, blk * LANES, MLSTM_CONV)
        yield

        pos = tau_f + (tile * TT + r0 + 1).astype(F32)
        pooled_blocks = []
        for b in range(2):
            cs_ = slice(b * LANES, (b + 1) * LANES)
            u_cur = hp_ref[r0:r0 + L, cs_]
            lvl = u_cur
            sums = []
            for li, sh in enumerate((1, 2, 4, 8)):
                if li == 0:
                    prev_tail = tail(hp_ref, hph_ref, n_hph, r0, sh * SUBLANES, cs_)
                else:
                    prev_tail = tail(ps_ref.at[li - 1], psh_ref.at[li - 1], n_psh, r0, sh * SUBLANES, cs_)
                ext = _ext_rows(prev_tail, lvl[L - sh * SUBLANES:L])
                lvl = lvl + _shifted(ext, lvl, sh)
                sums.append(lvl)
                if (li, b) in ps_carried:
                    ps_ref[li, r0:r0 + L, cs_] = lvl
                if b == 0 and li == 1:
                    break
            wsum = jnp.where(lo_half, sums[0], sums[1]) if b == 0 else jnp.where(lo_half, sums[2], sums[3])
            pooled_blocks.append((wsum / jnp.minimum(pos, win_blk[b]) - u_cur).astype(BF16))
        mix_ref[r0:r0 + L, 0:D_POOL] = (
            (_dot(jnp.concatenate(pooled_blocks, axis=1), poolw_ref[...]) + poolb_ref[...]) * pools_ref[...])
        yield

        gb = rest_ref[r0:r0 + L, R_G:R_G + LANES] + gbias_ref[...]
        sp_term = jnp.log(1.0 + jnp.exp(-jnp.abs(gb)))
        dt = jnp.maximum(gb, 0.0) + sp_term
        log_f = jnp.minimum(gb, 0.0) - sp_term
        is_dt = lane < G_I
        is_f = (lane >= G_F) & (lane < G_F + MLSTM_HEADS)
        v_cum = jnp.where(is_dt, dt * a_row, jnp.where(is_f, log_f, 0.0))
        hi, mid, lo = _split3(v_cum)
        cs3 = _dot(tril, jnp.concatenate([hi, mid, lo], axis=1))
        cs = cs3[:, 0:LANES] + cs3[:, LANES:2 * LANES] + cs3[:, 2 * LANES:3 * LANES]
        u_gate = jnp.where(is_dt, dt, gb)
        cs_t = cs.T
        ug_t = u_gate.T
        cs_last = cs[L - 1:L, :]
        e_col = jnp.exp(cs)
        w_col = jnp.exp(cs_last - cs) * dt
        e_last = jnp.exp(cs_last)
        yield

        for g in range(SSD_GROUPS):
            b_t = act_ref[r0:r0 + L, D_SSD + g * SSD_STATE:D_SSD + (g + 1) * SSD_STATE].T.astype(BF16)
            c_g = act_ref[r0:r0 + L, D_SSD + (SSD_GROUPS + g) * SSD_STATE:
                          D_SSD + (SSD_GROUPS + g + 1) * SSD_STATE].astype(BF16)
            state_g = sstate_ref[g]
            sc = _dot(c_g, jnp.concatenate([b_t, state_g.astype(BF16)], axis=1))
            s_g = sc[:, 0:L]
            y_off = sc[:, L:L + 4 * SSD_HEAD_DIM]
            xd_blocks = []
            cd_blocks = []
            for pr in range(2):
                h_even = 4 * g + 2 * pr
                col = h_even * SSD_HEAD_DIM
                xs = act_ref[r0:r0 + L, col:col + LANES]
                xs_b = xs.astype(BF16)
                m_pair = []
                for hh in range(2):
                    hd = h_even + hh
                    seg = jnp.where(causal, cs[:, hd:hd + 1] - cs_t[hd:hd + 1, :], neg_inf)
                    m_pair.append((s_g * (jnp.exp(seg) * ug_t[hd:hd + 1, :])).astype(BF16))
                yd = _dot(jnp.concatenate(m_pair, axis=0), xs_b)
                y_diag = jnp.where(lo_half, yd[0:L], yd[L:2 * L])
                e_exp = _pair_expand(e_col, h_even, (L, LANES), lo_half)
                w_exp = _pair_expand(w_col, h_even, (L, LANES), lo_half)
                y = (y_diag + y_off[:, pr * LANES:(pr + 1) * LANES] * e_exp
                     + xs * dskip_ref[:, col:col + LANES])
                z = rest_ref[r0:r0 + L, R_Z + col:R_Z + col + LANES]
                mix_ref[r0:r0 + L, D_POOL + col:D_POOL + col + LANES] = y * _silu(z)
                xd_blocks.append((xs * w_exp).astype(BF16))
                cd_blocks.append(_pair_expand(e_last, h_even, (1, LANES), lane_row < 64))
            xd_g = jnp.concatenate(xd_blocks, axis=1)
            cd_g = jnp.concatenate(cd_blocks, axis=1)
            new_states = _dot(b_t, xd_g)
            sstate_ref[g] = state_g * cd_g + new_states
            yield
        y_all = mix_ref[r0:r0 + L, D_POOL:D_POOL + D_SSD]
        ms_y = jnp.mean(y_all * y_all, axis=-1, keepdims=True)
        mix_ref[r0:r0 + L, D_POOL:D_POOL + D_SSD] = y_all * lax.rsqrt(ms_y + EPS) * snorm_ref[...]

        for pr in range(MLSTM_HEADS // 2):
            qcol = D_SSD_XBC + pr * LANES
            kcol = D_SSD_XBC + D_MLSTM + pr * LANES
            q_b = act_ref[r0:r0 + L, qcol:qcol + LANES] * (MLSTM_HEAD_DIM ** -0.5)
            k_t = act_ref[r0:r0 + L, kcol:kcol + LANES].T
            k_tb = k_t.astype(BF16)
            v_b = rest_ref[r0:r0 + L, R_V + pr * LANES:R_V + (pr + 1) * LANES]
            o_b = rest_ref[r0:r0 + L, R_O + pr * LANES:R_O + (pr + 1) * LANES]
            qms = [jnp.where(lo_half, q_b, 0.0), jnp.where(lo_half, 0.0, q_b)]
            s_pair = _dot(jnp.concatenate([q.astype(BF16) for q in qms], axis=0), k_tb)
            v_ones = jnp.concatenate([v_b, jnp.where(lane == 0, 1.0, 0.0)], axis=1).astype(BF16)
            state = mstate_ref[pr]
            lhs = []
            g_ts = []
            kw_rows = []
            decay_old = []
            decay_new = []
            for hh in range(2):
                hd = 2 * pr + hh
                s = s_pair[hh * L:(hh + 1) * L]
                b_row = cs_t[G_F + hd:G_F + hd + 1, :]
                r_row = ug_t[G_I + hd:G_I + hd + 1, :] - b_row
                b_last = jnp.sum(jnp.where(lane_row == L - 1, b_row, 0.0), axis=-1, keepdims=True)
                al_row = b_last + r_row
                m_loc = jnp.max(al_row, axis=-1, keepdims=True)
                prev_m_row = mm_ref[hd:hd + 1, :]
                rmask = jnp.where(causal, r_row, neg_inf)
                g_t = jnp.maximum(jnp.broadcast_to(jnp.max(rmask, axis=-1, keepdims=True), (L, LANES)),
                                  prev_m_row)
                p = (s * jnp.exp(rmask - g_t)).astype(BF16)
                q_inter = (qms[hh] * jnp.exp(prev_m_row - g_t)).astype(BF16)
                lhs.append(jnp.concatenate([p, q_inter], axis=1))
                g_ts.append(g_t)
                kw_rows.append(jnp.exp(al_row - m_loc))
                m_new = jnp.maximum(b_last + prev_m_row, m_loc)
                decay_old.append(jnp.exp(b_last + prev_m_row - m_new))
                decay_new.append(jnp.exp(m_loc - m_new))
                mm_ref[hd:hd + 1, :] = m_new
            rhs = jnp.concatenate([v_ones, state.astype(BF16)], axis=0)
            res_pair = _dot(jnp.concatenate(lhs, axis=0), rhs)
            hv = []
            for hh in range(2):
                hd = 2 * pr + hh
                res = res_pair[hh * L:(hh + 1) * L]
                den = jnp.maximum(jnp.abs(_bcast_lane(res, LANES, (L, LANES))),
                                  jnp.exp(-(_bcast_lane(cs, G_F + hd, (L, LANES)) + g_ts[hh])))
                hv.append(res[:, 0:LANES] / den)
            top = row < 64
            ktw = (k_t * jnp.where(top, kw_rows[0], kw_rows[1])).astype(BF16)
            c_loc = _dot(ktw, v_ones)
            own = top == lo_half
            d_old = jnp.where(top, decay_old[0], decay_old[1])
            d_new = jnp.where(top, decay_new[0], decay_new[1])
            mstate_ref[pr] = jnp.concatenate(
                [d_old * state[:, 0:LANES] + d_new * jnp.where(own, c_loc[:, 0:LANES], 0.0),
                 d_old * state[:, LANES:2 * LANES] + d_new * c_loc[:, LANES:2 * LANES]], axis=1)
            hcat = jax.nn.sigmoid(o_b) * jnp.where(lo_half, hv[0], hv[1])
            sq = hcat * hcat
            ss_lo = jnp.sum(jnp.where(lo_half, sq, 0.0), axis=-1, keepdims=True)
            ss_hi = jnp.sum(jnp.where(lo_half, 0.0, sq), axis=-1, keepdims=True)
            inv = jnp.where(lo_half, lax.rsqrt(ss_lo * (1.0 / MLSTM_HEAD_DIM) + EPS),
                            lax.rsqrt(ss_hi * (1.0 / MLSTM_HEAD_DIM) + EPS))
            mcol = D_POOL + D_SSD + pr * LANES
            mix_ref[r0:r0 + L, mcol:mcol + LANES] = hcat * inv * mnorm_ref[:, pr * LANES:(pr + 1) * LANES]
            if pr + 1 < MLSTM_HEADS // 2:
                yield

        yield

    n_stages = 3 + SSD_GROUPS + MLSTM_HEADS // 2
    gens = [chunk_stages(c) for c in range(TT // L)]
    for _ in range(n_stages):
        for gen in gens:
            next(gen)
            between()
    while pending:
        pending.pop(0)()

    o = _dot(mix_ref[...].astype(BF16), wout_ref[...])
    ms_o = jnp.mean(o * o, axis=-1, keepdims=True)
    o = o * lax.rsqrt(ms_o + EPS) * postn_ref[...]
    for c in range(TT // L):
        out_ref[0, c * L:(c + 1) * L, :] = x_rows(c * L) + o[c * L:(c + 1) * L]

    hph_ref[...] = hp_ref[TT - n_hph:TT, :]
    for lv, b in ps_carried:
        cs_ = slice(b * LANES, (b + 1) * LANES)
        psh_ref[lv, :, cs_] = ps_ref[lv, TT - n_psh:TT, cs_]


def ffn_kernel(layer, to_time_order, x_ref, wup_hbm, wdn_hbm, nw_ref, cw_ref, cb_ref, postn_ref, out_ref,
               wup_ref, wdn_ref, stage_ref, stage_dn_ref, stage_sem, halo_ref, a_ref, *out_scratch):
    TT = x_ref.shape[1]
    FT = FFN_FT
    L = CHUNK
    n_tail = (FFN_CONV - 1) * SUBLANES
    i = pl.program_id(1)

    @pl.when((pl.program_id(0) == 0) & (i == 0))
    def _():
        _stage_weights(_col_blocks(wup_hbm.at[layer], 0, wup_ref, 0, 2 * D_FF, STAGE_COLS),
                       stage_ref, stage_sem)
        _stage_weights([(wdn_hbm.at[layer, r:r + STAGE_COLS, :], wdn_ref.at[r:r + STAGE_COLS, :])
                        for r in range(0, D_FF, STAGE_COLS)], stage_dn_ref, stage_sem)

    @pl.when(i == 0)
    def _():
        halo_ref[...] = jnp.zeros(halo_ref.shape, F32)

    x = x_ref[0]
    ms = jnp.mean(x * x, axis=-1, keepdims=True)
    h = (x * lax.rsqrt(ms + EPS) * nw_ref[...]).astype(BF16)

    def conv_cols(col):
        u = _dot(h, wup_ref[:, col:col + FT])
        outs = []
        for c in range(TT // L):
            cur = u[c * L:(c + 1) * L]
            prev_tail = halo_ref[:, col:col + FT] if c == 0 else u[c * L - n_tail:c * L]
            outs.append(_causal_conv(prev_tail, cur, cw_ref, cb_ref[:, col:col + FT], col, FT, FFN_CONV))
        halo_ref[:, col:col + FT] = u[TT - n_tail:TT]
        return outs

    for j in range(D_FF // FT):
        gts = conv_cols(j * FT)
        vals = conv_cols(D_FF + j * FT)
        for c in range(TT // L):
            gt = gts[c]
            gelu = 0.5 * gt * (1.0 + jnp.tanh(math.sqrt(2.0 / math.pi) * (gt + 0.044715 * (gt * gt * gt))))
            a_ref[c * L:(c + 1) * L, j * FT:(j + 1) * FT] = (gelu * vals[c]).astype(BF16)

    f = _dot(a_ref[...], wdn_ref[...])
    ms_f = jnp.mean(f * f, axis=-1, keepdims=True)
    res = x_ref[0] + f * lax.rsqrt(ms_f + EPS) * postn_ref[...]
    if not to_time_order:
        out_ref[0] = res
        return

    obuf_ref, out_sem = out_scratch
    n_c = TT // L
    b = pl.program_id(0)
    step = b * pl.num_programs(1) + i
    n_steps = pl.num_programs(0) * pl.num_programs(1)
    slot = step % 2

    def out_copies(sl):
        return [pltpu.make_async_copy(obuf_ref.at[sl, :, :, s, :], out_ref.at[b, pl.ds(i * n_c, n_c), s],
                                      out_sem.at[sl]) for s in range(SUBLANES)]

    @pl.when(step >= 2)
    def _():
        for cp in out_copies(slot):
            cp.wait()

    obuf_ref[slot] = res.reshape(n_c, VROWS, SUBLANES, res.shape[1])
    for cp in out_copies(slot):
        cp.start()

    @pl.when(step == n_steps - 1)
    def _():
        for cp in out_copies(slot):
            cp.wait()

    @pl.when((step == n_steps - 1) & (n_steps >= 2))
    def _():
        for cp in out_copies(1 - slot):
            cp.wait()


def _const_spec(shape):
    nd = len(shape)
    return pl.BlockSpec(shape, lambda b, i: (0,) * nd, pipeline_mode=pl.Buffered(1))


def _mix_layer(layer, from_time_order, x, wout, win, nw, gbias, alog, xcw, xcb, qcw, qcb, poolw, poolb, pools,
               dskip, snorm, mnorm, postn):
    B, T, D = x.shape
    TT = MIX_TT
    weights = (wout,)
    consts = (win, nw, gbias, alog, xcw, xcb, qcw, qcb, poolw, poolb, pools, dskip, snorm, mnorm, postn)
    n_t = T // TT
    cur_spec = pl.BlockSpec((1, TT, D), lambda b, i: (b, jnp.minimum(i, n_t - 1), 0))
    prev_spec = pl.BlockSpec((1, TT, D), lambda b, i: (b, jnp.maximum(i - 1, 0), 0))
    max_conv_tail = (max(SSD_CONV, MLSTM_CONV) - 1) * SUBLANES
    max_pool_tail = (POOL_WINDOWS[-1] // 2) * SUBLANES
    if from_time_order:
        x_in = x.reshape(B, T // CHUNK, SUBLANES, VROWS, D)
        x_specs = [pl.BlockSpec(memory_space=pltpu.HBM)] * 2
        in_scratch = [pltpu.VMEM((3, TT // CHUNK, VROWS, SUBLANES, D), F32),
                      pltpu.SemaphoreType.DMA((3,))]
    else:
        x_in, x_specs, in_scratch = x, [cur_spec, prev_spec], []
    return pl.pallas_call(
        functools.partial(mix_kernel, layer, from_time_order, n_t),
        grid=(B, n_t + 1),
        in_specs=(x_specs + [pl.BlockSpec(memory_space=pltpu.HBM) for _ in weights]
                  + [_const_spec(c.shape) for c in consts]),
        out_specs=prev_spec,
        out_shape=jax.ShapeDtypeStruct(x.shape, x.dtype),
        scratch_shapes=[
            pltpu.VMEM((D_MODEL, N_IN_COLS), BF16),
            pltpu.VMEM((D_MODEL, D_MODEL), BF16),
            pltpu.VMEM((STAGE_SLOTS, D_MODEL, STAGE_COLS), F32),
            pltpu.SemaphoreType.DMA((STAGE_SLOTS,)),
            pltpu.VMEM((TT, D_MODEL), BF16),
            pltpu.VMEM((2, TT, N_HALO_COLS), F32),
            pltpu.VMEM((2, TT, N_REST_COLS), F32),
            pltpu.VMEM((max_conv_tail, N_HALO_COLS), F32),
            pltpu.VMEM((3, max_pool_tail, D_POOL), F32),
            pltpu.VMEM((TT, D_SSD_XBC + 2 * D_MLSTM), F32),
            pltpu.VMEM((TT, D_MODEL), F32),
            pltpu.VMEM((3, TT, D_POOL), F32),
            pltpu.VMEM((SSD_GROUPS, SSD_STATE, 4 * SSD_HEAD_DIM), F32),
            pltpu.VMEM((MLSTM_HEADS // 2, LANES, 2 * LANES), F32),
            pltpu.VMEM((SUBLANES, LANES), F32),
        ] + in_scratch,
        compiler_params=pltpu.CompilerParams(
            dimension_semantics=("arbitrary", "arbitrary"), vmem_limit_bytes=VMEM_LIMIT),
        name="mix_layer",
    )(x_in, x_in, *weights, *consts)


def _ffn_layer(layer, to_time_order, x, wup, wdn, nw, cw, cb, postn):
    B, T, D = x.shape
    TT = FFN_TT
    weights = (wup, wdn)
    consts = (nw, cw, cb, postn)
    x_spec = pl.BlockSpec((1, TT, D), lambda b, i: (b, i, 0))
    if to_time_order:
        out_spec = pl.BlockSpec(memory_space=pltpu.HBM)
        out_shape = jax.ShapeDtypeStruct((B, T // CHUNK, SUBLANES, VROWS, D), x.dtype)
        out_scratch = [pltpu.VMEM((2, TT // CHUNK, VROWS, SUBLANES, D), F32),
                       pltpu.SemaphoreType.DMA((2,))]
    else:
        out_spec, out_shape, out_scratch = x_spec, jax.ShapeDtypeStruct(x.shape, x.dtype), []
    return pl.pallas_call(
        functools.partial(ffn_kernel, layer, to_time_order),
        grid=(B, T // TT),
        in_specs=([x_spec] + [pl.BlockSpec(memory_space=pltpu.HBM) for _ in weights]
                  + [_const_spec(c.shape) for c in consts]),
        out_specs=out_spec,
        out_shape=out_shape,
        scratch_shapes=[
            pltpu.VMEM((D_MODEL, 2 * D_FF), BF16),
            pltpu.VMEM((D_FF, D_MODEL), BF16),
            pltpu.VMEM((STAGE_SLOTS, D_MODEL, STAGE_COLS), F32),
            pltpu.VMEM((STAGE_SLOTS, STAGE_COLS, D_MODEL), F32),
            pltpu.SemaphoreType.DMA((STAGE_SLOTS,)),
            pltpu.VMEM(((FFN_CONV - 1) * SUBLANES, 2 * D_FF), F32),
            pltpu.VMEM((TT, D_FF), BF16),
        ] + out_scratch,
        compiler_params=pltpu.CompilerParams(
            dimension_semantics=("arbitrary", "arbitrary"), vmem_limit_bytes=VMEM_LIMIT),
        name="ffn_layer",
    )(x, *weights, *consts).reshape(B, T, D)


def _row(v):
    return v.reshape(1, -1).astype(F32)


def _pad_lanes(v):
    return jnp.pad(v.astype(F32), (0, LANES - v.shape[0])).reshape(1, LANES)


def _prep_pool_w(w):
    out = jnp.zeros((D_POOL, D_POOL), F32)
    for g in range(len(POOL_WINDOWS)):
        s = g * POOL_GROUP_DIM
        out = lax.dynamic_update_slice(out, w[g].astype(F32), (s, s))
    return out.astype(BF16)


def kernel(x, pre_mix_norm, w_in, pool_w, pool_b, pool_scale, ssd_conv_w, ssd_conv_b, ssd_dt_bias, ssd_a_log, ssd_d, ssd_norm, mlstm_conv_w, mlstm_conv_b, mlstm_i_bias, mlstm_f_bias, mlstm_norm, w_out, post_mix_norm, pre_ffn_norm, ffn_w_up, ffn_conv_w, ffn_conv_b, ffn_w_down, post_ffn_norm):
    depth = w_in.shape[0]
    for l in range(depth):
        gbias = _pad_lanes(jnp.concatenate([ssd_dt_bias[l], mlstm_i_bias[l], mlstm_f_bias[l]]))
        x = _mix_layer(
            l, l == 0, x, w_out, w_in[l].astype(BF16), _row(pre_mix_norm[l]), gbias, _pad_lanes(ssd_a_log[l]),
            ssd_conv_w[l].astype(F32), _row(ssd_conv_b[l]), mlstm_conv_w[l].astype(F32), _row(mlstm_conv_b[l]),
            _prep_pool_w(pool_w[l]), _row(pool_b[l]), _row(pool_scale[l]),
            _row(jnp.repeat(ssd_d[l], SSD_HEAD_DIM)), _row(ssd_norm[l]), _row(mlstm_norm[l]),
            _row(post_mix_norm[l]))
        x = _ffn_layer(
            l, l == depth - 1, x, ffn_w_up, ffn_w_down, _row(pre_ffn_norm[l]), ffn_conv_w[l].astype(F32),
            _row(ffn_conv_b[l]), _row(post_ffn_norm[l]))
    return x
```

```python
import functools
import math

import jax
import jax.numpy as jnp
from jax import lax
from jax.experimental import pallas as pl
from jax.experimental.pallas import tpu as pltpu

F32 = jnp.float32
BF16 = jnp.bfloat16

D_MODEL = 1024
EPS = 1e-6

D_POOL = 256
POOL_GROUP_DIM = 64
POOL_WINDOWS = (2, 4, 8, 16)

D_SSD = 512
SSD_HEADS = 8
SSD_HEAD_DIM = 64
SSD_GROUPS = 2
SSD_STATE = 128
SSD_CONV = 4
D_SSD_XBC = D_SSD + 2 * SSD_GROUPS * SSD_STATE

D_MLSTM = 256
MLSTM_HEADS = 4
MLSTM_HEAD_DIM = 64
MLSTM_CONV = 4

D_FF = 2816
FFN_CONV = 3

CHUNK = 128
LANES = 128
SUBLANES = 8
VROWS = CHUNK // SUBLANES

C_POOL = 0
C_XBC = C_POOL + D_POOL
C_QK = C_XBC + D_SSD_XBC
N_HALO_COLS = C_QK + 2 * D_MLSTM
R_Z = 0
R_V = R_Z + D_SSD
R_O = R_V + D_MLSTM
R_G = R_O + D_MLSTM
N_REST_COLS = R_G + LANES
N_IN_COLS = N_HALO_COLS + N_REST_COLS
G_DT = 0
G_I = SSD_HEADS
G_F = G_I + MLSTM_HEADS

MIX_TT = 256
PROJ_COLS = 256
STAGE_COLS = 256
STAGE_SLOTS = 4
FFN_TT = 512
FFN_FT = 256
VMEM_LIMIT = 56 * 1024 * 1024


def _dot(a, b):
    return jnp.dot(a, b, preferred_element_type=F32)


def _silu(x):
    return x * jax.nn.sigmoid(x)


def _split3(a):
    hi = a.astype(BF16)
    r = a - hi.astype(F32)
    mid = r.astype(BF16)
    lo = (r - mid.astype(F32)).astype(BF16)
    return hi, mid, lo


def _bcast_lane(a, j, shape):
    return jnp.broadcast_to(a[:, j:j + 1], shape)


def _pair_expand(a, h_even, shape, lo_half):
    return jnp.where(lo_half, _bcast_lane(a, h_even, shape), _bcast_lane(a, h_even + 1, shape))


def _stage_weights(blocks, stage_ref, sem_ref):
    n_slots = stage_ref.shape[0]

    def copy(n):
        src, dst = blocks[n]
        rows, cols = src.shape
        return pltpu.make_async_copy(src, stage_ref.at[n % n_slots, 0:rows, 0:cols], sem_ref.at[n % n_slots])

    for n in range(min(n_slots - 1, len(blocks))):
        copy(n).start()
    for n, (src, dst) in enumerate(blocks):
        if n + n_slots - 1 < len(blocks):
            copy(n + n_slots - 1).start()
        copy(n).wait()
        rows, cols = src.shape
        dst[...] = stage_ref[n % n_slots, 0:rows, 0:cols].astype(BF16)


def _col_blocks(src_ref, src0, dst_ref, dst0, ncols, step):
    return [(src_ref.at[:, src0 + k:src0 + min(k + step, ncols)],
             dst_ref.at[:, dst0 + k:dst0 + min(k + step, ncols)]) for k in range(0, ncols, step)]


def _tau(p):
    return (p % SUBLANES) * VROWS + p // SUBLANES


def _ext_rows(prev_tail, cur_tail):
    n = cur_tail.shape[0] // SUBLANES
    sub0 = lax.broadcasted_iota(jnp.int32, (SUBLANES, cur_tail.shape[1]), 0) == 0
    out = []
    for j in range(n):
        sl = slice(j * SUBLANES, (j + 1) * SUBLANES)
        out.append(jnp.where(sub0, pltpu.roll(prev_tail[sl], 1, 0), pltpu.roll(cur_tail[sl], 1, 0)))
    return out


def _shifted(ext, cur, k):
    if k == 0:
        return cur
    return jnp.concatenate(ext[len(ext) - k:] + [cur[0:CHUNK - SUBLANES * k]], axis=0)


def _causal_conv(prev_tail, cur, w_ref, b_row, wcol, ncols, k_taps):
    n = k_taps - 1
    ext = _ext_rows(prev_tail, cur[CHUNK - n * SUBLANES:CHUNK])
    acc = b_row
    for k in range(k_taps):
        acc = acc + _shifted(ext, cur, n - k) * w_ref[k:k + 1, wcol:wcol + ncols]
    return acc


def _relayout_w_in(wraw_ref, win_ref):
    c_z = D_POOL
    c_xbc = c_z + D_SSD
    c_dt = c_xbc + D_SSD_XBC
    n_tail = SSD_HEADS + 4 * D_MLSTM + 2 * MLSTM_HEADS
    o_qk = SSD_HEADS
    o_v = o_qk + 2 * D_MLSTM
    o_if = o_v + 2 * D_MLSTM
    for r in range(0, D_MODEL, CHUNK):
        rows = slice(r, r + CHUNK)
        win_ref[rows, C_POOL:C_POOL + D_POOL] = wraw_ref[rows, 0:D_POOL]
        win_ref[rows, C_XBC:C_XBC + D_SSD_XBC] = wraw_ref[rows, c_xbc:c_dt]
        win_ref[rows, N_HALO_COLS + R_Z:N_HALO_COLS + R_Z + D_SSD] = wraw_ref[rows, c_z:c_xbc]
        tail = wraw_ref[rows, c_dt:c_dt + n_tail]
        win_ref[rows, C_QK:C_QK + 2 * D_MLSTM] = tail[:, o_qk:o_v]
        win_ref[rows, N_HALO_COLS + R_V:N_HALO_COLS + R_V + 2 * D_MLSTM] = tail[:, o_v:o_if]
        pad = jnp.zeros((CHUNK, LANES - SSD_HEADS - 2 * MLSTM_HEADS), tail.dtype)
        win_ref[rows, N_HALO_COLS + R_G:N_HALO_COLS + R_G + LANES] = jnp.concatenate(
            [tail[:, 0:o_qk], tail[:, o_if:n_tail], pad], axis=1)


def mix_kernel(layer, from_time_order, n_tiles, xc_ref, xp_ref, wout_hbm, wraw_ref, nw_ref, gbias_ref, alog_ref,
               xcw_ref, xcb_ref, qcw_ref, qcb_ref,
               poolw_ref, poolb_ref, pools_ref, dskip_ref, snorm_ref, mnorm_ref, postn_ref,
               out_ref, win_ref, wout_ref, stage_ref, stage_sem,
               h_ref, hp_ref, rest_ref, hph_ref, psh_ref, act_ref, mix_ref, ps_ref,
               sstate_ref, mstate_ref, mm_ref, *in_scratch):
    TT = hp_ref.shape[1]
    L = CHUNK
    n_c = TT // L
    b = pl.program_id(0)
    i = pl.program_id(1)
    n_t = n_tiles

    if from_time_order:
        xbuf_ref, in_sem = in_scratch
        n_slots = xbuf_ref.shape[0]

        def in_copies(tile):
            sl = tile % n_slots
            return [pltpu.make_async_copy(xc_ref.at[b, pl.ds(tile * n_c, n_c), s], xbuf_ref.at[sl, :, :, s, :],
                                          in_sem.at[sl]) for s in range(SUBLANES)]

        @pl.when(i == 0)
        def _():
            for cp in in_copies(0):
                cp.start()

        @pl.when(i + 1 < n_t)
        def _():
            for cp in in_copies(i + 1):
                cp.start()

        @pl.when(i < n_t)
        def _():
            for cp in in_copies(i):
                cp.wait()

        slot_cur = jnp.minimum(i, n_t - 1) % n_slots
        slot_prev = jnp.maximum(i - 1, 0) % n_slots

        def x_cur():
            return xbuf_ref[slot_cur].reshape(TT, xbuf_ref.shape[-1])

        def x_rows(r0):
            return xbuf_ref[slot_prev, r0 // L].reshape(L, xbuf_ref.shape[-1])
    else:
        def x_cur():
            return xc_ref[0]

        def x_rows(r0):
            return xp_ref[0, r0:r0 + L, :]

    @pl.when((pl.program_id(0) == 0) & (i == 0))
    def _():
        _relayout_w_in(wraw_ref, win_ref)
        _stage_weights(_col_blocks(wout_hbm.at[layer], 0, wout_ref, 0, D_MODEL, STAGE_COLS),
                       stage_ref, stage_sem)

    @pl.when(i == 0)
    def _():
        hph_ref[...] = jnp.zeros(hph_ref.shape, F32)
        psh_ref[...] = jnp.zeros(psh_ref.shape, F32)
        sstate_ref[...] = jnp.zeros(sstate_ref.shape, F32)
        mstate_ref[...] = jnp.zeros(mstate_ref.shape, F32)
        mm_ref[...] = jnp.zeros(mm_ref.shape, F32)

    def step(slot_proj, slot_mix):
        pending = []
        if slot_proj is not None:
            x = x_cur()
            ms = jnp.mean(x * x, axis=-1, keepdims=True)
            h_ref[...] = (x * lax.rsqrt(ms + EPS) * nw_ref[...]).astype(BF16)

            def proj_piece(dst_ref, c0, c1, w0):
                def piece():
                    dst_ref[slot_proj, :, c0:c1] = _dot(h_ref[...], win_ref[:, w0 + c0:w0 + c1])
                return piece

            pending += [proj_piece(hp_ref, c0, min(c0 + PROJ_COLS, N_HALO_COLS), 0)
                        for c0 in range(0, N_HALO_COLS, PROJ_COLS)]
            pending += [proj_piece(rest_ref, c0, min(c0 + PROJ_COLS, N_REST_COLS), N_HALO_COLS)
                        for c0 in range(0, N_REST_COLS, PROJ_COLS)]
        if slot_mix is not None:
            _mixers(i - 1, x_rows, hp_ref.at[slot_mix], rest_ref.at[slot_mix], gbias_ref, alog_ref,
                    xcw_ref, xcb_ref, qcw_ref, qcb_ref, poolw_ref, poolb_ref, pools_ref, dskip_ref, snorm_ref,
                    mnorm_ref, wout_ref, postn_ref, out_ref, hph_ref, psh_ref, act_ref, mix_ref, ps_ref,
                    sstate_ref, mstate_ref, mm_ref, pending)
        while pending:
            pending.pop(0)()

    last_slot = (n_tiles - 1) % 2

    @pl.when(i == 0)
    def _():
        step(0, None)

    @pl.when((i > 0) & (i < n_tiles) & (i % 2 == 0))
    def _():
        step(0, 1)

    @pl.when((i > 0) & (i < n_tiles) & (i % 2 == 1))
    def _():
        step(1, 0)

    @pl.when(i == n_tiles)
    def _():
        step(None, last_slot)


def _mixers(tile, x_rows, hp_ref, rest_ref, gbias_ref, alog_ref, xcw_ref, xcb_ref, qcw_ref, qcb_ref,
            poolw_ref, poolb_ref, pools_ref, dskip_ref, snorm_ref, mnorm_ref, wout_ref, postn_ref,
            out_ref, hph_ref, psh_ref, act_ref, mix_ref, ps_ref, sstate_ref, mstate_ref, mm_ref, pending):
    def between():
        if pending:
            pending.pop(0)()

    TT = hp_ref.shape[0]
    L = CHUNK
    n_hph = hph_ref.shape[0]
    n_psh = psh_ref.shape[1]
    ps_carried = ((0, 0), (0, 1), (1, 1), (2, 1))

    def tail(cur_ref, halo, n_halo, r0, n_rows, cols):
        if r0 == 0:
            return halo[n_halo - n_rows:n_halo, cols]
        return cur_ref[r0 - n_rows:r0, cols]

    lane = lax.broadcasted_iota(jnp.int32, (L, LANES), 1)
    row = lax.broadcasted_iota(jnp.int32, (L, LANES), 0)
    lo_half = lane < 64
    tau_row = _tau(row)
    causal = _tau(lane) <= tau_row
    tril = jnp.where(causal, 1.0, 0.0).astype(BF16)
    lane_row = lax.broadcasted_iota(jnp.int32, (1, LANES), 1)
    a_row = -jnp.exp(alog_ref[...])
    neg_inf = -jnp.inf
    win_blk = [jnp.where(lo_half, float(POOL_WINDOWS[2 * b]), float(POOL_WINDOWS[2 * b + 1])) for b in range(2)]
    tau_f = tau_row.astype(F32)

    def chunk_stages(c):
        r0 = c * L

        def conv_block(col, cw_ref, cb_ref, wcol, k_taps):
            cols = slice(col, col + LANES)
            cur = hp_ref[r0:r0 + L, cols]
            prev_tail = tail(hp_ref, hph_ref, n_hph, r0, (k_taps - 1) * SUBLANES, cols)
            return _silu(_causal_conv(prev_tail, cur, cw_ref, cb_ref[:, wcol:wcol + LANES], wcol, LANES, k_taps))

        for blk in range(D_SSD_XBC // LANES):
            act_ref[r0:r0 + L, blk * LANES:(blk + 1) * LANES] = conv_block(
                C_XBC + blk * LANES, xcw_ref, xcb_ref, blk * LANES, SSD_CONV)
        for blk in range(2 * D_MLSTM // LANES):
            act_ref[r0:r0 + L, D_SSD_XBC + blk * LANES:D_SSD_XBC + (blk + 1) * LANES] = conv_block(
                C_QK + blk * LANES, qcw_ref, qcb_ref, blk * LANES, MLSTM_CONV)
        yield

        pos = tau_f + (tile * TT + r0 + 1).astype(F32)
        pooled_blocks = []
        for b in range(2):
            cs_ = slice(b * LANES, (b + 1) * LANES)
            u_cur = hp_ref[r0:r0 + L, cs_]
            lvl = u_cur
            sums = []
            for li, sh in enumerate((1, 2, 4, 8)):
                if li == 0:
                    prev_tail = tail(hp_ref, hph_ref, n_hph, r0, sh * SUBLANES, cs_)
                else:
                    prev_tail = tail(ps_ref.at[li - 1], psh_ref.at[li - 1], n_psh, r0, sh * SUBLANES, cs_)
                ext = _ext_rows(prev_tail, lvl[L - sh * SUBLANES:L])
                lvl = lvl + _shifted(ext, lvl, sh)
                sums.append(lvl)
                if (li, b) in ps_carried:
                    ps_ref[li, r0:r0 + L, cs_] = lvl
                if b == 0 and li == 1:
                    break
            wsum = jnp.where(lo_half, sums[0], sums[1]) if b == 0 else jnp.where(lo_half, sums[2], sums[3])
            pooled_blocks.append((wsum / jnp.minimum(pos, win_blk[b]) - u_cur).astype(BF16))
        mix_ref[r0:r0 + L, 0:D_POOL] = (
            (_dot(jnp.concatenate(pooled_blocks, axis=1), poolw_ref[...]) + poolb_ref[...]) * pools_ref[...])
        yield

        gb = rest_ref[r0:r0 + L, R_G:R_G + LANES] + gbias_ref[...]
        sp_term = jnp.log(1.0 + jnp.exp(-jnp.abs(gb)))
        dt = jnp.maximum(gb, 0.0) + sp_term
        log_f = jnp.minimum(gb, 0.0) - sp_term
        is_dt = lane < G_I
        is_f = (lane >= G_F) & (lane < G_F + MLSTM_HEADS)
        v_cum = jnp.where(is_dt, dt * a_row, jnp.where(is_f, log_f, 0.0))
        hi, mid, lo = _split3(v_cum)
        cs3 = _dot(tril, jnp.concatenate([hi, mid, lo], axis=1))
        cs = cs3[:, 0:LANES] + cs3[:, LANES:2 * LANES] + cs3[:, 2 * LANES:3 * LANES]
        u_gate = jnp.where(is_dt, dt, gb)
        cs_t = cs.T
        ug_t = u_gate.T
        cs_last = cs[L - 1:L, :]
        e_col = jnp.exp(cs)
        w_col = jnp.exp(cs_last - cs) * dt
        e_last = jnp.exp(cs_last)
        yield

        for g in range(SSD_GROUPS):
            b_t = act_ref[r0:r0 + L, D_SSD + g * SSD_STATE:D_SSD + (g + 1) * SSD_STATE].T.astype(BF16)
            c_g = act_ref[r0:r0 + L, D_SSD + (SSD_GROUPS + g) * SSD_STATE:
                          D_SSD + (SSD_GROUPS + g + 1) * SSD_STATE].astype(BF16)
            state_g = sstate_ref[g]
            sc = _dot(c_g, jnp.concatenate([b_t, state_g.astype(BF16)], axis=1))
            s_g = sc[:, 0:L]
            y_off = sc[:, L:L + 4 * SSD_HEAD_DIM]
            xd_blocks = []
            cd_blocks = []
            for pr in range(2):
                h_even = 4 * g + 2 * pr
                col = h_even * SSD_HEAD_DIM
                xs = act_ref[r0:r0 + L, col:col + LANES]
                xs_b = xs.astype(BF16)
                m_pair = []
                for hh in range(2):
                    hd = h_even + hh
                    seg = jnp.where(causal, cs[:, hd:hd + 1] - cs_t[hd:hd + 1, :], neg_inf)
                    m_pair.append((s_g * (jnp.exp(seg) * ug_t[hd:hd + 1, :])).astype(BF16))
                yd = _dot(jnp.concatenate(m_pair, axis=0), xs_b)
                y_diag = jnp.where(lo_half, yd[0:L], yd[L:2 * L])
                e_exp = _pair_expand(e_col, h_even, (L, LANES), lo_half)
                w_exp = _pair_expand(w_col, h_even, (L, LANES), lo_half)
                y = (y_diag + y_off[:, pr * LANES:(pr + 1) * LANES] * e_exp
                     + xs * dskip_ref[:, col:col + LANES])
                z = rest_ref[r0:r0 + L, R_Z + col:R_Z + col + LANES]
                mix_ref[r0:r0 + L, D_POOL + col:D_POOL + col + LANES] = y * _silu(z)
                xd_blocks.append((xs * w_exp).astype(BF16))
                cd_blocks.append(_pair_expand(e_last, h_even, (1, LANES), lane_row < 64))
            xd_g = jnp.concatenate(xd_blocks, axis=1)
            cd_g = jnp.concatenate(cd_blocks, axis=1)
            new_states = _dot(b_t, xd_g)
            sstate_ref[g] = state_g * cd_g + new_states
            yield
        y_all = mix_ref[r0:r0 + L, D_POOL:D_POOL + D_SSD]
        ms_y = jnp.mean(y_all * y_all, axis=-1, keepdims=True)
        mix_ref[r0:r0 + L, D_POOL:D_POOL + D_SSD] = y_all * lax.rsqrt(ms_y + EPS) * snorm_ref[...]

        for pr in range(MLSTM_HEADS // 2):
            qcol = D_SSD_XBC + pr * LANES
            kcol = D_SSD_XBC + D_MLSTM + pr * LANES
            q_b = act_ref[r0:r0 + L, qcol:qcol + LANES] * (MLSTM_HEAD_DIM ** -0.5)
            k_t = act_ref[r0:r0 + L, kcol:kcol + LANES].T
            k_tb = k_t.astype(BF16)
            v_b = rest_ref[r0:r0 + L, R_V + pr * LANES:R_V + (pr + 1) * LANES]
            o_b = rest_ref[r0:r0 + L, R_O + pr * LANES:R_O + (pr + 1) * LANES]
            qms = [jnp.where(lo_half, q_b, 0.0), jnp.where(lo_half, 0.0, q_b)]
            s_pair = _dot(jnp.concatenate([q.astype(BF16) for q in qms], axis=0), k_tb)
            v_ones = jnp.concatenate([v_b, jnp.where(lane == 0, 1.0, 0.0)], axis=1).astype(BF16)
            state = mstate_ref[pr]
            lhs = []
            g_ts = []
            kw_rows = []
            decay_old = []
            decay_new = []
            for hh in range(2):
                hd = 2 * pr + hh
                s = s_pair[hh * L:(hh + 1) * L]
                b_row = cs_t[G_F + hd:G_F + hd + 1, :]
                r_row = ug_t[G_I + hd:G_I + hd + 1, :] - b_row
                b_last = jnp.sum(jnp.where(lane_row == L - 1, b_row, 0.0), axis=-1, keepdims=True)
                al_row = b_last + r_row
                m_loc = jnp.max(al_row, axis=-1, keepdims=True)
                prev_m_row = mm_ref[hd:hd + 1, :]
                rmask = jnp.where(causal, r_row, neg_inf)
                g_t = jnp.maximum(jnp.broadcast_to(jnp.max(rmask, axis=-1, keepdims=True), (L, LANES)),
                                  prev_m_row)
                p = (s * jnp.exp(rmask - g_t)).astype(BF16)
                q_inter = (qms[hh] * jnp.exp(prev_m_row - g_t)).astype(BF16)
                lhs.append(jnp.concatenate([p, q_inter], axis=1))
                g_ts.append(g_t)
                kw_rows.append(jnp.exp(al_row - m_loc))
                m_new = jnp.maximum(b_last + prev_m_row, m_loc)
                decay_old.append(jnp.exp(b_last + prev_m_row - m_new))
                decay_new.append(jnp.exp(m_loc - m_new))
                mm_ref[hd:hd + 1, :] = m_new
            rhs = jnp.concatenate([v_ones, state.astype(BF16)], axis=0)
            res_pair = _dot(jnp.concatenate(lhs, axis=0), rhs)
            hv = []
            for hh in range(2):
                hd = 2 * pr + hh
                res = res_pair[hh * L:(hh + 1) * L]
                den = jnp.maximum(jnp.abs(_bcast_lane(res, LANES, (L, LANES))),
                                  jnp.exp(-(_bcast_lane(cs, G_F + hd, (L, LANES)) + g_ts[hh])))
                hv.append(res[:, 0:LANES] / den)
            top = row < 64
            ktw = (k_t * jnp.where(top, kw_rows[0], kw_rows[1])).astype(BF16)
            c_loc = _dot(ktw, v_ones)
            own = top == lo_half
            d_old = jnp.where(top, decay_old[0], decay_old[1])
            d_new = jnp.where(top, decay_new[0], decay_new[1])
            mstate_ref[pr] = jnp.concatenate(
                [d_old * state[:, 0:LANES] + d_new * jnp.where(own, c_loc[:, 0:LANES], 0.0),
                 d_old * state[:, LANES:2 * LANES] + d_new * c_loc[:, LANES:2 * LANES]], axis=1)
            hcat = jax.nn.sigmoid(o_b) * jnp.where(lo_half, hv[0], hv[1])
            sq = hcat * hcat
            ss_lo = jnp.sum(jnp.where(lo_half, sq, 0.0), axis=-1, keepdims=True)
            ss_hi = jnp.sum(jnp.where(lo_half, 0.0, sq), axis=-1, keepdims=True)
            inv = jnp.where(lo_half, lax.rsqrt(ss_lo * (1.0 / MLSTM_HEAD_DIM) + EPS),
                            lax.rsqrt(ss_hi * (1.0 / MLSTM_HEAD_DIM) + EPS))
            mcol = D_POOL + D_SSD + pr * LANES
            mix_ref[r0:r0 + L, mcol:mcol + LANES] = hcat * inv * mnorm_ref[:, pr * LANES:(pr + 1) * LANES]
            if pr + 1 < MLSTM_HEADS // 2:
                yield

        yield

    n_stages = 3 + SSD_GROUPS + MLSTM_HEADS // 2
    gens = [chunk_stages(c) for c in range(TT // L)]
    for _ in range(n_stages):
        for gen in gens:
            next(gen)
            between()
    while pending:
        pending.pop(0)()

    o = _dot(mix_ref[...].astype(BF16), wout_ref[...])
    ms_o = jnp.mean(o * o, axis=-1, keepdims=True)
    o = o * lax.rsqrt(ms_o + EPS) * postn_ref[...]
    for c in range(TT // L):
        out_ref[0, c * L:(c + 1) * L, :] = x_rows(c * L) + o[c * L:(c + 1) * L]

    hph_ref[...] = hp_ref[TT - n_hph:TT, :]
    for lv, b in ps_carried:
        cs_ = slice(b * LANES, (b + 1) * LANES)
        psh_ref[lv, :, cs_] = ps_ref[lv, TT - n_psh:TT, cs_]


def ffn_kernel(layer, to_time_order, x_ref, wup_hbm, wdn_hbm, nw_ref, cw_ref, cb_ref, postn_ref, out_ref,
               wup_ref, wdn_ref, stage_ref, stage_dn_ref, stage_sem, halo_ref, a_ref, *out_scratch):
    TT = x_ref.shape[1]
    FT = FFN_FT
    L = CHUNK
    n_tail = (FFN_CONV - 1) * SUBLANES
    i = pl.program_id(1)

    @pl.when((pl.program_id(0) == 0) & (i == 0))
    def _():
        _stage_weights(_col_blocks(wup_hbm.at[layer], 0, wup_ref, 0, 2 * D_FF, STAGE_COLS),
                       stage_ref, stage_sem)
        _stage_weights([(wdn_hbm.at[layer, r:r + STAGE_COLS, :], wdn_ref.at[r:r + STAGE_COLS, :])
                        for r in range(0, D_FF, STAGE_COLS)], stage_dn_ref, stage_sem)

    @pl.when(i == 0)
    def _():
        halo_ref[...] = jnp.zeros(halo_ref.shape, F32)

    x = x_ref[0]
    ms = jnp.mean(x * x, axis=-1, keepdims=True)
    h = (x * lax.rsqrt(ms + EPS) * nw_ref[...]).astype(BF16)

    def conv_cols(col):
        u = _dot(h, wup_ref[:, col:col + FT])
        outs = []
        for c in range(TT // L):
            cur = u[c * L:(c + 1) * L]
            prev_tail = halo_ref[:, col:col + FT] if c == 0 else u[c * L - n_tail:c * L]
            outs.append(_causal_conv(prev_tail, cur, cw_ref, cb_ref[:, col:col + FT], col, FT, FFN_CONV))
        halo_ref[:, col:col + FT] = u[TT - n_tail:TT]
        return outs

    for j in range(D_FF // FT):
        gts = conv_cols(j * FT)
        vals = conv_cols(D_FF + j * FT)
        for c in range(TT // L):
            gt = gts[c]
            gelu = 0.5 * gt * (1.0 + jnp.tanh(math.sqrt(2.0 / math.pi) * (gt + 0.044715 * (gt * gt * gt))))
            a_ref[c * L:(c + 1) * L, j * FT:(j + 1) * FT] = (gelu * vals[c]).astype(BF16)

    f = _dot(a_ref[...], wdn_ref[...])
    ms_f = jnp.mean(f * f, axis=-1, keepdims=True)
    res = x_ref[0] + f * lax.rsqrt(ms_f + EPS) * postn_ref[...]
    if not to_time_order:
        out_ref[0] = res
        return

    obuf_ref, out_sem = out_scratch
    n_c = TT // L
    b = pl.program_id(0)
    step = b * pl.num_programs(1) + i
    n_steps = pl.num_programs(0) * pl.num_programs(1)
    slot = step % 2

    def out_copies(sl):
        return [pltpu.make_async_copy(obuf_ref.at[sl, :, :, s, :], out_ref.at[b, pl.ds(i * n_c, n_c), s],
                                      out_sem.at[sl]) for s in range(SUBLANES)]

    @pl.when(step >= 2)
    def _():
        for cp in out_copies(slot):
            cp.wait()

    obuf_ref[slot] = res.reshape(n_c, VROWS, SUBLANES, res.shape[1])
    for cp in out_copies(slot):
        cp.start()

    @pl.when(step == n_steps - 1)
    def _():
        for cp in out_copies(slot):
            cp.wait()

    @pl.when((step == n_steps - 1) & (n_steps >= 2))
    def _():
        for cp in out_copies(1 - slot):
            cp.wait()


def _const_spec(shape):
    nd = len(shape)
    return pl.BlockSpec(shape, lambda b, i: (0,) * nd, pipeline_mode=pl.Buffered(1))


def _mix_layer(layer, from_time_order, x, wout, win, nw, gbias, alog, xcw, xcb, qcw, qcb, poolw, poolb, pools,
               dskip, snorm, mnorm, postn):
    B, T, D = x.shape
    TT = MIX_TT
    weights = (wout,)
    consts = (nw, gbias, alog, xcw, xcb, qcw, qcb, poolw, poolb, pools, dskip, snorm, mnorm, postn)
    win_spec = pl.BlockSpec((None,) + win.shape[1:], lambda b, i: (layer, 0, 0), pipeline_mode=pl.Buffered(1))
    n_t = T // TT
    cur_spec = pl.BlockSpec((1, TT, D), lambda b, i: (b, jnp.minimum(i, n_t - 1), 0))
    prev_spec = pl.BlockSpec((1, TT, D), lambda b, i: (b, jnp.maximum(i - 1, 0), 0))
    max_conv_tail = (max(SSD_CONV, MLSTM_CONV) - 1) * SUBLANES
    max_pool_tail = (POOL_WINDOWS[-1] // 2) * SUBLANES
    if from_time_order:
        x_in = x.reshape(B, T // CHUNK, SUBLANES, VROWS, D)
        x_specs = [pl.BlockSpec(memory_space=pltpu.HBM)] * 2
        in_scratch = [pltpu.VMEM((3, TT // CHUNK, VROWS, SUBLANES, D), F32),
                      pltpu.SemaphoreType.DMA((3,))]
    else:
        x_in, x_specs, in_scratch = x, [cur_spec, prev_spec], []
    return pl.pallas_call(
        functools.partial(mix_kernel, layer, from_time_order, n_t),
        grid=(B, n_t + 1),
        in_specs=(x_specs + [pl.BlockSpec(memory_space=pltpu.HBM) for _ in weights] + [win_spec]
                  + [_const_spec(c.shape) for c in consts]),
        out_specs=prev_spec,
        out_shape=jax.ShapeDtypeStruct(x.shape, x.dtype),
        scratch_shapes=[
            pltpu.VMEM((D_MODEL, N_IN_COLS), BF16),
            pltpu.VMEM((D_MODEL, D_MODEL), BF16),
            pltpu.VMEM((STAGE_SLOTS, D_MODEL, STAGE_COLS), F32),
            pltpu.SemaphoreType.DMA((STAGE_SLOTS,)),
            pltpu.VMEM((TT, D_MODEL), BF16),
            pltpu.VMEM((2, TT, N_HALO_COLS), F32),
            pltpu.VMEM((2, TT, N_REST_COLS), F32),
            pltpu.VMEM((max_conv_tail, N_HALO_COLS), F32),
            pltpu.VMEM((3, max_pool_tail, D_POOL), F32),
            pltpu.VMEM((TT, D_SSD_XBC + 2 * D_MLSTM), F32),
            pltpu.VMEM((TT, D_MODEL), F32),
            pltpu.VMEM((3, TT, D_POOL), F32),
            pltpu.VMEM((SSD_GROUPS, SSD_STATE, 4 * SSD_HEAD_DIM), F32),
            pltpu.VMEM((MLSTM_HEADS // 2, LANES, 2 * LANES), F32),
            pltpu.VMEM((SUBLANES, LANES), F32),
        ] + in_scratch,
        compiler_params=pltpu.CompilerParams(
            dimension_semantics=("arbitrary", "arbitrary"), vmem_limit_bytes=VMEM_LIMIT),
        name="mix_layer",
    )(x_in, x_in, *weights, win, *consts)


def _ffn_layer(layer, to_time_order, x, wup, wdn, nw, cw, cb, postn):
    B, T, D = x.shape
    TT = FFN_TT
    weights = (wup, wdn)
    consts = (nw, cw, cb, postn)
    x_spec = pl.BlockSpec((1, TT, D), lambda b, i: (b, i, 0))
    if to_time_order:
        out_spec = pl.BlockSpec(memory_space=pltpu.HBM)
        out_shape = jax.ShapeDtypeStruct((B, T // CHUNK, SUBLANES, VROWS, D), x.dtype)
        out_scratch = [pltpu.VMEM((2, TT // CHUNK, VROWS, SUBLANES, D), F32),
                       pltpu.SemaphoreType.DMA((2,))]
    else:
        out_spec, out_shape, out_scratch = x_spec, jax.ShapeDtypeStruct(x.shape, x.dtype), []
    return pl.pallas_call(
        functools.partial(ffn_kernel, layer, to_time_order),
        grid=(B, T // TT),
        in_specs=([x_spec] + [pl.BlockSpec(memory_space=pltpu.HBM) for _ in weights]
                  + [_const_spec(c.shape) for c in consts]),
        out_specs=out_spec,
        out_shape=out_shape,
        scratch_shapes=[
            pltpu.VMEM((D_MODEL, 2 * D_FF), BF16),
            pltpu.VMEM((D_FF, D_MODEL), BF16),
            pltpu.VMEM((STAGE_SLOTS, D_MODEL, STAGE_COLS), F32),
            pltpu.VMEM((STAGE_SLOTS, STAGE_COLS, D_MODEL), F32),
            pltpu.SemaphoreType.DMA((STAGE_SLOTS,)),
            pltpu.VMEM(((FFN_CONV - 1) * SUBLANES, 2 * D_FF), F32),
            pltpu.VMEM((TT, D_FF), BF16),
        ] + out_scratch,
        compiler_params=pltpu.CompilerParams(
            dimension_semantics=("arbitrary", "arbitrary"), vmem_limit_bytes=VMEM_LIMIT),
        name="ffn_layer",
    )(x, *weights, *consts).reshape(B, T, D)


def _row(v):
    return v.reshape(1, -1).astype(F32)


def _pad_lanes(v):
    return jnp.pad(v.astype(F32), (0, LANES - v.shape[0])).reshape(1, LANES)


def _prep_pool_w(w):
    out = jnp.zeros((D_POOL, D_POOL), F32)
    for g in range(len(POOL_WINDOWS)):
        s = g * POOL_GROUP_DIM
        out = lax.dynamic_update_slice(out, w[g].astype(F32), (s, s))
    return out.astype(BF16)


def kernel(x, pre_mix_norm, w_in, pool_w, pool_b, pool_scale, ssd_conv_w, ssd_conv_b, ssd_dt_bias, ssd_a_log, ssd_d, ssd_norm, mlstm_conv_w, mlstm_conv_b, mlstm_i_bias, mlstm_f_bias, mlstm_norm, w_out, post_mix_norm, pre_ffn_norm, ffn_w_up, ffn_conv_w, ffn_conv_b, ffn_w_down, post_ffn_norm):
    depth = w_in.shape[0]
    w_in_bf = w_in.astype(BF16)
    for l in range(depth):
        gbias = _pad_lanes(jnp.concatenate([ssd_dt_bias[l], mlstm_i_bias[l], mlstm_f_bias[l]]))
        x = _mix_layer(
            l, l == 0, x, w_out, w_in_bf, _row(pre_mix_norm[l]), gbias, _pad_lanes(ssd_a_log[l]),
            ssd_conv_w[l].astype(F32), _row(ssd_conv_b[l]), mlstm_conv_w[l].astype(F32), _row(mlstm_conv_b[l]),
            _prep_pool_w(pool_w[l]), _row(pool_b[l]), _row(pool_scale[l]),
            _row(jnp.repeat(ssd_d[l], SSD_HEAD_DIM)), _row(ssd_norm[l]), _row(mlstm_norm[l]),
            _row(post_mix_norm[l]))
        x = _ffn_layer(
            l, l == depth - 1, x, ffn_w_up, ffn_w_down, _row(pre_ffn_norm[l]), ffn_conv_w[l].astype(F32),
            _row(ffn_conv_b[l]), _row(post_ffn_norm[l]))
    return x
```

```python
import functools
import math

import jax
import jax.numpy as jnp
from jax import lax
from jax.experimental import pallas as pl
from jax.experimental.pallas import tpu as pltpu

F32 = jnp.float32
BF16 = jnp.bfloat16

D_MODEL = 1024
EPS = 1e-6

D_POOL = 256
POOL_GROUP_DIM = 64
POOL_WINDOWS = (2, 4, 8, 16)

D_SSD = 512
SSD_HEADS = 8
SSD_HEAD_DIM = 64
SSD_GROUPS = 2
SSD_STATE = 128
SSD_CONV = 4
D_SSD_XBC = D_SSD + 2 * SSD_GROUPS * SSD_STATE

D_MLSTM = 256
MLSTM_HEADS = 4
MLSTM_HEAD_DIM = 64
MLSTM_CONV = 4

D_FF = 2816
FFN_CONV = 3

CHUNK = 128
LANES = 128
SUBLANES = 8
VROWS = CHUNK // SUBLANES
HEAD_DIM = LANES // 2
assert SSD_HEAD_DIM == HEAD_DIM and MLSTM_HEAD_DIM == HEAD_DIM

C_POOL = 0
C_XBC = C_POOL + D_POOL
C_QK = C_XBC + D_SSD_XBC
N_HALO_COLS = C_QK + 2 * D_MLSTM
R_Z = 0
R_V = R_Z + D_SSD
R_O = R_V + D_MLSTM
R_G = R_O + D_MLSTM
N_REST_COLS = R_G + LANES
N_IN_COLS = N_HALO_COLS + N_REST_COLS
G_DT = 0
G_I = SSD_HEADS
G_F = G_I + MLSTM_HEADS

MIX_TT = 256
PROJ_COLS = 256
STAGE_COLS = 256
STAGE_SLOTS = 4
FFN_TT = 512
FFN_FT = 256
VMEM_LIMIT = 56 * 1024 * 1024


def _dot(a, b):
    return jnp.dot(a, b, preferred_element_type=F32)


def _silu(x):
    return x * jax.nn.sigmoid(x)


def _split3(a):
    hi = a.astype(BF16)
    r = a - hi.astype(F32)
    mid = r.astype(BF16)
    lo = (r - mid.astype(F32)).astype(BF16)
    return hi, mid, lo


def _bcast_lane(a, j, shape):
    return jnp.broadcast_to(a[:, j:j + 1], shape)


def _pair_expand(a, h_even, shape, lo_half):
    return jnp.where(lo_half, _bcast_lane(a, h_even, shape), _bcast_lane(a, h_even + 1, shape))


def _stage_weights(blocks, stage_ref, sem_ref):
    n_slots = stage_ref.shape[0]

    def copy(n):
        src, dst = blocks[n]
        rows, cols = src.shape
        return pltpu.make_async_copy(src, stage_ref.at[n % n_slots, 0:rows, 0:cols], sem_ref.at[n % n_slots])

    for n in range(min(n_slots - 1, len(blocks))):
        copy(n).start()
    for n, (src, dst) in enumerate(blocks):
        if n + n_slots - 1 < len(blocks):
            copy(n + n_slots - 1).start()
        copy(n).wait()
        rows, cols = src.shape
        dst[...] = stage_ref[n % n_slots, 0:rows, 0:cols].astype(BF16)


def _col_blocks(src_ref, src0, dst_ref, dst0, ncols, step):
    return [(src_ref.at[:, src0 + k:src0 + min(k + step, ncols)],
             dst_ref.at[:, dst0 + k:dst0 + min(k + step, ncols)]) for k in range(0, ncols, step)]


def _tau(p):
    return (p % SUBLANES) * VROWS + p // SUBLANES


def _ext_rows(prev_tail, cur_tail):
    n = cur_tail.shape[0] // SUBLANES
    sub0 = lax.broadcasted_iota(jnp.int32, (SUBLANES, cur_tail.shape[1]), 0) == 0
    out = []
    for j in range(n):
        sl = slice(j * SUBLANES, (j + 1) * SUBLANES)
        out.append(jnp.where(sub0, pltpu.roll(prev_tail[sl], 1, 0), pltpu.roll(cur_tail[sl], 1, 0)))
    return out


def _shifted(ext, cur, k):
    if k == 0:
        return cur
    return jnp.concatenate(ext[len(ext) - k:] + [cur[0:CHUNK - SUBLANES * k]], axis=0)


def _causal_conv(prev_tail, cur, w_ref, b_row, wcol, ncols, k_taps):
    n = k_taps - 1
    ext = _ext_rows(prev_tail, cur[CHUNK - n * SUBLANES:CHUNK])
    acc = b_row
    for k in range(k_taps):
        acc = acc + _shifted(ext, cur, n - k) * w_ref[k:k + 1, wcol:wcol + ncols]
    return acc


def _relayout_w_in(wraw_ref, win_ref):
    c_z = D_POOL
    c_xbc = c_z + D_SSD
    c_dt = c_xbc + D_SSD_XBC
    n_tail = SSD_HEADS + 4 * D_MLSTM + 2 * MLSTM_HEADS
    o_qk = SSD_HEADS
    o_v = o_qk + 2 * D_MLSTM
    o_if = o_v + 2 * D_MLSTM
    for r in range(0, D_MODEL, CHUNK):
        rows = slice(r, r + CHUNK)
        win_ref[rows, C_POOL:C_POOL + D_POOL] = wraw_ref[rows, 0:D_POOL]
        win_ref[rows, C_XBC:C_XBC + D_SSD_XBC] = wraw_ref[rows, c_xbc:c_dt]
        win_ref[rows, N_HALO_COLS + R_Z:N_HALO_COLS + R_Z + D_SSD] = wraw_ref[rows, c_z:c_xbc]
        tail = wraw_ref[rows, c_dt:c_dt + n_tail]
        win_ref[rows, C_QK:C_QK + 2 * D_MLSTM] = tail[:, o_qk:o_v]
        win_ref[rows, N_HALO_COLS + R_V:N_HALO_COLS + R_V + 2 * D_MLSTM] = tail[:, o_v:o_if]
        pad = jnp.zeros((CHUNK, LANES - SSD_HEADS - 2 * MLSTM_HEADS), tail.dtype)
        win_ref[rows, N_HALO_COLS + R_G:N_HALO_COLS + R_G + LANES] = jnp.concatenate(
            [tail[:, 0:o_qk], tail[:, o_if:n_tail], pad], axis=1)


def mix_kernel(layer, from_time_order, n_tiles, xc_ref, xp_ref, wout_hbm, wraw_ref, nw_ref, gbias_ref, alog_ref,
               xcw_ref, xcb_ref, qcw_ref, qcb_ref,
               poolw_ref, poolb_ref, pools_ref, dskip_ref, snorm_ref, mnorm_ref, postn_ref,
               out_ref, win_ref, wout_ref, stage_ref, stage_sem,
               h_ref, hp_ref, rest_ref, hph_ref, psh_ref, act_ref, mix_ref, ps_ref,
               sstate_ref, mstate_ref, mm_ref, *in_scratch):
    TT = hp_ref.shape[1]
    L = CHUNK
    n_c = TT // L
    b = pl.program_id(0)
    i = pl.program_id(1)
    n_t = n_tiles

    if from_time_order:
        xbuf_ref, in_sem = in_scratch
        n_slots = xbuf_ref.shape[0]

        def in_copies(tile):
            sl = tile % n_slots
            return [pltpu.make_async_copy(xc_ref.at[b, pl.ds(tile * n_c, n_c), s], xbuf_ref.at[sl, :, :, s, :],
                                          in_sem.at[sl]) for s in range(SUBLANES)]

        @pl.when(i == 0)
        def _():
            for cp in in_copies(0):
                cp.start()

        @pl.when(i + 1 < n_t)
        def _():
            for cp in in_copies(i + 1):
                cp.start()

        @pl.when(i < n_t)
        def _():
            for cp in in_copies(i):
                cp.wait()

        slot_cur = jnp.minimum(i, n_t - 1) % n_slots
        slot_prev = jnp.maximum(i - 1, 0) % n_slots

        def x_cur():
            return xbuf_ref[slot_cur].reshape(TT, xbuf_ref.shape[-1])

        def x_rows(r0):
            return xbuf_ref[slot_prev, r0 // L].reshape(L, xbuf_ref.shape[-1])
    else:
        def x_cur():
            return xc_ref[0]

        def x_rows(r0):
            return xp_ref[0, r0:r0 + L, :]

    @pl.when((pl.program_id(0) == 0) & (i == 0))
    def _():
        _relayout_w_in(wraw_ref, win_ref)
        _stage_weights(_col_blocks(wout_hbm.at[layer], 0, wout_ref, 0, D_MODEL, STAGE_COLS),
                       stage_ref, stage_sem)

    @pl.when(i == 0)
    def _():
        hph_ref[...] = jnp.zeros(hph_ref.shape, F32)
        psh_ref[...] = jnp.zeros(psh_ref.shape, F32)
        sstate_ref[...] = jnp.zeros(sstate_ref.shape, F32)
        mstate_ref[...] = jnp.zeros(mstate_ref.shape, F32)
        mm_ref[...] = jnp.zeros(mm_ref.shape, F32)

    def step(slot_proj, slot_mix):
        pending = []
        if slot_proj is not None:
            x = x_cur()
            ms = jnp.mean(x * x, axis=-1, keepdims=True)
            h_ref[...] = (x * lax.rsqrt(ms + EPS) * nw_ref[...]).astype(BF16)

            def proj_piece(dst_ref, c0, c1, w0):
                def piece():
                    dst_ref[slot_proj, :, c0:c1] = _dot(h_ref[...], win_ref[:, w0 + c0:w0 + c1])
                return piece

            pending += [proj_piece(hp_ref, c0, min(c0 + PROJ_COLS, N_HALO_COLS), 0)
                        for c0 in range(0, N_HALO_COLS, PROJ_COLS)]
            pending += [proj_piece(rest_ref, c0, min(c0 + PROJ_COLS, N_REST_COLS), N_HALO_COLS)
                        for c0 in range(0, N_REST_COLS, PROJ_COLS)]
        if slot_mix is not None:
            _mixers(i - 1, x_rows, hp_ref.at[slot_mix], rest_ref.at[slot_mix], gbias_ref, alog_ref,
                    xcw_ref, xcb_ref, qcw_ref, qcb_ref, poolw_ref, poolb_ref, pools_ref, dskip_ref, snorm_ref,
                    mnorm_ref, wout_ref, postn_ref, out_ref, hph_ref, psh_ref, act_ref, mix_ref, ps_ref,
                    sstate_ref, mstate_ref, mm_ref, pending)
        while pending:
            pending.pop(0)()

    last_slot = (n_tiles - 1) % 2

    @pl.when(i == 0)
    def _():
        step(0, None)

    @pl.when((i > 0) & (i < n_tiles) & (i % 2 == 0))
    def _():
        step(0, 1)

    @pl.when((i > 0) & (i < n_tiles) & (i % 2 == 1))
    def _():
        step(1, 0)

    @pl.when(i == n_tiles)
    def _():
        step(None, last_slot)


def _mixers(tile, x_rows, hp_ref, rest_ref, gbias_ref, alog_ref, xcw_ref, xcb_ref, qcw_ref, qcb_ref,
            poolw_ref, poolb_ref, pools_ref, dskip_ref, snorm_ref, mnorm_ref, wout_ref, postn_ref,
            out_ref, hph_ref, psh_ref, act_ref, mix_ref, ps_ref, sstate_ref, mstate_ref, mm_ref, pending):
    def between():
        if pending:
            pending.pop(0)()

    TT = hp_ref.shape[0]
    L = CHUNK
    n_hph = hph_ref.shape[0]
    n_psh = psh_ref.shape[1]
    ps_carried = ((0, 0), (0, 1), (1, 1), (2, 1))

    def tail(cur_ref, halo, n_halo, r0, n_rows, cols):
        if r0 == 0:
            return halo[n_halo - n_rows:n_halo, cols]
        return cur_ref[r0 - n_rows:r0, cols]

    lane = lax.broadcasted_iota(jnp.int32, (L, LANES), 1)
    row = lax.broadcasted_iota(jnp.int32, (L, LANES), 0)
    lo_half = lane < HEAD_DIM
    tau_row = _tau(row)
    causal = _tau(lane) <= tau_row
    tril = jnp.where(causal, 1.0, 0.0).astype(BF16)
    lane_row = lax.broadcasted_iota(jnp.int32, (1, LANES), 1)
    a_row = -jnp.exp(alog_ref[...])
    neg_inf = -jnp.inf
    win_blk = [jnp.where(lo_half, float(POOL_WINDOWS[2 * b]), float(POOL_WINDOWS[2 * b + 1])) for b in range(2)]
    tau_f = tau_row.astype(F32)

    def chunk_stages(c):
        r0 = c * L

        def conv_block(col, cw_ref, cb_ref, wcol, k_taps):
            cols = slice(col, col + LANES)
            cur = hp_ref[r0:r0 + L, cols]
            prev_tail = tail(hp_ref, hph_ref, n_hph, r0, (k_taps - 1) * SUBLANES, cols)
            return _silu(_causal_conv(prev_tail, cur, cw_ref, cb_ref[:, wcol:wcol + LANES], wcol, LANES, k_taps))

        for blk in range(D_SSD_XBC // LANES):
            act_ref[r0:r0 + L, blk * LANES:(blk + 1) * LANES] = conv_block(
                C_XBC + blk * LANES, xcw_ref, xcb_ref, blk * LANES, SSD_CONV)
        for blk in range(2 * D_MLSTM // LANES):
            act_ref[r0:r0 + L, D_SSD_XBC + blk * LANES:D_SSD_XBC + (blk + 1) * LANES] = conv_block(
                C_QK + blk * LANES, qcw_ref, qcb_ref, blk * LANES, MLSTM_CONV)
        yield

        pos = tau_f + (tile * TT + r0 + 1).astype(F32)
        pooled_blocks = []
        for b in range(2):
            cs_ = slice(b * LANES, (b + 1) * LANES)
            u_cur = hp_ref[r0:r0 + L, cs_]
            lvl = u_cur
            sums = []
            for li, sh in enumerate((1, 2, 4, 8)):
                if li == 0:
                    prev_tail = tail(hp_ref, hph_ref, n_hph, r0, sh * SUBLANES, cs_)
                else:
                    prev_tail = tail(ps_ref.at[li - 1], psh_ref.at[li - 1], n_psh, r0, sh * SUBLANES, cs_)
                ext = _ext_rows(prev_tail, lvl[L - sh * SUBLANES:L])
                lvl = lvl + _shifted(ext, lvl, sh)
                sums.append(lvl)
                if (li, b) in ps_carried:
                    ps_ref[li, r0:r0 + L, cs_] = lvl
                if b == 0 and li == 1:
                    break
            wsum = jnp.where(lo_half, sums[0], sums[1]) if b == 0 else jnp.where(lo_half, sums[2], sums[3])
            pooled_blocks.append((wsum / jnp.minimum(pos, win_blk[b]) - u_cur).astype(BF16))
        mix_ref[r0:r0 + L, 0:D_POOL] = (
            (_dot(jnp.concatenate(pooled_blocks, axis=1), poolw_ref[...]) + poolb_ref[...]) * pools_ref[...])
        yield

        gb = rest_ref[r0:r0 + L, R_G:R_G + LANES] + gbias_ref[...]
        sp_term = jnp.log(1.0 + jnp.exp(-jnp.abs(gb)))
        dt = jnp.maximum(gb, 0.0) + sp_term
        log_f = jnp.minimum(gb, 0.0) - sp_term
        is_dt = lane < G_I
        is_f = (lane >= G_F) & (lane < G_F + MLSTM_HEADS)
        v_cum = jnp.where(is_dt, dt * a_row, jnp.where(is_f, log_f, 0.0))
        hi, mid, lo = _split3(v_cum)
        cs3 = _dot(tril, jnp.concatenate([hi, mid, lo], axis=1))
        cs = cs3[:, 0:LANES] + cs3[:, LANES:2 * LANES] + cs3[:, 2 * LANES:3 * LANES]
        u_gate = jnp.where(is_dt, dt, gb)
        cs_t = cs.T
        ug_t = u_gate.T
        cs_last = cs[L - 1:L, :]
        e_col = jnp.exp(cs)
        w_col = jnp.exp(cs_last - cs) * dt
        e_last = jnp.exp(cs_last)
        yield

        for g in range(SSD_GROUPS):
            b_t = act_ref[r0:r0 + L, D_SSD + g * SSD_STATE:D_SSD + (g + 1) * SSD_STATE].T.astype(BF16)
            c_g = act_ref[r0:r0 + L, D_SSD + (SSD_GROUPS + g) * SSD_STATE:
                          D_SSD + (SSD_GROUPS + g + 1) * SSD_STATE].astype(BF16)
            state_g = sstate_ref[g]
            sc = _dot(c_g, jnp.concatenate([b_t, state_g.astype(BF16)], axis=1))
            s_g = sc[:, 0:L]
            y_off = sc[:, L:L + 4 * SSD_HEAD_DIM]
            xd_blocks = []
            cd_blocks = []
            for pr in range(2):
                h_even = 4 * g + 2 * pr
                col = h_even * SSD_HEAD_DIM
                xs = act_ref[r0:r0 + L, col:col + LANES]
                xs_b = xs.astype(BF16)
                m_pair = []
                for hh in range(2):
                    hd = h_even + hh
                    seg = jnp.where(causal, cs[:, hd:hd + 1] - cs_t[hd:hd + 1, :], neg_inf)
                    m_pair.append((s_g * (jnp.exp(seg) * ug_t[hd:hd + 1, :])).astype(BF16))
                yd = _dot(jnp.concatenate(m_pair, axis=0), xs_b)
                y_diag = jnp.where(lo_half, yd[0:L], yd[L:2 * L])
                e_exp = _pair_expand(e_col, h_even, (L, LANES), lo_half)
                w_exp = _pair_expand(w_col, h_even, (L, LANES), lo_half)
                y = (y_diag + y_off[:, pr * LANES:(pr + 1) * LANES] * e_exp
                     + xs * dskip_ref[:, col:col + LANES])
                z = rest_ref[r0:r0 + L, R_Z + col:R_Z + col + LANES]
                mix_ref[r0:r0 + L, D_POOL + col:D_POOL + col + LANES] = y * _silu(z)
                xd_blocks.append((xs * w_exp).astype(BF16))
                cd_blocks.append(_pair_expand(e_last, h_even, (1, LANES), lane_row < HEAD_DIM))
            xd_g = jnp.concatenate(xd_blocks, axis=1)
            cd_g = jnp.concatenate(cd_blocks, axis=1)
            new_states = _dot(b_t, xd_g)
            sstate_ref[g] = state_g * cd_g + new_states
            yield
        y_all = mix_ref[r0:r0 + L, D_POOL:D_POOL + D_SSD]
        ms_y = jnp.mean(y_all * y_all, axis=-1, keepdims=True)
        mix_ref[r0:r0 + L, D_POOL:D_POOL + D_SSD] = y_all * lax.rsqrt(ms_y + EPS) * snorm_ref[...]

        for pr in range(MLSTM_HEADS // 2):
            qcol = D_SSD_XBC + pr * LANES
            kcol = D_SSD_XBC + D_MLSTM + pr * LANES
            q_b = act_ref[r0:r0 + L, qcol:qcol + LANES] * (MLSTM_HEAD_DIM ** -0.5)
            k_t = act_ref[r0:r0 + L, kcol:kcol + LANES].T
            k_tb = k_t.astype(BF16)
            v_b = rest_ref[r0:r0 + L, R_V + pr * LANES:R_V + (pr + 1) * LANES]
            o_b = rest_ref[r0:r0 + L, R_O + pr * LANES:R_O + (pr + 1) * LANES]
            qms = [jnp.where(lo_half, q_b, 0.0), jnp.where(lo_half, 0.0, q_b)]
            s_pair = _dot(jnp.concatenate([q.astype(BF16) for q in qms], axis=0), k_tb)
            v_ones = jnp.concatenate([v_b, jnp.where(lane == 0, 1.0, 0.0)], axis=1).astype(BF16)
            state = mstate_ref[pr]
            lhs = []
            g_ts = []
            kw_rows = []
            decay_old = []
            decay_new = []
            for hh in range(2):
                hd = 2 * pr + hh
                s = s_pair[hh * L:(hh + 1) * L]
                b_row = cs_t[G_F + hd:G_F + hd + 1, :]
                r_row = ug_t[G_I + hd:G_I + hd + 1, :] - b_row
                b_last = jnp.sum(jnp.where(lane_row == L - 1, b_row, 0.0), axis=-1, keepdims=True)
                al_row = b_last + r_row
                m_loc = jnp.max(al_row, axis=-1, keepdims=True)
                prev_m_row = mm_ref[hd:hd + 1, :]
                rmask = jnp.where(causal, r_row, neg_inf)
                g_t = jnp.maximum(jnp.broadcast_to(jnp.max(rmask, axis=-1, keepdims=True), (L, LANES)),
                                  prev_m_row)
                p = (s * jnp.exp(rmask - g_t)).astype(BF16)
                q_inter = (qms[hh] * jnp.exp(prev_m_row - g_t)).astype(BF16)
                lhs.append(jnp.concatenate([p, q_inter], axis=1))
                g_ts.append(g_t)
                kw_rows.append(jnp.exp(al_row - m_loc))
                m_new = jnp.maximum(b_last + prev_m_row, m_loc)
                decay_old.append(jnp.exp(b_last + prev_m_row - m_new))
                decay_new.append(jnp.exp(m_loc - m_new))
                mm_ref[hd:hd + 1, :] = m_new
            rhs = jnp.concatenate([v_ones, state.astype(BF16)], axis=0)
            res_pair = _dot(jnp.concatenate(lhs, axis=0), rhs)
            hv = []
            for hh in range(2):
                hd = 2 * pr + hh
                res = res_pair[hh * L:(hh + 1) * L]
                den = jnp.maximum(jnp.abs(_bcast_lane(res, LANES, (L, LANES))),
                                  jnp.exp(-(_bcast_lane(cs, G_F + hd, (L, LANES)) + g_ts[hh])))
                hv.append(res[:, 0:LANES] / den)
            top = row < HEAD_DIM
            ktw = (k_t * jnp.where(top, kw_rows[0], kw_rows[1])).astype(BF16)
            c_loc = _dot(ktw, v_ones)
            own = top == lo_half
            d_old = jnp.where(top, decay_old[0], decay_old[1])
            d_new = jnp.where(top, decay_new[0], decay_new[1])
            mstate_ref[pr] = jnp.concatenate(
                [d_old * state[:, 0:LANES] + d_new * jnp.where(own, c_loc[:, 0:LANES], 0.0),
                 d_old * state[:, LANES:2 * LANES] + d_new * c_loc[:, LANES:2 * LANES]], axis=1)
            hcat = jax.nn.sigmoid(o_b) * jnp.where(lo_half, hv[0], hv[1])
            sq = hcat * hcat
            ss_lo = jnp.sum(jnp.where(lo_half, sq, 0.0), axis=-1, keepdims=True)
            ss_hi = jnp.sum(jnp.where(lo_half, 0.0, sq), axis=-1, keepdims=True)
            inv = jnp.where(lo_half, lax.rsqrt(ss_lo * (1.0 / MLSTM_HEAD_DIM) + EPS),
                            lax.rsqrt(ss_hi * (1.0 / MLSTM_HEAD_DIM) + EPS))
            mcol = D_POOL + D_SSD + pr * LANES
            mix_ref[r0:r0 + L, mcol:mcol + LANES] = hcat * inv * mnorm_ref[:, pr * LANES:(pr + 1) * LANES]
            if pr + 1 < MLSTM_HEADS // 2:
                yield

        yield

    n_stages = 3 + SSD_GROUPS + MLSTM_HEADS // 2
    gens = [chunk_stages(c) for c in range(TT // L)]
    for _ in range(n_stages):
        for gen in gens:
            next(gen)
            between()
    while pending:
        pending.pop(0)()

    o = _dot(mix_ref[...].astype(BF16), wout_ref[...])
    ms_o = jnp.mean(o * o, axis=-1, keepdims=True)
    o = o * lax.rsqrt(ms_o + EPS) * postn_ref[...]
    for c in range(TT // L):
        out_ref[0, c * L:(c + 1) * L, :] = x_rows(c * L) + o[c * L:(c + 1) * L]

    hph_ref[...] = hp_ref[TT - n_hph:TT, :]
    for lv, b in ps_carried:
        cs_ = slice(b * LANES, (b + 1) * LANES)
        psh_ref[lv, :, cs_] = ps_ref[lv, TT - n_psh:TT, cs_]


def ffn_kernel(layer, to_time_order, x_ref, wup_hbm, wdn_hbm, nw_ref, cw_ref, cb_ref, postn_ref, out_ref,
               wup_ref, wdn_ref, stage_ref, stage_dn_ref, stage_sem, halo_ref, a_ref, *out_scratch):
    TT = x_ref.shape[1]
    FT = FFN_FT
    L = CHUNK
    n_c = TT // L
    n_j = D_FF // FT
    n_tail = (FFN_CONV - 1) * SUBLANES
    b = pl.program_id(0)
    i = pl.program_id(1)
    step = b * pl.num_programs(1) + i
    n_steps = pl.num_programs(0) * pl.num_programs(1)

    @pl.when(i == 0)
    def _():
        halo_ref[...] = jnp.zeros(halo_ref.shape, F32)

    if to_time_order:
        obuf_ref, out_sem = out_scratch
        slot = step % 2

        def out_copies(sl):
            return [pltpu.make_async_copy(obuf_ref.at[sl, :, :, s, :], out_ref.at[b, pl.ds(i * n_c, n_c), s],
                                          out_sem.at[sl]) for s in range(SUBLANES)]

        @pl.when(step >= 2)
        def _():
            for cp in out_copies(slot):
                cp.wait()

    @pl.when(step == 0)
    def _():
        _stage_weights(_col_blocks(wup_hbm.at[layer], 0, wup_ref, 0, 2 * D_FF, STAGE_COLS),
                       stage_ref, stage_sem)
        _stage_weights([(wdn_hbm.at[layer, r:r + STAGE_COLS, :], wdn_ref.at[r:r + STAGE_COLS, :])
                        for r in range(0, D_FF, STAGE_COLS)], stage_dn_ref, stage_sem)

    def tile():
        x = x_ref[0]
        ms = jnp.mean(x * x, axis=-1, keepdims=True)
        h = (x * lax.rsqrt(ms + EPS) * nw_ref[...]).astype(BF16)

        def conv_cols(col):
            u = _dot(h, wup_ref[:, col:col + FT])
            outs = []
            for c in range(n_c):
                cur = u[c * L:(c + 1) * L]
                prev_tail = halo_ref[:, col:col + FT] if c == 0 else u[c * L - n_tail:c * L]
                outs.append(_causal_conv(prev_tail, cur, cw_ref, cb_ref[:, col:col + FT], col, FT, FFN_CONV))
            halo_ref[:, col:col + FT] = u[TT - n_tail:TT]
            return outs

        for j in range(n_j):
            gts = conv_cols(j * FT)
            vals = conv_cols(D_FF + j * FT)
            for c in range(n_c):
                gt = gts[c]
                gelu = 0.5 * gt * (1.0 + jnp.tanh(math.sqrt(2.0 / math.pi) * (gt + 0.044715 * (gt * gt * gt))))
                a_ref[c * L:(c + 1) * L, j * FT:(j + 1) * FT] = (gelu * vals[c]).astype(BF16)

        f = _dot(a_ref[...], wdn_ref[...])
        ms_f = jnp.mean(f * f, axis=-1, keepdims=True)
        res = x_ref[0] + f * lax.rsqrt(ms_f + EPS) * postn_ref[...]
        if to_time_order:
            obuf_ref[slot] = res.reshape(n_c, VROWS, SUBLANES, res.shape[1])
            for cp in out_copies(slot):
                cp.start()
        else:
            out_ref[0] = res

    tile()

    if to_time_order:
        @pl.when(step == n_steps - 1)
        def _():
            for cp in out_copies(slot):
                cp.wait()

        @pl.when((step == n_steps - 1) & (n_steps >= 2))
        def _():
            for cp in out_copies(1 - slot):
                cp.wait()


def _const_spec(shape):
    nd = len(shape)
    return pl.BlockSpec(shape, lambda b, i: (0,) * nd, pipeline_mode=pl.Buffered(1))


def _mix_layer(layer, from_time_order, x, wout, win, nw, gbias, alog, xcw, xcb, qcw, qcb, poolw, poolb, pools,
               dskip, snorm, mnorm, postn):
    B, T, D = x.shape
    TT = MIX_TT
    weights = (wout,)
    assert x.dtype == F32 and wout.dtype == F32 and win.dtype == BF16
    consts = (nw, gbias, alog, xcw, xcb, qcw, qcb, poolw, poolb, pools, dskip, snorm, mnorm, postn)
    win_spec = pl.BlockSpec((None,) + win.shape[1:], lambda b, i: (layer, 0, 0), pipeline_mode=pl.Buffered(1))
    n_t = T // TT
    cur_spec = pl.BlockSpec((1, TT, D), lambda b, i: (b, jnp.minimum(i, n_t - 1), 0))
    prev_spec = pl.BlockSpec((1, TT, D), lambda b, i: (b, jnp.maximum(i - 1, 0), 0))
    max_conv_tail = (max(SSD_CONV, MLSTM_CONV) - 1) * SUBLANES
    max_pool_tail = (POOL_WINDOWS[-1] // 2) * SUBLANES
    if from_time_order:
        x_in = x.reshape(B, T // CHUNK, SUBLANES, VROWS, D)
        x_specs = [pl.BlockSpec(memory_space=pltpu.HBM)] * 2
        in_scratch = [pltpu.VMEM((3, TT // CHUNK, VROWS, SUBLANES, D), F32),
                      pltpu.SemaphoreType.DMA((3,))]
    else:
        x_in, x_specs, in_scratch = x, [cur_spec, prev_spec], []
    return pl.pallas_call(
        functools.partial(mix_kernel, layer, from_time_order, n_t),
        grid=(B, n_t + 1),
        in_specs=(x_specs + [pl.BlockSpec(memory_space=pltpu.HBM) for _ in weights] + [win_spec]
                  + [_const_spec(c.shape) for c in consts]),
        out_specs=prev_spec,
        out_shape=jax.ShapeDtypeStruct(x.shape, x.dtype),
        scratch_shapes=[
            pltpu.VMEM((D_MODEL, N_IN_COLS), BF16),
            pltpu.VMEM((D_MODEL, D_MODEL), BF16),
            pltpu.VMEM((STAGE_SLOTS, D_MODEL, STAGE_COLS), F32),
            pltpu.SemaphoreType.DMA((STAGE_SLOTS,)),
            pltpu.VMEM((TT, D_MODEL), BF16),
            pltpu.VMEM((2, TT, N_HALO_COLS), F32),
            pltpu.VMEM((2, TT, N_REST_COLS), F32),
            pltpu.VMEM((max_conv_tail, N_HALO_COLS), F32),
            pltpu.VMEM((3, max_pool_tail, D_POOL), F32),
            pltpu.VMEM((TT, D_SSD_XBC + 2 * D_MLSTM), F32),
            pltpu.VMEM((TT, D_MODEL), F32),
            pltpu.VMEM((3, TT, D_POOL), F32),
            pltpu.VMEM((SSD_GROUPS, SSD_STATE, 4 * SSD_HEAD_DIM), F32),
            pltpu.VMEM((MLSTM_HEADS // 2, LANES, 2 * LANES), F32),
            pltpu.VMEM((SUBLANES, LANES), F32),
        ] + in_scratch,
        compiler_params=pltpu.CompilerParams(
            dimension_semantics=("arbitrary", "arbitrary"), vmem_limit_bytes=VMEM_LIMIT),
        name="mix_layer",
    )(x_in, x_in, *weights, win, *consts)


def _ffn_layer(layer, to_time_order, x, wup, wdn, nw, cw, cb, postn):
    B, T, D = x.shape
    TT = FFN_TT
    weights = (wup, wdn)
    assert x.dtype == F32 and wup.dtype == F32 and wdn.dtype == F32
    consts = (nw, cw, cb, postn)
    x_spec = pl.BlockSpec((1, TT, D), lambda b, i: (b, i, 0))
    if to_time_order:
        out_spec = pl.BlockSpec(memory_space=pltpu.HBM)
        out_shape = jax.ShapeDtypeStruct((B, T // CHUNK, SUBLANES, VROWS, D), x.dtype)
        out_scratch = [pltpu.VMEM((2, TT // CHUNK, VROWS, SUBLANES, D), F32),
                       pltpu.SemaphoreType.DMA((2,))]
    else:
        out_spec, out_shape, out_scratch = x_spec, jax.ShapeDtypeStruct(x.shape, x.dtype), []
    return pl.pallas_call(
        functools.partial(ffn_kernel, layer, to_time_order),
        grid=(B, T // TT),
        in_specs=([x_spec] + [pl.BlockSpec(memory_space=pltpu.HBM) for _ in weights]
                  + [_const_spec(c.shape) for c in consts]),
        out_specs=out_spec,
        out_shape=out_shape,
        scratch_shapes=[
            pltpu.VMEM((D_MODEL, 2 * D_FF), BF16),
            pltpu.VMEM((D_FF, D_MODEL), BF16),
            pltpu.VMEM((STAGE_SLOTS, D_MODEL, STAGE_COLS), F32),
            pltpu.VMEM((STAGE_SLOTS, STAGE_COLS, D_MODEL), F32),
            pltpu.SemaphoreType.DMA((STAGE_SLOTS,)),
            pltpu.VMEM(((FFN_CONV - 1) * SUBLANES, 2 * D_FF), F32),
            pltpu.VMEM((TT, D_FF), BF16),
        ] + out_scratch,
        compiler_params=pltpu.CompilerParams(
            dimension_semantics=("arbitrary", "arbitrary"), vmem_limit_bytes=VMEM_LIMIT),
        name="ffn_layer",
    )(x, *weights, *consts).reshape(B, T, D)


def _row(v):
    return v.reshape(1, -1).astype(F32)


def _pad_lanes(v):
    return jnp.pad(v.astype(F32), (0, LANES - v.shape[0])).reshape(1, LANES)


def _prep_pool_w(w):
    out = jnp.zeros((D_POOL, D_POOL), F32)
    for g in range(len(POOL_WINDOWS)):
        s = g * POOL_GROUP_DIM
        out = lax.dynamic_update_slice(out, w[g].astype(F32), (s, s))
    return out.astype(BF16)


def kernel(x, pre_mix_norm, w_in, pool_w, pool_b, pool_scale, ssd_conv_w, ssd_conv_b, ssd_dt_bias, ssd_a_log, ssd_d, ssd_norm, mlstm_conv_w, mlstm_conv_b, mlstm_i_bias, mlstm_f_bias, mlstm_norm, w_out, post_mix_norm, pre_ffn_norm, ffn_w_up, ffn_conv_w, ffn_conv_b, ffn_w_down, post_ffn_norm):
    depth = w_in.shape[0]
    w_in_bf = w_in.astype(BF16)
    for l in range(depth):
        gbias = _pad_lanes(jnp.concatenate([ssd_dt_bias[l], mlstm_i_bias[l], mlstm_f_bias[l]]))
        x = _mix_layer(
            l, l == 0, x, w_out, w_in_bf, _row(pre_mix_norm[l]), gbias, _pad_lanes(ssd_a_log[l]),
            ssd_conv_w[l].astype(F32), _row(ssd_conv_b[l]), mlstm_conv_w[l].astype(F32), _row(mlstm_conv_b[l]),
            _prep_pool_w(pool_w[l]), _row(pool_b[l]), _row(pool_scale[l]),
            _row(jnp.repeat(ssd_d[l], SSD_HEAD_DIM)), _row(ssd_norm[l]), _row(mlstm_norm[l]),
            _row(post_mix_norm[l]))
        x = _ffn_layer(
            l, l == depth - 1, x, ffn_w_up, ffn_w_down, _row(pre_ffn_norm[l]), ffn_conv_w[l].astype(F32),
            _row(ffn_conv_b[l]), _row(post_ffn_norm[l]))
    return x
```

```python
import functools
import math

import jax
import jax.numpy as jnp
from jax import lax
from jax.experimental import pallas as pl
from jax.experimental.pallas import tpu as pltpu

F32 = jnp.float32
BF16 = jnp.bfloat16

D_MODEL = 1024
EPS = 1e-6

D_POOL = 256
POOL_GROUP_DIM = 64
POOL_WINDOWS = (2, 4, 8, 16)

D_SSD = 512
SSD_HEADS = 8
SSD_HEAD_DIM = 64
SSD_GROUPS = 2
SSD_STATE = 128
SSD_CONV = 4
D_SSD_XBC = D_SSD + 2 * SSD_GROUPS * SSD_STATE

D_MLSTM = 256
MLSTM_HEADS = 4
MLSTM_HEAD_DIM = 64
MLSTM_CONV = 4

D_FF = 2816
FFN_CONV = 3

CHUNK = 128
LANES = 128
SUBLANES = 8
VROWS = CHUNK // SUBLANES
HEAD_DIM = LANES // 2
assert SSD_HEAD_DIM == HEAD_DIM and MLSTM_HEAD_DIM == HEAD_DIM

C_POOL = 0
C_XBC = C_POOL + D_POOL
C_QK = C_XBC + D_SSD_XBC
N_HALO_COLS = C_QK + 2 * D_MLSTM
R_Z = 0
R_V = R_Z + D_SSD
R_O = R_V + D_MLSTM
R_G = R_O + D_MLSTM
N_REST_COLS = R_G + LANES
N_IN_COLS = N_HALO_COLS + N_REST_COLS
G_DT = 0
G_I = SSD_HEADS
G_F = G_I + MLSTM_HEADS

MIX_TT = 256
PROJ_COLS = 256
STAGE_COLS = 256
STAGE_SLOTS = 4
FFN_TT = 512
FFN_FT = 256
VMEM_LIMIT = 56 * 1024 * 1024


def _dot(a, b):
    return jnp.dot(a, b, preferred_element_type=F32)


def _sigmoid(x):
    return 0.5 * jnp.tanh(0.5 * x) + 0.5


def _silu(x):
    return x * _sigmoid(x)


def _split3(a):
    hi = a.astype(BF16)
    r = a - hi.astype(F32)
    mid = r.astype(BF16)
    lo = (r - mid.astype(F32)).astype(BF16)
    return hi, mid, lo


def _bcast_lane(a, j, shape):
    return jnp.broadcast_to(a[:, j:j + 1], shape)


def _pair_expand(a, h_even, shape, lo_half):
    return jnp.where(lo_half, _bcast_lane(a, h_even, shape), _bcast_lane(a, h_even + 1, shape))


def _stage_weights(blocks, stage_ref, sem_ref):
    n_slots = stage_ref.shape[0]

    def copy(n):
        src, dst = blocks[n]
        rows, cols = src.shape
        return pltpu.make_async_copy(src, stage_ref.at[n % n_slots, 0:rows, 0:cols], sem_ref.at[n % n_slots])

    for n in range(min(n_slots - 1, len(blocks))):
        copy(n).start()
    for n, (src, dst) in enumerate(blocks):
        if n + n_slots - 1 < len(blocks):
            copy(n + n_slots - 1).start()
        copy(n).wait()
        rows, cols = src.shape
        dst[...] = stage_ref[n % n_slots, 0:rows, 0:cols].astype(BF16)


def _col_blocks(src_ref, src0, dst_ref, dst0, ncols, step):
    return [(src_ref.at[:, src0 + k:src0 + min(k + step, ncols)],
             dst_ref.at[:, dst0 + k:dst0 + min(k + step, ncols)]) for k in range(0, ncols, step)]


def _tau(p):
    return (p % SUBLANES) * VROWS + p // SUBLANES


def _ext_rows(prev_tail, cur_tail):
    n = cur_tail.shape[0] // SUBLANES
    sub0 = lax.broadcasted_iota(jnp.int32, (SUBLANES, cur_tail.shape[1]), 0) == 0
    out = []
    for j in range(n):
        sl = slice(j * SUBLANES, (j + 1) * SUBLANES)
        out.append(jnp.where(sub0, pltpu.roll(prev_tail[sl], 1, 0), pltpu.roll(cur_tail[sl], 1, 0)))
    return out


def _shifted(ext, cur, k):
    if k == 0:
        return cur
    return jnp.concatenate(ext[len(ext) - k:] + [cur[0:CHUNK - SUBLANES * k]], axis=0)


def _causal_conv(prev_tail, cur, w_ref, b_row, wcol, ncols, k_taps):
    n = k_taps - 1
    ext = _ext_rows(prev_tail, cur[CHUNK - n * SUBLANES:CHUNK])
    acc = b_row
    for k in range(k_taps):
        acc = acc + _shifted(ext, cur, n - k) * w_ref[k:k + 1, wcol:wcol + ncols]
    return acc


def _relayout_w_in(wraw_ref, win_ref):
    c_z = D_POOL
    c_xbc = c_z + D_SSD
    c_dt = c_xbc + D_SSD_XBC
    n_tail = SSD_HEADS + 4 * D_MLSTM + 2 * MLSTM_HEADS
    o_qk = SSD_HEADS
    o_v = o_qk + 2 * D_MLSTM
    o_if = o_v + 2 * D_MLSTM
    for r in range(0, D_MODEL, CHUNK):
        rows = slice(r, r + CHUNK)
        win_ref[rows, C_POOL:C_POOL + D_POOL] = wraw_ref[rows, 0:D_POOL]
        win_ref[rows, C_XBC:C_XBC + D_SSD_XBC] = wraw_ref[rows, c_xbc:c_dt]
        win_ref[rows, N_HALO_COLS + R_Z:N_HALO_COLS + R_Z + D_SSD] = wraw_ref[rows, c_z:c_xbc]
        tail = wraw_ref[rows, c_dt:c_dt + n_tail]
        win_ref[rows, C_QK:C_QK + 2 * D_MLSTM] = tail[:, o_qk:o_v]
        win_ref[rows, N_HALO_COLS + R_V:N_HALO_COLS + R_V + 2 * D_MLSTM] = tail[:, o_v:o_if]
        pad = jnp.zeros((CHUNK, LANES - SSD_HEADS - 2 * MLSTM_HEADS), tail.dtype)
        win_ref[rows, N_HALO_COLS + R_G:N_HALO_COLS + R_G + LANES] = jnp.concatenate(
            [tail[:, 0:o_qk], tail[:, o_if:n_tail], pad], axis=1)


def mix_kernel(layer, from_time_order, n_tiles, xc_ref, xp_ref, wout_hbm, wraw_ref, nw_ref, gbias_ref, alog_ref,
               xcw_ref, xcb_ref, qcw_ref, qcb_ref,
               poolw_ref, poolb_ref, pools_ref, dskip_ref, snorm_ref, mnorm_ref, postn_ref,
               out_ref, win_ref, wout_ref, stage_ref, stage_sem,
               h_ref, hp_ref, rest_ref, hph_ref, psh_ref, act_ref, mix_ref, ps_ref,
               sstate_ref, mstate_ref, mm_ref, *in_scratch):
    TT = hp_ref.shape[1]
    L = CHUNK
    n_c = TT // L
    b = pl.program_id(0)
    i = pl.program_id(1)
    n_t = n_tiles

    if from_time_order:
        xbuf_ref, in_sem = in_scratch
        n_slots = xbuf_ref.shape[0]

        def in_copies(tile):
            sl = tile % n_slots
            return [pltpu.make_async_copy(xc_ref.at[b, pl.ds(tile * n_c, n_c), s], xbuf_ref.at[sl, :, :, s, :],
                                          in_sem.at[sl]) for s in range(SUBLANES)]

        @pl.when(i == 0)
        def _():
            for cp in in_copies(0):
                cp.start()

        @pl.when(i + 1 < n_t)
        def _():
            for cp in in_copies(i + 1):
                cp.start()

        @pl.when(i < n_t)
        def _():
            for cp in in_copies(i):
                cp.wait()

        slot_cur = jnp.minimum(i, n_t - 1) % n_slots
        slot_prev = jnp.maximum(i - 1, 0) % n_slots

        def x_cur():
            return xbuf_ref[slot_cur].reshape(TT, xbuf_ref.shape[-1])

        def x_rows(r0):
            return xbuf_ref[slot_prev, r0 // L].reshape(L, xbuf_ref.shape[-1])
    else:
        def x_cur():
            return xc_ref[0]

        def x_rows(r0):
            return xp_ref[0, r0:r0 + L, :]

    @pl.when((pl.program_id(0) == 0) & (i == 0))
    def _():
        _relayout_w_in(wraw_ref, win_ref)
        _stage_weights(_col_blocks(wout_hbm.at[layer], 0, wout_ref, 0, D_MODEL, STAGE_COLS),
                       stage_ref, stage_sem)

    @pl.when(i == 0)
    def _():
        hph_ref[...] = jnp.zeros(hph_ref.shape, F32)
        psh_ref[...] = jnp.zeros(psh_ref.shape, F32)
        sstate_ref[...] = jnp.zeros(sstate_ref.shape, F32)
        mstate_ref[...] = jnp.zeros(mstate_ref.shape, F32)
        mm_ref[...] = jnp.zeros(mm_ref.shape, F32)

    def step(slot_proj, slot_mix):
        pending = []
        if slot_proj is not None:
            x = x_cur()
            ms = jnp.mean(x * x, axis=-1, keepdims=True)
            h_ref[...] = (x * lax.rsqrt(ms + EPS) * nw_ref[...]).astype(BF16)

            def proj_piece(dst_ref, c0, c1, w0):
                def piece():
                    dst_ref[slot_proj, :, c0:c1] = _dot(h_ref[...], win_ref[:, w0 + c0:w0 + c1])
                return piece

            pending += [proj_piece(hp_ref, c0, min(c0 + PROJ_COLS, N_HALO_COLS), 0)
                        for c0 in range(0, N_HALO_COLS, PROJ_COLS)]
            pending += [proj_piece(rest_ref, c0, min(c0 + PROJ_COLS, N_REST_COLS), N_HALO_COLS)
                        for c0 in range(0, N_REST_COLS, PROJ_COLS)]
        if slot_mix is not None:
            _mixers(i - 1, x_rows, hp_ref.at[slot_mix], rest_ref.at[slot_mix], gbias_ref, alog_ref,
                    xcw_ref, xcb_ref, qcw_ref, qcb_ref, poolw_ref, poolb_ref, pools_ref, dskip_ref, snorm_ref,
                    mnorm_ref, wout_ref, postn_ref, out_ref, hph_ref, psh_ref, act_ref, mix_ref, ps_ref,
                    sstate_ref, mstate_ref, mm_ref, pending)
        while pending:
            pending.pop(0)()

    last_slot = (n_tiles - 1) % 2

    @pl.when(i == 0)
    def _():
        step(0, None)

    @pl.when((i > 0) & (i < n_tiles) & (i % 2 == 0))
    def _():
        step(0, 1)

    @pl.when((i > 0) & (i < n_tiles) & (i % 2 == 1))
    def _():
        step(1, 0)

    @pl.when(i == n_tiles)
    def _():
        step(None, last_slot)


def _mixers(tile, x_rows, hp_ref, rest_ref, gbias_ref, alog_ref, xcw_ref, xcb_ref, qcw_ref, qcb_ref,
            poolw_ref, poolb_ref, pools_ref, dskip_ref, snorm_ref, mnorm_ref, wout_ref, postn_ref,
            out_ref, hph_ref, psh_ref, act_ref, mix_ref, ps_ref, sstate_ref, mstate_ref, mm_ref, pending):
    def between():
        if pending:
            pending.pop(0)()

    TT = hp_ref.shape[0]
    L = CHUNK
    n_hph = hph_ref.shape[0]
    n_psh = psh_ref.shape[1]
    ps_carried = ((0, 0), (0, 1), (1, 1), (2, 1))

    def tail(cur_ref, halo, n_halo, r0, n_rows, cols):
        if r0 == 0:
            return halo[n_halo - n_rows:n_halo, cols]
        return cur_ref[r0 - n_rows:r0, cols]

    lane = lax.broadcasted_iota(jnp.int32, (L, LANES), 1)
    row = lax.broadcasted_iota(jnp.int32, (L, LANES), 0)
    lo_half = lane < HEAD_DIM
    tau_row = _tau(row)
    causal = _tau(lane) <= tau_row
    tril = jnp.where(causal, 1.0, 0.0).astype(BF16)
    lane_row = lax.broadcasted_iota(jnp.int32, (1, LANES), 1)
    a_row = -jnp.exp(alog_ref[...])
    neg_inf = -jnp.inf
    win_blk = [jnp.where(lo_half, float(POOL_WINDOWS[2 * b]), float(POOL_WINDOWS[2 * b + 1])) for b in range(2)]
    tau_f = tau_row.astype(F32)

    def chunk_stages(c):
        r0 = c * L

        def conv_block(col, cw_ref, cb_ref, wcol, k_taps):
            cols = slice(col, col + LANES)
            cur = hp_ref[r0:r0 + L, cols]
            prev_tail = tail(hp_ref, hph_ref, n_hph, r0, (k_taps - 1) * SUBLANES, cols)
            return _silu(_causal_conv(prev_tail, cur, cw_ref, cb_ref[:, wcol:wcol + LANES], wcol, LANES, k_taps))

        for blk in range(D_SSD_XBC // LANES):
            act_ref[r0:r0 + L, blk * LANES:(blk + 1) * LANES] = conv_block(
                C_XBC + blk * LANES, xcw_ref, xcb_ref, blk * LANES, SSD_CONV)
        for blk in range(2 * D_MLSTM // LANES):
            act_ref[r0:r0 + L, D_SSD_XBC + blk * LANES:D_SSD_XBC + (blk + 1) * LANES] = conv_block(
                C_QK + blk * LANES, qcw_ref, qcb_ref, blk * LANES, MLSTM_CONV)
        yield

        pos = tau_f + (tile * TT + r0 + 1).astype(F32)
        pooled_blocks = []
        for b in range(2):
            cs_ = slice(b * LANES, (b + 1) * LANES)
            u_cur = hp_ref[r0:r0 + L, cs_]
            lvl = u_cur
            sums = []
            for li, sh in enumerate((1, 2, 4, 8)):
                if li == 0:
                    prev_tail = tail(hp_ref, hph_ref, n_hph, r0, sh * SUBLANES, cs_)
                else:
                    prev_tail = tail(ps_ref.at[li - 1], psh_ref.at[li - 1], n_psh, r0, sh * SUBLANES, cs_)
                ext = _ext_rows(prev_tail, lvl[L - sh * SUBLANES:L])
                lvl = lvl + _shifted(ext, lvl, sh)
                sums.append(lvl)
                if (li, b) in ps_carried:
                    ps_ref[li, r0:r0 + L, cs_] = lvl
                if b == 0 and li == 1:
                    break
            wsum = jnp.where(lo_half, sums[0], sums[1]) if b == 0 else jnp.where(lo_half, sums[2], sums[3])
            pooled_blocks.append((wsum / jnp.minimum(pos, win_blk[b]) - u_cur).astype(BF16))
        mix_ref[r0:r0 + L, 0:D_POOL] = (
            (_dot(jnp.concatenate(pooled_blocks, axis=1), poolw_ref[...]) + poolb_ref[...]) * pools_ref[...])
        yield

        gb = rest_ref[r0:r0 + L, R_G:R_G + LANES] + gbias_ref[...]
        sp_term = jnp.log(1.0 + jnp.exp(-jnp.abs(gb)))
        dt = jnp.maximum(gb, 0.0) + sp_term
        log_f = jnp.minimum(gb, 0.0) - sp_term
        is_dt = lane < G_I
        is_f = (lane >= G_F) & (lane < G_F + MLSTM_HEADS)
        v_cum = jnp.where(is_dt, dt * a_row, jnp.where(is_f, log_f, 0.0))
        hi, mid, lo = _split3(v_cum)
        cs3 = _dot(tril, jnp.concatenate([hi, mid, lo], axis=1))
        cs = cs3[:, 0:LANES] + cs3[:, LANES:2 * LANES] + cs3[:, 2 * LANES:3 * LANES]
        u_gate = jnp.where(is_dt, dt, gb)
        cs_t = cs.T
        ug_t = u_gate.T
        cs_last = cs[L - 1:L, :]
        e_col = jnp.exp(cs)
        w_col = jnp.exp(cs_last - cs) * dt
        e_last = jnp.exp(cs_last)
        yield

        for g in range(SSD_GROUPS):
            b_t = act_ref[r0:r0 + L, D_SSD + g * SSD_STATE:D_SSD + (g + 1) * SSD_STATE].astype(BF16).T
            c_g = act_ref[r0:r0 + L, D_SSD + (SSD_GROUPS + g) * SSD_STATE:
                          D_SSD + (SSD_GROUPS + g + 1) * SSD_STATE].astype(BF16)
            state_g = sstate_ref[g]
            sc = _dot(c_g, jnp.concatenate([b_t, state_g.astype(BF16)], axis=1))
            s_g = sc[:, 0:L]
            y_off = sc[:, L:L + 4 * SSD_HEAD_DIM]
            xd_blocks = []
            cd_blocks = []
            for pr in range(2):
                h_even = 4 * g + 2 * pr
                col = h_even * SSD_HEAD_DIM
                xs = act_ref[r0:r0 + L, col:col + LANES]
                xs_b = xs.astype(BF16)
                m_pair = []
                for hh in range(2):
                    hd = h_even + hh
                    seg = jnp.where(causal, cs[:, hd:hd + 1] - cs_t[hd:hd + 1, :], neg_inf)
                    m_pair.append((s_g * (jnp.exp(seg) * ug_t[hd:hd + 1, :])).astype(BF16))
                yd = _dot(jnp.concatenate(m_pair, axis=0), xs_b)
                y_diag = jnp.where(lo_half, yd[0:L], yd[L:2 * L])
                e_exp = _pair_expand(e_col, h_even, (L, LANES), lo_half)
                w_exp = _pair_expand(w_col, h_even, (L, LANES), lo_half)
                y = (y_diag + y_off[:, pr * LANES:(pr + 1) * LANES] * e_exp
                     + xs * dskip_ref[:, col:col + LANES])
                z = rest_ref[r0:r0 + L, R_Z + col:R_Z + col + LANES]
                mix_ref[r0:r0 + L, D_POOL + col:D_POOL + col + LANES] = y * _silu(z)
                xd_blocks.append((xs * w_exp).astype(BF16))
                cd_blocks.append(_pair_expand(e_last, h_even, (1, LANES), lane_row < HEAD_DIM))
            xd_g = jnp.concatenate(xd_blocks, axis=1)
            cd_g = jnp.concatenate(cd_blocks, axis=1)
            new_states = _dot(b_t, xd_g)
            sstate_ref[g] = state_g * cd_g + new_states
            yield
        y_all = mix_ref[r0:r0 + L, D_POOL:D_POOL + D_SSD]
        ms_y = jnp.mean(y_all * y_all, axis=-1, keepdims=True)
        mix_ref[r0:r0 + L, D_POOL:D_POOL + D_SSD] = y_all * lax.rsqrt(ms_y + EPS) * snorm_ref[...]

        for pr in range(MLSTM_HEADS // 2):
            qcol = D_SSD_XBC + pr * LANES
            kcol = D_SSD_XBC + D_MLSTM + pr * LANES
            q_b = act_ref[r0:r0 + L, qcol:qcol + LANES] * (MLSTM_HEAD_DIM ** -0.5)
            k_t = act_ref[r0:r0 + L, kcol:kcol + LANES].T
            k_tb = k_t.astype(BF16)
            v_b = rest_ref[r0:r0 + L, R_V + pr * LANES:R_V + (pr + 1) * LANES]
            o_b = rest_ref[r0:r0 + L, R_O + pr * LANES:R_O + (pr + 1) * LANES]
            qms = [jnp.where(lo_half, q_b, 0.0), jnp.where(lo_half, 0.0, q_b)]
            s_pair = _dot(jnp.concatenate([q.astype(BF16) for q in qms], axis=0), k_tb)
            v_ones = jnp.concatenate([v_b, jnp.where(lane == 0, 1.0, 0.0)], axis=1).astype(BF16)
            state = mstate_ref[pr]
            lhs = []
            g_ts = []
            kw_rows = []
            decay_old = []
            decay_new = []
            for hh in range(2):
                hd = 2 * pr + hh
                s = s_pair[hh * L:(hh + 1) * L]
                b_row = cs_t[G_F + hd:G_F + hd + 1, :]
                r_row = ug_t[G_I + hd:G_I + hd + 1, :] - b_row
                b_last = jnp.sum(jnp.where(lane_row == L - 1, b_row, 0.0), axis=-1, keepdims=True)
                al_row = b_last + r_row
                m_loc = jnp.max(al_row, axis=-1, keepdims=True)
                prev_m_row = mm_ref[hd:hd + 1, :]
                rmask = jnp.where(causal, r_row, neg_inf)
                g_t = jnp.maximum(jnp.broadcast_to(jnp.max(rmask, axis=-1, keepdims=True), (L, LANES)),
                                  prev_m_row)
                p = (s * jnp.exp(rmask - g_t)).astype(BF16)
                q_inter = (qms[hh] * jnp.exp(prev_m_row - g_t)).astype(BF16)
                lhs.append(jnp.concatenate([p, q_inter], axis=1))
                g_ts.append(g_t)
                kw_rows.append(jnp.exp(al_row - m_loc))
                m_new = jnp.maximum(b_last + prev_m_row, m_loc)
                decay_old.append(jnp.exp(b_last + prev_m_row - m_new))
                decay_new.append(jnp.exp(m_loc - m_new))
                mm_ref[hd:hd + 1, :] = m_new
            rhs = jnp.concatenate([v_ones, state.astype(BF16)], axis=0)
            res_pair = _dot(jnp.concatenate(lhs, axis=0), rhs)
            hv = []
            for hh in range(2):
                hd = 2 * pr + hh
                res = res_pair[hh * L:(hh + 1) * L]
                den = jnp.maximum(jnp.abs(_bcast_lane(res, LANES, (L, LANES))),
                                  jnp.exp(-(_bcast_lane(cs, G_F + hd, (L, LANES)) + g_ts[hh])))
                hv.append(res[:, 0:LANES] / den)
            top = row < HEAD_DIM
            ktw = (k_t * jnp.where(top, kw_rows[0], kw_rows[1])).astype(BF16)
            c_loc = _dot(ktw, v_ones)
            own = top == lo_half
            d_old = jnp.where(top, decay_old[0], decay_old[1])
            d_new = jnp.where(top, decay_new[0], decay_new[1])
            mstate_ref[pr] = jnp.concatenate(
                [d_old * state[:, 0:LANES] + d_new * jnp.where(own, c_loc[:, 0:LANES], 0.0),
                 d_old * state[:, LANES:2 * LANES] + d_new * c_loc[:, LANES:2 * LANES]], axis=1)
            hcat = _sigmoid(o_b) * jnp.where(lo_half, hv[0], hv[1])
            sq = hcat * hcat
            ss_lo = jnp.sum(jnp.where(lo_half, sq, 0.0), axis=-1, keepdims=True)
            ss_hi = jnp.sum(jnp.where(lo_half, 0.0, sq), axis=-1, keepdims=True)
            inv = jnp.where(lo_half, lax.rsqrt(ss_lo * (1.0 / MLSTM_HEAD_DIM) + EPS),
                            lax.rsqrt(ss_hi * (1.0 / MLSTM_HEAD_DIM) + EPS))
            mcol = D_POOL + D_SSD + pr * LANES
            mix_ref[r0:r0 + L, mcol:mcol + LANES] = hcat * inv * mnorm_ref[:, pr * LANES:(pr + 1) * LANES]
            if pr + 1 < MLSTM_HEADS // 2:
                yield

        yield

    n_stages = 3 + SSD_GROUPS + MLSTM_HEADS // 2
    gens = [chunk_stages(c) for c in range(TT // L)]
    for _ in range(n_stages):
        for gen in gens:
            next(gen)
            between()
    while pending:
        pending.pop(0)()

    o = _dot(mix_ref[...].astype(BF16), wout_ref[...])
    ms_o = jnp.mean(o * o, axis=-1, keepdims=True)
    o = o * lax.rsqrt(ms_o + EPS) * postn_ref[...]
    for c in range(TT // L):
        out_ref[0, c * L:(c + 1) * L, :] = x_rows(c * L) + o[c * L:(c + 1) * L]

    hph_ref[...] = hp_ref[TT - n_hph:TT, :]
    for lv, b in ps_carried:
        cs_ = slice(b * LANES, (b + 1) * LANES)
        psh_ref[lv, :, cs_] = ps_ref[lv, TT - n_psh:TT, cs_]


def ffn_kernel(layer, to_time_order, x_ref, wup_hbm, wdn_hbm, nw_ref, cw_ref, cb_ref, postn_ref, out_ref,
               wup_ref, wdn_ref, stage_ref, stage_dn_ref, stage_sem, halo_ref, a_ref, *out_scratch):
    TT = x_ref.shape[1]
    FT = FFN_FT
    L = CHUNK
    n_c = TT // L
    n_j = D_FF // FT
    n_tail = (FFN_CONV - 1) * SUBLANES
    b = pl.program_id(0)
    i = pl.program_id(1)
    step = b * pl.num_programs(1) + i
    n_steps = pl.num_programs(0) * pl.num_programs(1)

    @pl.when(i == 0)
    def _():
        halo_ref[...] = jnp.zeros(halo_ref.shape, F32)

    if to_time_order:
        obuf_ref, out_sem = out_scratch
        slot = step % 2

        def out_copies(sl):
            return [pltpu.make_async_copy(obuf_ref.at[sl, :, :, s, :], out_ref.at[b, pl.ds(i * n_c, n_c), s],
                                          out_sem.at[sl]) for s in range(SUBLANES)]

        @pl.when(step >= 2)
        def _():
            for cp in out_copies(slot):
                cp.wait()

    @pl.when(step == 0)
    def _():
        _stage_weights(_col_blocks(wup_hbm.at[layer], 0, wup_ref, 0, 2 * D_FF, STAGE_COLS),
                       stage_ref, stage_sem)
        _stage_weights([(wdn_hbm.at[layer, r:r + STAGE_COLS, :], wdn_ref.at[r:r + STAGE_COLS, :])
                        for r in range(0, D_FF, STAGE_COLS)], stage_dn_ref, stage_sem)

    def tile():
        x = x_ref[0]
        ms = jnp.mean(x * x, axis=-1, keepdims=True)
        h = (x * lax.rsqrt(ms + EPS) * nw_ref[...]).astype(BF16)

        def conv_cols(col):
            u = _dot(h, wup_ref[:, col:col + FT])
            outs = []
            for c in range(n_c):
                cur = u[c * L:(c + 1) * L]
                prev_tail = halo_ref[:, col:col + FT] if c == 0 else u[c * L - n_tail:c * L]
                outs.append(_causal_conv(prev_tail, cur, cw_ref, cb_ref[:, col:col + FT], col, FT, FFN_CONV))
            halo_ref[:, col:col + FT] = u[TT - n_tail:TT]
            return outs

        for j in range(n_j):
            gts = conv_cols(j * FT)
            vals = conv_cols(D_FF + j * FT)
            for c in range(n_c):
                gt = gts[c]
                gelu = 0.5 * gt * (1.0 + jnp.tanh(math.sqrt(2.0 / math.pi) * (gt + 0.044715 * (gt * gt * gt))))
                a_ref[c * L:(c + 1) * L, j * FT:(j + 1) * FT] = (gelu * vals[c]).astype(BF16)

        f = _dot(a_ref[...], wdn_ref[...])
        ms_f = jnp.mean(f * f, axis=-1, keepdims=True)
        res = x_ref[0] + f * lax.rsqrt(ms_f + EPS) * postn_ref[...]
        if to_time_order:
            obuf_ref[slot] = res.reshape(n_c, VROWS, SUBLANES, res.shape[1])
            for cp in out_copies(slot):
                cp.start()
        else:
            out_ref[0] = res

    tile()

    if to_time_order:
        @pl.when(step == n_steps - 1)
        def _():
            for cp in out_copies(slot):
                cp.wait()

        @pl.when((step == n_steps - 1) & (n_steps >= 2))
        def _():
            for cp in out_copies(1 - slot):
                cp.wait()


def _const_spec(shape):
    nd = len(shape)
    return pl.BlockSpec(shape, lambda b, i: (0,) * nd, pipeline_mode=pl.Buffered(1))


def _mix_layer(layer, from_time_order, x, wout, win, nw, gbias, alog, xcw, xcb, qcw, qcb, poolw, poolb, pools,
               dskip, snorm, mnorm, postn):
    B, T, D = x.shape
    TT = MIX_TT
    weights = (wout,)
    assert x.dtype == F32 and wout.dtype == F32 and win.dtype == BF16
    consts = (nw, gbias, alog, xcw, xcb, qcw, qcb, poolw, poolb, pools, dskip, snorm, mnorm, postn)
    win_spec = pl.BlockSpec((None,) + win.shape[1:], lambda b, i: (layer, 0, 0), pipeline_mode=pl.Buffered(1))
    n_t = T // TT
    cur_spec = pl.BlockSpec((1, TT, D), lambda b, i: (b, jnp.minimum(i, n_t - 1), 0))
    prev_spec = pl.BlockSpec((1, TT, D), lambda b, i: (b, jnp.maximum(i - 1, 0), 0))
    max_conv_tail = (max(SSD_CONV, MLSTM_CONV) - 1) * SUBLANES
    max_pool_tail = (POOL_WINDOWS[-1] // 2) * SUBLANES
    if from_time_order:
        x_in = x.reshape(B, T // CHUNK, SUBLANES, VROWS, D)
        x_specs = [pl.BlockSpec(memory_space=pltpu.HBM)] * 2
        in_scratch = [pltpu.VMEM((3, TT // CHUNK, VROWS, SUBLANES, D), F32),
                      pltpu.SemaphoreType.DMA((3,))]
    else:
        x_in, x_specs, in_scratch = x, [cur_spec, prev_spec], []
    return pl.pallas_call(
        functools.partial(mix_kernel, layer, from_time_order, n_t),
        grid=(B, n_t + 1),
        in_specs=(x_specs + [pl.BlockSpec(memory_space=pltpu.HBM) for _ in weights] + [win_spec]
                  + [_const_spec(c.shape) for c in consts]),
        out_specs=prev_spec,
        out_shape=jax.ShapeDtypeStruct(x.shape, x.dtype),
        scratch_shapes=[
            pltpu.VMEM((D_MODEL, N_IN_COLS), BF16),
            pltpu.VMEM((D_MODEL, D_MODEL), BF16),
            pltpu.VMEM((STAGE_SLOTS, D_MODEL, STAGE_COLS), F32),
            pltpu.SemaphoreType.DMA((STAGE_SLOTS,)),
            pltpu.VMEM((TT, D_MODEL), BF16),
            pltpu.VMEM((2, TT, N_HALO_COLS), F32),
            pltpu.VMEM((2, TT, N_REST_COLS), F32),
            pltpu.VMEM((max_conv_tail, N_HALO_COLS), F32),
            pltpu.VMEM((3, max_pool_tail, D_POOL), F32),
            pltpu.VMEM((TT, D_SSD_XBC + 2 * D_MLSTM), F32),
            pltpu.VMEM((TT, D_MODEL), F32),
            pltpu.VMEM((3, TT, D_POOL), F32),
            pltpu.VMEM((SSD_GROUPS, SSD_STATE, 4 * SSD_HEAD_DIM), F32),
            pltpu.VMEM((MLSTM_HEADS // 2, LANES, 2 * LANES), F32),
            pltpu.VMEM((SUBLANES, LANES), F32),
        ] + in_scratch,
        compiler_params=pltpu.CompilerParams(
            dimension_semantics=("arbitrary", "arbitrary"), vmem_limit_bytes=VMEM_LIMIT),
        name="mix_layer",
    )(x_in, x_in, *weights, win, *consts)


def _ffn_layer(layer, to_time_order, x, wup, wdn, nw, cw, cb, postn):
    B, T, D = x.shape
    TT = FFN_TT
    weights = (wup, wdn)
    assert x.dtype == F32 and wup.dtype == F32 and wdn.dtype == F32
    consts = (nw, cw, cb, postn)
    x_spec = pl.BlockSpec((1, TT, D), lambda b, i: (b, i, 0))
    if to_time_order:
        out_spec = pl.BlockSpec(memory_space=pltpu.HBM)
        out_shape = jax.ShapeDtypeStruct((B, T // CHUNK, SUBLANES, VROWS, D), x.dtype)
        out_scratch = [pltpu.VMEM((2, TT // CHUNK, VROWS, SUBLANES, D), F32),
                       pltpu.SemaphoreType.DMA((2,))]
    else:
        out_spec, out_shape, out_scratch = x_spec, jax.ShapeDtypeStruct(x.shape, x.dtype), []
    return pl.pallas_call(
        functools.partial(ffn_kernel, layer, to_time_order),
        grid=(B, T // TT),
        in_specs=([x_spec] + [pl.BlockSpec(memory_space=pltpu.HBM) for _ in weights]
                  + [_const_spec(c.shape) for c in consts]),
        out_specs=out_spec,
        out_shape=out_shape,
        scratch_shapes=[
            pltpu.VMEM((D_MODEL, 2 * D_FF), BF16),
            pltpu.VMEM((D_FF, D_MODEL), BF16),
            pltpu.VMEM((STAGE_SLOTS, D_MODEL, STAGE_COLS), F32),
            pltpu.VMEM((STAGE_SLOTS, STAGE_COLS, D_MODEL), F32),
            pltpu.SemaphoreType.DMA((STAGE_SLOTS,)),
            pltpu.VMEM(((FFN_CONV - 1) * SUBLANES, 2 * D_FF), F32),
            pltpu.VMEM((TT, D_FF), BF16),
        ] + out_scratch,
        compiler_params=pltpu.CompilerParams(
            dimension_semantics=("arbitrary", "arbitrary"), vmem_limit_bytes=VMEM_LIMIT),
        name="ffn_layer",
    )(x, *weights, *consts).reshape(B, T, D)


def _row(v):
    return v.reshape(1, -1).astype(F32)


def _pad_lanes(v):
    return jnp.pad(v.astype(F32), (0, LANES - v.shape[0])).reshape(1, LANES)


def _prep_pool_w(w):
    out = jnp.zeros((D_POOL, D_POOL), F32)
    for g in range(len(POOL_WINDOWS)):
        s = g * POOL_GROUP_DIM
        out = lax.dynamic_update_slice(out, w[g].astype(F32), (s, s))
    return out.astype(BF16)


def kernel(x, pre_mix_norm, w_in, pool_w, pool_b, pool_scale, ssd_conv_w, ssd_conv_b, ssd_dt_bias, ssd_a_log, ssd_d, ssd_norm, mlstm_conv_w, mlstm_conv_b, mlstm_i_bias, mlstm_f_bias, mlstm_norm, w_out, post_mix_norm, pre_ffn_norm, ffn_w_up, ffn_conv_w, ffn_conv_b, ffn_w_down, post_ffn_norm):
    depth = w_in.shape[0]
    w_in_bf = w_in.astype(BF16)
    for l in range(depth):
        gbias = _pad_lanes(jnp.concatenate([ssd_dt_bias[l], mlstm_i_bias[l], mlstm_f_bias[l]]))
        x = _mix_layer(
            l, l == 0, x, w_out, w_in_bf, _row(pre_mix_norm[l]), gbias, _pad_lanes(ssd_a_log[l]),
            ssd_conv_w[l].astype(F32), _row(ssd_conv_b[l]), mlstm_conv_w[l].astype(F32), _row(mlstm_conv_b[l]),
            _prep_pool_w(pool_w[l]), _row(pool_b[l]), _row(pool_scale[l]),
            _row(jnp.repeat(ssd_d[l], SSD_HEAD_DIM)), _row(ssd_norm[l]), _row(mlstm_norm[l]),
            _row(post_mix_norm[l]))
        x = _ffn_layer(
            l, l == depth - 1, x, ffn_w_up, ffn_w_down, _row(pre_ffn_norm[l]), ffn_conv_w[l].astype(F32),
            _row(ffn_conv_b[l]), _row(post_ffn_norm[l]))
    return x
```

```python
import functools
import math

import jax
import jax.numpy as jnp
from jax import lax
from jax.experimental import pallas as pl
from jax.experimental.pallas import tpu as pltpu

F32 = jnp.float32
BF16 = jnp.bfloat16

D_MODEL = 1024
EPS = 1e-6

D_POOL = 256
POOL_GROUP_DIM = 64
POOL_WINDOWS = (2, 4, 8, 16)

D_SSD = 512
SSD_HEADS = 8
SSD_HEAD_DIM = 64
SSD_GROUPS = 2
SSD_STATE = 128
SSD_CONV = 4
D_SSD_XBC = D_SSD + 2 * SSD_GROUPS * SSD_STATE

D_MLSTM = 256
MLSTM_HEADS = 4
MLSTM_HEAD_DIM = 64
MLSTM_CONV = 4

D_FF = 2816
FFN_CONV = 3

CHUNK = 128
LANES = 128
SUBLANES = 8
VROWS = CHUNK // SUBLANES
HEAD_DIM = LANES // 2
assert SSD_HEAD_DIM == HEAD_DIM and MLSTM_HEAD_DIM == HEAD_DIM

C_POOL = 0
C_XBC = C_POOL + D_POOL
C_QK = C_XBC + D_SSD_XBC
N_HALO_COLS = C_QK + 2 * D_MLSTM
R_Z = 0
R_V = R_Z + D_SSD
R_O = R_V + D_MLSTM
R_G = R_O + D_MLSTM
N_REST_COLS = R_G + LANES
N_IN_COLS = N_HALO_COLS + N_REST_COLS
G_DT = 0
G_I = SSD_HEADS
G_F = G_I + MLSTM_HEADS

MIX_TT = 256
PROJ_COLS = 256
STAGE_COLS = 256
STAGE_SLOTS = 4
FFN_TT = 512
FFN_FT = 256
VMEM_LIMIT = 56 * 1024 * 1024


def _dot(a, b):
    return jnp.dot(a, b, preferred_element_type=F32)


def _sigmoid(x):
    return 0.5 * jnp.tanh(0.5 * x) + 0.5


def _silu(x):
    return x * _sigmoid(x)


def _split3(a):
    hi = a.astype(BF16)
    r = a - hi.astype(F32)
    mid = r.astype(BF16)
    lo = (r - mid.astype(F32)).astype(BF16)
    return hi, mid, lo


def _bcast_lane(a, j, shape):
    return jnp.broadcast_to(a[:, j:j + 1], shape)


def _pair_expand(a, h_even, shape, lo_half):
    return jnp.where(lo_half, _bcast_lane(a, h_even, shape), _bcast_lane(a, h_even + 1, shape))


def _stage_weights(blocks, stage_ref, sem_ref):
    n_slots = stage_ref.shape[0]

    def copy(n):
        src, dst = blocks[n]
        rows, cols = src.shape
        return pltpu.make_async_copy(src, stage_ref.at[n % n_slots, 0:rows, 0:cols], sem_ref.at[n % n_slots])

    for n in range(min(n_slots - 1, len(blocks))):
        copy(n).start()
    for n, (src, dst) in enumerate(blocks):
        if n + n_slots - 1 < len(blocks):
            copy(n + n_slots - 1).start()
        copy(n).wait()
        rows, cols = src.shape
        dst[...] = stage_ref[n % n_slots, 0:rows, 0:cols].astype(BF16)


def _col_blocks(src_ref, src0, dst_ref, dst0, ncols, step):
    return [(src_ref.at[:, src0 + k:src0 + min(k + step, ncols)],
             dst_ref.at[:, dst0 + k:dst0 + min(k + step, ncols)]) for k in range(0, ncols, step)]


def _tau(p):
    return (p % SUBLANES) * VROWS + p // SUBLANES


def _ext_rows(prev_tail, cur_tail):
    n = cur_tail.shape[0] // SUBLANES
    sub0 = lax.broadcasted_iota(jnp.int32, (SUBLANES, cur_tail.shape[1]), 0) == 0
    out = []
    for j in range(n):
        sl = slice(j * SUBLANES, (j + 1) * SUBLANES)
        out.append(jnp.where(sub0, pltpu.roll(prev_tail[sl], 1, 0), pltpu.roll(cur_tail[sl], 1, 0)))
    return out


def _shifted(ext, cur, k):
    if k == 0:
        return cur
    return jnp.concatenate(ext[len(ext) - k:] + [cur[0:CHUNK - SUBLANES * k]], axis=0)


def _causal_conv(prev_tail, cur, w_ref, b_row, wcol, ncols, k_taps):
    n = k_taps - 1
    ext = _ext_rows(prev_tail, cur[CHUNK - n * SUBLANES:CHUNK])
    acc = b_row
    for k in range(k_taps):
        acc = acc + _shifted(ext, cur, n - k) * w_ref[k:k + 1, wcol:wcol + ncols]
    return acc


def _relayout_w_in(wraw_ref, win_ref):
    c_z = D_POOL
    c_xbc = c_z + D_SSD
    c_dt = c_xbc + D_SSD_XBC
    n_tail = SSD_HEADS + 4 * D_MLSTM + 2 * MLSTM_HEADS
    o_qk = SSD_HEADS
    o_v = o_qk + 2 * D_MLSTM
    o_if = o_v + 2 * D_MLSTM
    for r in range(0, D_MODEL, CHUNK):
        rows = slice(r, r + CHUNK)
        win_ref[rows, C_POOL:C_POOL + D_POOL] = wraw_ref[rows, 0:D_POOL]
        win_ref[rows, C_XBC:C_XBC + D_SSD_XBC] = wraw_ref[rows, c_xbc:c_dt]
        win_ref[rows, N_HALO_COLS + R_Z:N_HALO_COLS + R_Z + D_SSD] = wraw_ref[rows, c_z:c_xbc]
        tail = wraw_ref[rows, c_dt:c_dt + n_tail]
        win_ref[rows, C_QK:C_QK + 2 * D_MLSTM] = tail[:, o_qk:o_v]
        win_ref[rows, N_HALO_COLS + R_V:N_HALO_COLS + R_V + 2 * D_MLSTM] = tail[:, o_v:o_if]
        pad = jnp.zeros((CHUNK, LANES - SSD_HEADS - 2 * MLSTM_HEADS), tail.dtype)
        win_ref[rows, N_HALO_COLS + R_G:N_HALO_COLS + R_G + LANES] = jnp.concatenate(
            [tail[:, 0:o_qk], tail[:, o_if:n_tail], pad], axis=1)


def mix_kernel(layer, from_time_order, n_tiles, xc_ref, xp_ref, wout_hbm, wraw_ref, nw_ref, gbias_ref, alog_ref,
               xcw_ref, xcb_ref, qcw_ref, qcb_ref,
               poolw_ref, poolb_ref, pools_ref, dskip_ref, snorm_ref, mnorm_ref, postn_ref,
               out_ref, win_ref, wout_ref, stage_ref, stage_sem,
               h_ref, hp_ref, rest_ref, hph_ref, psh_ref, act_ref, mix_ref, ps_ref,
               sstate_ref, mstate_ref, mm_ref, *in_scratch):
    TT = hp_ref.shape[1]
    L = CHUNK
    n_c = TT // L
    b = pl.program_id(0)
    i = pl.program_id(1)
    n_t = n_tiles

    if from_time_order:
        xbuf_ref, in_sem = in_scratch
        n_slots = xbuf_ref.shape[0]

        def in_copies(tile):
            sl = tile % n_slots
            return [pltpu.make_async_copy(xc_ref.at[b, pl.ds(tile * n_c, n_c), s], xbuf_ref.at[sl, :, :, s, :],
                                          in_sem.at[sl]) for s in range(SUBLANES)]

        @pl.when(i == 0)
        def _():
            for cp in in_copies(0):
                cp.start()

        @pl.when(i + 1 < n_t)
        def _():
            for cp in in_copies(i + 1):
                cp.start()

        @pl.when(i < n_t)
        def _():
            for cp in in_copies(i):
                cp.wait()

        slot_cur = jnp.minimum(i, n_t - 1) % n_slots
        slot_prev = jnp.maximum(i - 1, 0) % n_slots

        def x_cur():
            return xbuf_ref[slot_cur].reshape(TT, xbuf_ref.shape[-1])

        def x_rows(r0):
            return xbuf_ref[slot_prev, r0 // L].reshape(L, xbuf_ref.shape[-1])
    else:
        def x_cur():
            return xc_ref[0]

        def x_rows(r0):
            return xp_ref[0, r0:r0 + L, :]

    @pl.when((pl.program_id(0) == 0) & (i == 0))
    def _():
        _relayout_w_in(wraw_ref, win_ref)
        _stage_weights(_col_blocks(wout_hbm.at[layer], 0, wout_ref, 0, D_MODEL, STAGE_COLS),
                       stage_ref, stage_sem)

    @pl.when(i == 0)
    def _():
        hph_ref[...] = jnp.zeros(hph_ref.shape, F32)
        psh_ref[...] = jnp.zeros(psh_ref.shape, F32)
        sstate_ref[...] = jnp.zeros(sstate_ref.shape, F32)
        mstate_ref[...] = jnp.zeros(mstate_ref.shape, F32)
        mm_ref[...] = jnp.zeros(mm_ref.shape, F32)

    def step(slot_proj, slot_mix):
        pending = []
        if slot_proj is not None:
            x = x_cur()
            ms = jnp.mean(x * x, axis=-1, keepdims=True)
            h_ref[...] = (x * lax.rsqrt(ms + EPS) * nw_ref[...]).astype(BF16)

            def proj_piece(dst_ref, c0, c1, w0):
                def piece():
                    dst_ref[slot_proj, :, c0:c1] = _dot(h_ref[...], win_ref[:, w0 + c0:w0 + c1])
                return piece

            pending += [proj_piece(hp_ref, c0, min(c0 + PROJ_COLS, N_HALO_COLS), 0)
                        for c0 in range(0, N_HALO_COLS, PROJ_COLS)]
            pending += [proj_piece(rest_ref, c0, min(c0 + PROJ_COLS, N_REST_COLS), N_HALO_COLS)
                        for c0 in range(0, N_REST_COLS, PROJ_COLS)]
        if slot_mix is not None:
            _mixers(i - 1, x_rows, hp_ref.at[slot_mix], rest_ref.at[slot_mix], gbias_ref, alog_ref,
                    xcw_ref, xcb_ref, qcw_ref, qcb_ref, poolw_ref, poolb_ref, pools_ref, dskip_ref, snorm_ref,
                    mnorm_ref, wout_ref, postn_ref, out_ref, hph_ref, psh_ref, act_ref, mix_ref, ps_ref,
                    sstate_ref, mstate_ref, mm_ref, pending)
        while pending:
            pending.pop(0)()

    last_slot = (n_tiles - 1) % 2

    @pl.when(i == 0)
    def _():
        step(0, None)

    @pl.when((i > 0) & (i < n_tiles) & (i % 2 == 0))
    def _():
        step(0, 1)

    @pl.when((i > 0) & (i < n_tiles) & (i % 2 == 1))
    def _():
        step(1, 0)

    @pl.when(i == n_tiles)
    def _():
        step(None, last_slot)


def _mixers(tile, x_rows, hp_ref, rest_ref, gbias_ref, alog_ref, xcw_ref, xcb_ref, qcw_ref, qcb_ref,
            poolw_ref, poolb_ref, pools_ref, dskip_ref, snorm_ref, mnorm_ref, wout_ref, postn_ref,
            out_ref, hph_ref, psh_ref, act_ref, mix_ref, ps_ref, sstate_ref, mstate_ref, mm_ref, pending):
    def between():
        if pending:
            pending.pop(0)()

    TT = hp_ref.shape[0]
    L = CHUNK
    n_hph = hph_ref.shape[0]
    n_psh = psh_ref.shape[1]
    ps_carried = ((0, 0), (0, 1), (1, 1), (2, 1))

    def tail(cur_ref, halo, n_halo, r0, n_rows, cols):
        if r0 == 0:
            return halo[n_halo - n_rows:n_halo, cols]
        return cur_ref[r0 - n_rows:r0, cols]

    lane = lax.broadcasted_iota(jnp.int32, (L, LANES), 1)
    row = lax.broadcasted_iota(jnp.int32, (L, LANES), 0)
    lo_half = lane < HEAD_DIM
    tau_row = _tau(row)
    causal = _tau(lane) <= tau_row
    tril = jnp.where(causal, 1.0, 0.0).astype(BF16)
    lane_row = lax.broadcasted_iota(jnp.int32, (1, LANES), 1)
    a_row = -jnp.exp(alog_ref[...])
    neg_inf = -jnp.inf
    win_blk = [jnp.where(lo_half, float(POOL_WINDOWS[2 * b]), float(POOL_WINDOWS[2 * b + 1])) for b in range(2)]
    tau_f = tau_row.astype(F32)

    def chunk_stages(c):
        r0 = c * L

        def conv_block(col, cw_ref, cb_ref, wcol, k_taps):
            cols = slice(col, col + LANES)
            cur = hp_ref[r0:r0 + L, cols]
            prev_tail = tail(hp_ref, hph_ref, n_hph, r0, (k_taps - 1) * SUBLANES, cols)
            return _silu(_causal_conv(prev_tail, cur, cw_ref, cb_ref[:, wcol:wcol + LANES], wcol, LANES, k_taps))

        for blk in range(D_SSD_XBC // LANES):
            act_ref[r0:r0 + L, blk * LANES:(blk + 1) * LANES] = conv_block(
                C_XBC + blk * LANES, xcw_ref, xcb_ref, blk * LANES, SSD_CONV)
        for blk in range(2 * D_MLSTM // LANES):
            act_ref[r0:r0 + L, D_SSD_XBC + blk * LANES:D_SSD_XBC + (blk + 1) * LANES] = conv_block(
                C_QK + blk * LANES, qcw_ref, qcb_ref, blk * LANES, MLSTM_CONV)
        yield

        pos = tau_f + (tile * TT + r0 + 1).astype(F32)
        pooled_blocks = []
        for b in range(2):
            cs_ = slice(b * LANES, (b + 1) * LANES)
            u_cur = hp_ref[r0:r0 + L, cs_]
            lvl = u_cur
            sums = []
            for li, sh in enumerate((1, 2, 4, 8)):
                if li == 0:
                    prev_tail = tail(hp_ref, hph_ref, n_hph, r0, sh * SUBLANES, cs_)
                else:
                    prev_tail = tail(ps_ref.at[li - 1], psh_ref.at[li - 1], n_psh, r0, sh * SUBLANES, cs_)
                ext = _ext_rows(prev_tail, lvl[L - sh * SUBLANES:L])
                lvl = lvl + _shifted(ext, lvl, sh)
                sums.append(lvl)
                if (li, b) in ps_carried:
                    ps_ref[li, r0:r0 + L, cs_] = lvl
                if b == 0 and li == 1:
                    break
            wsum = jnp.where(lo_half, sums[0], sums[1]) if b == 0 else jnp.where(lo_half, sums[2], sums[3])
            pooled_blocks.append((wsum / jnp.minimum(pos, win_blk[b]) - u_cur).astype(BF16))
        mix_ref[r0:r0 + L, 0:D_POOL] = (
            (_dot(jnp.concatenate(pooled_blocks, axis=1), poolw_ref[...]) + poolb_ref[...]) * pools_ref[...])
        yield

        gb = rest_ref[r0:r0 + L, R_G:R_G + LANES] + gbias_ref[...]
        sp_term = jnp.log(1.0 + jnp.exp(-jnp.abs(gb)))
        dt = jnp.maximum(gb, 0.0) + sp_term
        log_f = jnp.minimum(gb, 0.0) - sp_term
        is_dt = lane < G_I
        is_f = (lane >= G_F) & (lane < G_F + MLSTM_HEADS)
        v_cum = jnp.where(is_dt, dt * a_row, jnp.where(is_f, log_f, 0.0))
        hi, mid, lo = _split3(v_cum)
        cs3 = _dot(tril, jnp.concatenate([hi, mid, lo], axis=1))
        cs = cs3[:, 0:LANES] + cs3[:, LANES:2 * LANES] + cs3[:, 2 * LANES:3 * LANES]
        u_gate = jnp.where(is_dt, dt, gb)
        cs_t = cs.T
        ug_t = u_gate.T
        cs_last = cs[L - 1:L, :]
        e_col = jnp.exp(cs)
        w_col = jnp.exp(cs_last - cs) * dt
        e_last = jnp.exp(cs_last)
        yield

        for g in range(SSD_GROUPS):
            b_t = act_ref[r0:r0 + L, D_SSD + g * SSD_STATE:D_SSD + (g + 1) * SSD_STATE].astype(BF16).T
            c_g = act_ref[r0:r0 + L, D_SSD + (SSD_GROUPS + g) * SSD_STATE:
                          D_SSD + (SSD_GROUPS + g + 1) * SSD_STATE].astype(BF16)
            state_g = sstate_ref[g]
            sc = _dot(c_g, jnp.concatenate([b_t, state_g.astype(BF16)], axis=1))
            s_g = sc[:, 0:L]
            y_off = sc[:, L:L + 4 * SSD_HEAD_DIM]
            xd_blocks = []
            cd_blocks = []
            for pr in range(2):
                h_even = 4 * g + 2 * pr
                col = h_even * SSD_HEAD_DIM
                xs = act_ref[r0:r0 + L, col:col + LANES]
                xs_b = xs.astype(BF16)
                m_pair = []
                for hh in range(2):
                    hd = h_even + hh
                    seg = jnp.where(causal, cs[:, hd:hd + 1] - cs_t[hd:hd + 1, :], neg_inf)
                    m_pair.append((s_g * (jnp.exp(seg) * ug_t[hd:hd + 1, :])).astype(BF16))
                yd = _dot(jnp.concatenate(m_pair, axis=0), xs_b)
                y_diag = jnp.where(lo_half, yd[0:L], yd[L:2 * L])
                e_exp = _pair_expand(e_col, h_even, (L, LANES), lo_half)
                w_exp = _pair_expand(w_col, h_even, (L, LANES), lo_half)
                y = (y_diag + y_off[:, pr * LANES:(pr + 1) * LANES] * e_exp
                     + xs * dskip_ref[:, col:col + LANES])
                z = rest_ref[r0:r0 + L, R_Z + col:R_Z + col + LANES]
                mix_ref[r0:r0 + L, D_POOL + col:D_POOL + col + LANES] = y * _silu(z)
                xd_blocks.append((xs * w_exp).astype(BF16))
                cd_blocks.append(_pair_expand(e_last, h_even, (1, LANES), lane_row < HEAD_DIM))
            xd_g = jnp.concatenate(xd_blocks, axis=1)
            cd_g = jnp.concatenate(cd_blocks, axis=1)
            new_states = _dot(b_t, xd_g)
            sstate_ref[g] = state_g * cd_g + new_states
            yield
        y_all = mix_ref[r0:r0 + L, D_POOL:D_POOL + D_SSD]
        ms_y = jnp.mean(y_all * y_all, axis=-1, keepdims=True)
        mix_ref[r0:r0 + L, D_POOL:D_POOL + D_SSD] = y_all * lax.rsqrt(ms_y + EPS) * snorm_ref[...]

        for pr in range(MLSTM_HEADS // 2):
            qcol = D_SSD_XBC + pr * LANES
            kcol = D_SSD_XBC + D_MLSTM + pr * LANES
            q_b = act_ref[r0:r0 + L, qcol:qcol + LANES] * (MLSTM_HEAD_DIM ** -0.5)
            k_t = act_ref[r0:r0 + L, kcol:kcol + LANES].T
            k_tb = k_t.astype(BF16)
            v_b = rest_ref[r0:r0 + L, R_V + pr * LANES:R_V + (pr + 1) * LANES]
            o_b = rest_ref[r0:r0 + L, R_O + pr * LANES:R_O + (pr + 1) * LANES]
            qms = [jnp.where(lo_half, q_b, 0.0), jnp.where(lo_half, 0.0, q_b)]
            s_pair = _dot(jnp.concatenate([q.astype(BF16) for q in qms], axis=0), k_tb)
            v_ones = jnp.concatenate([v_b, jnp.ones((L, LANES), F32)], axis=1).astype(BF16)
            state = mstate_ref[pr]
            lhs = []
            g_ts = []
            kw_rows = []
            decay_old = []
            decay_new = []
            for hh in range(2):
                hd = 2 * pr + hh
                s = s_pair[hh * L:(hh + 1) * L]
                b_row = cs_t[G_F + hd:G_F + hd + 1, :]
                r_row = ug_t[G_I + hd:G_I + hd + 1, :] - b_row
                b_last = jnp.sum(jnp.where(lane_row == L - 1, b_row, 0.0), axis=-1, keepdims=True)
                al_row = b_last + r_row
                m_loc = jnp.max(al_row, axis=-1, keepdims=True)
                prev_m_row = mm_ref[hd:hd + 1, :]
                rmask = jnp.where(causal, r_row, neg_inf)
                g_t = jnp.maximum(jnp.broadcast_to(jnp.max(rmask, axis=-1, keepdims=True), (L, LANES)),
                                  prev_m_row)
                p = (s * jnp.exp(rmask - g_t)).astype(BF16)
                q_inter = (qms[hh] * jnp.exp(prev_m_row - g_t)).astype(BF16)
                lhs.append(jnp.concatenate([p, q_inter], axis=1))
                g_ts.append(g_t)
                kw_rows.append(jnp.exp(al_row - m_loc))
                m_new = jnp.maximum(b_last + prev_m_row, m_loc)
                decay_old.append(jnp.exp(b_last + prev_m_row - m_new))
                decay_new.append(jnp.exp(m_loc - m_new))
                mm_ref[hd:hd + 1, :] = m_new
            rhs = jnp.concatenate([v_ones, state.astype(BF16)], axis=0)
            res_pair = _dot(jnp.concatenate(lhs, axis=0), rhs)
            hv = []
            for hh in range(2):
                hd = 2 * pr + hh
                res = res_pair[hh * L:(hh + 1) * L]
                den = jnp.maximum(jnp.abs(res[:, LANES:2 * LANES]),
                                  jnp.exp(-(_bcast_lane(cs, G_F + hd, (L, LANES)) + g_ts[hh])))
                hv.append(res[:, 0:LANES] / den)
            top = row < HEAD_DIM
            ktw = (k_t * jnp.where(top, kw_rows[0], kw_rows[1])).astype(BF16)
            c_loc = _dot(ktw, v_ones)
            own = top == lo_half
            d_old = jnp.where(top, decay_old[0], decay_old[1])
            d_new = jnp.where(top, decay_new[0], decay_new[1])
            mstate_ref[pr] = jnp.concatenate(
                [d_old * state[:, 0:LANES] + d_new * jnp.where(own, c_loc[:, 0:LANES], 0.0),
                 d_old * state[:, LANES:2 * LANES] + d_new * c_loc[:, LANES:2 * LANES]], axis=1)
            hcat = _sigmoid(o_b) * jnp.where(lo_half, hv[0], hv[1])
            sq = hcat * hcat
            ss_lo = jnp.sum(jnp.where(lo_half, sq, 0.0), axis=-1, keepdims=True)
            ss_hi = jnp.sum(jnp.where(lo_half, 0.0, sq), axis=-1, keepdims=True)
            inv = jnp.where(lo_half, lax.rsqrt(ss_lo * (1.0 / MLSTM_HEAD_DIM) + EPS),
                            lax.rsqrt(ss_hi * (1.0 / MLSTM_HEAD_DIM) + EPS))
            mcol = D_POOL + D_SSD + pr * LANES
            mix_ref[r0:r0 + L, mcol:mcol + LANES] = hcat * inv * mnorm_ref[:, pr * LANES:(pr + 1) * LANES]
            if pr + 1 < MLSTM_HEADS // 2:
                yield

        yield

    n_stages = 3 + SSD_GROUPS + MLSTM_HEADS // 2
    gens = [chunk_stages(c) for c in range(TT // L)]
    for _ in range(n_stages):
        for gen in gens:
            next(gen)
            between()
    while pending:
        pending.pop(0)()

    o = _dot(mix_ref[...].astype(BF16), wout_ref[...])
    ms_o = jnp.mean(o * o, axis=-1, keepdims=True)
    o = o * lax.rsqrt(ms_o + EPS) * postn_ref[...]
    for c in range(TT // L):
        out_ref[0, c * L:(c + 1) * L, :] = x_rows(c * L) + o[c * L:(c + 1) * L]

    hph_ref[...] = hp_ref[TT - n_hph:TT, :]
    for lv, b in ps_carried:
        cs_ = slice(b * LANES, (b + 1) * LANES)
        psh_ref[lv, :, cs_] = ps_ref[lv, TT - n_psh:TT, cs_]


def ffn_kernel(layer, to_time_order, x_ref, wup_hbm, wdn_hbm, nw_ref, cw_ref, cb_ref, postn_ref, out_ref,
               wup_ref, wdn_ref, stage_ref, stage_dn_ref, stage_sem, halo_ref, a_ref, *out_scratch):
    TT = x_ref.shape[1]
    FT = FFN_FT
    L = CHUNK
    n_c = TT // L
    n_j = D_FF // FT
    n_tail = (FFN_CONV - 1) * SUBLANES
    b = pl.program_id(0)
    i = pl.program_id(1)
    step = b * pl.num_programs(1) + i
    n_steps = pl.num_programs(0) * pl.num_programs(1)

    @pl.when(i == 0)
    def _():
        halo_ref[...] = jnp.zeros(halo_ref.shape, F32)

    if to_time_order:
        obuf_ref, out_sem = out_scratch
        slot = step % 2

        def out_copies(sl):
            return [pltpu.make_async_copy(obuf_ref.at[sl, :, :, s, :], out_ref.at[b, pl.ds(i * n_c, n_c), s],
                                          out_sem.at[sl]) for s in range(SUBLANES)]

        @pl.when(step >= 2)
        def _():
            for cp in out_copies(slot):
                cp.wait()

    @pl.when(step == 0)
    def _():
        _stage_weights(_col_blocks(wup_hbm.at[layer], 0, wup_ref, 0, 2 * D_FF, STAGE_COLS),
                       stage_ref, stage_sem)
        _stage_weights([(wdn_hbm.at[layer, r:r + STAGE_COLS, :], wdn_ref.at[r:r + STAGE_COLS, :])
                        for r in range(0, D_FF, STAGE_COLS)], stage_dn_ref, stage_sem)

    def tile():
        x = x_ref[0]
        ms = jnp.mean(x * x, axis=-1, keepdims=True)
        h = (x * lax.rsqrt(ms + EPS) * nw_ref[...]).astype(BF16)

        def conv_cols(col):
            u = _dot(h, wup_ref[:, col:col + FT])
            outs = []
            for c in range(n_c):
                cur = u[c * L:(c + 1) * L]
                prev_tail = halo_ref[:, col:col + FT] if c == 0 else u[c * L - n_tail:c * L]
                outs.append(_causal_conv(prev_tail, cur, cw_ref, cb_ref[:, col:col + FT], col, FT, FFN_CONV))
            halo_ref[:, col:col + FT] = u[TT - n_tail:TT]
            return outs

        for j in range(n_j):
            gts = conv_cols(j * FT)
            vals = conv_cols(D_FF + j * FT)
            for c in range(n_c):
                gt = gts[c]
                gelu = 0.5 * gt * (1.0 + jnp.tanh(math.sqrt(2.0 / math.pi) * (gt + 0.044715 * (gt * gt * gt))))
                a_ref[c * L:(c + 1) * L, j * FT:(j + 1) * FT] = (gelu * vals[c]).astype(BF16)

        f = _dot(a_ref[...], wdn_ref[...])
        ms_f = jnp.mean(f * f, axis=-1, keepdims=True)
        res = x_ref[0] + f * lax.rsqrt(ms_f + EPS) * postn_ref[...]
        if to_time_order:
            obuf_ref[slot] = res.reshape(n_c, VROWS, SUBLANES, res.shape[1])
            for cp in out_copies(slot):
                cp.start()
        else:
            out_ref[0] = res

    tile()

    if to_time_order:
        @pl.when(step == n_steps - 1)
        def _():
            for cp in out_copies(slot):
                cp.wait()

        @pl.when((step == n_steps - 1) & (n_steps >= 2))
        def _():
            for cp in out_copies(1 - slot):
                cp.wait()


def _const_spec(shape):
    nd = len(shape)
    return pl.BlockSpec(shape, lambda b, i: (0,) * nd, pipeline_mode=pl.Buffered(1))


def _mix_layer(layer, from_time_order, x, wout, win, nw, gbias, alog, xcw, xcb, qcw, qcb, poolw, poolb, pools,
               dskip, snorm, mnorm, postn):
    B, T, D = x.shape
    TT = MIX_TT
    weights = (wout,)
    assert x.dtype == F32 and wout.dtype == F32 and win.dtype == BF16
    consts = (nw, gbias, alog, xcw, xcb, qcw, qcb, poolw, poolb, pools, dskip, snorm, mnorm, postn)
    win_spec = pl.BlockSpec((None,) + win.shape[1:], lambda b, i: (layer, 0, 0), pipeline_mode=pl.Buffered(1))
    n_t = T // TT
    cur_spec = pl.BlockSpec((1, TT, D), lambda b, i: (b, jnp.minimum(i, n_t - 1), 0))
    prev_spec = pl.BlockSpec((1, TT, D), lambda b, i: (b, jnp.maximum(i - 1, 0), 0))
    max_conv_tail = (max(SSD_CONV, MLSTM_CONV) - 1) * SUBLANES
    max_pool_tail = (POOL_WINDOWS[-1] // 2) * SUBLANES
    if from_time_order:
        x_in = x.reshape(B, T // CHUNK, SUBLANES, VROWS, D)
        x_specs = [pl.BlockSpec(memory_space=pltpu.HBM)] * 2
        in_scratch = [pltpu.VMEM((3, TT // CHUNK, VROWS, SUBLANES, D), F32),
                      pltpu.SemaphoreType.DMA((3,))]
    else:
        x_in, x_specs, in_scratch = x, [cur_spec, prev_spec], []
    return pl.pallas_call(
        functools.partial(mix_kernel, layer, from_time_order, n_t),
        grid=(B, n_t + 1),
        in_specs=(x_specs + [pl.BlockSpec(memory_space=pltpu.HBM) for _ in weights] + [win_spec]
                  + [_const_spec(c.shape) for c in consts]),
        out_specs=prev_spec,
        out_shape=jax.ShapeDtypeStruct(x.shape, x.dtype),
        scratch_shapes=[
            pltpu.VMEM((D_MODEL, N_IN_COLS), BF16),
            pltpu.VMEM((D_MODEL, D_MODEL), BF16),
            pltpu.VMEM((STAGE_SLOTS, D_MODEL, STAGE_COLS), F32),
            pltpu.SemaphoreType.DMA((STAGE_SLOTS,)),
            pltpu.VMEM((TT, D_MODEL), BF16),
            pltpu.VMEM((2, TT, N_HALO_COLS), F32),
            pltpu.VMEM((2, TT, N_REST_COLS), F32),
            pltpu.VMEM((max_conv_tail, N_HALO_COLS), F32),
            pltpu.VMEM((3, max_pool_tail, D_POOL), F32),
            pltpu.VMEM((TT, D_SSD_XBC + 2 * D_MLSTM), F32),
            pltpu.VMEM((TT, D_MODEL), F32),
            pltpu.VMEM((3, TT, D_POOL), F32),
            pltpu.VMEM((SSD_GROUPS, SSD_STATE, 4 * SSD_HEAD_DIM), F32),
            pltpu.VMEM((MLSTM_HEADS // 2, LANES, 2 * LANES), F32),
            pltpu.VMEM((SUBLANES, LANES), F32),
        ] + in_scratch,
        compiler_params=pltpu.CompilerParams(
            dimension_semantics=("arbitrary", "arbitrary"), vmem_limit_bytes=VMEM_LIMIT),
        name="mix_layer",
    )(x_in, x_in, *weights, win, *consts)


def _ffn_layer(layer, to_time_order, x, wup, wdn, nw, cw, cb, postn):
    B, T, D = x.shape
    TT = FFN_TT
    weights = (wup, wdn)
    assert x.dtype == F32 and wup.dtype == F32 and wdn.dtype == F32
    consts = (nw, cw, cb, postn)
    x_spec = pl.BlockSpec((1, TT, D), lambda b, i: (b, i, 0))
    if to_time_order:
        out_spec = pl.BlockSpec(memory_space=pltpu.HBM)
        out_shape = jax.ShapeDtypeStruct((B, T // CHUNK, SUBLANES, VROWS, D), x.dtype)
        out_scratch = [pltpu.VMEM((2, TT // CHUNK, VROWS, SUBLANES, D), F32),
                       pltpu.SemaphoreType.DMA((2,))]
    else:
        out_spec, out_shape, out_scratch = x_spec, jax.ShapeDtypeStruct(x.shape, x.dtype), []
    return pl.pallas_call(
        functools.partial(ffn_kernel, layer, to_time_order),
        grid=(B, T // TT),
        in_specs=([x_spec] + [pl.BlockSpec(memory_space=pltpu.HBM) for _ in weights]
                  + [_const_spec(c.shape) for c in consts]),
        out_specs=out_spec,
        out_shape=out_shape,
        scratch_shapes=[
            pltpu.VMEM((D_MODEL, 2 * D_FF), BF16),
            pltpu.VMEM((D_FF, D_MODEL), BF16),
            pltpu.VMEM((STAGE_SLOTS, D_MODEL, STAGE_COLS), F32),
            pltpu.VMEM((STAGE_SLOTS, STAGE_COLS, D_MODEL), F32),
            pltpu.SemaphoreType.DMA((STAGE_SLOTS,)),
            pltpu.VMEM(((FFN_CONV - 1) * SUBLANES, 2 * D_FF), F32),
            pltpu.VMEM((TT, D_FF), BF16),
        ] + out_scratch,
        compiler_params=pltpu.CompilerParams(
            dimension_semantics=("arbitrary", "arbitrary"), vmem_limit_bytes=VMEM_LIMIT),
        name="ffn_layer",
    )(x, *weights, *consts).reshape(B, T, D)


def _row(v):
    return v.reshape(1, -1).astype(F32)


def _pad_lanes(v):
    return jnp.pad(v.astype(F32), (0, LANES - v.shape[0])).reshape(1, LANES)


def _prep_pool_w(w):
    out = jnp.zeros((D_POOL, D_POOL), F32)
    for g in range(len(POOL_WINDOWS)):
        s = g * POOL_GROUP_DIM
        out = lax.dynamic_update_slice(out, w[g].astype(F32), (s, s))
    return out.astype(BF16)


def kernel(x, pre_mix_norm, w_in, pool_w, pool_b, pool_scale, ssd_conv_w, ssd_conv_b, ssd_dt_bias, ssd_a_log, ssd_d, ssd_norm, mlstm_conv_w, mlstm_conv_b, mlstm_i_bias, mlstm_f_bias, mlstm_norm, w_out, post_mix_norm, pre_ffn_norm, ffn_w_up, ffn_conv_w, ffn_conv_b, ffn_w_down, post_ffn_norm):
    depth = w_in.shape[0]
    w_in_bf = w_in.astype(BF16)
    for l in range(depth):
        gbias = _pad_lanes(jnp.concatenate([ssd_dt_bias[l], mlstm_i_bias[l], mlstm_f_bias[l]]))
        x = _mix_layer(
            l, l == 0, x, w_out, w_in_bf, _row(pre_mix_norm[l]), gbias, _pad_lanes(ssd_a_log[l]),
            ssd_conv_w[l].astype(F32), _row(ssd_conv_b[l]), mlstm_conv_w[l].astype(F32), _row(mlstm_conv_b[l]),
            _prep_pool_w(pool_w[l]), _row(pool_b[l]), _row(pool_scale[l]),
            _row(jnp.repeat(ssd_d[l], SSD_HEAD_DIM)), _row(ssd_norm[l]), _row(mlstm_norm[l]),
            _row(post_mix_norm[l]))
        x = _ffn_layer(
            l, l == depth - 1, x, ffn_w_up, ffn_w_down, _row(pre_ffn_norm[l]), ffn_conv_w[l].astype(F32),
            _row(ffn_conv_b[l]), _row(post_ffn_norm[l]))
    return x
```

```python
import functools
import math

import jax
import jax.numpy as jnp
from jax import lax
from jax.experimental import pallas as pl
from jax.experimental.pallas import tpu as pltpu

F32 = jnp.float32
BF16 = jnp.bfloat16

D_MODEL = 1024
EPS = 1e-6

D_POOL = 256
POOL_GROUP_DIM = 64
POOL_WINDOWS = (2, 4, 8, 16)

D_SSD = 512
SSD_HEADS = 8
SSD_HEAD_DIM = 64
SSD_GROUPS = 2
SSD_STATE = 128
SSD_CONV = 4
D_SSD_XBC = D_SSD + 2 * SSD_GROUPS * SSD_STATE

D_MLSTM = 256
MLSTM_HEADS = 4
MLSTM_HEAD_DIM = 64
MLSTM_CONV = 4

D_FF = 2816
FFN_CONV = 3

CHUNK = 128
LANES = 128
SUBLANES = 8
VROWS = CHUNK // SUBLANES
HEAD_DIM = LANES // 2
assert SSD_HEAD_DIM == HEAD_DIM and MLSTM_HEAD_DIM == HEAD_DIM

C_POOL = 0
C_XBC = C_POOL + D_POOL
C_QK = C_XBC + D_SSD_XBC
N_HALO_COLS = C_QK + 2 * D_MLSTM
R_Z = 0
R_V = R_Z + D_SSD
R_O = R_V + D_MLSTM
R_G = R_O + D_MLSTM
N_REST_COLS = R_G + LANES
N_IN_COLS = N_HALO_COLS + N_REST_COLS
G_DT = 0
G_I = SSD_HEADS
G_F = G_I + MLSTM_HEADS

MIX_TT = 256
PROJ_COLS = 256
STAGE_COLS = 256
STAGE_SLOTS = 4
FFN_TT = 512
FFN_FT = 256
VMEM_LIMIT = 56 * 1024 * 1024


def _dot(a, b):
    return jnp.dot(a, b, preferred_element_type=F32)


def _sigmoid(x):
    return 0.5 * jnp.tanh(0.5 * x) + 0.5


def _silu(x):
    return x * _sigmoid(x)


def _split3(a):
    hi = a.astype(BF16)
    r = a - hi.astype(F32)
    mid = r.astype(BF16)
    lo = (r - mid.astype(F32)).astype(BF16)
    return hi, mid, lo


def _bcast_lane(a, j, shape):
    return jnp.broadcast_to(a[:, j:j + 1], shape)


def _pair_expand(a, h_even, shape, lo_half):
    return jnp.where(lo_half, _bcast_lane(a, h_even, shape), _bcast_lane(a, h_even + 1, shape))


def _stage_weights(blocks, stage_ref, sem_ref):
    n_slots = stage_ref.shape[0]

    def copy(n):
        src, dst = blocks[n]
        rows, cols = src.shape
        return pltpu.make_async_copy(src, stage_ref.at[n % n_slots, 0:rows, 0:cols], sem_ref.at[n % n_slots])

    for n in range(min(n_slots - 1, len(blocks))):
        copy(n).start()
    for n, (src, dst) in enumerate(blocks):
        if n + n_slots - 1 < len(blocks):
            copy(n + n_slots - 1).start()
        copy(n).wait()
        rows, cols = src.shape
        dst[...] = stage_ref[n % n_slots, 0:rows, 0:cols].astype(BF16)


def _col_blocks(src_ref, src0, dst_ref, dst0, ncols, step):
    return [(src_ref.at[:, src0 + k:src0 + min(k + step, ncols)],
             dst_ref.at[:, dst0 + k:dst0 + min(k + step, ncols)]) for k in range(0, ncols, step)]


def _tau(p):
    return (p % SUBLANES) * VROWS + p // SUBLANES


def _ext_rows(prev_tail, cur_tail):
    n = cur_tail.shape[0] // SUBLANES
    sub0 = lax.broadcasted_iota(jnp.int32, (SUBLANES, cur_tail.shape[1]), 0) == 0
    out = []
    for j in range(n):
        sl = slice(j * SUBLANES, (j + 1) * SUBLANES)
        out.append(jnp.where(sub0, pltpu.roll(prev_tail[sl], 1, 0), pltpu.roll(cur_tail[sl], 1, 0)))
    return out


def _shifted(ext, cur, k):
    if k == 0:
        return cur
    return jnp.concatenate(ext[len(ext) - k:] + [cur[0:CHUNK - SUBLANES * k]], axis=0)


def _causal_conv(prev_tail, cur, w_ref, b_row, wcol, ncols, k_taps):
    n = k_taps - 1
    ext = _ext_rows(prev_tail, cur[CHUNK - n * SUBLANES:CHUNK])
    acc = b_row
    for k in range(k_taps):
        acc = acc + _shifted(ext, cur, n - k) * w_ref[k:k + 1, wcol:wcol + ncols]
    return acc


def _relayout_w_in(wraw_ref, win_ref):
    c_z = D_POOL
    c_xbc = c_z + D_SSD
    c_dt = c_xbc + D_SSD_XBC
    n_tail = SSD_HEADS + 4 * D_MLSTM + 2 * MLSTM_HEADS
    o_qk = SSD_HEADS
    o_v = o_qk + 2 * D_MLSTM
    o_if = o_v + 2 * D_MLSTM
    for r in range(0, D_MODEL, CHUNK):
        rows = slice(r, r + CHUNK)
        win_ref[rows, C_POOL:C_POOL + D_POOL] = wraw_ref[rows, 0:D_POOL]
        win_ref[rows, C_XBC:C_XBC + D_SSD_XBC] = wraw_ref[rows, c_xbc:c_dt]
        win_ref[rows, N_HALO_COLS + R_Z:N_HALO_COLS + R_Z + D_SSD] = wraw_ref[rows, c_z:c_xbc]
        tail = wraw_ref[rows, c_dt:c_dt + n_tail]
        win_ref[rows, C_QK:C_QK + 2 * D_MLSTM] = tail[:, o_qk:o_v]
        win_ref[rows, N_HALO_COLS + R_V:N_HALO_COLS + R_V + 2 * D_MLSTM] = tail[:, o_v:o_if]
        pad = jnp.zeros((CHUNK, LANES - SSD_HEADS - 2 * MLSTM_HEADS), tail.dtype)
        win_ref[rows, N_HALO_COLS + R_G:N_HALO_COLS + R_G + LANES] = jnp.concatenate(
            [tail[:, 0:o_qk], tail[:, o_if:n_tail], pad], axis=1)


def mix_kernel(layer, from_time_order, n_tiles, xc_ref, xp_ref, wout_hbm, wraw_ref, nw_ref, gbias_ref, alog_ref,
               xcw_ref, xcb_ref, qcw_ref, qcb_ref,
               poolw_ref, poolb_ref, pools_ref, dskip_ref, snorm_ref, mnorm_ref, postn_ref,
               out_ref, win_ref, wout_ref, stage_ref, stage_sem,
               h_ref, hp_ref, rest_ref, hph_ref, psh_ref, act_ref, mix_ref, ps_ref,
               sstate_ref, mstate_ref, mm_ref, *in_scratch):
    TT = hp_ref.shape[1]
    L = CHUNK
    n_c = TT // L
    b = pl.program_id(0)
    i = pl.program_id(1)
    n_t = n_tiles

    if from_time_order:
        xbuf_ref, in_sem = in_scratch
        n_slots = xbuf_ref.shape[0]

        def in_copies(tile):
            sl = tile % n_slots
            return [pltpu.make_async_copy(xc_ref.at[b, pl.ds(tile * n_c, n_c), s], xbuf_ref.at[sl, :, :, s, :],
                                          in_sem.at[sl]) for s in range(SUBLANES)]

        @pl.when(i == 0)
        def _():
            for cp in in_copies(0):
                cp.start()

        @pl.when(i + 1 < n_t)
        def _():
            for cp in in_copies(i + 1):
                cp.start()

        @pl.when(i < n_t)
        def _():
            for cp in in_copies(i):
                cp.wait()

        slot_cur = jnp.minimum(i, n_t - 1) % n_slots
        slot_prev = jnp.maximum(i - 1, 0) % n_slots

        def x_cur():
            return xbuf_ref[slot_cur].reshape(TT, xbuf_ref.shape[-1])

        def x_rows(r0):
            return xbuf_ref[slot_prev, r0 // L].reshape(L, xbuf_ref.shape[-1])
    else:
        def x_cur():
            return xc_ref[0]

        def x_rows(r0):
            return xp_ref[0, r0:r0 + L, :]

    @pl.when((pl.program_id(0) == 0) & (i == 0))
    def _():
        _relayout_w_in(wraw_ref, win_ref)
        _stage_weights(_col_blocks(wout_hbm.at[layer], 0, wout_ref, 0, D_MODEL, STAGE_COLS),
                       stage_ref, stage_sem)

    @pl.when(i == 0)
    def _():
        hph_ref[...] = jnp.zeros(hph_ref.shape, F32)
        psh_ref[...] = jnp.zeros(psh_ref.shape, F32)
        sstate_ref[...] = jnp.zeros(sstate_ref.shape, F32)
        mstate_ref[...] = jnp.zeros(mstate_ref.shape, F32)
        mm_ref[...] = jnp.zeros(mm_ref.shape, F32)

    def step(slot_proj, slot_mix):
        pending = []
        if slot_proj is not None:
            x = x_cur()
            ms = jnp.mean(x * x, axis=-1, keepdims=True)
            h_ref[...] = (x * lax.rsqrt(ms + EPS) * nw_ref[...]).astype(BF16)

            def proj_piece(dst_ref, c0, c1, w0):
                def piece():
                    dst_ref[slot_proj, :, c0:c1] = _dot(h_ref[...], win_ref[:, w0 + c0:w0 + c1])
                return piece

            pending += [proj_piece(hp_ref, c0, min(c0 + PROJ_COLS, N_HALO_COLS), 0)
                        for c0 in range(0, N_HALO_COLS, PROJ_COLS)]
            pending += [proj_piece(rest_ref, c0, min(c0 + PROJ_COLS, N_REST_COLS), N_HALO_COLS)
                        for c0 in range(0, N_REST_COLS, PROJ_COLS)]
        if slot_mix is not None:
            _mixers(i - 1, x_rows, hp_ref.at[slot_mix], rest_ref.at[slot_mix], gbias_ref, alog_ref,
                    xcw_ref, xcb_ref, qcw_ref, qcb_ref, poolw_ref, poolb_ref, pools_ref, dskip_ref, snorm_ref,
                    mnorm_ref, wout_ref, postn_ref, out_ref, hph_ref, psh_ref, act_ref, mix_ref, ps_ref,
                    sstate_ref, mstate_ref, mm_ref, pending)
        while pending:
            pending.pop(0)()

    last_slot = (n_tiles - 1) % 2

    @pl.when(i == 0)
    def _():
        step(0, None)

    @pl.when((i > 0) & (i < n_tiles) & (i % 2 == 0))
    def _():
        step(0, 1)

    @pl.when((i > 0) & (i < n_tiles) & (i % 2 == 1))
    def _():
        step(1, 0)

    @pl.when(i == n_tiles)
    def _():
        step(None, last_slot)


def _mixers(tile, x_rows, hp_ref, rest_ref, gbias_ref, alog_ref, xcw_ref, xcb_ref, qcw_ref, qcb_ref,
            poolw_ref, poolb_ref, pools_ref, dskip_ref, snorm_ref, mnorm_ref, wout_ref, postn_ref,
            out_ref, hph_ref, psh_ref, act_ref, mix_ref, ps_ref, sstate_ref, mstate_ref, mm_ref, pending):
    def between():
        if pending:
            pending.pop(0)()

    TT = hp_ref.shape[0]
    L = CHUNK
    n_hph = hph_ref.shape[0]
    n_psh = psh_ref.shape[1]
    ps_carried = ((0, 0), (0, 1), (1, 1), (2, 1))

    def tail(cur_ref, halo, n_halo, r0, n_rows, cols):
        if r0 == 0:
            return halo[n_halo - n_rows:n_halo, cols]
        return cur_ref[r0 - n_rows:r0, cols]

    lane = lax.broadcasted_iota(jnp.int32, (L, LANES), 1)
    row = lax.broadcasted_iota(jnp.int32, (L, LANES), 0)
    lo_half = lane < HEAD_DIM
    tau_row = _tau(row)
    causal = _tau(lane) <= tau_row
    tril = jnp.where(causal, 1.0, 0.0).astype(BF16)
    lane_row = lax.broadcasted_iota(jnp.int32, (1, LANES), 1)
    a_row = -jnp.exp(alog_ref[...])
    neg_inf = -jnp.inf
    win_blk = [jnp.where(lo_half, float(POOL_WINDOWS[2 * b]), float(POOL_WINDOWS[2 * b + 1])) for b in range(2)]
    tau_f = tau_row.astype(F32)

    def chunk_stages(c):
        r0 = c * L

        def conv_block(col, cw_ref, cb_ref, wcol, k_taps):
            cols = slice(col, col + LANES)
            cur = hp_ref[r0:r0 + L, cols]
            prev_tail = tail(hp_ref, hph_ref, n_hph, r0, (k_taps - 1) * SUBLANES, cols)
            return _silu(_causal_conv(prev_tail, cur, cw_ref, cb_ref[:, wcol:wcol + LANES], wcol, LANES, k_taps))

        for blk in range(D_SSD_XBC // LANES):
            act_ref[r0:r0 + L, blk * LANES:(blk + 1) * LANES] = conv_block(
                C_XBC + blk * LANES, xcw_ref, xcb_ref, blk * LANES, SSD_CONV)
        for blk in range(2 * D_MLSTM // LANES):
            act_ref[r0:r0 + L, D_SSD_XBC + blk * LANES:D_SSD_XBC + (blk + 1) * LANES] = conv_block(
                C_QK + blk * LANES, qcw_ref, qcb_ref, blk * LANES, MLSTM_CONV)
        yield

        pos = tau_f + (tile * TT + r0 + 1).astype(F32)
        pooled_blocks = []
        for b in range(2):
            cs_ = slice(b * LANES, (b + 1) * LANES)
            u_cur = hp_ref[r0:r0 + L, cs_]
            lvl = u_cur
            sums = []
            for li, sh in enumerate((1, 2, 4, 8)):
                if li == 0:
                    prev_tail = tail(hp_ref, hph_ref, n_hph, r0, sh * SUBLANES, cs_)
                else:
                    prev_tail = tail(ps_ref.at[li - 1], psh_ref.at[li - 1], n_psh, r0, sh * SUBLANES, cs_)
                ext = _ext_rows(prev_tail, lvl[L - sh * SUBLANES:L])
                lvl = lvl + _shifted(ext, lvl, sh)
                sums.append(lvl)
                if (li, b) in ps_carried:
                    ps_ref[li, r0:r0 + L, cs_] = lvl
                if b == 0 and li == 1:
                    break
            wsum = jnp.where(lo_half, sums[0], sums[1]) if b == 0 else jnp.where(lo_half, sums[2], sums[3])
            pooled_blocks.append((wsum / jnp.minimum(pos, win_blk[b]) - u_cur).astype(BF16))
        mix_ref[r0:r0 + L, 0:D_POOL] = (
            (_dot(jnp.concatenate(pooled_blocks, axis=1), poolw_ref[...]) + poolb_ref[...]) * pools_ref[...])
        yield

        gb = rest_ref[r0:r0 + L, R_G:R_G + LANES] + gbias_ref[...]
        sp_term = jnp.log(1.0 + jnp.exp(-jnp.abs(gb)))
        dt = jnp.maximum(gb, 0.0) + sp_term
        log_f = jnp.minimum(gb, 0.0) - sp_term
        is_dt = lane < G_I
        is_f = (lane >= G_F) & (lane < G_F + MLSTM_HEADS)
        v_cum = jnp.where(is_dt, dt * a_row, jnp.where(is_f, log_f, 0.0))
        hi, mid, lo = _split3(v_cum)
        cs3 = _dot(tril, jnp.concatenate([hi, mid, lo], axis=1))
        cs = cs3[:, 0:LANES] + cs3[:, LANES:2 * LANES] + cs3[:, 2 * LANES:3 * LANES]
        u_gate = jnp.where(is_dt, dt, gb)
        cs_t = cs.T
        ug_t = u_gate.T
        cs_last = cs[L - 1:L, :]
        e_col = jnp.exp(cs)
        w_col = jnp.exp(cs_last - cs) * dt
        e_last = jnp.exp(cs_last)
        yield

        for g in range(SSD_GROUPS):
            b_t = act_ref[r0:r0 + L, D_SSD + g * SSD_STATE:D_SSD + (g + 1) * SSD_STATE].astype(BF16).T
            c_g = act_ref[r0:r0 + L, D_SSD + (SSD_GROUPS + g) * SSD_STATE:
                          D_SSD + (SSD_GROUPS + g + 1) * SSD_STATE].astype(BF16)
            state_g = sstate_ref[g]
            sc = _dot(c_g, jnp.concatenate([b_t, state_g.astype(BF16)], axis=1))
            s_g = sc[:, 0:L]
            y_off = sc[:, L:L + 4 * SSD_HEAD_DIM]
            xd_blocks = []
            cd_blocks = []
            for pr in range(2):
                h_even = 4 * g + 2 * pr
                col = h_even * SSD_HEAD_DIM
                xs = act_ref[r0:r0 + L, col:col + LANES]
                xs_b = xs.astype(BF16)
                m_pair = []
                for hh in range(2):
                    hd = h_even + hh
                    seg = jnp.where(causal, cs[:, hd:hd + 1] - cs_t[hd:hd + 1, :], neg_inf)
                    m_pair.append((s_g * (jnp.exp(seg) * ug_t[hd:hd + 1, :])).astype(BF16))
                yd = _dot(jnp.concatenate(m_pair, axis=0), xs_b)
                y_diag = jnp.where(lo_half, yd[0:L], yd[L:2 * L])
                e_exp = _pair_expand(e_col, h_even, (L, LANES), lo_half)
                w_exp = _pair_expand(w_col, h_even, (L, LANES), lo_half)
                y = (y_diag + y_off[:, pr * LANES:(pr + 1) * LANES] * e_exp
                     + xs * dskip_ref[:, col:col + LANES])
                z = rest_ref[r0:r0 + L, R_Z + col:R_Z + col + LANES]
                mix_ref[r0:r0 + L, D_POOL + col:D_POOL + col + LANES] = y * _silu(z)
                xd_blocks.append((xs * w_exp).astype(BF16))
                cd_blocks.append(_pair_expand(e_last, h_even, (1, LANES), lane_row < HEAD_DIM))
            xd_g = jnp.concatenate(xd_blocks, axis=1)
            cd_g = jnp.concatenate(cd_blocks, axis=1)
            new_states = _dot(b_t, xd_g)
            sstate_ref[g] = state_g * cd_g + new_states
            yield
        y_all = mix_ref[r0:r0 + L, D_POOL:D_POOL + D_SSD]
        ms_y = jnp.mean(y_all * y_all, axis=-1, keepdims=True)
        mix_ref[r0:r0 + L, D_POOL:D_POOL + D_SSD] = y_all * lax.rsqrt(ms_y + EPS) * snorm_ref[...]

        for pr in range(MLSTM_HEADS // 2):
            qcol = D_SSD_XBC + pr * LANES
            kcol = D_SSD_XBC + D_MLSTM + pr * LANES
            q_b = act_ref[r0:r0 + L, qcol:qcol + LANES] * (MLSTM_HEAD_DIM ** -0.5)
            k_t = act_ref[r0:r0 + L, kcol:kcol + LANES].T
            k_tb = k_t.astype(BF16)
            v_b = rest_ref[r0:r0 + L, R_V + pr * LANES:R_V + (pr + 1) * LANES]
            o_b = rest_ref[r0:r0 + L, R_O + pr * LANES:R_O + (pr + 1) * LANES]
            qms = [jnp.where(lo_half, q_b, 0.0), jnp.where(lo_half, 0.0, q_b)]
            s_pair = _dot(jnp.concatenate([q.astype(BF16) for q in qms], axis=0), k_tb)
            v_ones = jnp.concatenate([v_b, jnp.ones((L, LANES), F32)], axis=1).astype(BF16)
            state = mstate_ref[pr]
            lhs = []
            g_ts = []
            kw_rows = []
            decay_old = []
            decay_new = []
            for hh in range(2):
                hd = 2 * pr + hh
                s = s_pair[hh * L:(hh + 1) * L]
                b_row = cs_t[G_F + hd:G_F + hd + 1, :]
                r_row = ug_t[G_I + hd:G_I + hd + 1, :] - b_row
                b_last = jnp.sum(jnp.where(lane_row == L - 1, b_row, 0.0), axis=-1, keepdims=True)
                al_row = b_last + r_row
                m_loc = jnp.max(al_row, axis=-1, keepdims=True)
                prev_m_row = mm_ref[hd:hd + 1, :]
                rmask = jnp.where(causal, r_row, neg_inf)
                g_t = jnp.maximum(jnp.broadcast_to(jnp.max(rmask, axis=-1, keepdims=True), (L, LANES)),
                                  prev_m_row)
                p = (s * jnp.exp(rmask - g_t)).astype(BF16)
                q_inter = (qms[hh] * jnp.exp(prev_m_row - g_t)).astype(BF16)
                lhs.append(jnp.concatenate([p, q_inter], axis=1))
                g_ts.append(g_t)
                kw_rows.append(jnp.exp(al_row - m_loc))
                m_new = jnp.maximum(b_last + prev_m_row, m_loc)
                decay_old.append(jnp.exp(b_last + prev_m_row - m_new))
                decay_new.append(jnp.exp(m_loc - m_new))
                mm_ref[hd:hd + 1, :] = m_new
            rhs = jnp.concatenate([v_ones, state.astype(BF16)], axis=0)
            res_pair = _dot(jnp.concatenate(lhs, axis=0), rhs)
            hv = []
            for hh in range(2):
                hd = 2 * pr + hh
                res = res_pair[hh * L:(hh + 1) * L]
                den = jnp.maximum(jnp.abs(res[:, LANES:2 * LANES]),
                                  jnp.exp(-(_bcast_lane(cs, G_F + hd, (L, LANES)) + g_ts[hh])))
                hv.append(res[:, 0:LANES] / den)
            top = row < HEAD_DIM
            ktw = (k_t * jnp.where(top, kw_rows[0], kw_rows[1])).astype(BF16)
            c_loc = _dot(ktw, v_ones)
            own = top == lo_half
            d_old = jnp.where(top, decay_old[0], decay_old[1])
            d_new = jnp.where(top, decay_new[0], decay_new[1])
            mstate_ref[pr] = jnp.concatenate(
                [d_old * state[:, 0:LANES] + d_new * jnp.where(own, c_loc[:, 0:LANES], 0.0),
                 d_old * state[:, LANES:2 * LANES] + d_new * c_loc[:, LANES:2 * LANES]], axis=1)
            hcat = _sigmoid(o_b) * jnp.where(lo_half, hv[0], hv[1])
            sq = hcat * hcat
            ss_lo = jnp.sum(jnp.where(lo_half, sq, 0.0), axis=-1, keepdims=True)
            ss_hi = jnp.sum(jnp.where(lo_half, 0.0, sq), axis=-1, keepdims=True)
            inv = jnp.where(lo_half, lax.rsqrt(ss_lo * (1.0 / MLSTM_HEAD_DIM) + EPS),
                            lax.rsqrt(ss_hi * (1.0 / MLSTM_HEAD_DIM) + EPS))
            mcol = D_POOL + D_SSD + pr * LANES
            mix_ref[r0:r0 + L, mcol:mcol + LANES] = hcat * inv * mnorm_ref[:, pr * LANES:(pr + 1) * LANES]
            if pr + 1 < MLSTM_HEADS // 2:
                yield

        yield

    n_stages = 3 + SSD_GROUPS + MLSTM_HEADS // 2
    gens = [chunk_stages(c) for c in range(TT // L)]
    for _ in range(n_stages):
        for gen in gens:
            next(gen)
            between()
    while pending:
        pending.pop(0)()

    o = _dot(mix_ref[...].astype(BF16), wout_ref[...])
    ms_o = jnp.mean(o * o, axis=-1, keepdims=True)
    o = o * lax.rsqrt(ms_o + EPS) * postn_ref[...]
    for c in range(TT // L):
        out_ref[0, c * L:(c + 1) * L, :] = x_rows(c * L) + o[c * L:(c + 1) * L]

    hph_ref[...] = hp_ref[TT - n_hph:TT, :]
    for lv, b in ps_carried:
        cs_ = slice(b * LANES, (b + 1) * LANES)
        psh_ref[lv, :, cs_] = ps_ref[lv, TT - n_psh:TT, cs_]


def ffn_kernel(layer, to_time_order, x_ref, wup_hbm, wdn_hbm, nw_ref, cw_ref, cb_ref, postn_ref, out_ref,
               wup_ref, wdn_ref, stage_ref, stage_dn_ref, stage_sem, halo_ref, a_ref, *out_scratch):
    TT = x_ref.shape[1]
    FT = FFN_FT
    L = CHUNK
    n_c = TT // L
    n_j = D_FF // FT
    n_tail = (FFN_CONV - 1) * SUBLANES
    b = pl.program_id(0)
    i = pl.program_id(1)
    step = b * pl.num_programs(1) + i
    n_steps = pl.num_programs(0) * pl.num_programs(1)

    @pl.when(i == 0)
    def _():
        halo_ref[...] = jnp.zeros(halo_ref.shape, F32)

    if to_time_order:
        obuf_ref, out_sem = out_scratch
        slot = step % 2

        def out_copies(sl):
            return [pltpu.make_async_copy(obuf_ref.at[sl, :, :, s, :], out_ref.at[b, pl.ds(i * n_c, n_c), s],
                                          out_sem.at[sl]) for s in range(SUBLANES)]

        @pl.when(step >= 2)
        def _():
            for cp in out_copies(slot):
                cp.wait()

    @pl.when(step == 0)
    def _():
        assert STAGE_COLS == FT
        up_blocks = []
        for j in range(n_j):
            for src0 in (j * FT, D_FF + j * FT):
                up_blocks += _col_blocks(wup_hbm.at[layer], src0, wup_ref, len(up_blocks) * FT, FT, STAGE_COLS)
        _stage_weights(up_blocks, stage_ref, stage_sem)
        _stage_weights([(wdn_hbm.at[layer, r:r + STAGE_COLS, :], wdn_ref.at[r:r + STAGE_COLS, :])
                        for r in range(0, D_FF, STAGE_COLS)], stage_dn_ref, stage_sem)

    def tile():
        x = x_ref[0]
        ms = jnp.mean(x * x, axis=-1, keepdims=True)
        h = (x * lax.rsqrt(ms + EPS) * nw_ref[...]).astype(BF16)

        def conv_cols(u, col):
            outs = []
            for c in range(n_c):
                cur = u[c * L:(c + 1) * L]
                prev_tail = halo_ref[:, col:col + FT] if c == 0 else u[c * L - n_tail:c * L]
                outs.append(_causal_conv(prev_tail, cur, cw_ref, cb_ref[:, col:col + FT], col, FT, FFN_CONV))
            halo_ref[:, col:col + FT] = u[TT - n_tail:TT]
            return outs

        for j in range(n_j):
            u_gv = _dot(h, wup_ref[:, 2 * j * FT:2 * (j + 1) * FT])
            gts = conv_cols(u_gv[:, 0:FT], j * FT)
            vals = conv_cols(u_gv[:, FT:2 * FT], D_FF + j * FT)
            for c in range(n_c):
                gt = gts[c]
                gelu = 0.5 * gt * (1.0 + jnp.tanh(math.sqrt(2.0 / math.pi) * (gt + 0.044715 * (gt * gt * gt))))
                a_ref[c * L:(c + 1) * L, j * FT:(j + 1) * FT] = (gelu * vals[c]).astype(BF16)

        f = _dot(a_ref[...], wdn_ref[...])
        ms_f = jnp.mean(f * f, axis=-1, keepdims=True)
        res = x_ref[0] + f * lax.rsqrt(ms_f + EPS) * postn_ref[...]
        if to_time_order:
            obuf_ref[slot] = res.reshape(n_c, VROWS, SUBLANES, res.shape[1])
            for cp in out_copies(slot):
                cp.start()
        else:
            out_ref[0] = res

    tile()

    if to_time_order:
        @pl.when(step == n_steps - 1)
        def _():
            for cp in out_copies(slot):
                cp.wait()

        @pl.when((step == n_steps - 1) & (n_steps >= 2))
        def _():
            for cp in out_copies(1 - slot):
                cp.wait()


def _const_spec(shape):
    nd = len(shape)
    return pl.BlockSpec(shape, lambda b, i: (0,) * nd, pipeline_mode=pl.Buffered(1))


def _mix_layer(layer, from_time_order, x, wout, win, nw, gbias, alog, xcw, xcb, qcw, qcb, poolw, poolb, pools,
               dskip, snorm, mnorm, postn):
    B, T, D = x.shape
    TT = MIX_TT
    weights = (wout,)
    assert x.dtype == F32 and wout.dtype == F32 and win.dtype == BF16
    consts = (nw, gbias, alog, xcw, xcb, qcw, qcb, poolw, poolb, pools, dskip, snorm, mnorm, postn)
    win_spec = pl.BlockSpec((None,) + win.shape[1:], lambda b, i: (layer, 0, 0), pipeline_mode=pl.Buffered(1))
    n_t = T // TT
    cur_spec = pl.BlockSpec((1, TT, D), lambda b, i: (b, jnp.minimum(i, n_t - 1), 0))
    prev_spec = pl.BlockSpec((1, TT, D), lambda b, i: (b, jnp.maximum(i - 1, 0), 0))
    max_conv_tail = (max(SSD_CONV, MLSTM_CONV) - 1) * SUBLANES
    max_pool_tail = (POOL_WINDOWS[-1] // 2) * SUBLANES
    if from_time_order:
        x_in = x.reshape(B, T // CHUNK, SUBLANES, VROWS, D)
        x_specs = [pl.BlockSpec(memory_space=pltpu.HBM)] * 2
        in_scratch = [pltpu.VMEM((3, TT // CHUNK, VROWS, SUBLANES, D), F32),
                      pltpu.SemaphoreType.DMA((3,))]
    else:
        x_in, x_specs, in_scratch = x, [cur_spec, prev_spec], []
    return pl.pallas_call(
        functools.partial(mix_kernel, layer, from_time_order, n_t),
        grid=(B, n_t + 1),
        in_specs=(x_specs + [pl.BlockSpec(memory_space=pltpu.HBM) for _ in weights] + [win_spec]
                  + [_const_spec(c.shape) for c in consts]),
        out_specs=prev_spec,
        out_shape=jax.ShapeDtypeStruct(x.shape, x.dtype),
        scratch_shapes=[
            pltpu.VMEM((D_MODEL, N_IN_COLS), BF16),
            pltpu.VMEM((D_MODEL, D_MODEL), BF16),
            pltpu.VMEM((STAGE_SLOTS, D_MODEL, STAGE_COLS), F32),
            pltpu.SemaphoreType.DMA((STAGE_SLOTS,)),
            pltpu.VMEM((TT, D_MODEL), BF16),
            pltpu.VMEM((2, TT, N_HALO_COLS), F32),
            pltpu.VMEM((2, TT, N_REST_COLS), F32),
            pltpu.VMEM((max_conv_tail, N_HALO_COLS), F32),
            pltpu.VMEM((3, max_pool_tail, D_POOL), F32),
            pltpu.VMEM((TT, D_SSD_XBC + 2 * D_MLSTM), F32),
            pltpu.VMEM((TT, D_MODEL), F32),
            pltpu.VMEM((3, TT, D_POOL), F32),
            pltpu.VMEM((SSD_GROUPS, SSD_STATE, 4 * SSD_HEAD_DIM), F32),
            pltpu.VMEM((MLSTM_HEADS // 2, LANES, 2 * LANES), F32),
            pltpu.VMEM((SUBLANES, LANES), F32),
        ] + in_scratch,
        compiler_params=pltpu.CompilerParams(
            dimension_semantics=("arbitrary", "arbitrary"), vmem_limit_bytes=VMEM_LIMIT),
        name="mix_layer",
    )(x_in, x_in, *weights, win, *consts)


def _ffn_layer(layer, to_time_order, x, wup, wdn, nw, cw, cb, postn):
    B, T, D = x.shape
    TT = FFN_TT
    weights = (wup, wdn)
    assert x.dtype == F32 and wup.dtype == F32 and wdn.dtype == F32
    consts = (nw, cw, cb, postn)
    x_spec = pl.BlockSpec((1, TT, D), lambda b, i: (b, i, 0))
    if to_time_order:
        out_spec = pl.BlockSpec(memory_space=pltpu.HBM)
        out_shape = jax.ShapeDtypeStruct((B, T // CHUNK, SUBLANES, VROWS, D), x.dtype)
        out_scratch = [pltpu.VMEM((2, TT // CHUNK, VROWS, SUBLANES, D), F32),
                       pltpu.SemaphoreType.DMA((2,))]
    else:
        out_spec, out_shape, out_scratch = x_spec, jax.ShapeDtypeStruct(x.shape, x.dtype), []
    return pl.pallas_call(
        functools.partial(ffn_kernel, layer, to_time_order),
        grid=(B, T // TT),
        in_specs=([x_spec] + [pl.BlockSpec(memory_space=pltpu.HBM) for _ in weights]
                  + [_const_spec(c.shape) for c in consts]),
        out_specs=out_spec,
        out_shape=out_shape,
        scratch_shapes=[
            pltpu.VMEM((D_MODEL, 2 * D_FF), BF16),
            pltpu.VMEM((D_FF, D_MODEL), BF16),
            pltpu.VMEM((STAGE_SLOTS, D_MODEL, STAGE_COLS), F32),
            pltpu.VMEM((STAGE_SLOTS, STAGE_COLS, D_MODEL), F32),
            pltpu.SemaphoreType.DMA((STAGE_SLOTS,)),
            pltpu.VMEM(((FFN_CONV - 1) * SUBLANES, 2 * D_FF), F32),
            pltpu.VMEM((TT, D_FF), BF16),
        ] + out_scratch,
        compiler_params=pltpu.CompilerParams(
            dimension_semantics=("arbitrary", "arbitrary"), vmem_limit_bytes=VMEM_LIMIT),
        name="ffn_layer",
    )(x, *weights, *consts).reshape(B, T, D)


def _row(v):
    return v.reshape(1, -1).astype(F32)


def _pad_lanes(v):
    return jnp.pad(v.astype(F32), (0, LANES - v.shape[0])).reshape(1, LANES)


def _prep_pool_w(w):
    out = jnp.zeros((D_POOL, D_POOL), F32)
    for g in range(len(POOL_WINDOWS)):
        s = g * POOL_GROUP_DIM
        out = lax.dynamic_update_slice(out, w[g].astype(F32), (s, s))
    return out.astype(BF16)


def kernel(x, pre_mix_norm, w_in, pool_w, pool_b, pool_scale, ssd_conv_w, ssd_conv_b, ssd_dt_bias, ssd_a_log, ssd_d, ssd_norm, mlstm_conv_w, mlstm_conv_b, mlstm_i_bias, mlstm_f_bias, mlstm_norm, w_out, post_mix_norm, pre_ffn_norm, ffn_w_up, ffn_conv_w, ffn_conv_b, ffn_w_down, post_ffn_norm):
    depth = w_in.shape[0]
    w_in_bf = w_in.astype(BF16)
    for l in range(depth):
        gbias = _pad_lanes(jnp.concatenate([ssd_dt_bias[l], mlstm_i_bias[l], mlstm_f_bias[l]]))
        x = _mix_layer(
            l, l == 0, x, w_out, w_in_bf, _row(pre_mix_norm[l]), gbias, _pad_lanes(ssd_a_log[l]),
            ssd_conv_w[l].astype(F32), _row(ssd_conv_b[l]), mlstm_conv_w[l].astype(F32), _row(mlstm_conv_b[l]),
            _prep_pool_w(pool_w[l]), _row(pool_b[l]), _row(pool_scale[l]),
            _row(jnp.repeat(ssd_d[l], SSD_HEAD_DIM)), _row(ssd_norm[l]), _row(mlstm_norm[l]),
            _row(post_mix_norm[l]))
        x = _ffn_layer(
            l, l == depth - 1, x, ffn_w_up, ffn_w_down, _row(pre_ffn_norm[l]), ffn_conv_w[l].astype(F32),
            _row(ffn_conv_b[l]), _row(post_ffn_norm[l]))
    return x
```

```python
import functools
import math

import jax
import jax.numpy as jnp
from jax import lax
from jax.experimental import pallas as pl
from jax.experimental.pallas import tpu as pltpu

F32 = jnp.float32
BF16 = jnp.bfloat16

D_MODEL = 1024
EPS = 1e-6

D_POOL = 256
POOL_GROUP_DIM = 64
POOL_WINDOWS = (2, 4, 8, 16)

D_SSD = 512
SSD_HEADS = 8
SSD_HEAD_DIM = 64
SSD_GROUPS = 2
SSD_STATE = 128
SSD_CONV = 4
D_SSD_XBC = D_SSD + 2 * SSD_GROUPS * SSD_STATE

D_MLSTM = 256
MLSTM_HEADS = 4
MLSTM_HEAD_DIM = 64
MLSTM_CONV = 4

D_FF = 2816
FFN_CONV = 3

CHUNK = 128
LANES = 128
SUBLANES = 8
VROWS = CHUNK // SUBLANES
HEAD_DIM = LANES // 2
assert SSD_HEAD_DIM == HEAD_DIM and MLSTM_HEAD_DIM == HEAD_DIM

C_POOL = 0
C_XBC = C_POOL + D_POOL
C_QK = C_XBC + D_SSD_XBC
N_HALO_COLS = C_QK + 2 * D_MLSTM
R_Z = 0
R_V = R_Z + D_SSD
R_O = R_V + D_MLSTM
R_G = R_O + D_MLSTM
N_REST_COLS = R_G + LANES
N_IN_COLS = N_HALO_COLS + N_REST_COLS
G_DT = 0
G_I = SSD_HEADS
G_F = G_I + MLSTM_HEADS

MIX_TT = 256
PROJ_COLS = 256
STAGE_COLS = 256
STAGE_SLOTS = 4
FFN_TT = 512
FFN_FT = 256
VMEM_LIMIT = 56 * 1024 * 1024


def _dot(a, b):
    return jnp.dot(a, b, preferred_element_type=F32)


def _sigmoid(x):
    return 0.5 * jnp.tanh(0.5 * x) + 0.5


def _silu(x):
    return x * _sigmoid(x)


def _split3(a):
    hi = a.astype(BF16)
    r = a - hi.astype(F32)
    mid = r.astype(BF16)
    lo = (r - mid.astype(F32)).astype(BF16)
    return hi, mid, lo


def _bcast_lane(a, j, shape):
    return jnp.broadcast_to(a[:, j:j + 1], shape)


def _pair_expand(a, h_even, shape, lo_half):
    return jnp.where(lo_half, _bcast_lane(a, h_even, shape), _bcast_lane(a, h_even + 1, shape))


def _stage_weights(blocks, stage_ref, sem_ref):
    n_slots = stage_ref.shape[0]

    def copy(n):
        src, dst = blocks[n]
        rows, cols = src.shape
        return pltpu.make_async_copy(src, stage_ref.at[n % n_slots, 0:rows, 0:cols], sem_ref.at[n % n_slots])

    for n in range(min(n_slots - 1, len(blocks))):
        copy(n).start()
    for n, (src, dst) in enumerate(blocks):
        if n + n_slots - 1 < len(blocks):
            copy(n + n_slots - 1).start()
        copy(n).wait()
        rows, cols = src.shape
        dst[...] = stage_ref[n % n_slots, 0:rows, 0:cols].astype(BF16)


def _col_blocks(src_ref, src0, dst_ref, dst0, ncols, step):
    return [(src_ref.at[:, src0 + k:src0 + min(k + step, ncols)],
             dst_ref.at[:, dst0 + k:dst0 + min(k + step, ncols)]) for k in range(0, ncols, step)]


def _tau(p):
    return (p % SUBLANES) * VROWS + p // SUBLANES


def _ext_rows(prev_tail, cur_tail):
    n = cur_tail.shape[0] // SUBLANES
    sub0 = lax.broadcasted_iota(jnp.int32, (SUBLANES, cur_tail.shape[1]), 0) == 0
    out = []
    for j in range(n):
        sl = slice(j * SUBLANES, (j + 1) * SUBLANES)
        out.append(jnp.where(sub0, pltpu.roll(prev_tail[sl], 1, 0), pltpu.roll(cur_tail[sl], 1, 0)))
    return out


def _shifted(ext, cur, k):
    if k == 0:
        return cur
    return jnp.concatenate(ext[len(ext) - k:] + [cur[0:CHUNK - SUBLANES * k]], axis=0)


def _causal_conv(prev_tail, cur, w_ref, b_row, wcol, ncols, k_taps):
    n = k_taps - 1
    ext = _ext_rows(prev_tail, cur[CHUNK - n * SUBLANES:CHUNK])
    acc = b_row
    for k in range(k_taps):
        acc = acc + _shifted(ext, cur, n - k) * w_ref[k:k + 1, wcol:wcol + ncols]
    return acc


def _relayout_w_in(wraw_ref, win_ref):
    c_z = D_POOL
    c_xbc = c_z + D_SSD
    c_dt = c_xbc + D_SSD_XBC
    n_tail = SSD_HEADS + 4 * D_MLSTM + 2 * MLSTM_HEADS
    o_qk = SSD_HEADS
    o_v = o_qk + 2 * D_MLSTM
    o_if = o_v + 2 * D_MLSTM
    for r in range(0, D_MODEL, CHUNK):
        rows = slice(r, r + CHUNK)
        win_ref[rows, C_POOL:C_POOL + D_POOL] = wraw_ref[rows, 0:D_POOL]
        win_ref[rows, C_XBC:C_XBC + D_SSD_XBC] = wraw_ref[rows, c_xbc:c_dt]
        win_ref[rows, N_HALO_COLS + R_Z:N_HALO_COLS + R_Z + D_SSD] = wraw_ref[rows, c_z:c_xbc]
        tail = wraw_ref[rows, c_dt:c_dt + n_tail]
        win_ref[rows, C_QK:C_QK + 2 * D_MLSTM] = tail[:, o_qk:o_v]
        win_ref[rows, N_HALO_COLS + R_V:N_HALO_COLS + R_V + 2 * D_MLSTM] = tail[:, o_v:o_if]
        pad = jnp.zeros((CHUNK, LANES - SSD_HEADS - 2 * MLSTM_HEADS), tail.dtype)
        win_ref[rows, N_HALO_COLS + R_G:N_HALO_COLS + R_G + LANES] = jnp.concatenate(
            [tail[:, 0:o_qk], tail[:, o_if:n_tail], pad], axis=1)


def mix_kernel(layer, from_time_order, n_tiles, xc_ref, xp_ref, wout_hbm, wraw_ref, nw_ref, gbias_ref, alog_ref,
               xcw_ref, xcb_ref, qcw_ref, qcb_ref,
               poolw_ref, poolb_ref, pools_ref, dskip_ref, snorm_ref, mnorm_ref, postn_ref,
               out_ref, win_ref, wout_ref, stage_ref, stage_sem,
               h_ref, hp_ref, rest_ref, hph_ref, psh_ref, act_ref, mix_ref, ps_ref,
               sstate_ref, mstate_ref, mm_ref, *in_scratch):
    TT = hp_ref.shape[1]
    L = CHUNK
    n_c = TT // L
    b = pl.program_id(0)
    i = pl.program_id(1)
    n_t = n_tiles

    if from_time_order:
        xbuf_ref, in_sem = in_scratch
        n_slots = xbuf_ref.shape[0]

        def in_copies(tile):
            sl = tile % n_slots
            return [pltpu.make_async_copy(xc_ref.at[b, pl.ds(tile * n_c, n_c), s], xbuf_ref.at[sl, :, :, s, :],
                                          in_sem.at[sl]) for s in range(SUBLANES)]

        @pl.when(i == 0)
        def _():
            for s, cp in enumerate(in_copies(0)):
                cp.start(priority=s % 2)

        @pl.when(i + 1 < n_t)
        def _():
            for s, cp in enumerate(in_copies(i + 1)):
                cp.start(priority=s % 2)

        @pl.when(i < n_t)
        def _():
            for cp in in_copies(i):
                cp.wait()

        slot_cur = jnp.minimum(i, n_t - 1) % n_slots
        slot_prev = jnp.maximum(i - 1, 0) % n_slots

        def x_cur():
            return xbuf_ref[slot_cur].reshape(TT, xbuf_ref.shape[-1])

        def x_rows(r0):
            return xbuf_ref[slot_prev, r0 // L].reshape(L, xbuf_ref.shape[-1])
    else:
        def x_cur():
            return xc_ref[0]

        def x_rows(r0):
            return xp_ref[0, r0:r0 + L, :]

    @pl.when((pl.program_id(0) == 0) & (i == 0))
    def _():
        _relayout_w_in(wraw_ref, win_ref)
        _stage_weights(_col_blocks(wout_hbm.at[layer], 0, wout_ref, 0, D_MODEL, STAGE_COLS),
                       stage_ref, stage_sem)

    @pl.when(i == 0)
    def _():
        hph_ref[...] = jnp.zeros(hph_ref.shape, F32)
        psh_ref[...] = jnp.zeros(psh_ref.shape, F32)
        sstate_ref[...] = jnp.zeros(sstate_ref.shape, F32)
        mstate_ref[...] = jnp.zeros(mstate_ref.shape, F32)
        mm_ref[...] = jnp.zeros(mm_ref.shape, F32)

    def step(slot_proj, slot_mix):
        pending = []
        if slot_proj is not None:
            x = x_cur()
            ms = jnp.mean(x * x, axis=-1, keepdims=True)
            h_ref[...] = (x * lax.rsqrt(ms + EPS) * nw_ref[...]).astype(BF16)

            def proj_piece(dst_ref, c0, c1, w0):
                def piece():
                    dst_ref[slot_proj, :, c0:c1] = _dot(h_ref[...], win_ref[:, w0 + c0:w0 + c1])
                return piece

            pending += [proj_piece(hp_ref, c0, min(c0 + PROJ_COLS, N_HALO_COLS), 0)
                        for c0 in range(0, N_HALO_COLS, PROJ_COLS)]
            pending += [proj_piece(rest_ref, c0, min(c0 + PROJ_COLS, N_REST_COLS), N_HALO_COLS)
                        for c0 in range(0, N_REST_COLS, PROJ_COLS)]
        if slot_mix is not None:
            _mixers(i - 1, x_rows, hp_ref.at[slot_mix], rest_ref.at[slot_mix], gbias_ref, alog_ref,
                    xcw_ref, xcb_ref, qcw_ref, qcb_ref, poolw_ref, poolb_ref, pools_ref, dskip_ref, snorm_ref,
                    mnorm_ref, wout_ref, postn_ref, out_ref, hph_ref, psh_ref, act_ref, mix_ref, ps_ref,
                    sstate_ref, mstate_ref, mm_ref, pending)
        while pending:
            pending.pop(0)()

    last_slot = (n_tiles - 1) % 2

    @pl.when(i == 0)
    def _():
        step(0, None)

    @pl.when((i > 0) & (i < n_tiles) & (i % 2 == 0))
    def _():
        step(0, 1)

    @pl.when((i > 0) & (i < n_tiles) & (i % 2 == 1))
    def _():
        step(1, 0)

    @pl.when(i == n_tiles)
    def _():
        step(None, last_slot)


def _mixers(tile, x_rows, hp_ref, rest_ref, gbias_ref, alog_ref, xcw_ref, xcb_ref, qcw_ref, qcb_ref,
            poolw_ref, poolb_ref, pools_ref, dskip_ref, snorm_ref, mnorm_ref, wout_ref, postn_ref,
            out_ref, hph_ref, psh_ref, act_ref, mix_ref, ps_ref, sstate_ref, mstate_ref, mm_ref, pending):
    def between():
        if pending:
            pending.pop(0)()

    TT = hp_ref.shape[0]
    L = CHUNK
    n_hph = hph_ref.shape[0]
    n_psh = psh_ref.shape[1]
    ps_carried = ((0, 0), (0, 1), (1, 1), (2, 1))

    def tail(cur_ref, halo, n_halo, r0, n_rows, cols):
        if r0 == 0:
            return halo[n_halo - n_rows:n_halo, cols]
        return cur_ref[r0 - n_rows:r0, cols]

    lane = lax.broadcasted_iota(jnp.int32, (L, LANES), 1)
    row = lax.broadcasted_iota(jnp.int32, (L, LANES), 0)
    lo_half = lane < HEAD_DIM
    tau_row = _tau(row)
    causal = _tau(lane) <= tau_row
    tril = jnp.where(causal, 1.0, 0.0).astype(BF16)
    lane_row = lax.broadcasted_iota(jnp.int32, (1, LANES), 1)
    a_row = -jnp.exp(alog_ref[...])
    neg_inf = -jnp.inf
    win_blk = [jnp.where(lo_half, float(POOL_WINDOWS[2 * b]), float(POOL_WINDOWS[2 * b + 1])) for b in range(2)]
    tau_f = tau_row.astype(F32)

    def chunk_stages(c):
        r0 = c * L

        def conv_block(col, cw_ref, cb_ref, wcol, k_taps):
            cols = slice(col, col + LANES)
            cur = hp_ref[r0:r0 + L, cols]
            prev_tail = tail(hp_ref, hph_ref, n_hph, r0, (k_taps - 1) * SUBLANES, cols)
            return _silu(_causal_conv(prev_tail, cur, cw_ref, cb_ref[:, wcol:wcol + LANES], wcol, LANES, k_taps))

        for blk in range(D_SSD_XBC // LANES):
            act_ref[r0:r0 + L, blk * LANES:(blk + 1) * LANES] = conv_block(
                C_XBC + blk * LANES, xcw_ref, xcb_ref, blk * LANES, SSD_CONV)
        for blk in range(2 * D_MLSTM // LANES):
            act_ref[r0:r0 + L, D_SSD_XBC + blk * LANES:D_SSD_XBC + (blk + 1) * LANES] = conv_block(
                C_QK + blk * LANES, qcw_ref, qcb_ref, blk * LANES, MLSTM_CONV)
        yield

        pos = tau_f + (tile * TT + r0 + 1).astype(F32)
        pooled_blocks = []
        for b in range(2):
            cs_ = slice(b * LANES, (b + 1) * LANES)
            u_cur = hp_ref[r0:r0 + L, cs_]
            lvl = u_cur
            sums = []
            for li, sh in enumerate((1, 2, 4, 8)):
                if li == 0:
                    prev_tail = tail(hp_ref, hph_ref, n_hph, r0, sh * SUBLANES, cs_)
                else:
                    prev_tail = tail(ps_ref.at[li - 1], psh_ref.at[li - 1], n_psh, r0, sh * SUBLANES, cs_)
                ext = _ext_rows(prev_tail, lvl[L - sh * SUBLANES:L])
                lvl = lvl + _shifted(ext, lvl, sh)
                sums.append(lvl)
                if (li, b) in ps_carried:
                    ps_ref[li, r0:r0 + L, cs_] = lvl
                if b == 0 and li == 1:
                    break
            wsum = jnp.where(lo_half, sums[0], sums[1]) if b == 0 else jnp.where(lo_half, sums[2], sums[3])
            pooled_blocks.append((wsum / jnp.minimum(pos, win_blk[b]) - u_cur).astype(BF16))
        mix_ref[r0:r0 + L, 0:D_POOL] = (
            (_dot(jnp.concatenate(pooled_blocks, axis=1), poolw_ref[...]) + poolb_ref[...]) * pools_ref[...])
        yield

        gb = rest_ref[r0:r0 + L, R_G:R_G + LANES] + gbias_ref[...]
        sp_term = jnp.log(1.0 + jnp.exp(-jnp.abs(gb)))
        dt = jnp.maximum(gb, 0.0) + sp_term
        log_f = jnp.minimum(gb, 0.0) - sp_term
        is_dt = lane < G_I
        is_f = (lane >= G_F) & (lane < G_F + MLSTM_HEADS)
        v_cum = jnp.where(is_dt, dt * a_row, jnp.where(is_f, log_f, 0.0))
        hi, mid, lo = _split3(v_cum)
        cs3 = _dot(tril, jnp.concatenate([hi, mid, lo], axis=1))
        cs = cs3[:, 0:LANES] + cs3[:, LANES:2 * LANES] + cs3[:, 2 * LANES:3 * LANES]
        u_gate = jnp.where(is_dt, dt, gb)
        cs_t = cs.T
        ug_t = u_gate.T
        cs_last = cs[L - 1:L, :]
        e_col = jnp.exp(cs)
        w_col = jnp.exp(cs_last - cs) * dt
        e_last = jnp.exp(cs_last)
        yield

        for g in range(SSD_GROUPS):
            b_t = act_ref[r0:r0 + L, D_SSD + g * SSD_STATE:D_SSD + (g + 1) * SSD_STATE].astype(BF16).T
            c_g = act_ref[r0:r0 + L, D_SSD + (SSD_GROUPS + g) * SSD_STATE:
                          D_SSD + (SSD_GROUPS + g + 1) * SSD_STATE].astype(BF16)
            state_g = sstate_ref[g]
            sc = _dot(c_g, jnp.concatenate([b_t, state_g.astype(BF16)], axis=1))
            s_g = sc[:, 0:L]
            y_off = sc[:, L:L + 4 * SSD_HEAD_DIM]
            xd_blocks = []
            cd_blocks = []
            for pr in range(2):
                h_even = 4 * g + 2 * pr
                col = h_even * SSD_HEAD_DIM
                xs = act_ref[r0:r0 + L, col:col + LANES]
                xs_b = xs.astype(BF16)
                m_pair = []
                for hh in range(2):
                    hd = h_even + hh
                    seg = jnp.where(causal, cs[:, hd:hd + 1] - cs_t[hd:hd + 1, :], neg_inf)
                    m_pair.append((s_g * (jnp.exp(seg) * ug_t[hd:hd + 1, :])).astype(BF16))
                yd = _dot(jnp.concatenate(m_pair, axis=0), xs_b)
                y_diag = jnp.where(lo_half, yd[0:L], yd[L:2 * L])
                e_exp = _pair_expand(e_col, h_even, (L, LANES), lo_half)
                w_exp = _pair_expand(w_col, h_even, (L, LANES), lo_half)
                y = (y_diag + y_off[:, pr * LANES:(pr + 1) * LANES] * e_exp
                     + xs * dskip_ref[:, col:col + LANES])
                z = rest_ref[r0:r0 + L, R_Z + col:R_Z + col + LANES]
                mix_ref[r0:r0 + L, D_POOL + col:D_POOL + col + LANES] = y * _silu(z)
                xd_blocks.append((xs * w_exp).astype(BF16))
                cd_blocks.append(_pair_expand(e_last, h_even, (1, LANES), lane_row < HEAD_DIM))
            xd_g = jnp.concatenate(xd_blocks, axis=1)
            cd_g = jnp.concatenate(cd_blocks, axis=1)
            new_states = _dot(b_t, xd_g)
            sstate_ref[g] = state_g * cd_g + new_states
            yield
        y_all = mix_ref[r0:r0 + L, D_POOL:D_POOL + D_SSD]
        ms_y = jnp.mean(y_all * y_all, axis=-1, keepdims=True)
        mix_ref[r0:r0 + L, D_POOL:D_POOL + D_SSD] = y_all * lax.rsqrt(ms_y + EPS) * snorm_ref[...]

        for pr in range(MLSTM_HEADS // 2):
            qcol = D_SSD_XBC + pr * LANES
            kcol = D_SSD_XBC + D_MLSTM + pr * LANES
            q_b = act_ref[r0:r0 + L, qcol:qcol + LANES] * (MLSTM_HEAD_DIM ** -0.5)
            k_t = act_ref[r0:r0 + L, kcol:kcol + LANES].T
            k_tb = k_t.astype(BF16)
            v_b = rest_ref[r0:r0 + L, R_V + pr * LANES:R_V + (pr + 1) * LANES]
            o_b = rest_ref[r0:r0 + L, R_O + pr * LANES:R_O + (pr + 1) * LANES]
            qms = [jnp.where(lo_half, q_b, 0.0), jnp.where(lo_half, 0.0, q_b)]
            s_pair = _dot(jnp.concatenate([q.astype(BF16) for q in qms], axis=0), k_tb)
            v_ones = jnp.concatenate([v_b, jnp.ones((L, LANES), F32)], axis=1).astype(BF16)
            state = mstate_ref[pr]
            lhs = []
            g_ts = []
            kw_rows = []
            decay_old = []
            decay_new = []
            for hh in range(2):
                hd = 2 * pr + hh
                s = s_pair[hh * L:(hh + 1) * L]
                b_row = cs_t[G_F + hd:G_F + hd + 1, :]
                r_row = ug_t[G_I + hd:G_I + hd + 1, :] - b_row
                b_last = jnp.sum(jnp.where(lane_row == L - 1, b_row, 0.0), axis=-1, keepdims=True)
                al_row = b_last + r_row
                m_loc = jnp.max(al_row, axis=-1, keepdims=True)
                prev_m_row = mm_ref[hd:hd + 1, :]
                rmask = jnp.where(causal, r_row, neg_inf)
                g_t = jnp.maximum(jnp.broadcast_to(jnp.max(rmask, axis=-1, keepdims=True), (L, LANES)),
                                  prev_m_row)
                p = (s * jnp.exp(rmask - g_t)).astype(BF16)
                q_inter = (qms[hh] * jnp.exp(prev_m_row - g_t)).astype(BF16)
                lhs.append(jnp.concatenate([p, q_inter], axis=1))
                g_ts.append(g_t)
                kw_rows.append(jnp.exp(al_row - m_loc))
                m_new = jnp.maximum(b_last + prev_m_row, m_loc)
                decay_old.append(jnp.exp(b_last + prev_m_row - m_new))
                decay_new.append(jnp.exp(m_loc - m_new))
                mm_ref[hd:hd + 1, :] = m_new
            rhs = jnp.concatenate([v_ones, state.astype(BF16)], axis=0)
            res_pair = _dot(jnp.concatenate(lhs, axis=0), rhs)
            hv = []
            for hh in range(2):
                hd = 2 * pr + hh
                res = res_pair[hh * L:(hh + 1) * L]
                den = jnp.maximum(jnp.abs(res[:, LANES:2 * LANES]),
                                  jnp.exp(-(_bcast_lane(cs, G_F + hd, (L, LANES)) + g_ts[hh])))
                hv.append(res[:, 0:LANES] / den)
            top = row < HEAD_DIM
            ktw = (k_t * jnp.where(top, kw_rows[0], kw_rows[1])).astype(BF16)
            c_loc = _dot(ktw, v_ones)
            own = top == lo_half
            d_old = jnp.where(top, decay_old[0], decay_old[1])
            d_new = jnp.where(top, decay_new[0], decay_new[1])
            mstate_ref[pr] = jnp.concatenate(
                [d_old * state[:, 0:LANES] + d_new * jnp.where(own, c_loc[:, 0:LANES], 0.0),
                 d_old * state[:, LANES:2 * LANES] + d_new * c_loc[:, LANES:2 * LANES]], axis=1)
            hcat = _sigmoid(o_b) * jnp.where(lo_half, hv[0], hv[1])
            sq = hcat * hcat
            ss_lo = jnp.sum(jnp.where(lo_half, sq, 0.0), axis=-1, keepdims=True)
            ss_hi = jnp.sum(jnp.where(lo_half, 0.0, sq), axis=-1, keepdims=True)
            inv = jnp.where(lo_half, lax.rsqrt(ss_lo * (1.0 / MLSTM_HEAD_DIM) + EPS),
                            lax.rsqrt(ss_hi * (1.0 / MLSTM_HEAD_DIM) + EPS))
            mcol = D_POOL + D_SSD + pr * LANES
            mix_ref[r0:r0 + L, mcol:mcol + LANES] = hcat * inv * mnorm_ref[:, pr * LANES:(pr + 1) * LANES]
            if pr + 1 < MLSTM_HEADS // 2:
                yield

        yield

    n_stages = 3 + SSD_GROUPS + MLSTM_HEADS // 2
    gens = [chunk_stages(c) for c in range(TT // L)]
    for _ in range(n_stages):
        for gen in gens:
            next(gen)
            between()
    while pending:
        pending.pop(0)()

    o = _dot(mix_ref[...].astype(BF16), wout_ref[...])
    ms_o = jnp.mean(o * o, axis=-1, keepdims=True)
    o = o * lax.rsqrt(ms_o + EPS) * postn_ref[...]
    for c in range(TT // L):
        out_ref[0, c * L:(c + 1) * L, :] = x_rows(c * L) + o[c * L:(c + 1) * L]

    hph_ref[...] = hp_ref[TT - n_hph:TT, :]
    for lv, b in ps_carried:
        cs_ = slice(b * LANES, (b + 1) * LANES)
        psh_ref[lv, :, cs_] = ps_ref[lv, TT - n_psh:TT, cs_]


def ffn_kernel(layer, to_time_order, x_ref, wup_hbm, wdn_hbm, nw_ref, cw_ref, cb_ref, postn_ref, out_ref,
               wup_ref, wdn_ref, stage_ref, stage_dn_ref, stage_sem, halo_ref, a_ref, *out_scratch):
    TT = x_ref.shape[1]
    FT = FFN_FT
    L = CHUNK
    n_c = TT // L
    n_j = D_FF // FT
    n_tail = (FFN_CONV - 1) * SUBLANES
    b = pl.program_id(0)
    i = pl.program_id(1)
    step = b * pl.num_programs(1) + i
    n_steps = pl.num_programs(0) * pl.num_programs(1)

    @pl.when(i == 0)
    def _():
        halo_ref[...] = jnp.zeros(halo_ref.shape, F32)

    if to_time_order:
        obuf_ref, out_sem = out_scratch
        slot = step % 2

        def out_copies(sl):
            return [pltpu.make_async_copy(obuf_ref.at[sl, :, :, s, :], out_ref.at[b, pl.ds(i * n_c, n_c), s],
                                          out_sem.at[sl]) for s in range(SUBLANES)]

        @pl.when(step >= 2)
        def _():
            for cp in out_copies(slot):
                cp.wait()

    @pl.when(step == 0)
    def _():
        _stage_weights(_col_blocks(wup_hbm.at[layer], 0, wup_ref, 0, 2 * D_FF, STAGE_COLS),
                       stage_ref, stage_sem)
        _stage_weights([(wdn_hbm.at[layer, r:r + STAGE_COLS, :], wdn_ref.at[r:r + STAGE_COLS, :])
                        for r in range(0, D_FF, STAGE_COLS)], stage_dn_ref, stage_sem)

    def tile():
        x = x_ref[0]
        ms = jnp.mean(x * x, axis=-1, keepdims=True)
        h = (x * lax.rsqrt(ms + EPS) * nw_ref[...]).astype(BF16)

        def conv_cols(col):
            u = _dot(h, wup_ref[:, col:col + FT])
            outs = []
            for c in range(n_c):
                cur = u[c * L:(c + 1) * L]
                prev_tail = halo_ref[:, col:col + FT] if c == 0 else u[c * L - n_tail:c * L]
                outs.append(_causal_conv(prev_tail, cur, cw_ref, cb_ref[:, col:col + FT], col, FT, FFN_CONV))
            halo_ref[:, col:col + FT] = u[TT - n_tail:TT]
            return outs

        for j in range(n_j):
            gts = conv_cols(j * FT)
            vals = conv_cols(D_FF + j * FT)
            for c in range(n_c):
                gt = gts[c]
                gelu = 0.5 * gt * (1.0 + jnp.tanh(math.sqrt(2.0 / math.pi) * (gt + 0.044715 * (gt * gt * gt))))
                a_ref[c * L:(c + 1) * L, j * FT:(j + 1) * FT] = (gelu * vals[c]).astype(BF16)

        f = _dot(a_ref[...], wdn_ref[...])
        ms_f = jnp.mean(f * f, axis=-1, keepdims=True)
        res = x_ref[0] + f * lax.rsqrt(ms_f + EPS) * postn_ref[...]
        if to_time_order:
            obuf_ref[slot] = res.reshape(n_c, VROWS, SUBLANES, res.shape[1])
            for s, cp in enumerate(out_copies(slot)):
                cp.start(priority=s % 2)
        else:
            out_ref[0] = res

    tile()

    if to_time_order:
        @pl.when(step == n_steps - 1)
        def _():
            for cp in out_copies(slot):
                cp.wait()

        @pl.when((step == n_steps - 1) & (n_steps >= 2))
        def _():
            for cp in out_copies(1 - slot):
                cp.wait()


def _const_spec(shape):
    nd = len(shape)
    return pl.BlockSpec(shape, lambda b, i: (0,) * nd, pipeline_mode=pl.Buffered(1))


def _mix_layer(layer, from_time_order, x, wout, win, nw, gbias, alog, xcw, xcb, qcw, qcb, poolw, poolb, pools,
               dskip, snorm, mnorm, postn):
    B, T, D = x.shape
    TT = MIX_TT
    weights = (wout,)
    assert x.dtype == F32 and wout.dtype == F32 and win.dtype == BF16
    consts = (nw, gbias, alog, xcw, xcb, qcw, qcb, poolw, poolb, pools, dskip, snorm, mnorm, postn)
    win_spec = pl.BlockSpec((None,) + win.shape[1:], lambda b, i: (layer, 0, 0), pipeline_mode=pl.Buffered(1))
    n_t = T // TT
    cur_spec = pl.BlockSpec((1, TT, D), lambda b, i: (b, jnp.minimum(i, n_t - 1), 0))
    prev_spec = pl.BlockSpec((1, TT, D), lambda b, i: (b, jnp.maximum(i - 1, 0), 0))
    max_conv_tail = (max(SSD_CONV, MLSTM_CONV) - 1) * SUBLANES
    max_pool_tail = (POOL_WINDOWS[-1] // 2) * SUBLANES
    if from_time_order:
        x_in = x.reshape(B, T // CHUNK, SUBLANES, VROWS, D)
        x_specs = [pl.BlockSpec(memory_space=pltpu.HBM)] * 2
        in_scratch = [pltpu.VMEM((3, TT // CHUNK, VROWS, SUBLANES, D), F32),
                      pltpu.SemaphoreType.DMA((3,))]
    else:
        x_in, x_specs, in_scratch = x, [cur_spec, prev_spec], []
    return pl.pallas_call(
        functools.partial(mix_kernel, layer, from_time_order, n_t),
        grid=(B, n_t + 1),
        in_specs=(x_specs + [pl.BlockSpec(memory_space=pltpu.HBM) for _ in weights] + [win_spec]
                  + [_const_spec(c.shape) for c in consts]),
        out_specs=prev_spec,
        out_shape=jax.ShapeDtypeStruct(x.shape, x.dtype),
        scratch_shapes=[
            pltpu.VMEM((D_MODEL, N_IN_COLS), BF16),
            pltpu.VMEM((D_MODEL, D_MODEL), BF16),
            pltpu.VMEM((STAGE_SLOTS, D_MODEL, STAGE_COLS), F32),
            pltpu.SemaphoreType.DMA((STAGE_SLOTS,)),
            pltpu.VMEM((TT, D_MODEL), BF16),
            pltpu.VMEM((2, TT, N_HALO_COLS), F32),
            pltpu.VMEM((2, TT, N_REST_COLS), F32),
            pltpu.VMEM((max_conv_tail, N_HALO_COLS), F32),
            pltpu.VMEM((3, max_pool_tail, D_POOL), F32),
            pltpu.VMEM((TT, D_SSD_XBC + 2 * D_MLSTM), F32),
            pltpu.VMEM((TT, D_MODEL), F32),
            pltpu.VMEM((3, TT, D_POOL), F32),
            pltpu.VMEM((SSD_GROUPS, SSD_STATE, 4 * SSD_HEAD_DIM), F32),
            pltpu.VMEM((MLSTM_HEADS // 2, LANES, 2 * LANES), F32),
            pltpu.VMEM((SUBLANES, LANES), F32),
        ] + in_scratch,
        compiler_params=pltpu.CompilerParams(
            dimension_semantics=("arbitrary", "arbitrary"), vmem_limit_bytes=VMEM_LIMIT),
        name="mix_layer",
    )(x_in, x_in, *weights, win, *consts)


def _ffn_layer(layer, to_time_order, x, wup, wdn, nw, cw, cb, postn):
    B, T, D = x.shape
    TT = FFN_TT
    weights = (wup, wdn)
    assert x.dtype == F32 and wup.dtype == F32 and wdn.dtype == F32
    consts = (nw, cw, cb, postn)
    x_spec = pl.BlockSpec((1, TT, D), lambda b, i: (b, i, 0))
    if to_time_order:
        out_spec = pl.BlockSpec(memory_space=pltpu.HBM)
        out_shape = jax.ShapeDtypeStruct((B, T // CHUNK, SUBLANES, VROWS, D), x.dtype)
        out_scratch = [pltpu.VMEM((2, TT // CHUNK, VROWS, SUBLANES, D), F32),
                       pltpu.SemaphoreType.DMA((2,))]
    else:
        out_spec, out_shape, out_scratch = x_spec, jax.ShapeDtypeStruct(x.shape, x.dtype), []
    return pl.pallas_call(
        functools.partial(ffn_kernel, layer, to_time_order),
        grid=(B, T // TT),
        in_specs=([x_spec] + [pl.BlockSpec(memory_space=pltpu.HBM) for _ in weights]
                  + [_const_spec(c.shape) for c in consts]),
        out_specs=out_spec,
        out_shape=out_shape,
        scratch_shapes=[
            pltpu.VMEM((D_MODEL, 2 * D_FF), BF16),
            pltpu.VMEM((D_FF, D_MODEL), BF16),
            pltpu.VMEM((STAGE_SLOTS, D_MODEL, STAGE_COLS), F32),
            pltpu.VMEM((STAGE_SLOTS, STAGE_COLS, D_MODEL), F32),
            pltpu.SemaphoreType.DMA((STAGE_SLOTS,)),
            pltpu.VMEM(((FFN_CONV - 1) * SUBLANES, 2 * D_FF), F32),
            pltpu.VMEM((TT, D_FF), BF16),
        ] + out_scratch,
        compiler_params=pltpu.CompilerParams(
            dimension_semantics=("arbitrary", "arbitrary"), vmem_limit_bytes=VMEM_LIMIT),
        name="ffn_layer",
    )(x, *weights, *consts).reshape(B, T, D)


def _row(v):
    return v.reshape(1, -1).astype(F32)


def _pad_lanes(v):
    return jnp.pad(v.astype(F32), (0, LANES - v.shape[0])).reshape(1, LANES)


def _prep_pool_w(w):
    out = jnp.zeros((D_POOL, D_POOL), F32)
    for g in range(len(POOL_WINDOWS)):
        s = g * POOL_GROUP_DIM
        out = lax.dynamic_update_slice(out, w[g].astype(F32), (s, s))
    return out.astype(BF16)


def kernel(x, pre_mix_norm, w_in, pool_w, pool_b, pool_scale, ssd_conv_w, ssd_conv_b, ssd_dt_bias, ssd_a_log, ssd_d, ssd_norm, mlstm_conv_w, mlstm_conv_b, mlstm_i_bias, mlstm_f_bias, mlstm_norm, w_out, post_mix_norm, pre_ffn_norm, ffn_w_up, ffn_conv_w, ffn_conv_b, ffn_w_down, post_ffn_norm):
    depth = w_in.shape[0]
    w_in_bf = w_in.astype(BF16)
    for l in range(depth):
        gbias = _pad_lanes(jnp.concatenate([ssd_dt_bias[l], mlstm_i_bias[l], mlstm_f_bias[l]]))
        x = _mix_layer(
            l, l == 0, x, w_out, w_in_bf, _row(pre_mix_norm[l]), gbias, _pad_lanes(ssd_a_log[l]),
            ssd_conv_w[l].astype(F32), _row(ssd_conv_b[l]), mlstm_conv_w[l].astype(F32), _row(mlstm_conv_b[l]),
            _prep_pool_w(pool_w[l]), _row(pool_b[l]), _row(pool_scale[l]),
            _row(jnp.repeat(ssd_d[l], SSD_HEAD_DIM)), _row(ssd_norm[l]), _row(mlstm_norm[l]),
            _row(post_mix_norm[l]))
        x = _ffn_layer(
            l, l == depth - 1, x, ffn_w_up, ffn_w_down, _row(pre_ffn_norm[l]), ffn_conv_w[l].astype(F32),
            _row(ffn_conv_b[l]), _row(post_ffn_norm[l]))
    return x
```
